```python
import math
import jax, jax.numpy as jnp
from jax import lax
import numpy as np

D_MODEL = 1024
BATCH = 1
SEQ = 16384
DEPTH = 1
DEC_BATCH = 16
DEC_SEQ = 16
PAST_LEN = 1024

CHUNK = 64
D_MIX = D_MODEL
D_POOL = D_MIX // 2
D_SSM = D_MIX - D_POOL
POOL_WINDOWS = (2, 4, 8, 16)
N_POOL_GROUPS = len(POOL_WINDOWS)
POOL_GW = D_POOL // N_POOL_GROUPS
POOL_HIST = max(POOL_WINDOWS) - 1
SSM_GROUP = 16
N_SSM_GROUPS = D_SSM // SSM_GROUP
SSM_STATE = 64
DT_MIN = 1e-3
DT_MAX = 1e-1
N_MEM = 256
MEM_HEADS = 4
MEM_HEAD_DIM = D_MODEL // MEM_HEADS
D_FF = 4 * D_MODEL
EPS = 1e-6

kernel_name = "hybrid_pool_s5_stream_encoder_step"


def rmsnorm(x, g):
    xf = x.astype(jnp.float32)
    y = xf * lax.rsqrt(jnp.mean(xf * xf, axis=-1, keepdims=True) + EPS)
    return (y * g.astype(jnp.float32)).astype(x.dtype)


def pool_mixer(u, hist, pos0, w_pool, pool_scale):
    S = u.shape[1]
    u_ext = jnp.concatenate([hist.astype(u.dtype), u], axis=1)
    uf = u_ext.astype(jnp.float32)
    c = jnp.pad(jnp.cumsum(uf, axis=1), ((0, 0), (1, 0), (0, 0)))
    cnt_pos = pos0 + jnp.arange(S) + 1
    cur = uf[:, POOL_HIST:]
    outs = []
    for g, w in enumerate(POOL_WINDOWS):
        sl = slice(g * POOL_GW, (g + 1) * POOL_GW)
        cg = c[..., sl]
        hi = POOL_HIST + 1
        win_sum = cg[:, hi:hi + S] - cg[:, hi - w:hi - w + S]
        cnt = jnp.minimum(cnt_pos, w).astype(jnp.float32)[None, :, None]
        pooled = win_sum / cnt - cur[..., sl]
        outs.append(jnp.einsum('bsc,cd->bsd', pooled, w_pool[g].astype(jnp.float32)))
    y = jnp.concatenate(outs, axis=-1) * pool_scale.astype(jnp.float32)
    return y.astype(u.dtype), u_ext[:, -POOL_HIST:]


def s5_mixer(u, h0_re, h0_im, a_re, a_im, b_re, b_im, c_re, c_im, d_skip, log_dt, w_glu, b_glu):
    B, S, _ = u.shape
    f32 = jnp.float32
    uf = u.astype(f32).reshape(B, S, N_SSM_GROUPS, SSM_GROUP)
    A = lax.complex(a_re.astype(f32), a_im.astype(f32))
    dt = jnp.exp(log_dt.astype(f32))[:, None]
    a_bar = jnp.exp(A * dt)
    b_bar = ((a_bar - 1.0) / A)[..., None] * lax.complex(b_re.astype(f32), b_im.astype(f32))
    bu = jnp.einsum('gph,bsgh->bsgp', b_bar, uf.astype(jnp.complex64))
    h0 = lax.complex(h0_re.astype(f32), h0_im.astype(f32))
    bu = bu.at[:, 0].add(a_bar * h0)
    a_seq = jnp.broadcast_to(a_bar, bu.shape)

    def combine(left, right):
        al, bl = left
        ar, br = right
        return ar * al, ar * bl + br

    _, h = lax.associative_scan(combine, (a_seq, bu), axis=1)
    cm = lax.complex(c_re.astype(f32), c_im.astype(f32))
    y = jnp.einsum('ghp,bsgp->bsgh', cm, h).real + d_skip.astype(f32) * uf
    y = jax.nn.gelu(y.reshape(B, S, D_SSM))
    y = y * jax.nn.sigmoid(y @ w_glu.astype(f32) + b_glu.astype(f32))
    h_last = h[:, -1]
    return y.astype(u.dtype), h_last.real.astype(h0_re.dtype), h_last.imag.astype(h0_im.dtype)


def mem_kv(mem, g_mem, w_k, w_v):
    B = mem.shape[0]
    m = rmsnorm(mem, g_mem)
    k = (m @ w_k).reshape(B, N_MEM, MEM_HEADS, MEM_HEAD_DIM)
    v = (m @ w_v).reshape(B, N_MEM, MEM_HEADS, MEM_HEAD_DIM)
    return k, v


def cross_attn(h, k, v, w_q, w_o):
    B, S, _ = h.shape
    q = (h @ w_q).reshape(B, S, MEM_HEADS, MEM_HEAD_DIM)
    s = jnp.einsum('bshd,bnhd->bhsn', q, k).astype(jnp.float32) * (MEM_HEAD_DIM ** -0.5)
    p = jax.nn.softmax(s, axis=-1).astype(v.dtype)
    o = jnp.einsum('bhsn,bnhd->bshd', p, v).reshape(B, S, MEM_HEADS * MEM_HEAD_DIM)
    return o @ w_o


def layer(x, pool_hist, pos0, h0_re, h0_im, mk, mv,
          g_mix, w_in, w_pool, pool_scale, a_re, a_im, b_re, b_im, c_re, c_im, d_skip, log_dt,
          w_glu, b_glu, w_out, g_xattn, w_q, w_o, g_mlp, w_up, w_down):
    h = rmsnorm(x, g_mix)
    z = h @ w_in
    y_pool, new_hist = pool_mixer(z[..., :D_POOL], pool_hist, pos0, w_pool, pool_scale)
    y_ssm, hr, hi = s5_mixer(z[..., D_POOL:], h0_re, h0_im, a_re, a_im, b_re, b_im,
                             c_re, c_im, d_skip, log_dt, w_glu, b_glu)
    x = x + jnp.concatenate([y_pool, y_ssm], axis=-1) @ w_out
    x = x + cross_attn(rmsnorm(x, g_xattn), mk, mv, w_q, w_o)
    hm = rmsnorm(x, g_mlp)
    x = x + jnp.square(jax.nn.relu(hm @ w_up)) @ w_down
    return x, new_hist, hr, hi


def setup_inputs(seed: int = 0) -> dict:
    key = jax.random.key(seed)
    ks = iter(jax.random.split(key, 48))
    f32 = jnp.float32

    def nrm(shape, scale):
        return jax.random.normal(next(ks), shape, f32) * scale

    L, G, P, H = DEPTH, N_SSM_GROUPS, SSM_STATE, SSM_GROUP
    n = jnp.arange(P, dtype=f32)
    inp = {
        "x_prompt": nrm((BATCH, SEQ, D_MODEL), 1.0),
        "x_sample": nrm((DEC_BATCH, DEC_SEQ, D_MODEL), 1.0),
        "cache_mem_k": nrm((L, DEC_BATCH, N_MEM, MEM_HEADS, MEM_HEAD_DIM), 1.0),
        "cache_mem_v": nrm((L, DEC_BATCH, N_MEM, MEM_HEADS, MEM_HEAD_DIM), 1.0),
        "state_pool": nrm((L, DEC_BATCH, POOL_HIST, D_POOL), 1.0),
        "state_ssm_re": nrm((L, DEC_BATCH, G, P), 0.1),
        "state_ssm_im": nrm((L, DEC_BATCH, G, P), 0.1),
        "mem_prompt": nrm((BATCH, N_MEM, D_MODEL), 1.0),
        "g_mix": 1.0 + nrm((L, D_MODEL), 0.02),
        "w_in": nrm((L, D_MODEL, D_MIX), D_MODEL ** -0.5),
        "w_pool": nrm((L, N_POOL_GROUPS, POOL_GW, POOL_GW), POOL_GW ** -0.5),
        "pool_scale": 1.0 + nrm((L, D_POOL), 0.1),
        "ssm_a_re": -0.5 + nrm((L, G, P), 0.01),
        "ssm_a_im": math.pi * n + nrm((L, G, P), 0.01),
        "ssm_b_re": nrm((L, G, P, H), (2.0 * H) ** -0.5),
        "ssm_b_im": nrm((L, G, P, H), (2.0 * H) ** -0.5),
        "ssm_c_re": nrm((L, G, H, P), (2.0 * P) ** -0.5 * 4.0),
        "ssm_c_im": nrm((L, G, H, P), (2.0 * P) ** -0.5 * 4.0),
        "ssm_d": nrm((L, G, H), 1.0),
        "ssm_log_dt": jax.random.uniform(next(ks), (L, G), f32, math.log(DT_MIN), math.log(DT_MAX)),
        "w_glu": nrm((L, D_SSM, D_SSM), D_SSM ** -0.5),
        "b_glu": nrm((L, D_SSM), 0.02),
        "w_out": nrm((L, D_MIX, D_MODEL), D_MIX ** -0.5),
        "g_xattn": 1.0 + nrm((L, D_MODEL), 0.02),
        "g_mem": 1.0 + nrm((L, D_MODEL), 0.02),
        "w_q": nrm((L, D_MODEL, MEM_HEADS * MEM_HEAD_DIM), D_MODEL ** -0.5),
        "w_k": nrm((L, D_MODEL, MEM_HEADS * MEM_HEAD_DIM), D_MODEL ** -0.5),
        "w_v": nrm((L, D_MODEL, MEM_HEADS * MEM_HEAD_DIM), D_MODEL ** -0.5),
        "w_o": nrm((L, MEM_HEADS * MEM_HEAD_DIM, D_MODEL), D_MODEL ** -0.5),
        "g_mlp": 1.0 + nrm((L, D_MODEL), 0.02),
        "w_up": nrm((L, D_MODEL, D_FF), D_MODEL ** -0.5),
        "w_down": nrm((L, D_FF, D_MODEL), D_FF ** -0.5),
        "g_final": 1.0 + nrm((D_MODEL,), 0.02),
    }
    return inp


def reference(x_prompt, x_sample, cache_mem_k, cache_mem_v, state_pool, state_ssm_re, state_ssm_im,
              mem_prompt, g_mix, w_in, w_pool, pool_scale, ssm_a_re, ssm_a_im, ssm_b_re, ssm_b_im,
              ssm_c_re, ssm_c_im, ssm_d, ssm_log_dt, w_glu, b_glu, w_out, g_xattn, g_mem,
              w_q, w_k, w_v, w_o, g_mlp, w_up, w_down, g_final):
    xp, xs = x_prompt, x_sample
    Bp = x_prompt.shape[0]
    mk_p, mv_p, pool_p, sre_p, sim_p = [], [], [], [], []
    pool_s, sre_s, sim_s = [], [], []
    for l in range(DEPTH):
        wts = (g_mix[l], w_in[l], w_pool[l], pool_scale[l], ssm_a_re[l], ssm_a_im[l],
               ssm_b_re[l], ssm_b_im[l], ssm_c_re[l], ssm_c_im[l], ssm_d[l], ssm_log_dt[l],
               w_glu[l], b_glu[l], w_out[l], g_xattn[l], w_q[l], w_o[l], g_mlp[l], w_up[l], w_down[l])
        mk, mv = mem_kv(mem_prompt, g_mem[l], w_k[l], w_v[l])
        hist0 = jnp.zeros((Bp, POOL_HIST, D_POOL), xp.dtype)
        zs = jnp.zeros((Bp, N_SSM_GROUPS, SSM_STATE), state_ssm_re.dtype)
        xp, ph, hr, hi = layer(xp, hist0, 0, zs, zs, mk, mv, *wts)
        mk_p.append(mk); mv_p.append(mv); pool_p.append(ph); sre_p.append(hr); sim_p.append(hi)
        xs, sh, shr, shi = layer(xs, state_pool[l], PAST_LEN, state_ssm_re[l], state_ssm_im[l],
                                 cache_mem_k[l], cache_mem_v[l], *wts)
        pool_s.append(sh); sre_s.append(shr); sim_s.append(shi)
    y_prompt = rmsnorm(xp, g_final)
    y_sample = rmsnorm(xs, g_final)
    return (y_prompt, y_sample,
            jnp.stack(mk_p), jnp.stack(mv_p), jnp.stack(pool_p), jnp.stack(sre_p), jnp.stack(sim_p),
            jnp.stack(pool_s), jnp.stack(sre_s), jnp.stack(sim_s))
```

```python
import functools
import math

import jax
import jax.numpy as jnp
from jax import lax
from jax.experimental import pallas as pl
from jax.experimental.pallas import tpu as pltpu

F32 = jnp.float32
BF16 = jnp.bfloat16

EPS = 1e-6
POOL_WINDOWS = (2, 4, 8, 16)
POOL_HIST = max(POOL_WINDOWS) - 1
HIST_ROWS = 16
SSM_GROUP = 16
SSM_STATE = 64
MEM_HEADS = 4
N_HALF = 2

PROMPT_BLOCK = 512
PROMPT_CHUNK = 64
FF_CHUNK = 1024
VMEM_LIMIT = 56 * 1024 * 1024


def _rmsnorm(x, g):
    return x * lax.rsqrt(jnp.mean(x * x, axis=-1, keepdims=True) + EPS) * g


def _dot(a, b):
    return jnp.dot(a, b, preferred_element_type=F32)


def _pool_windows(zbuf, row0, nrows, pos, wpool_ref, pscale_ref):
    gw = wpool_ref.shape[-1]
    outs = []
    for g, w in enumerate(POOL_WINDOWS):
        lanes = slice(g * gw, (g + 1) * gw)
        cur = zbuf[pl.ds(row0, nrows), lanes]
        win = cur
        for j in range(1, w):
            win = win + zbuf[pl.ds(row0 - j, nrows), lanes]
        cnt = jnp.minimum(pos + 1, w).astype(F32)
        pooled = win / cnt - cur
        outs.append(_dot(pooled.astype(BF16), wpool_ref[g]))
    return jnp.concatenate(outs, axis=-1) * pscale_ref[...]


def _cmul(ar, ai, br, bi):
    return ar * br - ai * bi, ar * bi + ai * br


def _s5_scan(u, n_chunks, T, sequential, h_in,
             bblk_ref, cblk_ref, pneg_re_ref, pneg_im_ref, ppos_re_ref, ppos_im_ref,
             a1_re_ref, a1_im_ref, bu_ref, xs_ref, hout_re_ref, hout_im_ref):
    hw = bblk_ref.shape[1]
    nh = bblk_ref.shape[2] // 2
    u_bf = u.astype(BF16)
    for k in range(N_HALF):
        bu_ref[:, k * 2 * nh:(k + 1) * 2 * nh] = _dot(u_bf[:, k * hw:(k + 1) * hw], bblk_ref[k])

    def re_cols(k):
        return slice(k * 2 * nh, k * 2 * nh + nh)

    def im_cols(k):
        return slice(k * 2 * nh + nh, (k + 1) * 2 * nh)

    def st_cols(k):
        return slice(k * nh, (k + 1) * nh)

    def prescale(c, carry):
        rows = pl.ds(pl.multiple_of(c * T, T), T)
        for k in range(N_HALF):
            xr, xi = _cmul(bu_ref[rows, re_cols(k)], bu_ref[rows, im_cols(k)],
                           pneg_re_ref[0:T, st_cols(k)], pneg_im_ref[0:T, st_cols(k)])
            xs_ref[rows, re_cols(k)] = xr.astype(BF16)
            xs_ref[rows, im_cols(k)] = xi.astype(BF16)
        return carry

    lax.fori_loop(0, n_chunks, prescale, 0)

    r_i = lax.broadcasted_iota(jnp.int32, (T, T), 0)
    c_i = lax.broadcasted_iota(jnp.int32, (T, T), 1)
    ltri = jnp.where(r_i >= c_i, 1.0, 0.0).astype(BF16)
    for c in range(n_chunks):
        rows = pl.ds(c * T, T)
        bu_ref[rows, :] = _dot(ltri, xs_ref[rows, :])

    def postscale(c, carry):
        rows = pl.ds(pl.multiple_of(c * T, T), T)
        new = []
        for k in range(N_HALF):
            if sequential:
                hr, hi = carry[2 * k], carry[2 * k + 1]
            else:
                hr = h_in[0][pl.ds(c, 1), st_cols(k)]
                hi = h_in[1][pl.ds(c, 1), st_cols(k)]
            cr, ci = _cmul(a1_re_ref[:, st_cols(k)], a1_im_ref[:, st_cols(k)], hr, hi)
            wr = bu_ref[rows, re_cols(k)] + cr
            wi = bu_ref[rows, im_cols(k)] + ci
            sr, si = _cmul(ppos_re_ref[0:T, st_cols(k)], ppos_im_ref[0:T, st_cols(k)], wr, wi)
            xs_ref[rows, re_cols(k)] = sr.astype(BF16)
            xs_ref[rows, im_cols(k)] = si.astype(BF16)
            er, ei = sr[T - 1:T, :], si[T - 1:T, :]
            if sequential:
                new += [er, ei]
            else:
                hout_re_ref[pl.ds(c, 1), st_cols(k)] = er
                hout_im_ref[pl.ds(c, 1), st_cols(k)] = ei
        return tuple(new) if sequential else carry

    if sequential:
        init = []
        for k in range(N_HALF):
            init += [h_in[0][:, st_cols(k)], h_in[1][:, st_cols(k)]]
        fin = lax.fori_loop(0, n_chunks, postscale, tuple(init))
        for k in range(N_HALF):
            hout_re_ref[:, st_cols(k)] = fin[2 * k]
            hout_im_ref[:, st_cols(k)] = fin[2 * k + 1]
    else:
        lax.fori_loop(0, n_chunks, postscale, 0)

    ys = [_dot(xs_ref[:, k * 2 * nh:(k + 1) * 2 * nh], cblk_ref[k]) for k in range(N_HALF)]
    return jnp.concatenate(ys, axis=-1)


def _mix_tail(x, y_pool, u, y_state, dskip_ref, wglu_ref, bglu_ref, wout_ref):
    y = jax.nn.gelu(y_state + dskip_ref[...] * u)
    y = y * jax.nn.sigmoid(_dot(y.astype(BF16), wglu_ref[...]) + bglu_ref[...])
    ycat = jnp.concatenate([y_pool.astype(BF16), y.astype(BF16)], axis=-1)
    return x + _dot(ycat, wout_ref[...])


def _mixer_prompt_kernel(x_ref, gmix_ref, win_ref, wpool_ref, pscale_ref, bblk_ref, cblk_ref,
                         pneg_re_ref, pneg_im_ref, ppos_re_ref, ppos_im_ref, a1_re_ref, a1_im_ref,
                         dskip_ref, wglu_ref, bglu_ref, wout_ref,
                         out_ref, hist_ref, hre_ref, him_ref,
                         zbuf, bu_ref, xs_ref, *, T):
    i = pl.program_id(0)
    ts = x_ref.shape[0]
    d_pool = zbuf.shape[1]

    @pl.when(i == 0)
    def _():
        zbuf[0:HIST_ROWS, :] = jnp.zeros((HIST_ROWS, d_pool), F32)
        hre_ref[...] = jnp.zeros(hre_ref.shape, F32)
        him_ref[...] = jnp.zeros(him_ref.shape, F32)

    x = x_ref[...]
    z = _dot(_rmsnorm(x, gmix_ref[...]).astype(BF16), win_ref[...])
    zbuf[HIST_ROWS:HIST_ROWS + ts, :] = z[:, :d_pool]
    u = z[:, d_pool:]

    pos = i * ts + lax.broadcasted_iota(jnp.int32, (ts, 1), 0)
    y_pool = _pool_windows(zbuf, HIST_ROWS, ts, pos, wpool_ref, pscale_ref)

    h_in = (hre_ref[...], him_ref[...])
    y_state = _s5_scan(u, ts // T, T, True, h_in, bblk_ref, cblk_ref,
                       pneg_re_ref, pneg_im_ref, ppos_re_ref, ppos_im_ref,
                       a1_re_ref, a1_im_ref, bu_ref, xs_ref, hre_ref, him_ref)

    out_ref[...] = _mix_tail(x, y_pool, u, y_state, dskip_ref, wglu_ref, bglu_ref, wout_ref)

    tail = zbuf[ts:ts + HIST_ROWS, :]
    zbuf[0:HIST_ROWS, :] = tail
    hist_ref[...] = tail


def _mixer_sample_kernel(x_ref, hist_in_ref, h0_re_ref, h0_im_ref,
                         gmix_ref, win_ref, wpool_ref, pscale_ref, bblk_ref, cblk_ref,
                         pneg_re_ref, pneg_im_ref, ppos_re_ref, ppos_im_ref, a1_re_ref, a1_im_ref,
                         dskip_ref, wglu_ref, bglu_ref, wout_ref,
                         out_ref, hist_ref, hre_ref, him_ref,
                         zbuf, ypool_ref, bu_ref, xs_ref, *, n_streams, T, pos0):
    d_pool = zbuf.shape[1]
    ext = HIST_ROWS + T

    x = x_ref[...]
    z = _dot(_rmsnorm(x, gmix_ref[...]).astype(BF16), win_ref[...])
    u = z[:, d_pool:]
    for b in range(n_streams):
        zbuf[b * ext:b * ext + 1, :] = jnp.zeros((1, d_pool), F32)
        zbuf[b * ext + 1:b * ext + HIST_ROWS, :] = hist_in_ref[b]
        zbuf[b * ext + HIST_ROWS:(b + 1) * ext, :] = z[b * T:(b + 1) * T, :d_pool]
        hist_ref[b] = zbuf[(b + 1) * ext - POOL_HIST:(b + 1) * ext, :]

    nrows = n_streams * ext - HIST_ROWS
    r = lax.broadcasted_iota(jnp.int32, (nrows, 1), 0)
    pos = pos0 + lax.rem(r, ext)
    y_all = _pool_windows(zbuf, HIST_ROWS, nrows, pos, wpool_ref, pscale_ref)
    for b in range(n_streams):
        ypool_ref[b * T:(b + 1) * T, :] = y_all[b * ext:b * ext + T, :]

    y_state = _s5_scan(u, n_streams, T, False, (h0_re_ref, h0_im_ref), bblk_ref, cblk_ref,
                       pneg_re_ref, pneg_im_ref, ppos_re_ref, ppos_im_ref,
                       a1_re_ref, a1_im_ref, bu_ref, xs_ref, hre_ref, him_ref)

    out_ref[...] = _mix_tail(x, ypool_ref[...], u, y_state, dskip_ref, wglu_ref, bglu_ref, wout_ref)


def _const_spec(shape):
    nd = len(shape)
    return pl.BlockSpec(shape, lambda *_: (0,) * nd)


def _mixer_weights(p):
    return (p["g_mix"], p["w_in"], p["w_pool"], p["pool_scale"], p["bblk"], p["cblk"],
            p["pneg_re"], p["pneg_im"], p["ppos_re"], p["ppos_im"], p["a1_re"], p["a1_im"],
            p["d_skip"], p["w_glu"], p["b_glu"], p["w_out"])


def _mixer_prompt(x, p):
    S, D = x.shape
    ts, T = PROMPT_BLOCK, PROMPT_CHUNK
    d_pool = p["w_pool"].shape[0] * p["w_pool"].shape[-1]
    n_state = p["a1_re"].shape[-1]
    weights = _mixer_weights(p)
    return pl.pallas_call(
        functools.partial(_mixer_prompt_kernel, T=T),
        grid=(S // ts,),
        in_specs=[pl.BlockSpec((ts, D), lambda i: (i, 0))] + [_const_spec(w.shape) for w in weights],
        out_specs=[pl.BlockSpec((ts, D), lambda i: (i, 0)),
                   _const_spec((HIST_ROWS, d_pool)),
                   _const_spec((1, n_state)), _const_spec((1, n_state))],
        out_shape=[jax.ShapeDtypeStruct((S, D), F32),
                   jax.ShapeDtypeStruct((HIST_ROWS, d_pool), F32),
                   jax.ShapeDtypeStruct((1, n_state), F32),
                   jax.ShapeDtypeStruct((1, n_state), F32)],
        scratch_shapes=[pltpu.VMEM((HIST_ROWS + ts, d_pool), F32),
                        pltpu.VMEM((ts, 2 * n_state), F32),
                        pltpu.VMEM((ts, 2 * n_state), BF16)],
        compiler_params=pltpu.CompilerParams(dimension_semantics=("arbitrary",),
                                             vmem_limit_bytes=VMEM_LIMIT),
        name="mixer_prompt",
    )(x, *weights)


def _mixer_sample(x, hist, h0_re, h0_im, p, pos0):
    n_streams, T, D = x.shape
    rows = n_streams * T
    d_pool = hist.shape[-1]
    n_state = p["a1_re"].shape[-1]
    weights = _mixer_weights(p)
    return pl.pallas_call(
        functools.partial(_mixer_sample_kernel, n_streams=n_streams, T=T, pos0=pos0),
        out_shape=[jax.ShapeDtypeStruct((rows, D), F32),
                   jax.ShapeDtypeStruct((n_streams, POOL_HIST, d_pool), F32),
                   jax.ShapeDtypeStruct((n_streams, n_state), F32),
                   jax.ShapeDtypeStruct((n_streams, n_state), F32)],
        scratch_shapes=[pltpu.VMEM((n_streams * (HIST_ROWS + T), d_pool), F32),
                        pltpu.VMEM((rows, d_pool), F32),
                        pltpu.VMEM((rows, 2 * n_state), F32),
                        pltpu.VMEM((rows, 2 * n_state), BF16)],
        compiler_params=pltpu.CompilerParams(vmem_limit_bytes=VMEM_LIMIT),
        name="mixer_sample",
    )(x.reshape(rows, D), hist, h0_re, h0_im, *weights)


def _mem_kv_kernel(mem_ref, g_ref, wk_ref, wv_ref, k_ref, v_ref):
    m = _rmsnorm(mem_ref[...], g_ref[...]).astype(BF16)
    k_ref[...] = _dot(m, wk_ref[...])
    v_ref[...] = _dot(m, wv_ref[...])


def _mem_kv(mem, g_mem, w_k, w_v):
    n, d = mem.shape
    return pl.pallas_call(
        _mem_kv_kernel,
        out_shape=[jax.ShapeDtypeStruct((n, d), F32), jax.ShapeDtypeStruct((n, d), F32)],
        compiler_params=pltpu.CompilerParams(vmem_limit_bytes=VMEM_LIMIT),
        name="mem_kv",
    )(mem, g_mem, w_k, w_v)


def _attend(q, k, v, scale):
    hd = q.shape[-1] // MEM_HEADS
    outs = []
    for h in range(MEM_HEADS):
        cols = slice(h * hd, (h + 1) * hd)
        s = lax.dot_general(q[:, cols], k[:, cols].astype(BF16), (((1,), (1,)), ((), ())),
                            preferred_element_type=F32) * scale
        e = jnp.exp(s - jnp.max(s, axis=-1, keepdims=True))
        prob = e / jnp.sum(e, axis=-1, keepdims=True)
        outs.append(_dot(prob.astype(BF16), v[:, cols].astype(BF16)))
    return jnp.concatenate(outs, axis=-1)


def _xattn_kernel(x_ref, k_ref, v_ref, g_ref, wq_ref, wo_ref, out_ref, *, scale):
    x = x_ref[...]
    q = _dot(_rmsnorm(x, g_ref[...]).astype(BF16), wq_ref[...]).astype(BF16)
    k = k_ref[0]
    v = v_ref[0]
    o = _attend(q, k, v, scale)
    out_ref[...] = x + _dot(o.astype(BF16), wo_ref[...])


def _xattn(x, k, v, p, rows_per_step):
    rows, D = x.shape
    n_kv, n_mem, _ = k.shape
    kv_map = (lambda i: (i, 0, 0)) if n_kv > 1 else (lambda i: (0, 0, 0))
    scale = (D // MEM_HEADS) ** -0.5
    return pl.pallas_call(
        functools.partial(_xattn_kernel, scale=scale),
        grid=(rows // rows_per_step,),
        in_specs=[pl.BlockSpec((rows_per_step, D), lambda i: (i, 0)),
                  pl.BlockSpec((1, n_mem, D), kv_map),
                  pl.BlockSpec((1, n_mem, D), kv_map),
                  _const_spec(p["g_xattn"].shape), _const_spec(p["w_q"].shape), _const_spec(p["w_o"].shape)],
        out_specs=pl.BlockSpec((rows_per_step, D), lambda i: (i, 0)),
        out_shape=jax.ShapeDtypeStruct((rows, D), F32),
        compiler_params=pltpu.CompilerParams(dimension_semantics=("arbitrary",),
                                             vmem_limit_bytes=VMEM_LIMIT),
        name="xattn",
    )(x, k, v, p["g_xattn"], p["w_q"], p["w_o"])


def _mlp_kernel(x_ref, g_ref, wup_ref, wdown_ref, gfin_ref, out_ref):
    x = x_ref[...]
    hm = _rmsnorm(x, g_ref[...]).astype(BF16)
    acc = x
    for j in range(wup_ref.shape[1] // FF_CHUNK):
        cols = slice(j * FF_CHUNK, (j + 1) * FF_CHUNK)
        up = _dot(hm, wup_ref[:, cols])
        act = jnp.square(jnp.maximum(up, 0.0)).astype(BF16)
        acc = acc + _dot(act, wdown_ref[cols, :])
    out_ref[...] = _rmsnorm(acc, gfin_ref[...])


def _mlp(x, p, rows_per_step):
    rows, D = x.shape
    return pl.pallas_call(
        _mlp_kernel,
        grid=(rows // rows_per_step,),
        in_specs=[pl.BlockSpec((rows_per_step, D), lambda i: (i, 0)),
                  _const_spec(p["g_mlp"].shape), _const_spec(p["w_up"].shape),
                  _const_spec(p["w_down"].shape), _const_spec(p["g_final"].shape)],
        out_specs=pl.BlockSpec((rows_per_step, D), lambda i: (i, 0)),
        out_shape=jax.ShapeDtypeStruct((rows, D), F32),
        compiler_params=pltpu.CompilerParams(dimension_semantics=("arbitrary",),
                                             vmem_limit_bytes=VMEM_LIMIT),
        name="mlp",
    )(x, p["g_mlp"], p["w_up"], p["w_down"], p["g_final"])


def _block_diag(t):
    nhalf, gi, r, c = t.shape
    eye = jnp.eye(gi, dtype=t.dtype)
    return jnp.einsum("kirc,ij->kirjc", t, eye).reshape(nhalf, gi * r, gi * c)


def _prepare_layer(l, g_mix, w_in, w_pool, pool_scale, a_re, a_im, b_re, b_im, c_re, c_im, d_skip,
                   log_dt, w_glu, b_glu, w_out, g_xattn, g_mem, w_q, w_k, w_v, w_o, g_mlp, w_up,
                   w_down, g_final, t_max):
    G, P = a_re.shape[1:]
    gi = G // N_HALF
    ar, ai = a_re[l].astype(F32), a_im[l].astype(F32)
    dt = jnp.exp(log_dt[l].astype(F32))[:, None]
    lam_re, lam_im = ar * dt, ai * dt

    def a_pow(kk):
        kk = kk.astype(F32)[:, None, None]
        mag = jnp.exp(kk * lam_re)
        ang = kk * lam_im
        return (mag * jnp.cos(ang)).reshape(-1, G * P), (mag * jnp.sin(ang)).reshape(-1, G * P)

    steps = jnp.arange(t_max)
    pneg_re, pneg_im = a_pow(-steps)
    ppos_re, ppos_im = a_pow(steps)
    a1_re, a1_im = a_pow(jnp.ones((1,), F32))

    ab_re, ab_im = a1_re.reshape(G, P), a1_im.reshape(G, P)
    den = ar * ar + ai * ai
    coef_re = ((ab_re - 1.0) * ar + ab_im * ai) / den
    coef_im = (ab_im * ar - (ab_re - 1.0) * ai) / den
    br, bi = b_re[l].astype(F32), b_im[l].astype(F32)
    bb_re = coef_re[..., None] * br - coef_im[..., None] * bi
    bb_im = coef_re[..., None] * bi + coef_im[..., None] * br

    def to_b(t):
        return _block_diag(t.reshape(N_HALF, gi, P, -1).transpose(0, 1, 3, 2))

    def to_c(t):
        return _block_diag(t.reshape(N_HALF, gi, -1, P).transpose(0, 1, 3, 2))

    bblk = jnp.concatenate([to_b(bb_re), to_b(bb_im)], axis=-1).astype(BF16)
    cblk = jnp.concatenate([to_c(c_re[l].astype(F32)), to_c(-c_im[l].astype(F32))], axis=1).astype(BF16)

    row = lambda v: v.astype(F32).reshape(1, -1)
    return dict(
        g_mix=row(g_mix[l]), w_in=w_in[l].astype(BF16), w_pool=w_pool[l].astype(BF16),
        pool_scale=row(pool_scale[l]), bblk=bblk, cblk=cblk,
        pneg_re=pneg_re, pneg_im=pneg_im, ppos_re=ppos_re, ppos_im=ppos_im, a1_re=a1_re, a1_im=a1_im,
        d_skip=row(d_skip[l]), w_glu=w_glu[l].astype(BF16), b_glu=row(b_glu[l]),
        w_out=w_out[l].astype(BF16), g_xattn=row(g_xattn[l]), g_mem=row(g_mem[l]),
        w_q=w_q[l].astype(BF16), w_k=w_k[l].astype(BF16), w_v=w_v[l].astype(BF16),
        w_o=w_o[l].astype(BF16), g_mlp=row(g_mlp[l]), w_up=w_up[l].astype(BF16),
        w_down=w_down[l].astype(BF16), g_final=row(g_final))


def kernel(x_prompt, x_sample, cache_mem_k, cache_mem_v, state_pool, state_ssm_re, state_ssm_im, mem_prompt, g_mix, w_in, w_pool, pool_scale, ssm_a_re, ssm_a_im, ssm_b_re, ssm_b_im, ssm_c_re, ssm_c_im, ssm_d, ssm_log_dt, w_glu, b_glu, w_out, g_xattn, g_mem, w_q, w_k, w_v, w_o, g_mlp, w_up, w_down, g_final):
    depth = g_mix.shape[0]
    assert depth == 1 and x_prompt.shape[0] == 1, "single layer, single prompt stream"
    Bp, S, D = x_prompt.shape
    Bs, Ts, _ = x_sample.shape
    n_mem = mem_prompt.shape[1]
    G, P = ssm_a_re.shape[1:]
    past_len = 1024
    assert S % PROMPT_BLOCK == 0 and PROMPT_BLOCK % PROMPT_CHUNK == 0 and Ts <= PROMPT_CHUNK

    l = 0
    p = _prepare_layer(l, g_mix, w_in, w_pool, pool_scale, ssm_a_re, ssm_a_im, ssm_b_re, ssm_b_im,
                       ssm_c_re, ssm_c_im, ssm_d, ssm_log_dt, w_glu, b_glu, w_out, g_xattn, g_mem,
                       w_q, w_k, w_v, w_o, g_mlp, w_up, w_down, g_final, PROMPT_CHUNK)

    mk, mv = _mem_kv(mem_prompt[0], p["g_mem"], p["w_k"], p["w_v"])
    x1, hist_p, hre_p, him_p = _mixer_prompt(x_prompt[0], p)
    x2 = _xattn(x1, mk[None], mv[None], p, PROMPT_BLOCK)
    y_prompt = _mlp(x2, p, PROMPT_BLOCK)

    xs1, hist_s, hre_s, him_s = _mixer_sample(
        x_sample, state_pool[l], state_ssm_re[l].reshape(Bs, G * P), state_ssm_im[l].reshape(Bs, G * P),
        p, past_len)
    xs2 = _xattn(xs1, cache_mem_k[l].reshape(Bs, n_mem, D), cache_mem_v[l].reshape(Bs, n_mem, D), p, Ts)
    y_sample = _mlp(xs2, p, Bs * Ts)

    hd = D // MEM_HEADS
    return (y_prompt[None], y_sample.reshape(Bs, Ts, D),
            mk.reshape(1, Bp, n_mem, MEM_HEADS, hd), mv.reshape(1, Bp, n_mem, MEM_HEADS, hd),
            hist_p[1:].reshape(1, Bp, POOL_HIST, -1),
            hre_p.reshape(1, Bp, G, P), him_p.reshape(1, Bp, G, P),
            hist_s[None], hre_s.reshape(1, Bs, G, P), him_s.reshape(1, Bs, G, P))
```

```python
import functools
import math

import jax
import jax.numpy as jnp
from jax import lax
from jax.experimental import pallas as pl
from jax.experimental.pallas import tpu as pltpu

F32 = jnp.float32
BF16 = jnp.bfloat16

EPS = 1e-6
POOL_WINDOWS = (2, 4, 8, 16)
POOL_HIST = max(POOL_WINDOWS) - 1
HIST_ROWS = 16
SSM_GROUP = 16
SSM_STATE = 64
MEM_HEADS = 4
N_HALF = 2

PROMPT_BLOCK = 512
PROMPT_CHUNK = 64
FF_CHUNK = 1024
VMEM_LIMIT = 56 * 1024 * 1024


def _rmsnorm(x, g):
    return x * lax.rsqrt(jnp.mean(x * x, axis=-1, keepdims=True) + EPS) * g


def _dot(a, b):
    return jnp.dot(a, b, preferred_element_type=F32)


def _pool_windows(zbuf, row0, nrows, pos, wpool_ref, pscale_ref):
    gw = wpool_ref.shape[-1]
    assert row0 == HIST_ROWS
    outs = []
    for g, w in enumerate(POOL_WINDOWS):
        lanes = slice(g * gw, (g + 1) * gw)
        win = zbuf[pl.ds(0, row0 + nrows), lanes]
        k = 1
        while k < w:
            win = win + pltpu.roll(win, k, 0)
            k *= 2
        cur = zbuf[pl.ds(row0, nrows), lanes]
        cnt = jnp.minimum(pos + 1, w).astype(F32)
        pooled = win[row0:, :] / cnt - cur
        outs.append(_dot(pooled.astype(BF16), wpool_ref[g]))
    return jnp.concatenate(outs, axis=-1) * pscale_ref[...]


def _cmul(ar, ai, br, bi):
    return ar * br - ai * bi, ar * bi + ai * br


def _s5_scan(u, n_chunks, T, sequential, h_in,
             bblk_ref, cblk_ref, pneg_re_ref, pneg_im_ref, ppos_re_ref, ppos_im_ref,
             a1_re_ref, a1_im_ref, bu_ref, xs_ref, hout_re_ref, hout_im_ref):
    hw = bblk_ref.shape[1]
    nh = bblk_ref.shape[2] // 2
    u_bf = u.astype(BF16)
    for k in range(N_HALF):
        bu_ref[:, k * 2 * nh:(k + 1) * 2 * nh] = _dot(u_bf[:, k * hw:(k + 1) * hw], bblk_ref[k])

    def re_cols(k):
        return slice(k * 2 * nh, k * 2 * nh + nh)

    def im_cols(k):
        return slice(k * 2 * nh + nh, (k + 1) * 2 * nh)

    def st_cols(k):
        return slice(k * nh, (k + 1) * nh)

    def prescale(c, carry):
        rows = pl.ds(c * T, T)
        for k in range(N_HALF):
            xr, xi = _cmul(bu_ref[rows, re_cols(k)], bu_ref[rows, im_cols(k)],
                           pneg_re_ref[0:T, st_cols(k)], pneg_im_ref[0:T, st_cols(k)])
            xs_ref[rows, re_cols(k)] = xr.astype(BF16)
            xs_ref[rows, im_cols(k)] = xi.astype(BF16)
        return carry

    for c in range(n_chunks):
        prescale(c, 0)

    r_i = lax.broadcasted_iota(jnp.int32, (T, T), 0)
    c_i = lax.broadcasted_iota(jnp.int32, (T, T), 1)
    ltri = jnp.where(r_i >= c_i, 1.0, 0.0).astype(BF16)
    for c in range(n_chunks):
        rows = pl.ds(c * T, T)
        bu_ref[rows, :] = _dot(ltri, xs_ref[rows, :])

    def postscale(c, carry):
        rows = pl.ds(c * T, T)
        new = []
        for k in range(N_HALF):
            if sequential:
                hr, hi = carry[2 * k], carry[2 * k + 1]
            else:
                hr = h_in[0][pl.ds(c, 1), st_cols(k)]
                hi = h_in[1][pl.ds(c, 1), st_cols(k)]
            cr, ci = _cmul(a1_re_ref[:, st_cols(k)], a1_im_ref[:, st_cols(k)], hr, hi)
            wr = bu_ref[rows, re_cols(k)] + cr
            wi = bu_ref[rows, im_cols(k)] + ci
            sr, si = _cmul(ppos_re_ref[0:T, st_cols(k)], ppos_im_ref[0:T, st_cols(k)], wr, wi)
            xs_ref[rows, re_cols(k)] = sr.astype(BF16)
            xs_ref[rows, im_cols(k)] = si.astype(BF16)
            er, ei = sr[T - 1:T, :], si[T - 1:T, :]
            if sequential:
                new += [er, ei]
            else:
                hout_re_ref[pl.ds(c, 1), st_cols(k)] = er
                hout_im_ref[pl.ds(c, 1), st_cols(k)] = ei
        return tuple(new) if sequential else carry

    if sequential:
        init = []
        for k in range(N_HALF):
            init += [h_in[0][:, st_cols(k)], h_in[1][:, st_cols(k)]]
        fin = tuple(init)
        for c in range(n_chunks):
            fin = postscale(c, fin)
        for k in range(N_HALF):
            hout_re_ref[:, st_cols(k)] = fin[2 * k]
            hout_im_ref[:, st_cols(k)] = fin[2 * k + 1]
    else:
        for c in range(n_chunks):
            postscale(c, 0)

    ys = [_dot(xs_ref[:, k * 2 * nh:(k + 1) * 2 * nh], cblk_ref[k]) for k in range(N_HALF)]
    return jnp.concatenate(ys, axis=-1)


def _mix_tail(x, y_pool, u, y_state, dskip_ref, wglu_ref, bglu_ref, wout_ref):
    y = jax.nn.gelu(y_state + dskip_ref[...] * u)
    y = y * jax.nn.sigmoid(_dot(y.astype(BF16), wglu_ref[...]) + bglu_ref[...])
    ycat = jnp.concatenate([y_pool.astype(BF16), y.astype(BF16)], axis=-1)
    return x + _dot(ycat, wout_ref[...])


def _mixer_prompt_kernel(x_ref, gmix_ref, win_ref, wpool_ref, pscale_ref, bblk_ref, cblk_ref,
                         pneg_re_ref, pneg_im_ref, ppos_re_ref, ppos_im_ref, a1_re_ref, a1_im_ref,
                         dskip_ref, wglu_ref, bglu_ref, wout_ref,
                         out_ref, hist_ref, hre_ref, him_ref,
                         zbuf, bu_ref, xs_ref, *, T):
    i = pl.program_id(0)
    ts = x_ref.shape[0]
    d_pool = zbuf.shape[1]

    @pl.when(i == 0)
    def _():
        zbuf[0:HIST_ROWS, :] = jnp.zeros((HIST_ROWS, d_pool), F32)
        hre_ref[...] = jnp.zeros(hre_ref.shape, F32)
        him_ref[...] = jnp.zeros(him_ref.shape, F32)

    x = x_ref[...]
    z = _dot(_rmsnorm(x, gmix_ref[...]).astype(BF16), win_ref[...])
    zbuf[HIST_ROWS:HIST_ROWS + ts, :] = z[:, :d_pool]
    u = z[:, d_pool:]

    pos = i * ts + lax.broadcasted_iota(jnp.int32, (ts, 1), 0)
    y_pool = _pool_windows(zbuf, HIST_ROWS, ts, pos, wpool_ref, pscale_ref)

    h_in = (hre_ref[...], him_ref[...])
    y_state = _s5_scan(u, ts // T, T, True, h_in, bblk_ref, cblk_ref,
                       pneg_re_ref, pneg_im_ref, ppos_re_ref, ppos_im_ref,
                       a1_re_ref, a1_im_ref, bu_ref, xs_ref, hre_ref, him_ref)

    out_ref[...] = _mix_tail(x, y_pool, u, y_state, dskip_ref, wglu_ref, bglu_ref, wout_ref)

    tail = zbuf[ts:ts + HIST_ROWS, :]
    zbuf[0:HIST_ROWS, :] = tail
    hist_ref[...] = tail


def _mixer_sample_kernel(x_ref, hist_in_ref, h0_re_ref, h0_im_ref,
                         gmix_ref, win_ref, wpool_ref, pscale_ref, bblk_ref, cblk_ref,
                         pneg_re_ref, pneg_im_ref, ppos_re_ref, ppos_im_ref, a1_re_ref, a1_im_ref,
                         dskip_ref, wglu_ref, bglu_ref, wout_ref,
                         out_ref, hist_ref, hre_ref, him_ref,
                         zbuf, ypool_ref, bu_ref, xs_ref, *, n_streams, T, pos0):
    d_pool = zbuf.shape[1]
    ext = HIST_ROWS + T

    x = x_ref[...]
    z = _dot(_rmsnorm(x, gmix_ref[...]).astype(BF16), win_ref[...])
    u = z[:, d_pool:]
    for b in range(n_streams):
        zbuf[b * ext:b * ext + 1, :] = jnp.zeros((1, d_pool), F32)
        zbuf[b * ext + 1:b * ext + HIST_ROWS, :] = hist_in_ref[b]
        zbuf[b * ext + HIST_ROWS:(b + 1) * ext, :] = z[b * T:(b + 1) * T, :d_pool]
        hist_ref[b] = zbuf[(b + 1) * ext - POOL_HIST:(b + 1) * ext, :]

    nrows = n_streams * ext - HIST_ROWS
    r = lax.broadcasted_iota(jnp.int32, (nrows, 1), 0)
    pos = pos0 + lax.rem(r, ext)
    y_all = _pool_windows(zbuf, HIST_ROWS, nrows, pos, wpool_ref, pscale_ref)
    for b in range(n_streams):
        ypool_ref[b * T:(b + 1) * T, :] = y_all[b * ext:b * ext + T, :]

    y_state = _s5_scan(u, n_streams, T, False, (h0_re_ref, h0_im_ref), bblk_ref, cblk_ref,
                       pneg_re_ref, pneg_im_ref, ppos_re_ref, ppos_im_ref,
                       a1_re_ref, a1_im_ref, bu_ref, xs_ref, hre_ref, him_ref)

    out_ref[...] = _mix_tail(x, ypool_ref[...], u, y_state, dskip_ref, wglu_ref, bglu_ref, wout_ref)


def _const_spec(shape):
    nd = len(shape)
    return pl.BlockSpec(shape, lambda *_: (0,) * nd)


def _weight_spec(shape):
    nd = len(shape)
    return pl.BlockSpec(shape, lambda *_: (0,) * nd, pipeline_mode=pl.Buffered(1))


def _mixer_weights(p):
    return (p["g_mix"], p["w_in"], p["w_pool"], p["pool_scale"], p["bblk"], p["cblk"],
            p["pneg_re"], p["pneg_im"], p["ppos_re"], p["ppos_im"], p["a1_re"], p["a1_im"],
            p["d_skip"], p["w_glu"], p["b_glu"], p["w_out"])


def _mixer_prompt(x, p):
    S, D = x.shape
    ts, T = PROMPT_BLOCK, PROMPT_CHUNK
    d_pool = p["w_pool"].shape[0] * p["w_pool"].shape[-1]
    n_state = p["a1_re"].shape[-1]
    weights = _mixer_weights(p)
    return pl.pallas_call(
        functools.partial(_mixer_prompt_kernel, T=T),
        grid=(S // ts,),
        in_specs=[pl.BlockSpec((ts, D), lambda i: (i, 0))] + [_weight_spec(w.shape) for w in weights],
        out_specs=[pl.BlockSpec((ts, D), lambda i: (i, 0)),
                   _const_spec((HIST_ROWS, d_pool)),
                   _const_spec((1, n_state)), _const_spec((1, n_state))],
        out_shape=[jax.ShapeDtypeStruct((S, D), F32),
                   jax.ShapeDtypeStruct((HIST_ROWS, d_pool), F32),
                   jax.ShapeDtypeStruct((1, n_state), F32),
                   jax.ShapeDtypeStruct((1, n_state), F32)],
        scratch_shapes=[pltpu.VMEM((HIST_ROWS + ts, d_pool), F32),
                        pltpu.VMEM((ts, 2 * n_state), F32),
                        pltpu.VMEM((ts, 2 * n_state), BF16)],
        compiler_params=pltpu.CompilerParams(dimension_semantics=("arbitrary",),
                                             vmem_limit_bytes=VMEM_LIMIT),
        name="mixer_prompt",
    )(x, *weights)


def _mixer_sample(x, hist, h0_re, h0_im, p, pos0):
    n_streams, T, D = x.shape
    rows = n_streams * T
    d_pool = hist.shape[-1]
    n_state = p["a1_re"].shape[-1]
    weights = _mixer_weights(p)
    return pl.pallas_call(
        functools.partial(_mixer_sample_kernel, n_streams=n_streams, T=T, pos0=pos0),
        out_shape=[jax.ShapeDtypeStruct((rows, D), F32),
                   jax.ShapeDtypeStruct((n_streams, POOL_HIST, d_pool), F32),
                   jax.ShapeDtypeStruct((n_streams, n_state), F32),
                   jax.ShapeDtypeStruct((n_streams, n_state), F32)],
        scratch_shapes=[pltpu.VMEM((n_streams * (HIST_ROWS + T), d_pool), F32),
                        pltpu.VMEM((rows, d_pool), F32),
                        pltpu.VMEM((rows, 2 * n_state), F32),
                        pltpu.VMEM((rows, 2 * n_state), BF16)],
        compiler_params=pltpu.CompilerParams(vmem_limit_bytes=VMEM_LIMIT),
        name="mixer_sample",
    )(x.reshape(rows, D), hist, h0_re, h0_im, *weights)


def _mem_kv_kernel(mem_ref, g_ref, wk_ref, wv_ref, k_ref, v_ref):
    m = _rmsnorm(mem_ref[...], g_ref[...]).astype(BF16)
    k_ref[...] = _dot(m, wk_ref[...])
    v_ref[...] = _dot(m, wv_ref[...])


def _mem_kv(mem, g_mem, w_k, w_v):
    n, d = mem.shape
    return pl.pallas_call(
        _mem_kv_kernel,
        out_shape=[jax.ShapeDtypeStruct((n, d), F32), jax.ShapeDtypeStruct((n, d), F32)],
        compiler_params=pltpu.CompilerParams(vmem_limit_bytes=VMEM_LIMIT),
        name="mem_kv",
    )(mem, g_mem, w_k, w_v)


def _attend(q, k, v, scale):
    hd = q.shape[-1] // MEM_HEADS
    outs = []
    for h in range(MEM_HEADS):
        cols = slice(h * hd, (h + 1) * hd)
        s = lax.dot_general(q[:, cols], k[:, cols].astype(BF16), (((1,), (1,)), ((), ())),
                            preferred_element_type=F32) * scale
        e = jnp.exp(s - jnp.max(s, axis=-1, keepdims=True))
        prob = e / jnp.sum(e, axis=-1, keepdims=True)
        outs.append(_dot(prob.astype(BF16), v[:, cols].astype(BF16)))
    return jnp.concatenate(outs, axis=-1)


def _xattn_kernel(x_ref, k_ref, v_ref, g_ref, wq_ref, wo_ref, out_ref, *, scale):
    x = x_ref[...]
    q = _dot(_rmsnorm(x, g_ref[...]).astype(BF16), wq_ref[...]).astype(BF16)
    k = k_ref[0]
    v = v_ref[0]
    o = _attend(q, k, v, scale)
    out_ref[...] = x + _dot(o.astype(BF16), wo_ref[...])


def _xattn(x, k, v, p, rows_per_step):
    rows, D = x.shape
    n_kv, n_mem, _ = k.shape
    kv_map = (lambda i: (i, 0, 0)) if n_kv > 1 else (lambda i: (0, 0, 0))
    scale = (D // MEM_HEADS) ** -0.5
    return pl.pallas_call(
        functools.partial(_xattn_kernel, scale=scale),
        grid=(rows // rows_per_step,),
        in_specs=[pl.BlockSpec((rows_per_step, D), lambda i: (i, 0)),
                  pl.BlockSpec((1, n_mem, D), kv_map),
                  pl.BlockSpec((1, n_mem, D), kv_map),
                  _weight_spec(p["g_xattn"].shape), _weight_spec(p["w_q"].shape), _weight_spec(p["w_o"].shape)],
        out_specs=pl.BlockSpec((rows_per_step, D), lambda i: (i, 0)),
        out_shape=jax.ShapeDtypeStruct((rows, D), F32),
        compiler_params=pltpu.CompilerParams(dimension_semantics=("arbitrary",),
                                             vmem_limit_bytes=VMEM_LIMIT),
        name="xattn",
    )(x, k, v, p["g_xattn"], p["w_q"], p["w_o"])


def _mlp_kernel(x_ref, g_ref, wup_ref, wdown_ref, gfin_ref, out_ref):
    x = x_ref[...]
    hm = _rmsnorm(x, g_ref[...]).astype(BF16)
    acc = x
    for j in range(wup_ref.shape[1] // FF_CHUNK):
        cols = slice(j * FF_CHUNK, (j + 1) * FF_CHUNK)
        up = _dot(hm, wup_ref[:, cols])
        act = jnp.square(jnp.maximum(up, 0.0)).astype(BF16)
        acc = acc + _dot(act, wdown_ref[cols, :])
    out_ref[...] = _rmsnorm(acc, gfin_ref[...])


def _mlp(x, p, rows_per_step):
    rows, D = x.shape
    return pl.pallas_call(
        _mlp_kernel,
        grid=(rows // rows_per_step,),
        in_specs=[pl.BlockSpec((rows_per_step, D), lambda i: (i, 0)),
                  _weight_spec(p["g_mlp"].shape), _weight_spec(p["w_up"].shape),
                  _weight_spec(p["w_down"].shape), _weight_spec(p["g_final"].shape)],
        out_specs=pl.BlockSpec((rows_per_step, D), lambda i: (i, 0)),
        out_shape=jax.ShapeDtypeStruct((rows, D), F32),
        compiler_params=pltpu.CompilerParams(dimension_semantics=("arbitrary",),
                                             vmem_limit_bytes=VMEM_LIMIT),
        name="mlp",
    )(x, p["g_mlp"], p["w_up"], p["w_down"], p["g_final"])


def _block_diag(t):
    nhalf, gi, r, c = t.shape
    eye = jnp.eye(gi, dtype=t.dtype)
    return jnp.einsum("kirc,ij->kirjc", t, eye).reshape(nhalf, gi * r, gi * c)


def _prepare_layer(l, g_mix, w_in, w_pool, pool_scale, a_re, a_im, b_re, b_im, c_re, c_im, d_skip,
                   log_dt, w_glu, b_glu, w_out, g_xattn, g_mem, w_q, w_k, w_v, w_o, g_mlp, w_up,
                   w_down, g_final, t_max):
    G, P = a_re.shape[1:]
    gi = G // N_HALF
    ar, ai = a_re[l].astype(F32), a_im[l].astype(F32)
    dt = jnp.exp(log_dt[l].astype(F32))[:, None]
    lam_re, lam_im = ar * dt, ai * dt

    def a_pow(kk):
        kk = kk.astype(F32)[:, None, None]
        mag = jnp.exp(kk * lam_re)
        ang = kk * lam_im
        return (mag * jnp.cos(ang)).reshape(-1, G * P), (mag * jnp.sin(ang)).reshape(-1, G * P)

    steps = jnp.arange(t_max)
    pneg_re, pneg_im = a_pow(-steps)
    ppos_re, ppos_im = a_pow(steps)
    a1_re, a1_im = a_pow(jnp.ones((1,), F32))

    ab_re, ab_im = a1_re.reshape(G, P), a1_im.reshape(G, P)
    den = ar * ar + ai * ai
    coef_re = ((ab_re - 1.0) * ar + ab_im * ai) / den
    coef_im = (ab_im * ar - (ab_re - 1.0) * ai) / den
    br, bi = b_re[l].astype(F32), b_im[l].astype(F32)
    bb_re = coef_re[..., None] * br - coef_im[..., None] * bi
    bb_im = coef_re[..., None] * bi + coef_im[..., None] * br

    def to_b(t):
        return _block_diag(t.reshape(N_HALF, gi, P, -1).transpose(0, 1, 3, 2))

    def to_c(t):
        return _block_diag(t.reshape(N_HALF, gi, -1, P).transpose(0, 1, 3, 2))

    bblk = jnp.concatenate([to_b(bb_re), to_b(bb_im)], axis=-1).astype(BF16)
    cblk = jnp.concatenate([to_c(c_re[l].astype(F32)), to_c(-c_im[l].astype(F32))], axis=1).astype(BF16)

    row = lambda v: v.astype(F32).reshape(1, -1)
    return dict(
        g_mix=row(g_mix[l]), w_in=w_in[l].astype(BF16), w_pool=w_pool[l].astype(BF16),
        pool_scale=row(pool_scale[l]), bblk=bblk, cblk=cblk,
        pneg_re=pneg_re, pneg_im=pneg_im, ppos_re=ppos_re, ppos_im=ppos_im, a1_re=a1_re, a1_im=a1_im,
        d_skip=row(d_skip[l]), w_glu=w_glu[l].astype(BF16), b_glu=row(b_glu[l]),
        w_out=w_out[l].astype(BF16), g_xattn=row(g_xattn[l]), g_mem=row(g_mem[l]),
        w_q=w_q[l].astype(BF16), w_k=w_k[l].astype(BF16), w_v=w_v[l].astype(BF16),
        w_o=w_o[l].astype(BF16), g_mlp=row(g_mlp[l]), w_up=w_up[l].astype(BF16),
        w_down=w_down[l].astype(BF16), g_final=row(g_final))


def kernel(x_prompt, x_sample, cache_mem_k, cache_mem_v, state_pool, state_ssm_re, state_ssm_im, mem_prompt, g_mix, w_in, w_pool, pool_scale, ssm_a_re, ssm_a_im, ssm_b_re, ssm_b_im, ssm_c_re, ssm_c_im, ssm_d, ssm_log_dt, w_glu, b_glu, w_out, g_xattn, g_mem, w_q, w_k, w_v, w_o, g_mlp, w_up, w_down, g_final):
    depth = g_mix.shape[0]
    assert depth == 1 and x_prompt.shape[0] == 1, "single layer, single prompt stream"
    Bp, S, D = x_prompt.shape
    Bs, Ts, _ = x_sample.shape
    n_mem = mem_prompt.shape[1]
    G, P = ssm_a_re.shape[1:]
    past_len = 1024
    assert S % PROMPT_BLOCK == 0 and PROMPT_BLOCK % PROMPT_CHUNK == 0 and Ts <= PROMPT_CHUNK

    l = 0
    p = _prepare_layer(l, g_mix, w_in, w_pool, pool_scale, ssm_a_re, ssm_a_im, ssm_b_re, ssm_b_im,
                       ssm_c_re, ssm_c_im, ssm_d, ssm_log_dt, w_glu, b_glu, w_out, g_xattn, g_mem,
                       w_q, w_k, w_v, w_o, g_mlp, w_up, w_down, g_final, PROMPT_CHUNK)

    mk, mv = _mem_kv(mem_prompt[0], p["g_mem"], p["w_k"], p["w_v"])
    x1, hist_p, hre_p, him_p = _mixer_prompt(x_prompt[0], p)
    x2 = _xattn(x1, mk[None], mv[None], p, PROMPT_BLOCK)
    y_prompt = _mlp(x2, p, PROMPT_BLOCK)

    xs1, hist_s, hre_s, him_s = _mixer_sample(
        x_sample, state_pool[l], state_ssm_re[l].reshape(Bs, G * P), state_ssm_im[l].reshape(Bs, G * P),
        p, past_len)
    xs2 = _xattn(xs1, cache_mem_k[l].reshape(Bs, n_mem, D), cache_mem_v[l].reshape(Bs, n_mem, D), p, Ts)
    y_sample = _mlp(xs2, p, Bs * Ts)

    hd = D // MEM_HEADS
    return (y_prompt[None], y_sample.reshape(Bs, Ts, D),
            mk.reshape(1, Bp, n_mem, MEM_HEADS, hd), mv.reshape(1, Bp, n_mem, MEM_HEADS, hd),
            hist_p[1:].reshape(1, Bp, POOL_HIST, -1),
            hre_p.reshape(1, Bp, G, P), him_p.reshape(1, Bp, G, P),
            hist_s[None], hre_s.reshape(1, Bs, G, P), him_s.reshape(1, Bs, G, P))
```

```python
import functools

import jax
import jax.numpy as jnp
from jax import lax
from jax.experimental import pallas as pl
from jax.experimental.pallas import tpu as pltpu

F32 = jnp.float32
BF16 = jnp.bfloat16

EPS = 1e-6
PAST_LEN = 1024
POOL_WINDOWS = (2, 4, 8, 16)
POOL_HIST = max(POOL_WINDOWS) - 1
HIST_ROWS = 16
MEM_HEADS = 4
N_HALF = 2

PROMPT_BLOCK = 512
PROMPT_CHUNK = 64
FF_CHUNK = 1024
VMEM_LIMIT = 56 * 1024 * 1024


def _rmsnorm(x, g):
    return x * lax.rsqrt(jnp.mean(x * x, axis=-1, keepdims=True) + EPS) * g


def _dot(a, b):
    return jnp.dot(a.astype(BF16), b.astype(BF16), preferred_element_type=F32)


def _pool_windows(zbuf, nrows, pos, r):
    gw = r["w_pool"].shape[-1]
    outs = []
    for g, w in enumerate(POOL_WINDOWS):
        lanes = slice(g * gw, (g + 1) * gw)
        win = zbuf[pl.ds(0, HIST_ROWS + nrows), lanes]
        k = 1
        while k < w:
            win = win + pltpu.roll(win, k, 0)
            k *= 2
        cur = zbuf[pl.ds(HIST_ROWS, nrows), lanes]
        cnt = jnp.minimum(pos + 1, w).astype(F32)
        pooled = win[HIST_ROWS:, :] / cnt - cur
        outs.append(_dot(pooled, r["w_pool"][g]))
    return jnp.concatenate(outs, axis=-1) * r["pool_scale"][...]


def _cmul(ar, ai, br, bi):
    return ar * br - ai * bi, ar * bi + ai * br


def _s5_scan(u, n_chunks, T, h_carry, h_rows, r, bu_ref, xsc_ref, hout_re_ref, hout_im_ref):
    rows_all = u.shape[0]
    hw = r["bblk"].shape[1]
    nh = r["bblk"].shape[2] // 2
    u_bf = u.astype(BF16)
    for k in range(N_HALF):
        bu_ref[0:rows_all, k * 2 * nh:(k + 1) * 2 * nh] = _dot(u_bf[:, k * hw:(k + 1) * hw], r["bblk"][k])

    def re_cols(k):
        return slice(k * 2 * nh, k * 2 * nh + nh)

    def im_cols(k):
        return slice(k * 2 * nh + nh, (k + 1) * 2 * nh)

    def st_cols(k):
        return slice(k * nh, (k + 1) * nh)

    for c in range(n_chunks):
        rows = pl.ds(c * T, T)
        for k in range(N_HALF):
            xr, xi = _cmul(bu_ref[rows, re_cols(k)], bu_ref[rows, im_cols(k)],
                           r["pneg_re"][0:T, st_cols(k)], r["pneg_im"][0:T, st_cols(k)])
            xsc_ref[rows, re_cols(k)] = xr.astype(BF16)
            xsc_ref[rows, im_cols(k)] = xi.astype(BF16)

    r_i = lax.broadcasted_iota(jnp.int32, (T, T), 0)
    c_i = lax.broadcasted_iota(jnp.int32, (T, T), 1)
    ltri = jnp.where(r_i >= c_i, 1.0, 0.0).astype(BF16)
    for c in range(n_chunks):
        rows = pl.ds(c * T, T)
        bu_ref[rows, :] = _dot(ltri, xsc_ref[rows, :])

    carry = h_carry
    for c in range(n_chunks):
        rows = pl.ds(c * T, T)
        new_re, new_im = [], []
        for k in range(N_HALF):
            if h_rows is None:
                hr, hi = carry[0][:, st_cols(k)], carry[1][:, st_cols(k)]
            else:
                hr = h_rows[0][pl.ds(c, 1), st_cols(k)]
                hi = h_rows[1][pl.ds(c, 1), st_cols(k)]
            cr, ci = _cmul(r["a1_re"][:, st_cols(k)], r["a1_im"][:, st_cols(k)], hr, hi)
            wr = bu_ref[rows, re_cols(k)] + cr
            wi = bu_ref[rows, im_cols(k)] + ci
            sr, si = _cmul(r["ppos_re"][0:T, st_cols(k)], r["ppos_im"][0:T, st_cols(k)], wr, wi)
            xsc_ref[rows, re_cols(k)] = sr.astype(BF16)
            xsc_ref[rows, im_cols(k)] = si.astype(BF16)
            new_re.append(sr[T - 1:T, :])
            new_im.append(si[T - 1:T, :])
        end = (jnp.concatenate(new_re, axis=-1), jnp.concatenate(new_im, axis=-1))
        if h_rows is None:
            carry = end
        else:
            hout_re_ref[pl.ds(c, 1), :] = end[0]
            hout_im_ref[pl.ds(c, 1), :] = end[1]
    if h_rows is None:
        hout_re_ref[...] = carry[0]
        hout_im_ref[...] = carry[1]

    ys = [_dot(xsc_ref[0:rows_all, k * 2 * nh:(k + 1) * 2 * nh], r["cblk"][k]) for k in range(N_HALF)]
    return jnp.concatenate(ys, axis=-1)


def _mix_tail(x, y_pool, u, y_state, r):
    y = jax.nn.gelu(y_state + r["d_skip"][...] * u)
    y = y * jax.nn.sigmoid(_dot(y, r["w_glu"][...]) + r["b_glu"][...])
    ycat = jnp.concatenate([y_pool.astype(BF16), y.astype(BF16)], axis=-1)
    return x + _dot(ycat, r["w_out"][...])


_MIXER_IN = ("xp", "xs", "hist_in", "h0_re", "h0_im",
             "g_mix", "w_in", "w_pool", "pool_scale", "bblk", "cblk",
             "pneg_re", "pneg_im", "ppos_re", "ppos_im", "a1_re", "a1_im",
             "d_skip", "w_glu", "b_glu", "w_out")
_MIXER_OUT = ("outp", "outs", "histp", "hrep", "himp", "hists", "hres", "hims")
_MIXER_BF16 = ("w_in", "w_pool", "w_glu", "w_out")


def _mixer_kernel(*refs, n_prompt, n_streams, Tp, Ts):
    r = dict(zip(_MIXER_IN + _MIXER_OUT, refs))
    zbuf, ypool_ref, bu_ref, xsc_ref = refs[len(_MIXER_IN) + len(_MIXER_OUT):][:4]
    bf16_weights = dict(zip(_MIXER_BF16, refs[len(_MIXER_IN) + len(_MIXER_OUT) + 4:]))
    i = pl.program_id(0)
    ts = r["xp"].shape[0]
    d_pool = zbuf.shape[1]

    @pl.when(i == 0)
    def _():
        zbuf[0:HIST_ROWS, :] = jnp.zeros((HIST_ROWS, d_pool), F32)
        r["hrep"][...] = jnp.zeros(r["hrep"].shape, F32)
        r["himp"][...] = jnp.zeros(r["himp"].shape, F32)
        for name, ref in bf16_weights.items():
            ref[...] = r[name][...].astype(BF16)

    r.update(bf16_weights)

    @pl.when(i < n_prompt)
    def _():
        x = r["xp"][...]
        z = _dot(_rmsnorm(x, r["g_mix"][...]), r["w_in"][...])
        zbuf[HIST_ROWS:HIST_ROWS + ts, :] = z[:, :d_pool]
        u = z[:, d_pool:]
        pos = i * ts + lax.broadcasted_iota(jnp.int32, (ts, 1), 0)
        y_pool = _pool_windows(zbuf, ts, pos, r)
        y_state = _s5_scan(u, ts // Tp, Tp, (r["hrep"][...], r["himp"][...]), None, r,
                           bu_ref, xsc_ref, r["hrep"], r["himp"])
        r["outp"][...] = _mix_tail(x, y_pool, u, y_state, r)
        tail = zbuf[ts:ts + HIST_ROWS, :]
        zbuf[0:HIST_ROWS, :] = tail
        r["histp"][...] = tail

    @pl.when(i == n_prompt)
    def _():
        ext = HIST_ROWS + Ts
        x = r["xs"][...]
        z = _dot(_rmsnorm(x, r["g_mix"][...]), r["w_in"][...])
        u = z[:, d_pool:]
        for b in range(n_streams):
            zbuf[b * ext:b * ext + 1, :] = jnp.zeros((1, d_pool), F32)
            zbuf[b * ext + 1:b * ext + HIST_ROWS, :] = r["hist_in"][b]
            zbuf[b * ext + HIST_ROWS:(b + 1) * ext, :] = z[b * Ts:(b + 1) * Ts, :d_pool]
            r["hists"][b] = zbuf[(b + 1) * ext - POOL_HIST:(b + 1) * ext, :]
        nrows = n_streams * ext - HIST_ROWS
        ridx = lax.broadcasted_iota(jnp.int32, (nrows, 1), 0)
        pos = PAST_LEN + lax.rem(ridx, ext)
        y_all = _pool_windows(zbuf, nrows, pos, r)
        for b in range(n_streams):
            ypool_ref[b * Ts:(b + 1) * Ts, :] = y_all[b * ext:b * ext + Ts, :]
        y_state = _s5_scan(u, n_streams, Ts, None, (r["h0_re"], r["h0_im"]), r,
                           bu_ref, xsc_ref, r["hres"], r["hims"])
        r["outs"][...] = _mix_tail(x, ypool_ref[...], u, y_state, r)


def _const_spec(shape):
    nd = len(shape)
    return pl.BlockSpec(shape, lambda *_: (0,) * nd)


def _weight_spec(shape):
    nd = len(shape)
    return pl.BlockSpec(shape, lambda *_: (0,) * nd, pipeline_mode=pl.Buffered(1))


def _prompt_spec(ts, d, n_prompt):
    return pl.BlockSpec((ts, d), lambda i: (jnp.minimum(i, n_prompt - 1), 0))


def _mixer(xp, xs, hist_in, h0_re, h0_im, p, n_streams, Ts):
    S, D = xp.shape
    rows_s = xs.shape[0]
    ts = PROMPT_BLOCK
    n_prompt = S // ts
    d_pool = hist_in.shape[-1]
    n_state = p["a1_re"].shape[-1]
    assert n_streams * (HIST_ROWS + Ts) <= HIST_ROWS + ts and rows_s <= ts
    args = dict(p, xp=xp, xs=xs, hist_in=hist_in, h0_re=h0_re, h0_im=h0_im)
    ins = [args[n] for n in _MIXER_IN]
    in_specs = [_prompt_spec(ts, D, n_prompt)] + [_weight_spec(a.shape) for a in ins[1:]]
    out_shapes = dict(
        outp=(S, D), outs=(rows_s, D), histp=(HIST_ROWS, d_pool), hrep=(1, n_state), himp=(1, n_state),
        hists=(n_streams, POOL_HIST, d_pool), hres=(n_streams, n_state), hims=(n_streams, n_state))
    out_specs = [_prompt_spec(ts, D, n_prompt)] + [_const_spec(out_shapes[n]) for n in _MIXER_OUT[1:]]
    return pl.pallas_call(
        functools.partial(_mixer_kernel, n_prompt=n_prompt, n_streams=n_streams, Tp=PROMPT_CHUNK, Ts=Ts),
        grid=(n_prompt + 1,),
        in_specs=in_specs,
        out_specs=out_specs,
        out_shape=[jax.ShapeDtypeStruct(out_shapes[n], F32) for n in _MIXER_OUT],
        scratch_shapes=[pltpu.VMEM((HIST_ROWS + ts, d_pool), F32),
                        pltpu.VMEM((rows_s, d_pool), F32),
                        pltpu.VMEM((ts, 2 * n_state), F32),
                        pltpu.VMEM((ts, 2 * n_state), BF16)]
        + [pltpu.VMEM(args[n].shape, BF16) for n in _MIXER_BF16],
        compiler_params=pltpu.CompilerParams(dimension_semantics=("arbitrary",),
                                             vmem_limit_bytes=VMEM_LIMIT),
        name="mixer",
    )(*ins)


def _softmax_rows(s):
    e = jnp.exp(s - jnp.max(s, axis=-1, keepdims=True))
    return e / jnp.sum(e, axis=-1, keepdims=True)


def _attend(q, head_k, head_v, scale):
    hd = q.shape[-1] // MEM_HEADS
    outs = []
    for h in range(MEM_HEADS):
        s = lax.dot_general(q[:, h * hd:(h + 1) * hd], head_k(h).astype(BF16), (((1,), (1,)), ((), ())),
                            preferred_element_type=F32) * scale
        outs.append(_dot(_softmax_rows(s), head_v(h)))
    return jnp.concatenate(outs, axis=-1)


_XATTN_IN = ("xp", "xs", "mem", "kc", "vc", "g_mem", "w_k", "w_v", "g_xattn", "w_q", "w_o")
_XATTN_OUT = ("outp", "outs", "mk", "mv")


def _xattn_kernel(*refs, n_prompt, n_streams, Ts, scale):
    r = dict(zip(_XATTN_IN + _XATTN_OUT, refs))
    kp_ref, vp_ref, qs_ref, os_ref = refs[len(_XATTN_IN) + len(_XATTN_OUT):]
    i = pl.program_id(0)
    hd = kp_ref.shape[-1] // MEM_HEADS

    @pl.when(i == 0)
    def _():
        m = _rmsnorm(r["mem"][...], r["g_mem"][...]).astype(BF16)
        k = _dot(m, r["w_k"][...])
        v = _dot(m, r["w_v"][...])
        r["mk"][...] = k
        r["mv"][...] = v
        kp_ref[...] = k.astype(BF16)
        vp_ref[...] = v.astype(BF16)

    @pl.when(i < n_prompt)
    def _():
        x = r["xp"][...]
        q = _dot(_rmsnorm(x, r["g_xattn"][...]), r["w_q"][...]).astype(BF16)
        o = _attend(q, lambda h: kp_ref[:, h * hd:(h + 1) * hd], lambda h: vp_ref[:, h * hd:(h + 1) * hd], scale)
        r["outp"][...] = x + _dot(o, r["w_o"][...])

    @pl.when(i == n_prompt)
    def _():
        qs_ref[...] = _dot(_rmsnorm(r["xs"][...], r["g_xattn"][...]), r["w_q"][...]).astype(BF16)

    @pl.when(i >= n_prompt)
    def _():
        rows = pl.ds(pl.multiple_of((i - n_prompt) * Ts, Ts), Ts)
        os_ref[rows, :] = _attend(qs_ref[rows, :], lambda h: r["kc"][0, :, h * hd:(h + 1) * hd],
                                  lambda h: r["vc"][0, :, h * hd:(h + 1) * hd], scale)

    @pl.when(i == n_prompt + n_streams - 1)
    def _():
        r["outs"][...] = r["xs"][...] + _dot(os_ref[...], r["w_o"][...])


def _xattn(xp, xs, mem, kc, vc, p, n_streams, Ts):
    S, D = xp.shape
    rows_s = xs.shape[0]
    ts = PROMPT_BLOCK
    n_prompt = S // ts
    n_mem = mem.shape[0]
    hd = D // MEM_HEADS
    args = dict(p, xp=xp, xs=xs, mem=mem, kc=kc, vc=vc)
    ins = [args[n] for n in _XATTN_IN]
    cache_spec = pl.BlockSpec((1, n_mem, D), lambda i: (jnp.maximum(i - n_prompt, 0), 0, 0))
    in_specs = [_prompt_spec(ts, D, n_prompt)]
    for n in _XATTN_IN[1:]:
        in_specs.append(cache_spec if n in ("kc", "vc") else _weight_spec(args[n].shape))
    out_shapes = dict(outp=(S, D), outs=(rows_s, D), mk=(n_mem, D), mv=(n_mem, D))
    out_specs = [_prompt_spec(ts, D, n_prompt)] + [_const_spec(out_shapes[n]) for n in _XATTN_OUT[1:]]
    return pl.pallas_call(
        functools.partial(_xattn_kernel, n_prompt=n_prompt, n_streams=n_streams, Ts=Ts, scale=hd ** -0.5),
        grid=(n_prompt + n_streams,),
        in_specs=in_specs,
        out_specs=out_specs,
        out_shape=[jax.ShapeDtypeStruct(out_shapes[n], F32) for n in _XATTN_OUT],
        scratch_shapes=[pltpu.VMEM((n_mem, D), BF16), pltpu.VMEM((n_mem, D), BF16),
                        pltpu.VMEM((rows_s, D), BF16), pltpu.VMEM((rows_s, D), F32)],
        compiler_params=pltpu.CompilerParams(dimension_semantics=("arbitrary",),
                                             vmem_limit_bytes=VMEM_LIMIT),
        name="xattn",
    )(*ins)


def _mlp_rows(x, g_ref, wup_ref, wdown_ref, gfin_ref):
    hm = _rmsnorm(x, g_ref[...]).astype(BF16)
    acc = x
    for j in range(wup_ref.shape[1] // FF_CHUNK):
        cols = slice(j * FF_CHUNK, (j + 1) * FF_CHUNK)
        up = _dot(hm, wup_ref[:, cols])
        acc = acc + _dot(jnp.square(jnp.maximum(up, 0.0)), wdown_ref[cols, :])
    return _rmsnorm(acc, gfin_ref[...])


def _mlp_kernel(xp_ref, xs_ref, g_ref, wup_ref, wdown_ref, gfin_ref, outp_ref, outs_ref, *, n_prompt):
    i = pl.program_id(0)

    @pl.when(i < n_prompt)
    def _():
        outp_ref[...] = _mlp_rows(xp_ref[...], g_ref, wup_ref, wdown_ref, gfin_ref)

    @pl.when(i == n_prompt)
    def _():
        outs_ref[...] = _mlp_rows(xs_ref[...], g_ref, wup_ref, wdown_ref, gfin_ref)


def _mlp(xp, xs, p):
    S, D = xp.shape
    ts = PROMPT_BLOCK
    n_prompt = S // ts
    weights = (p["g_mlp"], p["w_up"], p["w_down"], p["g_final"])
    return pl.pallas_call(
        functools.partial(_mlp_kernel, n_prompt=n_prompt),
        grid=(n_prompt + 1,),
        in_specs=[_prompt_spec(ts, D, n_prompt), _weight_spec(xs.shape)] + [_weight_spec(w.shape) for w in weights],
        out_specs=[_prompt_spec(ts, D, n_prompt), _const_spec(xs.shape)],
        out_shape=[jax.ShapeDtypeStruct(xp.shape, F32), jax.ShapeDtypeStruct(xs.shape, F32)],
        compiler_params=pltpu.CompilerParams(dimension_semantics=("arbitrary",),
                                             vmem_limit_bytes=VMEM_LIMIT),
        name="mlp",
    )(xp, xs, *weights)


def _block_diag(t):
    nhalf, gi, r, c = t.shape
    eye = jnp.eye(gi, dtype=t.dtype)
    return jnp.einsum("kirc,ij->kirjc", t, eye).reshape(nhalf, gi * r, gi * c)


def _prepare_layer(l, g_mix, w_in, w_pool, pool_scale, a_re, a_im, b_re, b_im, c_re, c_im, d_skip,
                   log_dt, w_glu, b_glu, w_out, g_xattn, g_mem, w_q, w_k, w_v, w_o, g_mlp, w_up,
                   w_down, g_final, t_max):
    G, P = a_re.shape[1:]
    gi = G // N_HALF
    ar, ai = a_re[l].astype(F32), a_im[l].astype(F32)
    dt = jnp.exp(log_dt[l].astype(F32))[:, None]
    lam_re, lam_im = ar * dt, ai * dt

    def a_pow(kk):
        kk = kk.astype(F32)[:, None, None]
        mag = jnp.exp(kk * lam_re)
        ang = kk * lam_im
        return (mag * jnp.cos(ang)).reshape(-1, G * P), (mag * jnp.sin(ang)).reshape(-1, G * P)

    steps = jnp.arange(t_max)
    pneg_re, pneg_im = a_pow(-steps)
    ppos_re, ppos_im = a_pow(steps)
    a1_re, a1_im = a_pow(jnp.ones((1,), F32))

    ab_re, ab_im = a1_re.reshape(G, P), a1_im.reshape(G, P)
    den = ar * ar + ai * ai
    coef_re = ((ab_re - 1.0) * ar + ab_im * ai) / den
    coef_im = (ab_im * ar - (ab_re - 1.0) * ai) / den
    br, bi = b_re[l].astype(F32), b_im[l].astype(F32)
    bb_re = coef_re[..., None] * br - coef_im[..., None] * bi
    bb_im = coef_re[..., None] * bi + coef_im[..., None] * br

    def to_b(t):
        return _block_diag(t.reshape(N_HALF, gi, P, -1).transpose(0, 1, 3, 2))

    def to_c(t):
        return _block_diag(t.reshape(N_HALF, gi, -1, P).transpose(0, 1, 3, 2))

    bblk = jnp.concatenate([to_b(bb_re), to_b(bb_im)], axis=-1).astype(BF16)
    cblk = jnp.concatenate([to_c(c_re[l].astype(F32)), to_c(-c_im[l].astype(F32))], axis=1).astype(BF16)

    row = lambda v: v.astype(F32).reshape(1, -1)
    return dict(
        g_mix=row(g_mix[l]), w_in=w_in[l], w_pool=w_pool[l], pool_scale=row(pool_scale[l]),
        bblk=bblk, cblk=cblk,
        pneg_re=pneg_re, pneg_im=pneg_im, ppos_re=ppos_re, ppos_im=ppos_im, a1_re=a1_re, a1_im=a1_im,
        d_skip=row(d_skip[l]), w_glu=w_glu[l], b_glu=row(b_glu[l]), w_out=w_out[l],
        g_xattn=row(g_xattn[l]), g_mem=row(g_mem[l]), w_q=w_q[l], w_k=w_k[l], w_v=w_v[l], w_o=w_o[l],
        g_mlp=row(g_mlp[l]), w_up=w_up[l], w_down=w_down[l], g_final=row(g_final))


def kernel(x_prompt, x_sample, cache_mem_k, cache_mem_v, state_pool, state_ssm_re, state_ssm_im, mem_prompt, g_mix, w_in, w_pool, pool_scale, ssm_a_re, ssm_a_im, ssm_b_re, ssm_b_im, ssm_c_re, ssm_c_im, ssm_d, ssm_log_dt, w_glu, b_glu, w_out, g_xattn, g_mem, w_q, w_k, w_v, w_o, g_mlp, w_up, w_down, g_final):
    depth = g_mix.shape[0]
    assert depth == 1 and x_prompt.shape[0] == 1, "single layer, single prompt stream"
    Bp, S, D = x_prompt.shape
    Bs, Ts, _ = x_sample.shape
    n_mem = mem_prompt.shape[1]
    G, P = ssm_a_re.shape[1:]
    assert S % PROMPT_BLOCK == 0 and PROMPT_BLOCK % PROMPT_CHUNK == 0 and Ts <= PROMPT_CHUNK

    l = 0
    p = _prepare_layer(l, g_mix, w_in, w_pool, pool_scale, ssm_a_re, ssm_a_im, ssm_b_re, ssm_b_im,
                       ssm_c_re, ssm_c_im, ssm_d, ssm_log_dt, w_glu, b_glu, w_out, g_xattn, g_mem,
                       w_q, w_k, w_v, w_o, g_mlp, w_up, w_down, g_final, PROMPT_CHUNK)

    xp1, xs1, hist_p, hre_p, him_p, hist_s, hre_s, him_s = _mixer(
        x_prompt[0], x_sample.reshape(Bs * Ts, D), state_pool[l],
        state_ssm_re[l].reshape(Bs, G * P), state_ssm_im[l].reshape(Bs, G * P), p, Bs, Ts)
    xp2, xs2, mk, mv = _xattn(xp1, xs1, mem_prompt[0],
                              cache_mem_k[l].reshape(Bs, n_mem, D).astype(BF16),
                              cache_mem_v[l].reshape(Bs, n_mem, D).astype(BF16), p, Bs, Ts)
    y_prompt, y_sample = _mlp(xp2, xs2, p)

    hd = D // MEM_HEADS
    return (y_prompt[None], y_sample.reshape(Bs, Ts, D),
            mk.reshape(1, Bp, n_mem, MEM_HEADS, hd), mv.reshape(1, Bp, n_mem, MEM_HEADS, hd),
            hist_p[1:].reshape(1, Bp, POOL_HIST, -1),
            hre_p.reshape(1, Bp, G, P), him_p.reshape(1, Bp, G, P),
            hist_s[None], hre_s.reshape(1, Bs, G, P), him_s.reshape(1, Bs, G, P))
```

```python
import functools

import jax
import jax.numpy as jnp
from jax import lax
from jax.experimental import pallas as pl
from jax.experimental.pallas import tpu as pltpu

F32 = jnp.float32
BF16 = jnp.bfloat16

EPS = 1e-6
PAST_LEN = 1024
POOL_WINDOWS = (2, 4, 8, 16)
POOL_HIST = max(POOL_WINDOWS) - 1
HIST_ROWS = 16
MEM_HEADS = 4
N_HALF = 2

PROMPT_BLOCK = 512
PROMPT_CHUNK = 64
FF_CHUNK = 1024
VMEM_LIMIT = 56 * 1024 * 1024


def _rmsnorm(x, g):
    return x * lax.rsqrt(jnp.mean(x * x, axis=-1, keepdims=True) + EPS) * g


def _dot(a, b):
    return jnp.dot(a.astype(BF16), b.astype(BF16), preferred_element_type=F32)


def _pool_windows(zbuf, nrows, pos, r):
    gw = r["w_pool"].shape[-1]
    outs = []
    for g, w in enumerate(POOL_WINDOWS):
        lanes = slice(g * gw, (g + 1) * gw)
        win = zbuf[pl.ds(0, HIST_ROWS + nrows), lanes]
        k = 1
        while k < w:
            win = win + pltpu.roll(win, k, 0)
            k *= 2
        cur = zbuf[pl.ds(HIST_ROWS, nrows), lanes]
        cnt = jnp.minimum(pos + 1, w).astype(F32)
        pooled = win[HIST_ROWS:, :] / cnt - cur
        outs.append(_dot(pooled, r["w_pool"][g]))
    return jnp.concatenate(outs, axis=-1) * r["pool_scale"][...]


def _cmul(ar, ai, br, bi):
    return ar * br - ai * bi, ar * bi + ai * br


def _s5_scan(u, n_chunks, T, h_carry, h_rows, r, bu_ref, xsc_ref, hout_re_ref, hout_im_ref):
    rows_all = u.shape[0]
    hw = r["bblk"].shape[1]
    nh = r["bblk"].shape[2] // 2
    u_bf = u.astype(BF16)
    for k in range(N_HALF):
        bu_ref[0:rows_all, k * 2 * nh:(k + 1) * 2 * nh] = _dot(u_bf[:, k * hw:(k + 1) * hw], r["bblk"][k])

    def re_cols(k):
        return slice(k * 2 * nh, k * 2 * nh + nh)

    def im_cols(k):
        return slice(k * 2 * nh + nh, (k + 1) * 2 * nh)

    def st_cols(k):
        return slice(k * nh, (k + 1) * nh)

    for c in range(n_chunks):
        rows = pl.ds(c * T, T)
        for k in range(N_HALF):
            xr, xi = _cmul(bu_ref[rows, re_cols(k)], bu_ref[rows, im_cols(k)],
                           r["pneg_re"][0:T, st_cols(k)], r["pneg_im"][0:T, st_cols(k)])
            xsc_ref[rows, re_cols(k)] = xr.astype(BF16)
            xsc_ref[rows, im_cols(k)] = xi.astype(BF16)

    r_i = lax.broadcasted_iota(jnp.int32, (T, T), 0)
    c_i = lax.broadcasted_iota(jnp.int32, (T, T), 1)
    ltri = jnp.where(r_i >= c_i, 1.0, 0.0).astype(BF16)
    for c in range(n_chunks):
        rows = pl.ds(c * T, T)
        bu_ref[rows, :] = _dot(ltri, xsc_ref[rows, :])

    carry = h_carry
    for c in range(n_chunks):
        rows = pl.ds(c * T, T)
        new_re, new_im = [], []
        for k in range(N_HALF):
            if h_rows is None:
                hr, hi = carry[0][:, st_cols(k)], carry[1][:, st_cols(k)]
            else:
                hr = h_rows[0][pl.ds(c, 1), st_cols(k)]
                hi = h_rows[1][pl.ds(c, 1), st_cols(k)]
            cr, ci = _cmul(r["a1_re"][:, st_cols(k)], r["a1_im"][:, st_cols(k)], hr, hi)
            wr = bu_ref[rows, re_cols(k)] + cr
            wi = bu_ref[rows, im_cols(k)] + ci
            sr, si = _cmul(r["ppos_re"][0:T, st_cols(k)], r["ppos_im"][0:T, st_cols(k)], wr, wi)
            xsc_ref[rows, re_cols(k)] = sr.astype(BF16)
            xsc_ref[rows, im_cols(k)] = si.astype(BF16)
            new_re.append(sr[T - 1:T, :])
            new_im.append(si[T - 1:T, :])
        end = (jnp.concatenate(new_re, axis=-1), jnp.concatenate(new_im, axis=-1))
        if h_rows is None:
            carry = end
        else:
            hout_re_ref[pl.ds(c, 1), :] = end[0]
            hout_im_ref[pl.ds(c, 1), :] = end[1]
    if h_rows is None:
        hout_re_ref[...] = carry[0]
        hout_im_ref[...] = carry[1]

    ys = [_dot(xsc_ref[0:rows_all, k * 2 * nh:(k + 1) * 2 * nh], r["cblk"][k]) for k in range(N_HALF)]
    return jnp.concatenate(ys, axis=-1)


def _mix_tail(x, y_pool, u, y_state, r):
    y = jax.nn.gelu(y_state + r["d_skip"][...] * u)
    y = y * jax.nn.sigmoid(_dot(y, r["w_glu"][...]) + r["b_glu"][...])
    ycat = jnp.concatenate([y_pool.astype(BF16), y.astype(BF16)], axis=-1)
    return x + _dot(ycat, r["w_out"][...])


_MIXER_IN = ("xp", "xs", "hist_in", "h0_re", "h0_im",
             "g_mix", "w_in", "w_pool", "pool_scale", "bblk", "cblk",
             "pneg_re", "pneg_im", "ppos_re", "ppos_im", "a1_re", "a1_im",
             "d_skip", "w_glu", "b_glu", "w_out")
_MIXER_OUT = ("outp", "outs", "histp", "hrep", "himp", "hists", "hres", "hims")
_MIXER_BF16 = ("w_in", "w_pool", "w_glu", "w_out")


def _mixer_kernel(*refs, n_prompt, n_streams, Tp, Ts):
    r = dict(zip(_MIXER_IN + _MIXER_OUT, refs))
    zbuf, ypool_ref, bu_ref, xsc_ref = refs[len(_MIXER_IN) + len(_MIXER_OUT):][:4]
    bf16_weights = dict(zip(_MIXER_BF16, refs[len(_MIXER_IN) + len(_MIXER_OUT) + 4:]))
    i = pl.program_id(0)
    ts = r["xp"].shape[0]
    d_pool = zbuf.shape[1]

    @pl.when(i == 0)
    def _():
        zbuf[0:HIST_ROWS, :] = jnp.zeros((HIST_ROWS, d_pool), F32)
        r["hrep"][...] = jnp.zeros(r["hrep"].shape, F32)
        r["himp"][...] = jnp.zeros(r["himp"].shape, F32)
        for name, ref in bf16_weights.items():
            ref[...] = r[name][...].astype(BF16)

    r.update(bf16_weights)

    @pl.when(i < n_prompt)
    def _():
        x = r["xp"][...]
        z = _dot(_rmsnorm(x, r["g_mix"][...]), r["w_in"][...])
        zbuf[HIST_ROWS:HIST_ROWS + ts, :] = z[:, :d_pool]
        u = z[:, d_pool:]
        pos = i * ts + lax.broadcasted_iota(jnp.int32, (ts, 1), 0)
        y_pool = _pool_windows(zbuf, ts, pos, r)
        y_state = _s5_scan(u, ts // Tp, Tp, (r["hrep"][...], r["himp"][...]), None, r,
                           bu_ref, xsc_ref, r["hrep"], r["himp"])
        r["outp"][...] = _mix_tail(x, y_pool, u, y_state, r)
        tail = zbuf[ts:ts + HIST_ROWS, :]
        zbuf[0:HIST_ROWS, :] = tail
        r["histp"][...] = tail

    @pl.when(i == n_prompt)
    def _():
        ext = HIST_ROWS + Ts
        x = r["xs"][...]
        z = _dot(_rmsnorm(x, r["g_mix"][...]), r["w_in"][...])
        u = z[:, d_pool:]
        for b in range(n_streams):
            zbuf[b * ext:b * ext + 1, :] = jnp.zeros((1, d_pool), F32)
            zbuf[b * ext + 1:b * ext + HIST_ROWS, :] = r["hist_in"][b]
            zbuf[b * ext + HIST_ROWS:(b + 1) * ext, :] = z[b * Ts:(b + 1) * Ts, :d_pool]
            r["hists"][b] = zbuf[(b + 1) * ext - POOL_HIST:(b + 1) * ext, :]
        nrows = n_streams * ext - HIST_ROWS
        ridx = lax.broadcasted_iota(jnp.int32, (nrows, 1), 0)
        pos = PAST_LEN + lax.rem(ridx, ext)
        y_all = _pool_windows(zbuf, nrows, pos, r)
        for b in range(n_streams):
            ypool_ref[b * Ts:(b + 1) * Ts, :] = y_all[b * ext:b * ext + Ts, :]
        y_state = _s5_scan(u, n_streams, Ts, None, (r["h0_re"], r["h0_im"]), r,
                           bu_ref, xsc_ref, r["hres"], r["hims"])
        r["outs"][...] = _mix_tail(x, ypool_ref[...], u, y_state, r)


def _const_spec(shape):
    nd = len(shape)
    return pl.BlockSpec(shape, lambda *_: (0,) * nd)


def _weight_spec(shape):
    nd = len(shape)
    return pl.BlockSpec(shape, lambda *_: (0,) * nd, pipeline_mode=pl.Buffered(1))


def _prompt_spec(ts, d, n_prompt):
    return pl.BlockSpec((ts, d), lambda i: (jnp.minimum(i, n_prompt - 1), 0))


def _mixer(xp, xs, hist_in, h0_re, h0_im, p, n_streams, Ts):
    S, D = xp.shape
    rows_s = xs.shape[0]
    ts = PROMPT_BLOCK
    n_prompt = S // ts
    d_pool = hist_in.shape[-1]
    n_state = p["a1_re"].shape[-1]
    assert n_streams * (HIST_ROWS + Ts) <= HIST_ROWS + ts and rows_s <= ts
    args = dict(p, xp=xp, xs=xs, hist_in=hist_in, h0_re=h0_re, h0_im=h0_im)
    ins = [args[n] for n in _MIXER_IN]
    in_specs = [_prompt_spec(ts, D, n_prompt)] + [_weight_spec(a.shape) for a in ins[1:]]
    out_shapes = dict(
        outp=(S, D), outs=(rows_s, D), histp=(HIST_ROWS, d_pool), hrep=(1, n_state), himp=(1, n_state),
        hists=(n_streams, POOL_HIST, d_pool), hres=(n_streams, n_state), hims=(n_streams, n_state))
    out_specs = [_prompt_spec(ts, D, n_prompt)] + [_const_spec(out_shapes[n]) for n in _MIXER_OUT[1:]]
    return pl.pallas_call(
        functools.partial(_mixer_kernel, n_prompt=n_prompt, n_streams=n_streams, Tp=PROMPT_CHUNK, Ts=Ts),
        grid=(n_prompt + 1,),
        in_specs=in_specs,
        out_specs=out_specs,
        out_shape=[jax.ShapeDtypeStruct(out_shapes[n], F32) for n in _MIXER_OUT],
        scratch_shapes=[pltpu.VMEM((HIST_ROWS + ts, d_pool), F32),
                        pltpu.VMEM((rows_s, d_pool), F32),
                        pltpu.VMEM((ts, 2 * n_state), F32),
                        pltpu.VMEM((ts, 2 * n_state), BF16)]
        + [pltpu.VMEM(args[n].shape, BF16) for n in _MIXER_BF16],
        compiler_params=pltpu.CompilerParams(dimension_semantics=("arbitrary",),
                                             vmem_limit_bytes=VMEM_LIMIT),
        name="mixer",
    )(*ins)


def _softmax_rows(s):
    e = jnp.exp(s - jnp.max(s, axis=-1, keepdims=True))
    return e / jnp.sum(e, axis=-1, keepdims=True)


def _attend(q, head_k, head_v, scale):
    hd = q.shape[-1] // MEM_HEADS
    outs = []
    for h in range(MEM_HEADS):
        s = lax.dot_general(q[:, h * hd:(h + 1) * hd], head_k(h).astype(BF16), (((1,), (1,)), ((), ())),
                            preferred_element_type=F32) * scale
        outs.append(_dot(_softmax_rows(s), head_v(h)))
    return jnp.concatenate(outs, axis=-1)


_XATTN_IN = ("xp", "xs", "mem", "kc", "vc", "g_mem", "w_k", "w_v", "g_xattn", "w_q", "w_o")
_XATTN_OUT = ("outp", "outs", "mk", "mv")


def _xattn_kernel(*refs, n_prompt, n_streams, Ts, scale):
    r = dict(zip(_XATTN_IN + _XATTN_OUT, refs))
    kp_ref, vp_ref, qs_ref, os_ref, kbuf, vbuf, sem = refs[len(_XATTN_IN) + len(_XATTN_OUT):]
    i = pl.program_id(0)
    hd = kp_ref.shape[-1] // MEM_HEADS

    def cache_copies(b):
        slot = lax.rem(b, 2)
        return [pltpu.make_async_copy(src.at[0, b, :, h, :], dst.at[slot, h], sem.at[slot, j * MEM_HEADS + h])
                for j, (src, dst) in enumerate(((r["kc"], kbuf), (r["vc"], vbuf))) for h in range(MEM_HEADS)]

    @pl.when(i == 0)
    def _():
        m = _rmsnorm(r["mem"][...], r["g_mem"][...]).astype(BF16)
        k = _dot(m, r["w_k"][...])
        v = _dot(m, r["w_v"][...])
        r["mk"][...] = k
        r["mv"][...] = v
        kp_ref[...] = k.astype(BF16)
        vp_ref[...] = v.astype(BF16)

    @pl.when(i < n_prompt)
    def _():
        x = r["xp"][...]
        q = _dot(_rmsnorm(x, r["g_xattn"][...]), r["w_q"][...]).astype(BF16)
        o = _attend(q, lambda h: kp_ref[:, h * hd:(h + 1) * hd], lambda h: vp_ref[:, h * hd:(h + 1) * hd], scale)
        r["outp"][...] = x + _dot(o, r["w_o"][...])

    @pl.when(i == n_prompt - 1)
    def _():
        for cp in cache_copies(0):
            cp.start()

    @pl.when(i == n_prompt)
    def _():
        qs_ref[...] = _dot(_rmsnorm(r["xs"][...], r["g_xattn"][...]), r["w_q"][...]).astype(BF16)

    @pl.when(i >= n_prompt)
    def _():
        b = i - n_prompt
        slot = lax.rem(b, 2)

        @pl.when(b + 1 < n_streams)
        def _():
            for cp in cache_copies(b + 1):
                cp.start()

        for cp in cache_copies(b):
            cp.wait()
        rows = pl.ds(pl.multiple_of(b * Ts, Ts), Ts)
        os_ref[rows, :] = _attend(qs_ref[rows, :], lambda h: kbuf[slot, h], lambda h: vbuf[slot, h], scale)

    @pl.when(i == n_prompt + n_streams - 1)
    def _():
        r["outs"][...] = r["xs"][...] + _dot(os_ref[...], r["w_o"][...])


def _xattn(xp, xs, mem, kc, vc, p, n_streams, Ts):
    S, D = xp.shape
    rows_s = xs.shape[0]
    ts = PROMPT_BLOCK
    n_prompt = S // ts
    n_mem = mem.shape[0]
    hd = D // MEM_HEADS
    args = dict(p, xp=xp, xs=xs, mem=mem, kc=kc, vc=vc)
    ins = [args[n] for n in _XATTN_IN]
    cache_spec = pl.BlockSpec(memory_space=pl.ANY)
    in_specs = [_prompt_spec(ts, D, n_prompt)]
    for n in _XATTN_IN[1:]:
        in_specs.append(cache_spec if n in ("kc", "vc") else _weight_spec(args[n].shape))
    out_shapes = dict(outp=(S, D), outs=(rows_s, D), mk=(n_mem, D), mv=(n_mem, D))
    out_specs = [_prompt_spec(ts, D, n_prompt)] + [_const_spec(out_shapes[n]) for n in _XATTN_OUT[1:]]
    return pl.pallas_call(
        functools.partial(_xattn_kernel, n_prompt=n_prompt, n_streams=n_streams, Ts=Ts, scale=hd ** -0.5),
        grid=(n_prompt + n_streams,),
        in_specs=in_specs,
        out_specs=out_specs,
        out_shape=[jax.ShapeDtypeStruct(out_shapes[n], F32) for n in _XATTN_OUT],
        scratch_shapes=[pltpu.VMEM((n_mem, D), BF16), pltpu.VMEM((n_mem, D), BF16),
                        pltpu.VMEM((rows_s, D), BF16), pltpu.VMEM((rows_s, D), F32),
                        pltpu.VMEM((2, MEM_HEADS, n_mem, hd), F32), pltpu.VMEM((2, MEM_HEADS, n_mem, hd), F32),
                        pltpu.SemaphoreType.DMA((2, 2 * MEM_HEADS))],
        compiler_params=pltpu.CompilerParams(dimension_semantics=("arbitrary",),
                                             vmem_limit_bytes=VMEM_LIMIT),
        name="xattn",
    )(*ins)


def _mlp_rows(x, g_ref, wup_ref, wdown_ref, gfin_ref):
    hm = _rmsnorm(x, g_ref[...]).astype(BF16)
    acc = x
    for j in range(wup_ref.shape[1] // FF_CHUNK):
        cols = slice(j * FF_CHUNK, (j + 1) * FF_CHUNK)
        up = _dot(hm, wup_ref[:, cols])
        acc = acc + _dot(jnp.square(jnp.maximum(up, 0.0)), wdown_ref[cols, :])
    return _rmsnorm(acc, gfin_ref[...])


def _mlp_kernel(xp_ref, xs_ref, g_ref, wup_ref, wdown_ref, gfin_ref, outp_ref, outs_ref, *, n_prompt):
    i = pl.program_id(0)

    @pl.when(i < n_prompt)
    def _():
        outp_ref[...] = _mlp_rows(xp_ref[...], g_ref, wup_ref, wdown_ref, gfin_ref)

    @pl.when(i == n_prompt)
    def _():
        outs_ref[...] = _mlp_rows(xs_ref[...], g_ref, wup_ref, wdown_ref, gfin_ref)


def _mlp(xp, xs, p):
    S, D = xp.shape
    ts = PROMPT_BLOCK
    n_prompt = S // ts
    weights = (p["g_mlp"], p["w_up"], p["w_down"], p["g_final"])
    return pl.pallas_call(
        functools.partial(_mlp_kernel, n_prompt=n_prompt),
        grid=(n_prompt + 1,),
        in_specs=[_prompt_spec(ts, D, n_prompt), _weight_spec(xs.shape)] + [_weight_spec(w.shape) for w in weights],
        out_specs=[_prompt_spec(ts, D, n_prompt), _const_spec(xs.shape)],
        out_shape=[jax.ShapeDtypeStruct(xp.shape, F32), jax.ShapeDtypeStruct(xs.shape, F32)],
        compiler_params=pltpu.CompilerParams(dimension_semantics=("arbitrary",),
                                             vmem_limit_bytes=VMEM_LIMIT),
        name="mlp",
    )(xp, xs, *weights)


def _block_diag(t):
    nhalf, gi, r, c = t.shape
    eye = jnp.eye(gi, dtype=t.dtype)
    return jnp.einsum("kirc,ij->kirjc", t, eye).reshape(nhalf, gi * r, gi * c)


def _prepare_layer(l, g_mix, w_in, w_pool, pool_scale, a_re, a_im, b_re, b_im, c_re, c_im, d_skip,
                   log_dt, w_glu, b_glu, w_out, g_xattn, g_mem, w_q, w_k, w_v, w_o, g_mlp, w_up,
                   w_down, g_final, t_max):
    G, P = a_re.shape[1:]
    gi = G // N_HALF
    ar, ai = a_re[l].astype(F32), a_im[l].astype(F32)
    dt = jnp.exp(log_dt[l].astype(F32))[:, None]
    lam_re, lam_im = ar * dt, ai * dt

    def a_pow(kk):
        kk = kk.astype(F32)[:, None, None]
        mag = jnp.exp(kk * lam_re)
        ang = kk * lam_im
        return (mag * jnp.cos(ang)).reshape(-1, G * P), (mag * jnp.sin(ang)).reshape(-1, G * P)

    steps = jnp.arange(t_max)
    pneg_re, pneg_im = a_pow(-steps)
    ppos_re, ppos_im = a_pow(steps)
    a1_re, a1_im = a_pow(jnp.ones((1,), F32))

    ab_re, ab_im = a1_re.reshape(G, P), a1_im.reshape(G, P)
    den = ar * ar + ai * ai
    coef_re = ((ab_re - 1.0) * ar + ab_im * ai) / den
    coef_im = (ab_im * ar - (ab_re - 1.0) * ai) / den
    br, bi = b_re[l].astype(F32), b_im[l].astype(F32)
    bb_re = coef_re[..., None] * br - coef_im[..., None] * bi
    bb_im = coef_re[..., None] * bi + coef_im[..., None] * br

    def to_b(t):
        return _block_diag(t.reshape(N_HALF, gi, P, -1).transpose(0, 1, 3, 2))

    def to_c(t):
        return _block_diag(t.reshape(N_HALF, gi, -1, P).transpose(0, 1, 3, 2))

    bblk = jnp.concatenate([to_b(bb_re), to_b(bb_im)], axis=-1).astype(BF16)
    cblk = jnp.concatenate([to_c(c_re[l].astype(F32)), to_c(-c_im[l].astype(F32))], axis=1).astype(BF16)

    row = lambda v: v.astype(F32).reshape(1, -1)
    return dict(
        g_mix=row(g_mix[l]), w_in=w_in[l], w_pool=w_pool[l], pool_scale=row(pool_scale[l]),
        bblk=bblk, cblk=cblk,
        pneg_re=pneg_re, pneg_im=pneg_im, ppos_re=ppos_re, ppos_im=ppos_im, a1_re=a1_re, a1_im=a1_im,
        d_skip=row(d_skip[l]), w_glu=w_glu[l], b_glu=row(b_glu[l]), w_out=w_out[l],
        g_xattn=row(g_xattn[l]), g_mem=row(g_mem[l]), w_q=w_q[l], w_k=w_k[l], w_v=w_v[l], w_o=w_o[l],
        g_mlp=row(g_mlp[l]), w_up=w_up[l], w_down=w_down[l], g_final=row(g_final))


def kernel(x_prompt, x_sample, cache_mem_k, cache_mem_v, state_pool, state_ssm_re, state_ssm_im, mem_prompt, g_mix, w_in, w_pool, pool_scale, ssm_a_re, ssm_a_im, ssm_b_re, ssm_b_im, ssm_c_re, ssm_c_im, ssm_d, ssm_log_dt, w_glu, b_glu, w_out, g_xattn, g_mem, w_q, w_k, w_v, w_o, g_mlp, w_up, w_down, g_final):
    depth = g_mix.shape[0]
    assert depth == 1 and x_prompt.shape[0] == 1, "single layer, single prompt stream"
    Bp, S, D = x_prompt.shape
    Bs, Ts, _ = x_sample.shape
    n_mem = mem_prompt.shape[1]
    G, P = ssm_a_re.shape[1:]
    assert S % PROMPT_BLOCK == 0 and PROMPT_BLOCK % PROMPT_CHUNK == 0 and Ts <= PROMPT_CHUNK

    l = 0
    p = _prepare_layer(l, g_mix, w_in, w_pool, pool_scale, ssm_a_re, ssm_a_im, ssm_b_re, ssm_b_im,
                       ssm_c_re, ssm_c_im, ssm_d, ssm_log_dt, w_glu, b_glu, w_out, g_xattn, g_mem,
                       w_q, w_k, w_v, w_o, g_mlp, w_up, w_down, g_final, PROMPT_CHUNK)

    xp1, xs1, hist_p, hre_p, him_p, hist_s, hre_s, him_s = _mixer(
        x_prompt[0], x_sample.reshape(Bs * Ts, D), state_pool[l],
        state_ssm_re[l].reshape(Bs, G * P), state_ssm_im[l].reshape(Bs, G * P), p, Bs, Ts)
    xp2, xs2, mk, mv = _xattn(xp1, xs1, mem_prompt[0], cache_mem_k[l:l + 1], cache_mem_v[l:l + 1], p, Bs, Ts)
    y_prompt, y_sample = _mlp(xp2, xs2, p)

    hd = D // MEM_HEADS
    return (y_prompt[None], y_sample.reshape(Bs, Ts, D),
            mk.reshape(1, Bp, n_mem, MEM_HEADS, hd), mv.reshape(1, Bp, n_mem, MEM_HEADS, hd),
            hist_p[1:].reshape(1, Bp, POOL_HIST, -1),
            hre_p.reshape(1, Bp, G, P), him_p.reshape(1, Bp, G, P),
            hist_s[None], hre_s.reshape(1, Bs, G, P), him_s.reshape(1, Bs, G, P))
```

```python
import functools

import jax
import jax.numpy as jnp
from jax import lax
from jax.experimental import pallas as pl
from jax.experimental.pallas import tpu as pltpu

F32 = jnp.float32
BF16 = jnp.bfloat16

EPS = 1e-6
PAST_LEN = 1024
POOL_WINDOWS = (2, 4, 8, 16)
POOL_HIST = max(POOL_WINDOWS) - 1
HIST_ROWS = 16
MEM_HEADS = 4
N_HALF = 2
S5_TILE = 256

PROMPT_BLOCK = 512
PROMPT_CHUNK = 64
FF_CHUNK = 1024
VMEM_LIMIT = 56 * 1024 * 1024


def _rmsnorm(x, g):
    return x * lax.rsqrt(jnp.mean(x * x, axis=-1, keepdims=True) + EPS) * g


def _dot(a, b):
    return jnp.dot(a.astype(BF16), b.astype(BF16), preferred_element_type=F32)


def _pool_windows(zbuf, nrows, pos, r):
    gw = r["w_pool"].shape[-1]
    outs = []
    for g, w in enumerate(POOL_WINDOWS):
        lanes = slice(g * gw, (g + 1) * gw)
        win = zbuf[pl.ds(0, HIST_ROWS + nrows), lanes]
        k = 1
        while k < w:
            win = win + pltpu.roll(win, k, 0)
            k *= 2
        cur = zbuf[pl.ds(HIST_ROWS, nrows), lanes]
        cnt = jnp.minimum(pos + 1, w).astype(F32)
        pooled = win[HIST_ROWS:, :] / cnt - cur
        outs.append(_dot(pooled, r["w_pool"][g]))
    return jnp.concatenate(outs, axis=-1) * r["pool_scale"][...]


def _cmul(ar, ai, br, bi):
    return ar * br - ai * bi, ar * bi + ai * br


def _s5_scan(u, n_chunks, T, h_carry, h_rows, r, hout_re_ref, hout_im_ref):
    hw = r["bblk"].shape[1]
    nh = r["bblk"].shape[2] // 2
    u_bf = u.astype(BF16)
    r_i = lax.broadcasted_iota(jnp.int32, (T, T), 0)
    c_i = lax.broadcasted_iota(jnp.int32, (T, T), 1)
    ltri = jnp.where(r_i >= c_i, 1.0, 0.0).astype(BF16)

    ys = []
    for k in range(N_HALF):
        uk = u_bf[:, k * hw:(k + 1) * hw]
        yk = None
        for j in range(nh // S5_TILE):
            re_c = slice(j * S5_TILE, (j + 1) * S5_TILE)
            im_c = slice(nh + j * S5_TILE, nh + (j + 1) * S5_TILE)
            st = slice(k * nh + j * S5_TILE, k * nh + (j + 1) * S5_TILE)
            bu_re = _dot(uk, r["bblk"][k, :, re_c])
            bu_im = _dot(uk, r["bblk"][k, :, im_c])
            if h_rows is None:
                carry = (h_carry[0][:, st], h_carry[1][:, st])
            h_re, h_im = [], []
            for c in range(n_chunks):
                rows = slice(c * T, (c + 1) * T)
                xr, xi = _cmul(bu_re[rows], bu_im[rows], r["pneg_re"][0:T, st], r["pneg_im"][0:T, st])
                wr = _dot(ltri, xr)
                wi = _dot(ltri, xi)
                if h_rows is not None:
                    carry = (h_rows[0][pl.ds(c, 1), st], h_rows[1][pl.ds(c, 1), st])
                cr, ci = _cmul(r["a1_re"][:, st], r["a1_im"][:, st], carry[0], carry[1])
                sr, si = _cmul(r["ppos_re"][0:T, st], r["ppos_im"][0:T, st], wr + cr, wi + ci)
                carry = (sr[T - 1:T, :], si[T - 1:T, :])
                if h_rows is not None:
                    hout_re_ref[pl.ds(c, 1), st] = carry[0]
                    hout_im_ref[pl.ds(c, 1), st] = carry[1]
                h_re.append(sr.astype(BF16))
                h_im.append(si.astype(BF16))
            if h_rows is None:
                hout_re_ref[:, st] = carry[0]
                hout_im_ref[:, st] = carry[1]
            part = (_dot(jnp.concatenate(h_re, axis=0), r["cblk"][k, re_c, :])
                    + _dot(jnp.concatenate(h_im, axis=0), r["cblk"][k, im_c, :]))
            yk = part if yk is None else yk + part
        ys.append(yk)
    return jnp.concatenate(ys, axis=-1)


def _mix_tail(x, y_pool, u, y_state, r):
    y = jax.nn.gelu(y_state + r["d_skip"][...] * u)
    y = y * jax.nn.sigmoid(_dot(y, r["w_glu"][...]) + r["b_glu"][...])
    ycat = jnp.concatenate([y_pool.astype(BF16), y.astype(BF16)], axis=-1)
    return x + _dot(ycat, r["w_out"][...])


_MIXER_IN = ("xp", "xs", "hist_in", "h0_re", "h0_im",
             "g_mix", "w_in", "w_pool", "pool_scale", "bblk", "cblk",
             "pneg_re", "pneg_im", "ppos_re", "ppos_im", "a1_re", "a1_im",
             "d_skip", "w_glu", "b_glu", "w_out")
_MIXER_OUT = ("outp", "outs", "histp", "hrep", "himp", "hists", "hres", "hims")
_MIXER_BF16 = ("w_in", "w_pool", "w_glu", "w_out")


def _mixer_kernel(*refs, n_prompt, n_streams, Tp, Ts):
    r = dict(zip(_MIXER_IN + _MIXER_OUT, refs))
    zbuf, ypool_ref = refs[len(_MIXER_IN) + len(_MIXER_OUT):][:2]
    bf16_weights = dict(zip(_MIXER_BF16, refs[len(_MIXER_IN) + len(_MIXER_OUT) + 2:]))
    i = pl.program_id(0)
    ts = r["xp"].shape[0]
    d_pool = zbuf.shape[1]

    @pl.when(i == 0)
    def _():
        zbuf[0:HIST_ROWS, :] = jnp.zeros((HIST_ROWS, d_pool), F32)
        r["hrep"][...] = jnp.zeros(r["hrep"].shape, F32)
        r["himp"][...] = jnp.zeros(r["himp"].shape, F32)
        for name, ref in bf16_weights.items():
            ref[...] = r[name][...].astype(BF16)

    r.update(bf16_weights)

    @pl.when(i < n_prompt)
    def _():
        x = r["xp"][...]
        z = _dot(_rmsnorm(x, r["g_mix"][...]), r["w_in"][...])
        zbuf[HIST_ROWS:HIST_ROWS + ts, :] = z[:, :d_pool]
        u = z[:, d_pool:]
        pos = i * ts + lax.broadcasted_iota(jnp.int32, (ts, 1), 0)
        y_pool = _pool_windows(zbuf, ts, pos, r)
        y_state = _s5_scan(u, ts // Tp, Tp, (r["hrep"][...], r["himp"][...]), None, r, r["hrep"], r["himp"])
        r["outp"][...] = _mix_tail(x, y_pool, u, y_state, r)
        tail = zbuf[ts:ts + HIST_ROWS, :]
        zbuf[0:HIST_ROWS, :] = tail
        r["histp"][...] = tail

    @pl.when(i == n_prompt)
    def _():
        ext = HIST_ROWS + Ts
        x = r["xs"][...]
        z = _dot(_rmsnorm(x, r["g_mix"][...]), r["w_in"][...])
        u = z[:, d_pool:]
        for b in range(n_streams):
            zbuf[b * ext:b * ext + 1, :] = jnp.zeros((1, d_pool), F32)
            zbuf[b * ext + 1:b * ext + HIST_ROWS, :] = r["hist_in"][b]
            zbuf[b * ext + HIST_ROWS:(b + 1) * ext, :] = z[b * Ts:(b + 1) * Ts, :d_pool]
            r["hists"][b] = zbuf[(b + 1) * ext - POOL_HIST:(b + 1) * ext, :]
        nrows = n_streams * ext - HIST_ROWS
        ridx = lax.broadcasted_iota(jnp.int32, (nrows, 1), 0)
        pos = PAST_LEN + lax.rem(ridx, ext)
        y_all = _pool_windows(zbuf, nrows, pos, r)
        for b in range(n_streams):
            ypool_ref[b * Ts:(b + 1) * Ts, :] = y_all[b * ext:b * ext + Ts, :]
        y_state = _s5_scan(u, n_streams, Ts, None, (r["h0_re"], r["h0_im"]), r, r["hres"], r["hims"])
        r["outs"][...] = _mix_tail(x, ypool_ref[...], u, y_state, r)


def _const_spec(shape):
    nd = len(shape)
    return pl.BlockSpec(shape, lambda *_: (0,) * nd)


def _weight_spec(shape):
    nd = len(shape)
    return pl.BlockSpec(shape, lambda *_: (0,) * nd, pipeline_mode=pl.Buffered(1))


def _prompt_spec(ts, d, n_prompt):
    return pl.BlockSpec((ts, d), lambda i: (jnp.minimum(i, n_prompt - 1), 0))


def _mixer(xp, xs, hist_in, h0_re, h0_im, p, n_streams, Ts):
    S, D = xp.shape
    rows_s = xs.shape[0]
    ts = PROMPT_BLOCK
    n_prompt = S // ts
    d_pool = hist_in.shape[-1]
    n_state = p["a1_re"].shape[-1]
    assert n_streams * (HIST_ROWS + Ts) <= HIST_ROWS + ts and rows_s <= ts
    args = dict(p, xp=xp, xs=xs, hist_in=hist_in, h0_re=h0_re, h0_im=h0_im)
    ins = [args[n] for n in _MIXER_IN]
    in_specs = [_prompt_spec(ts, D, n_prompt)] + [_weight_spec(a.shape) for a in ins[1:]]
    out_shapes = dict(
        outp=(S, D), outs=(rows_s, D), histp=(HIST_ROWS, d_pool), hrep=(1, n_state), himp=(1, n_state),
        hists=(n_streams, POOL_HIST, d_pool), hres=(n_streams, n_state), hims=(n_streams, n_state))
    out_specs = [_prompt_spec(ts, D, n_prompt)] + [_const_spec(out_shapes[n]) for n in _MIXER_OUT[1:]]
    return pl.pallas_call(
        functools.partial(_mixer_kernel, n_prompt=n_prompt, n_streams=n_streams, Tp=PROMPT_CHUNK, Ts=Ts),
        grid=(n_prompt + 1,),
        in_specs=in_specs,
        out_specs=out_specs,
        out_shape=[jax.ShapeDtypeStruct(out_shapes[n], F32) for n in _MIXER_OUT],
        scratch_shapes=[pltpu.VMEM((HIST_ROWS + ts, d_pool), F32),
                        pltpu.VMEM((rows_s, d_pool), F32)]
        + [pltpu.VMEM(args[n].shape, BF16) for n in _MIXER_BF16],
        compiler_params=pltpu.CompilerParams(dimension_semantics=("arbitrary",),
                                             vmem_limit_bytes=VMEM_LIMIT),
        name="mixer",
    )(*ins)


def _softmax_rows(s):
    e = jnp.exp(s - jnp.max(s, axis=-1, keepdims=True))
    return e / jnp.sum(e, axis=-1, keepdims=True)


def _attend(q, head_k, head_v, scale):
    hd = q.shape[-1] // MEM_HEADS
    outs = []
    for h in range(MEM_HEADS):
        s = lax.dot_general(q[:, h * hd:(h + 1) * hd], head_k(h).astype(BF16), (((1,), (1,)), ((), ())),
                            preferred_element_type=F32) * scale
        outs.append(_dot(_softmax_rows(s), head_v(h)))
    return jnp.concatenate(outs, axis=-1)


_XATTN_IN = ("xp", "xs", "mem", "kc", "vc", "g_mem", "w_k", "w_v", "g_xattn", "w_q", "w_o")
_XATTN_OUT = ("outp", "outs", "mk", "mv")


def _xattn_kernel(*refs, n_prompt, n_streams, Ts, scale):
    r = dict(zip(_XATTN_IN + _XATTN_OUT, refs))
    kp_ref, vp_ref, qs_ref, os_ref, kbuf, vbuf, sem = refs[len(_XATTN_IN) + len(_XATTN_OUT):]
    i = pl.program_id(0)
    hd = kp_ref.shape[-1] // MEM_HEADS

    def cache_copies(b):
        slot = lax.rem(b, 2)
        return [pltpu.make_async_copy(src.at[0, b, :, h, :], dst.at[slot, h], sem.at[slot, j * MEM_HEADS + h])
                for j, (src, dst) in enumerate(((r["kc"], kbuf), (r["vc"], vbuf))) for h in range(MEM_HEADS)]

    @pl.when(i == 0)
    def _():
        m = _rmsnorm(r["mem"][...], r["g_mem"][...]).astype(BF16)
        k = _dot(m, r["w_k"][...])
        v = _dot(m, r["w_v"][...])
        r["mk"][...] = k
        r["mv"][...] = v
        kp_ref[...] = k.astype(BF16)
        vp_ref[...] = v.astype(BF16)

    @pl.when(i < n_prompt)
    def _():
        x = r["xp"][...]
        q = _dot(_rmsnorm(x, r["g_xattn"][...]), r["w_q"][...]).astype(BF16)
        o = _attend(q, lambda h: kp_ref[:, h * hd:(h + 1) * hd], lambda h: vp_ref[:, h * hd:(h + 1) * hd], scale)
        r["outp"][...] = x + _dot(o, r["w_o"][...])

    @pl.when(i == n_prompt - 1)
    def _():
        for cp in cache_copies(0):
            cp.start()

    @pl.when(i == n_prompt)
    def _():
        qs_ref[...] = _dot(_rmsnorm(r["xs"][...], r["g_xattn"][...]), r["w_q"][...]).astype(BF16)

    @pl.when(i >= n_prompt)
    def _():
        b = i - n_prompt
        slot = lax.rem(b, 2)

        @pl.when(b + 1 < n_streams)
        def _():
            for cp in cache_copies(b + 1):
                cp.start()

        for cp in cache_copies(b):
            cp.wait()
        rows = pl.ds(pl.multiple_of(b * Ts, Ts), Ts)
        os_ref[rows, :] = _attend(qs_ref[rows, :], lambda h: kbuf[slot, h], lambda h: vbuf[slot, h], scale)

    @pl.when(i == n_prompt + n_streams - 1)
    def _():
        r["outs"][...] = r["xs"][...] + _dot(os_ref[...], r["w_o"][...])


def _xattn(xp, xs, mem, kc, vc, p, n_streams, Ts):
    S, D = xp.shape
    rows_s = xs.shape[0]
    ts = PROMPT_BLOCK
    n_prompt = S // ts
    n_mem = mem.shape[0]
    hd = D // MEM_HEADS
    args = dict(p, xp=xp, xs=xs, mem=mem, kc=kc, vc=vc)
    ins = [args[n] for n in _XATTN_IN]
    cache_spec = pl.BlockSpec(memory_space=pl.ANY)
    in_specs = [_prompt_spec(ts, D, n_prompt)]
    for n in _XATTN_IN[1:]:
        in_specs.append(cache_spec if n in ("kc", "vc") else _weight_spec(args[n].shape))
    out_shapes = dict(outp=(S, D), outs=(rows_s, D), mk=(n_mem, D), mv=(n_mem, D))
    out_specs = [_prompt_spec(ts, D, n_prompt)] + [_const_spec(out_shapes[n]) for n in _XATTN_OUT[1:]]
    return pl.pallas_call(
        functools.partial(_xattn_kernel, n_prompt=n_prompt, n_streams=n_streams, Ts=Ts, scale=hd ** -0.5),
        grid=(n_prompt + n_streams,),
        in_specs=in_specs,
        out_specs=out_specs,
        out_shape=[jax.ShapeDtypeStruct(out_shapes[n], F32) for n in _XATTN_OUT],
        scratch_shapes=[pltpu.VMEM((n_mem, D), BF16), pltpu.VMEM((n_mem, D), BF16),
                        pltpu.VMEM((rows_s, D), BF16), pltpu.VMEM((rows_s, D), F32),
                        pltpu.VMEM((2, MEM_HEADS, n_mem, hd), F32), pltpu.VMEM((2, MEM_HEADS, n_mem, hd), F32),
                        pltpu.SemaphoreType.DMA((2, 2 * MEM_HEADS))],
        compiler_params=pltpu.CompilerParams(dimension_semantics=("arbitrary",),
                                             vmem_limit_bytes=VMEM_LIMIT),
        name="xattn",
    )(*ins)


def _mlp_rows(x, g_ref, wup_ref, wdown_ref, gfin_ref):
    hm = _rmsnorm(x, g_ref[...]).astype(BF16)
    acc = x
    for j in range(wup_ref.shape[1] // FF_CHUNK):
        cols = slice(j * FF_CHUNK, (j + 1) * FF_CHUNK)
        up = _dot(hm, wup_ref[:, cols])
        acc = acc + _dot(jnp.square(jnp.maximum(up, 0.0)), wdown_ref[cols, :])
    return _rmsnorm(acc, gfin_ref[...])


def _mlp_kernel(xp_ref, xs_ref, g_ref, wup_ref, wdown_ref, gfin_ref, outp_ref, outs_ref, *, n_prompt):
    i = pl.program_id(0)

    @pl.when(i < n_prompt)
    def _():
        outp_ref[...] = _mlp_rows(xp_ref[...], g_ref, wup_ref, wdown_ref, gfin_ref)

    @pl.when(i == n_prompt)
    def _():
        outs_ref[...] = _mlp_rows(xs_ref[...], g_ref, wup_ref, wdown_ref, gfin_ref)


def _mlp(xp, xs, p):
    S, D = xp.shape
    ts = PROMPT_BLOCK
    n_prompt = S // ts
    weights = (p["g_mlp"], p["w_up"], p["w_down"], p["g_final"])
    return pl.pallas_call(
        functools.partial(_mlp_kernel, n_prompt=n_prompt),
        grid=(n_prompt + 1,),
        in_specs=[_prompt_spec(ts, D, n_prompt), _weight_spec(xs.shape)] + [_weight_spec(w.shape) for w in weights],
        out_specs=[_prompt_spec(ts, D, n_prompt), _const_spec(xs.shape)],
        out_shape=[jax.ShapeDtypeStruct(xp.shape, F32), jax.ShapeDtypeStruct(xs.shape, F32)],
        compiler_params=pltpu.CompilerParams(dimension_semantics=("arbitrary",),
                                             vmem_limit_bytes=VMEM_LIMIT),
        name="mlp",
    )(xp, xs, *weights)


def _block_diag(t):
    nhalf, gi, r, c = t.shape
    eye = jnp.eye(gi, dtype=t.dtype)
    return jnp.einsum("kirc,ij->kirjc", t, eye).reshape(nhalf, gi * r, gi * c)


def _prepare_layer(l, g_mix, w_in, w_pool, pool_scale, a_re, a_im, b_re, b_im, c_re, c_im, d_skip,
                   log_dt, w_glu, b_glu, w_out, g_xattn, g_mem, w_q, w_k, w_v, w_o, g_mlp, w_up,
                   w_down, g_final, t_max):
    G, P = a_re.shape[1:]
    gi = G // N_HALF
    ar, ai = a_re[l].astype(F32), a_im[l].astype(F32)
    dt = jnp.exp(log_dt[l].astype(F32))[:, None]
    lam_re, lam_im = ar * dt, ai * dt

    def a_pow(kk):
        kk = kk.astype(F32)[:, None, None]
        mag = jnp.exp(kk * lam_re)
        ang = kk * lam_im
        return (mag * jnp.cos(ang)).reshape(-1, G * P), (mag * jnp.sin(ang)).reshape(-1, G * P)

    steps = jnp.arange(t_max)
    pneg_re, pneg_im = a_pow(-steps)
    ppos_re, ppos_im = a_pow(steps)
    a1_re, a1_im = a_pow(jnp.ones((1,), F32))

    ab_re, ab_im = a1_re.reshape(G, P), a1_im.reshape(G, P)
    den = ar * ar + ai * ai
    coef_re = ((ab_re - 1.0) * ar + ab_im * ai) / den
    coef_im = (ab_im * ar - (ab_re - 1.0) * ai) / den
    br, bi = b_re[l].astype(F32), b_im[l].astype(F32)
    bb_re = coef_re[..., None] * br - coef_im[..., None] * bi
    bb_im = coef_re[..., None] * bi + coef_im[..., None] * br

    def to_b(t):
        return _block_diag(t.reshape(N_HALF, gi, P, -1).transpose(0, 1, 3, 2))

    def to_c(t):
        return _block_diag(t.reshape(N_HALF, gi, -1, P).transpose(0, 1, 3, 2))

    bblk = jnp.concatenate([to_b(bb_re), to_b(bb_im)], axis=-1).astype(BF16)
    cblk = jnp.concatenate([to_c(c_re[l].astype(F32)), to_c(-c_im[l].astype(F32))], axis=1).astype(BF16)

    row = lambda v: v.astype(F32).reshape(1, -1)
    return dict(
        g_mix=row(g_mix[l]), w_in=w_in[l], w_pool=w_pool[l], pool_scale=row(pool_scale[l]),
        bblk=bblk, cblk=cblk,
        pneg_re=pneg_re, pneg_im=pneg_im, ppos_re=ppos_re, ppos_im=ppos_im, a1_re=a1_re, a1_im=a1_im,
        d_skip=row(d_skip[l]), w_glu=w_glu[l], b_glu=row(b_glu[l]), w_out=w_out[l],
        g_xattn=row(g_xattn[l]), g_mem=row(g_mem[l]), w_q=w_q[l], w_k=w_k[l], w_v=w_v[l], w_o=w_o[l],
        g_mlp=row(g_mlp[l]), w_up=w_up[l], w_down=w_down[l], g_final=row(g_final))


def kernel(x_prompt, x_sample, cache_mem_k, cache_mem_v, state_pool, state_ssm_re, state_ssm_im, mem_prompt, g_mix, w_in, w_pool, pool_scale, ssm_a_re, ssm_a_im, ssm_b_re, ssm_b_im, ssm_c_re, ssm_c_im, ssm_d, ssm_log_dt, w_glu, b_glu, w_out, g_xattn, g_mem, w_q, w_k, w_v, w_o, g_mlp, w_up, w_down, g_final):
    depth = g_mix.shape[0]
    assert depth == 1 and x_prompt.shape[0] == 1, "single layer, single prompt stream"
    Bp, S, D = x_prompt.shape
    Bs, Ts, _ = x_sample.shape
    n_mem = mem_prompt.shape[1]
    G, P = ssm_a_re.shape[1:]
    assert S % PROMPT_BLOCK == 0 and PROMPT_BLOCK % PROMPT_CHUNK == 0 and Ts <= PROMPT_CHUNK

    l = 0
    p = _prepare_layer(l, g_mix, w_in, w_pool, pool_scale, ssm_a_re, ssm_a_im, ssm_b_re, ssm_b_im,
                       ssm_c_re, ssm_c_im, ssm_d, ssm_log_dt, w_glu, b_glu, w_out, g_xattn, g_mem,
                       w_q, w_k, w_v, w_o, g_mlp, w_up, w_down, g_final, PROMPT_CHUNK)

    xp1, xs1, hist_p, hre_p, him_p, hist_s, hre_s, him_s = _mixer(
        x_prompt[0], x_sample.reshape(Bs * Ts, D), state_pool[l],
        state_ssm_re[l].reshape(Bs, G * P), state_ssm_im[l].reshape(Bs, G * P), p, Bs, Ts)
    xp2, xs2, mk, mv = _xattn(xp1, xs1, mem_prompt[0], cache_mem_k[l:l + 1], cache_mem_v[l:l + 1], p, Bs, Ts)
    y_prompt, y_sample = _mlp(xp2, xs2, p)

    hd = D // MEM_HEADS
    return (y_prompt[None], y_sample.reshape(Bs, Ts, D),
            mk.reshape(1, Bp, n_mem, MEM_HEADS, hd), mv.reshape(1, Bp, n_mem, MEM_HEADS, hd),
            hist_p[1:].reshape(1, Bp, POOL_HIST, -1),
            hre_p.reshape(1, Bp, G, P), him_p.reshape(1, Bp, G, P),
            hist_s[None], hre_s.reshape(1, Bs, G, P), him_s.reshape(1, Bs, G, P))
```

```python
import functools

import jax
import jax.numpy as jnp
from jax import lax
from jax.experimental import pallas as pl
from jax.experimental.pallas import tpu as pltpu

F32 = jnp.float32
BF16 = jnp.bfloat16

EPS = 1e-6
PAST_LEN = 1024
POOL_WINDOWS = (2, 4, 8, 16)
POOL_HIST = max(POOL_WINDOWS) - 1
HIST_ROWS = 16
MEM_HEADS = 4
N_HALF = 2
S5_TILE = 256

PROMPT_BLOCK = 512
PROMPT_CHUNK = 64
FF_CHUNK = 1024
VMEM_LIMIT = 56 * 1024 * 1024


def _rmsnorm(x, g):
    return x * lax.rsqrt(jnp.mean(x * x, axis=-1, keepdims=True) + EPS) * g


def _dot(a, b):
    return jnp.dot(a.astype(BF16), b.astype(BF16), preferred_element_type=F32)


def _pool_windows(zbuf, nrows, pos, r):
    gw = r["w_pool"].shape[-1]
    outs = []
    for g, w in enumerate(POOL_WINDOWS):
        lanes = slice(g * gw, (g + 1) * gw)
        win = zbuf[pl.ds(0, HIST_ROWS + nrows), lanes]
        k = 1
        while k < w:
            win = win + pltpu.roll(win, k, 0)
            k *= 2
        cur = zbuf[pl.ds(HIST_ROWS, nrows), lanes]
        cnt = jnp.minimum(pos + 1, w).astype(F32)
        pooled = win[HIST_ROWS:, :] / cnt - cur
        outs.append(_dot(pooled, r["w_pool"][g]))
    return jnp.concatenate(outs, axis=-1) * r["pool_scale"][...]


def _cmul(ar, ai, br, bi):
    return ar * br - ai * bi, ar * bi + ai * br


def _s5_scan(u, n_chunks, T, h_carry, h_rows, r, hout_re_ref, hout_im_ref):
    hw = r["bblk"].shape[1]
    nh = r["bblk"].shape[2] // 2
    u_bf = u.astype(BF16)
    r_i = lax.broadcasted_iota(jnp.int32, (T, T), 0)
    c_i = lax.broadcasted_iota(jnp.int32, (T, T), 1)
    ltri = jnp.where(r_i >= c_i, 1.0, 0.0).astype(BF16)

    ys = []
    for k in range(N_HALF):
        uk = u_bf[:, k * hw:(k + 1) * hw]
        yk = None
        for j in range(nh // S5_TILE):
            re_c = slice(j * S5_TILE, (j + 1) * S5_TILE)
            im_c = slice(nh + j * S5_TILE, nh + (j + 1) * S5_TILE)
            st = slice(k * nh + j * S5_TILE, k * nh + (j + 1) * S5_TILE)
            bu_re = _dot(uk, r["bblk"][k, :, re_c])
            bu_im = _dot(uk, r["bblk"][k, :, im_c])
            if h_rows is None:
                carry = (h_carry[0][:, st], h_carry[1][:, st])
            h_re, h_im = [], []
            for c in range(n_chunks):
                rows = slice(c * T, (c + 1) * T)
                xr, xi = _cmul(bu_re[rows], bu_im[rows], r["pneg_re"][0:T, st], r["pneg_im"][0:T, st])
                wr = _dot(ltri, xr)
                wi = _dot(ltri, xi)
                if h_rows is not None:
                    carry = (h_rows[0][pl.ds(c, 1), st], h_rows[1][pl.ds(c, 1), st])
                cr, ci = _cmul(r["a1_re"][:, st], r["a1_im"][:, st], carry[0], carry[1])
                sr, si = _cmul(r["ppos_re"][0:T, st], r["ppos_im"][0:T, st], wr + cr, wi + ci)
                carry = (sr[T - 1:T, :], si[T - 1:T, :])
                if h_rows is not None:
                    hout_re_ref[pl.ds(c, 1), st] = carry[0]
                    hout_im_ref[pl.ds(c, 1), st] = carry[1]
                h_re.append(sr.astype(BF16))
                h_im.append(si.astype(BF16))
            if h_rows is None:
                hout_re_ref[:, st] = carry[0]
                hout_im_ref[:, st] = carry[1]
            part = (_dot(jnp.concatenate(h_re, axis=0), r["cblk"][k, re_c, :])
                    + _dot(jnp.concatenate(h_im, axis=0), r["cblk"][k, im_c, :]))
            yk = part if yk is None else yk + part
        ys.append(yk)
    return jnp.concatenate(ys, axis=-1)


def _mix_tail(x, y_pool, u, y_state, r):
    y = jax.nn.gelu(y_state + r["d_skip"][...] * u)
    y = y * jax.nn.sigmoid(_dot(y, r["w_glu"][...]) + r["b_glu"][...])
    ycat = jnp.concatenate([y_pool.astype(BF16), y.astype(BF16)], axis=-1)
    return x + _dot(ycat, r["w_out"][...])


_MIXER_IN = ("xp", "xs", "hist_in", "h0_re", "h0_im",
             "g_mix", "w_in", "w_pool", "pool_scale", "bblk", "cblk",
             "pneg_re", "pneg_im", "ppos_re", "ppos_im", "a1_re", "a1_im",
             "d_skip", "w_glu", "b_glu", "w_out")
_MIXER_OUT = ("outp", "outs", "histp", "hrep", "himp", "hists", "hres", "hims")
_MIXER_BF16 = ("w_in", "w_pool", "w_glu", "w_out")


def _mixer_kernel(*refs, n_prompt, n_streams, Tp, Ts):
    r = dict(zip(_MIXER_IN + _MIXER_OUT, refs))
    zbuf, ypool_ref = refs[len(_MIXER_IN) + len(_MIXER_OUT):][:2]
    bf16_weights = dict(zip(_MIXER_BF16, refs[len(_MIXER_IN) + len(_MIXER_OUT) + 2:]))
    i = pl.program_id(0)
    ts = r["xp"].shape[0]
    d_pool = zbuf.shape[1]

    @pl.when(i == 0)
    def _():
        zbuf[0:HIST_ROWS, :] = jnp.zeros((HIST_ROWS, d_pool), F32)
        r["hrep"][...] = jnp.zeros(r["hrep"].shape, F32)
        r["himp"][...] = jnp.zeros(r["himp"].shape, F32)
        for name, ref in bf16_weights.items():
            ref[...] = r[name][...].astype(BF16)

    r.update(bf16_weights)

    @pl.when(i < n_prompt)
    def _():
        x = r["xp"][...]
        z = _dot(_rmsnorm(x, r["g_mix"][...]), r["w_in"][...])
        zbuf[HIST_ROWS:HIST_ROWS + ts, :] = z[:, :d_pool]
        u = z[:, d_pool:]
        pos = i * ts + lax.broadcasted_iota(jnp.int32, (ts, 1), 0)
        y_pool = _pool_windows(zbuf, ts, pos, r)
        y_state = _s5_scan(u, ts // Tp, Tp, (r["hrep"][...], r["himp"][...]), None, r, r["hrep"], r["himp"])
        r["outp"][...] = _mix_tail(x, y_pool, u, y_state, r)
        tail = zbuf[ts:ts + HIST_ROWS, :]
        zbuf[0:HIST_ROWS, :] = tail
        r["histp"][...] = tail

    @pl.when(i == n_prompt)
    def _():
        ext = HIST_ROWS + Ts
        x = r["xs"][...]
        z = _dot(_rmsnorm(x, r["g_mix"][...]), r["w_in"][...])
        u = z[:, d_pool:]
        for b in range(n_streams):
            zbuf[b * ext:b * ext + 1, :] = jnp.zeros((1, d_pool), F32)
            zbuf[b * ext + 1:b * ext + HIST_ROWS, :] = r["hist_in"][b]
            zbuf[b * ext + HIST_ROWS:(b + 1) * ext, :] = z[b * Ts:(b + 1) * Ts, :d_pool]
            r["hists"][b] = zbuf[(b + 1) * ext - POOL_HIST:(b + 1) * ext, :]
        nrows = n_streams * ext - HIST_ROWS
        ridx = lax.broadcasted_iota(jnp.int32, (nrows, 1), 0)
        pos = PAST_LEN + lax.rem(ridx, ext)
        y_all = _pool_windows(zbuf, nrows, pos, r)
        for b in range(n_streams):
            ypool_ref[b * Ts:(b + 1) * Ts, :] = y_all[b * ext:b * ext + Ts, :]
        y_state = _s5_scan(u, n_streams, Ts, None, (r["h0_re"], r["h0_im"]), r, r["hres"], r["hims"])
        r["outs"][...] = _mix_tail(x, ypool_ref[...], u, y_state, r)


def _const_spec(shape):
    nd = len(shape)
    return pl.BlockSpec(shape, lambda *_: (0,) * nd)


def _weight_spec(shape):
    nd = len(shape)
    return pl.BlockSpec(shape, lambda *_: (0,) * nd, pipeline_mode=pl.Buffered(1))


def _prompt_spec(ts, d, n_prompt):
    return pl.BlockSpec((ts, d), lambda i: (jnp.minimum(i, n_prompt - 1), 0))


def _mixer(xp, xs, hist_in, h0_re, h0_im, p, n_streams, Ts):
    S, D = xp.shape
    rows_s = xs.shape[0]
    ts = PROMPT_BLOCK
    n_prompt = S // ts
    d_pool = hist_in.shape[-1]
    n_state = p["a1_re"].shape[-1]
    assert n_streams * (HIST_ROWS + Ts) <= HIST_ROWS + ts and rows_s <= ts
    args = dict(p, xp=xp, xs=xs, hist_in=hist_in, h0_re=h0_re, h0_im=h0_im)
    ins = [args[n] for n in _MIXER_IN]
    in_specs = [_prompt_spec(ts, D, n_prompt)] + [_weight_spec(a.shape) for a in ins[1:]]
    out_shapes = dict(
        outp=(S, D), outs=(rows_s, D), histp=(HIST_ROWS, d_pool), hrep=(1, n_state), himp=(1, n_state),
        hists=(n_streams, POOL_HIST, d_pool), hres=(n_streams, n_state), hims=(n_streams, n_state))
    out_specs = [_prompt_spec(ts, D, n_prompt)] + [_const_spec(out_shapes[n]) for n in _MIXER_OUT[1:]]
    return pl.pallas_call(
        functools.partial(_mixer_kernel, n_prompt=n_prompt, n_streams=n_streams, Tp=PROMPT_CHUNK, Ts=Ts),
        grid=(n_prompt + 1,),
        in_specs=in_specs,
        out_specs=out_specs,
        out_shape=[jax.ShapeDtypeStruct(out_shapes[n], F32) for n in _MIXER_OUT],
        scratch_shapes=[pltpu.VMEM((HIST_ROWS + ts, d_pool), F32),
                        pltpu.VMEM((rows_s, d_pool), F32)]
        + [pltpu.VMEM(args[n].shape, BF16) for n in _MIXER_BF16],
        compiler_params=pltpu.CompilerParams(dimension_semantics=("arbitrary",),
                                             vmem_limit_bytes=VMEM_LIMIT),
        name="mixer",
    )(*ins)


def _softmax_rows(s):
    e = jnp.exp(s - jnp.max(s, axis=-1, keepdims=True))
    return e / jnp.sum(e, axis=-1, keepdims=True)


def _attend(q_heads, k_heads, v_heads, scale):
    outs = []
    for q, k, v in zip(q_heads, k_heads, v_heads):
        s = lax.dot_general(q, k.astype(BF16), (((1,), (1,)), ((), ())), preferred_element_type=F32) * scale
        outs.append(_dot(_softmax_rows(s), v))
    return outs


_XATTN_IN = ("xp", "xs", "mem", "kc", "vc", "g_mem", "w_k", "w_v", "g_xattn", "w_q", "w_o")
_XATTN_OUT = ("outp", "outs", "mk", "mv")


def _xattn_kernel(*refs, n_prompt, n_streams, Ts, hpu, scale):
    r = dict(zip(_XATTN_IN + _XATTN_OUT, refs))
    kp_ref, vp_ref, qs_ref, os_ref, kbuf, vbuf, sem = refs[len(_XATTN_IN) + len(_XATTN_OUT):]
    i = pl.program_id(0)
    hd = kp_ref.shape[-1] // MEM_HEADS
    ups = MEM_HEADS // hpu

    def unit_copies(unit, hg):
        b, slot = unit // ups, lax.rem(unit, 2)
        return [pltpu.make_async_copy(src.at[0, b, :, hg * hpu + e, :], dst.at[slot, e], sem.at[slot, j * hpu + e])
                for j, (src, dst) in enumerate(((r["kc"], kbuf), (r["vc"], vbuf))) for e in range(hpu)]

    def for_unit(unit, fn):
        for hg in range(ups):
            @pl.when(lax.rem(unit, ups) == hg)
            def _():
                for cp in unit_copies(unit, hg):
                    fn(cp)

    @pl.when(i == 0)
    def _():
        for_unit(i, lambda cp: cp.start())
        m = _rmsnorm(r["mem"][...], r["g_mem"][...]).astype(BF16)
        k = _dot(m, r["w_k"][...])
        v = _dot(m, r["w_v"][...])
        r["mk"][...] = k
        r["mv"][...] = v
        kp_ref[...] = k.astype(BF16)
        vp_ref[...] = v.astype(BF16)
        q = _dot(_rmsnorm(r["xs"][...], r["g_xattn"][...]), r["w_q"][...]).astype(BF16)
        for h in range(MEM_HEADS):
            qs_ref[h] = q[:, h * hd:(h + 1) * hd]

    @pl.when(i < n_prompt)
    def _():
        @pl.when(i + 1 < n_prompt)
        def _():
            for_unit(i + 1, lambda cp: cp.start())

        for_unit(i, lambda cp: cp.wait())

        x = r["xp"][...]
        q = _dot(_rmsnorm(x, r["g_xattn"][...]), r["w_q"][...]).astype(BF16)
        o = _attend([q[:, h * hd:(h + 1) * hd] for h in range(MEM_HEADS)],
                    [kp_ref[:, h * hd:(h + 1) * hd] for h in range(MEM_HEADS)],
                    [vp_ref[:, h * hd:(h + 1) * hd] for h in range(MEM_HEADS)], scale)
        r["outp"][...] = x + _dot(jnp.concatenate(o, axis=-1), r["w_o"][...])

        b, hg, slot = i // ups, lax.rem(i, ups), lax.rem(i, 2)
        rows = pl.ds(pl.multiple_of(b * Ts, Ts), Ts)
        heads = [hg * hpu + e for e in range(hpu)]
        o_s = _attend([qs_ref[h, rows, :] for h in heads], [kbuf[slot, e] for e in range(hpu)],
                      [vbuf[slot, e] for e in range(hpu)], scale)
        for h, o_h in zip(heads, o_s):
            os_ref[h, rows, :] = o_h.astype(BF16)

    @pl.when(i == n_prompt)
    def _():
        o = jnp.concatenate([os_ref[h] for h in range(MEM_HEADS)], axis=-1)
        r["outs"][...] = r["xs"][...] + _dot(o, r["w_o"][...])


def _xattn(xp, xs, mem, kc, vc, p, n_streams, Ts):
    S, D = xp.shape
    rows_s = xs.shape[0]
    ts = PROMPT_BLOCK
    n_prompt = S // ts
    n_mem = mem.shape[0]
    hd = D // MEM_HEADS
    hpu = n_streams * MEM_HEADS // n_prompt
    assert hpu * n_prompt == n_streams * MEM_HEADS and MEM_HEADS % hpu == 0
    args = dict(p, xp=xp, xs=xs, mem=mem, kc=kc, vc=vc)
    ins = [args[n] for n in _XATTN_IN]
    cache_spec = pl.BlockSpec(memory_space=pl.ANY)
    in_specs = [_prompt_spec(ts, D, n_prompt)]
    for n in _XATTN_IN[1:]:
        in_specs.append(cache_spec if n in ("kc", "vc") else _weight_spec(args[n].shape))
    out_shapes = dict(outp=(S, D), outs=(rows_s, D), mk=(n_mem, D), mv=(n_mem, D))
    out_specs = [_prompt_spec(ts, D, n_prompt)] + [_const_spec(out_shapes[n]) for n in _XATTN_OUT[1:]]
    return pl.pallas_call(
        functools.partial(_xattn_kernel, n_prompt=n_prompt, n_streams=n_streams, Ts=Ts, hpu=hpu, scale=hd ** -0.5),
        grid=(n_prompt + 1,),
        in_specs=in_specs,
        out_specs=out_specs,
        out_shape=[jax.ShapeDtypeStruct(out_shapes[n], F32) for n in _XATTN_OUT],
        scratch_shapes=[pltpu.VMEM((n_mem, D), BF16), pltpu.VMEM((n_mem, D), BF16),
                        pltpu.VMEM((MEM_HEADS, rows_s, hd), BF16), pltpu.VMEM((MEM_HEADS, rows_s, hd), BF16),
                        pltpu.VMEM((2, hpu, n_mem, hd), F32), pltpu.VMEM((2, hpu, n_mem, hd), F32),
                        pltpu.SemaphoreType.DMA((2, 2 * hpu))],
        compiler_params=pltpu.CompilerParams(dimension_semantics=("arbitrary",),
                                             vmem_limit_bytes=VMEM_LIMIT),
        name="xattn",
    )(*ins)


def _mlp_rows(x, g_ref, wup_ref, wdown_ref, gfin_ref):
    hm = _rmsnorm(x, g_ref[...]).astype(BF16)
    acc = x
    for j in range(wup_ref.shape[1] // FF_CHUNK):
        cols = slice(j * FF_CHUNK, (j + 1) * FF_CHUNK)
        up = _dot(hm, wup_ref[:, cols])
        acc = acc + _dot(jnp.square(jnp.maximum(up, 0.0)), wdown_ref[cols, :])
    return _rmsnorm(acc, gfin_ref[...])


def _mlp_kernel(xp_ref, xs_ref, g_ref, wup_ref, wdown_ref, gfin_ref, outp_ref, outs_ref, *, n_prompt):
    i = pl.program_id(0)

    @pl.when(i < n_prompt)
    def _():
        outp_ref[...] = _mlp_rows(xp_ref[...], g_ref, wup_ref, wdown_ref, gfin_ref)

    @pl.when(i == n_prompt)
    def _():
        outs_ref[...] = _mlp_rows(xs_ref[...], g_ref, wup_ref, wdown_ref, gfin_ref)


def _mlp(xp, xs, p):
    S, D = xp.shape
    ts = PROMPT_BLOCK
    n_prompt = S // ts
    weights = (p["g_mlp"], p["w_up"], p["w_down"], p["g_final"])
    return pl.pallas_call(
        functools.partial(_mlp_kernel, n_prompt=n_prompt),
        grid=(n_prompt + 1,),
        in_specs=[_prompt_spec(ts, D, n_prompt), _weight_spec(xs.shape)] + [_weight_spec(w.shape) for w in weights],
        out_specs=[_prompt_spec(ts, D, n_prompt), _const_spec(xs.shape)],
        out_shape=[jax.ShapeDtypeStruct(xp.shape, F32), jax.ShapeDtypeStruct(xs.shape, F32)],
        compiler_params=pltpu.CompilerParams(dimension_semantics=("arbitrary",),
                                             vmem_limit_bytes=VMEM_LIMIT),
        name="mlp",
    )(xp, xs, *weights)


def _block_diag(t):
    nhalf, gi, r, c = t.shape
    eye = jnp.eye(gi, dtype=t.dtype)
    return jnp.einsum("kirc,ij->kirjc", t, eye).reshape(nhalf, gi * r, gi * c)


def _prepare_layer(l, g_mix, w_in, w_pool, pool_scale, a_re, a_im, b_re, b_im, c_re, c_im, d_skip,
                   log_dt, w_glu, b_glu, w_out, g_xattn, g_mem, w_q, w_k, w_v, w_o, g_mlp, w_up,
                   w_down, g_final, t_max):
    G, P = a_re.shape[1:]
    gi = G // N_HALF
    ar, ai = a_re[l].astype(F32), a_im[l].astype(F32)
    dt = jnp.exp(log_dt[l].astype(F32))[:, None]
    lam_re, lam_im = ar * dt, ai * dt

    def a_pow(kk):
        kk = kk.astype(F32)[:, None, None]
        mag = jnp.exp(kk * lam_re)
        ang = kk * lam_im
        return (mag * jnp.cos(ang)).reshape(-1, G * P), (mag * jnp.sin(ang)).reshape(-1, G * P)

    steps = jnp.arange(t_max)
    pneg_re, pneg_im = a_pow(-steps)
    ppos_re, ppos_im = a_pow(steps)
    a1_re, a1_im = a_pow(jnp.ones((1,), F32))

    ab_re, ab_im = a1_re.reshape(G, P), a1_im.reshape(G, P)
    den = ar * ar + ai * ai
    coef_re = ((ab_re - 1.0) * ar + ab_im * ai) / den
    coef_im = (ab_im * ar - (ab_re - 1.0) * ai) / den
    br, bi = b_re[l].astype(F32), b_im[l].astype(F32)
    bb_re = coef_re[..., None] * br - coef_im[..., None] * bi
    bb_im = coef_re[..., None] * bi + coef_im[..., None] * br

    def to_b(t):
        return _block_diag(t.reshape(N_HALF, gi, P, -1).transpose(0, 1, 3, 2))

    def to_c(t):
        return _block_diag(t.reshape(N_HALF, gi, -1, P).transpose(0, 1, 3, 2))

    bblk = jnp.concatenate([to_b(bb_re), to_b(bb_im)], axis=-1).astype(BF16)
    cblk = jnp.concatenate([to_c(c_re[l].astype(F32)), to_c(-c_im[l].astype(F32))], axis=1).astype(BF16)

    row = lambda v: v.astype(F32).reshape(1, -1)
    return dict(
        g_mix=row(g_mix[l]), w_in=w_in[l], w_pool=w_pool[l], pool_scale=row(pool_scale[l]),
        bblk=bblk, cblk=cblk,
        pneg_re=pneg_re, pneg_im=pneg_im, ppos_re=ppos_re, ppos_im=ppos_im, a1_re=a1_re, a1_im=a1_im,
        d_skip=row(d_skip[l]), w_glu=w_glu[l], b_glu=row(b_glu[l]), w_out=w_out[l],
        g_xattn=row(g_xattn[l]), g_mem=row(g_mem[l]), w_q=w_q[l], w_k=w_k[l], w_v=w_v[l], w_o=w_o[l],
        g_mlp=row(g_mlp[l]), w_up=w_up[l], w_down=w_down[l], g_final=row(g_final))


def kernel(x_prompt, x_sample, cache_mem_k, cache_mem_v, state_pool, state_ssm_re, state_ssm_im, mem_prompt, g_mix, w_in, w_pool, pool_scale, ssm_a_re, ssm_a_im, ssm_b_re, ssm_b_im, ssm_c_re, ssm_c_im, ssm_d, ssm_log_dt, w_glu, b_glu, w_out, g_xattn, g_mem, w_q, w_k, w_v, w_o, g_mlp, w_up, w_down, g_final):
    depth = g_mix.shape[0]
    assert depth == 1 and x_prompt.shape[0] == 1, "single layer, single prompt stream"
    Bp, S, D = x_prompt.shape
    Bs, Ts, _ = x_sample.shape
    n_mem = mem_prompt.shape[1]
    G, P = ssm_a_re.shape[1:]
    assert S % PROMPT_BLOCK == 0 and PROMPT_BLOCK % PROMPT_CHUNK == 0 and Ts <= PROMPT_CHUNK

    l = 0
    p = _prepare_layer(l, g_mix, w_in, w_pool, pool_scale, ssm_a_re, ssm_a_im, ssm_b_re, ssm_b_im,
                       ssm_c_re, ssm_c_im, ssm_d, ssm_log_dt, w_glu, b_glu, w_out, g_xattn, g_mem,
                       w_q, w_k, w_v, w_o, g_mlp, w_up, w_down, g_final, PROMPT_CHUNK)

    xp1, xs1, hist_p, hre_p, him_p, hist_s, hre_s, him_s = _mixer(
        x_prompt[0], x_sample.reshape(Bs * Ts, D), state_pool[l],
        state_ssm_re[l].reshape(Bs, G * P), state_ssm_im[l].reshape(Bs, G * P), p, Bs, Ts)
    xp2, xs2, mk, mv = _xattn(xp1, xs1, mem_prompt[0], cache_mem_k[l:l + 1], cache_mem_v[l:l + 1], p, Bs, Ts)
    y_prompt, y_sample = _mlp(xp2, xs2, p)

    hd = D // MEM_HEADS
    return (y_prompt[None], y_sample.reshape(Bs, Ts, D),
            mk.reshape(1, Bp, n_mem, MEM_HEADS, hd), mv.reshape(1, Bp, n_mem, MEM_HEADS, hd),
            hist_p[1:].reshape(1, Bp, POOL_HIST, -1),
            hre_p.reshape(1, Bp, G, P), him_p.reshape(1, Bp, G, P),
            hist_s[None], hre_s.reshape(1, Bs, G, P), him_s.reshape(1, Bs, G, P))
```

```python
import functools

import jax
import jax.numpy as jnp
from jax import lax
from jax.experimental import pallas as pl
from jax.experimental.pallas import tpu as pltpu

F32 = jnp.float32
BF16 = jnp.bfloat16

EPS = 1e-6
PAST_LEN = 1024
POOL_WINDOWS = (2, 4, 8, 16)
POOL_HIST = max(POOL_WINDOWS) - 1
HIST_ROWS = 16
MEM_HEADS = 4
N_HALF = 2
S5_TILE = 256

PROMPT_BLOCK = 512
PROMPT_CHUNK = 128
FF_CHUNK = 1024
VMEM_LIMIT = 56 * 1024 * 1024


def _rmsnorm(x, g):
    return x * lax.rsqrt(jnp.mean(x * x, axis=-1, keepdims=True) + EPS) * g


def _dot(a, b):
    return jnp.dot(a.astype(BF16), b.astype(BF16), preferred_element_type=F32)


def _pool_windows(zbuf, nrows, pos, r):
    gw = r["w_pool"].shape[-1]
    outs = []
    for g, w in enumerate(POOL_WINDOWS):
        lanes = slice(g * gw, (g + 1) * gw)
        win = zbuf[pl.ds(0, HIST_ROWS + nrows), lanes]
        k = 1
        while k < w:
            win = win + pltpu.roll(win, k, 0)
            k *= 2
        cur = zbuf[pl.ds(HIST_ROWS, nrows), lanes]
        cnt = jnp.minimum(pos + 1, w).astype(F32)
        pooled = win[HIST_ROWS:, :] / cnt - cur
        outs.append(_dot(pooled, r["w_pool"][g]))
    return jnp.concatenate(outs, axis=-1) * r["pool_scale"][...]


def _cmul(ar, ai, br, bi):
    return ar * br - ai * bi, ar * bi + ai * br


def _s5_scan(u, n_chunks, T, end_row, h_carry, h_rows, r, hout_re_ref, hout_im_ref):
    hw = r["bblk"].shape[1]
    nh = r["bblk"].shape[2] // 2
    u_bf = u.astype(BF16)
    r_i = lax.broadcasted_iota(jnp.int32, (T, T), 0)
    c_i = lax.broadcasted_iota(jnp.int32, (T, T), 1)
    ltri = jnp.where(r_i >= c_i, 1.0, 0.0).astype(BF16)

    ys = []
    for k in range(N_HALF):
        uk = u_bf[:, k * hw:(k + 1) * hw]
        yk = None
        for j in range(nh // S5_TILE):
            re_c = slice(j * S5_TILE, (j + 1) * S5_TILE)
            im_c = slice(nh + j * S5_TILE, nh + (j + 1) * S5_TILE)
            st = slice(k * nh + j * S5_TILE, k * nh + (j + 1) * S5_TILE)
            bu_re = _dot(uk, r["bblk"][k, :, re_c])
            bu_im = _dot(uk, r["bblk"][k, :, im_c])
            if h_rows is None:
                carry = (h_carry[0][:, st], h_carry[1][:, st])
            h_re, h_im = [], []
            for c in range(n_chunks):
                rows = slice(c * T, (c + 1) * T)
                xr, xi = _cmul(bu_re[rows].astype(BF16), bu_im[rows].astype(BF16),
                               r["pneg_re"][0:T, st], r["pneg_im"][0:T, st])
                wr = _dot(ltri, xr)
                wi = _dot(ltri, xi)
                if h_rows is not None:
                    carry = (h_rows[0][pl.ds(c, 1), st], h_rows[1][pl.ds(c, 1), st])
                cr, ci = _cmul(r["a1_re"][:, st], r["a1_im"][:, st], carry[0], carry[1])
                wr, wi = wr + cr, wi + ci
                sr, si = _cmul(r["ppos_re"][0:T, st], r["ppos_im"][0:T, st], wr.astype(BF16), wi.astype(BF16))
                carry = _cmul(r["pend_re"][end_row:end_row + 1, st], r["pend_im"][end_row:end_row + 1, st],
                              wr[T - 1:T, :], wi[T - 1:T, :])
                if h_rows is not None:
                    hout_re_ref[pl.ds(c, 1), st] = carry[0]
                    hout_im_ref[pl.ds(c, 1), st] = carry[1]
                h_re.append(sr)
                h_im.append(si)
            if h_rows is None:
                hout_re_ref[:, st] = carry[0]
                hout_im_ref[:, st] = carry[1]
            part = (_dot(jnp.concatenate(h_re, axis=0), r["cblk"][k, re_c, :])
                    + _dot(jnp.concatenate(h_im, axis=0), r["cblk"][k, im_c, :]))
            yk = part if yk is None else yk + part
        ys.append(yk)
    return jnp.concatenate(ys, axis=-1)


def _mix_tail(x, y_pool, u, y_state, r):
    y = jax.nn.gelu(y_state + r["d_skip"][...] * u)
    y = y * jax.nn.sigmoid(_dot(y, r["w_glu"][...]) + r["b_glu"][...])
    ycat = jnp.concatenate([y_pool.astype(BF16), y.astype(BF16)], axis=-1)
    return x + _dot(ycat, r["w_out"][...])


_MIXER_IN = ("xp", "xs", "hist_in", "h0_re", "h0_im",
             "g_mix", "w_in", "w_pool", "pool_scale", "bblk", "cblk",
             "pneg_re", "pneg_im", "ppos_re", "ppos_im", "pend_re", "pend_im", "a1_re", "a1_im",
             "d_skip", "w_glu", "b_glu", "w_out")
_MIXER_OUT = ("outp", "outs", "histp", "hrep", "himp", "hists", "hres", "hims")
_MIXER_BF16 = ("w_in", "w_pool", "w_glu", "w_out")


def _mixer_kernel(*refs, n_prompt, n_streams, Tp, Ts):
    r = dict(zip(_MIXER_IN + _MIXER_OUT, refs))
    zbuf, ypool_ref = refs[len(_MIXER_IN) + len(_MIXER_OUT):][:2]
    bf16_weights = dict(zip(_MIXER_BF16, refs[len(_MIXER_IN) + len(_MIXER_OUT) + 2:]))
    i = pl.program_id(0)
    ts = r["xp"].shape[0]
    d_pool = zbuf.shape[1]

    @pl.when(i == 0)
    def _():
        zbuf[0:HIST_ROWS, :] = jnp.zeros((HIST_ROWS, d_pool), F32)
        r["hrep"][...] = jnp.zeros(r["hrep"].shape, F32)
        r["himp"][...] = jnp.zeros(r["himp"].shape, F32)
        for name, ref in bf16_weights.items():
            ref[...] = r[name][...].astype(BF16)

    r.update(bf16_weights)

    @pl.when(i < n_prompt)
    def _():
        x = r["xp"][...]
        z = _dot(_rmsnorm(x, r["g_mix"][...]), r["w_in"][...])
        zbuf[HIST_ROWS:HIST_ROWS + ts, :] = z[:, :d_pool]
        u = z[:, d_pool:]
        pos = i * ts + lax.broadcasted_iota(jnp.int32, (ts, 1), 0)
        y_pool = _pool_windows(zbuf, ts, pos, r)
        y_state = _s5_scan(u, ts // Tp, Tp, 0, (r["hrep"][...], r["himp"][...]), None, r, r["hrep"], r["himp"])
        r["outp"][...] = _mix_tail(x, y_pool, u, y_state, r)
        tail = zbuf[ts:ts + HIST_ROWS, :]
        zbuf[0:HIST_ROWS, :] = tail
        r["histp"][...] = tail

    @pl.when(i == n_prompt)
    def _():
        ext = HIST_ROWS + Ts
        x = r["xs"][...]
        z = _dot(_rmsnorm(x, r["g_mix"][...]), r["w_in"][...])
        u = z[:, d_pool:]
        for b in range(n_streams):
            zbuf[b * ext:b * ext + 1, :] = jnp.zeros((1, d_pool), F32)
            zbuf[b * ext + 1:b * ext + HIST_ROWS, :] = r["hist_in"][b]
            zbuf[b * ext + HIST_ROWS:(b + 1) * ext, :] = z[b * Ts:(b + 1) * Ts, :d_pool]
            r["hists"][b] = zbuf[(b + 1) * ext - POOL_HIST:(b + 1) * ext, :]
        nrows = n_streams * ext - HIST_ROWS
        ridx = lax.broadcasted_iota(jnp.int32, (nrows, 1), 0)
        pos = PAST_LEN + lax.rem(ridx, ext)
        y_all = _pool_windows(zbuf, nrows, pos, r)
        for b in range(n_streams):
            ypool_ref[b * Ts:(b + 1) * Ts, :] = y_all[b * ext:b * ext + Ts, :]
        y_state = _s5_scan(u, n_streams, Ts, 1, None, (r["h0_re"], r["h0_im"]), r, r["hres"], r["hims"])
        r["outs"][...] = _mix_tail(x, ypool_ref[...], u, y_state, r)


def _const_spec(shape):
    nd = len(shape)
    return pl.BlockSpec(shape, lambda *_: (0,) * nd)


def _weight_spec(shape):
    nd = len(shape)
    return pl.BlockSpec(shape, lambda *_: (0,) * nd, pipeline_mode=pl.Buffered(1))


def _prompt_spec(ts, d, n_prompt):
    return pl.BlockSpec((ts, d), lambda i: (jnp.minimum(i, n_prompt - 1), 0))


def _mixer(xp, xs, hist_in, h0_re, h0_im, p, n_streams, Ts):
    S, D = xp.shape
    rows_s = xs.shape[0]
    ts = PROMPT_BLOCK
    n_prompt = S // ts
    d_pool = hist_in.shape[-1]
    n_state = p["a1_re"].shape[-1]
    assert n_streams * (HIST_ROWS + Ts) <= HIST_ROWS + ts and rows_s <= ts
    args = dict(p, xp=xp, xs=xs, hist_in=hist_in, h0_re=h0_re, h0_im=h0_im)
    ins = [args[n] for n in _MIXER_IN]
    in_specs = [_prompt_spec(ts, D, n_prompt)] + [_weight_spec(a.shape) for a in ins[1:]]
    out_shapes = dict(
        outp=(S, D), outs=(rows_s, D), histp=(HIST_ROWS, d_pool), hrep=(1, n_state), himp=(1, n_state),
        hists=(n_streams, POOL_HIST, d_pool), hres=(n_streams, n_state), hims=(n_streams, n_state))
    out_specs = [_prompt_spec(ts, D, n_prompt)] + [_const_spec(out_shapes[n]) for n in _MIXER_OUT[1:]]
    return pl.pallas_call(
        functools.partial(_mixer_kernel, n_prompt=n_prompt, n_streams=n_streams, Tp=PROMPT_CHUNK, Ts=Ts),
        grid=(n_prompt + 1,),
        in_specs=in_specs,
        out_specs=out_specs,
        out_shape=[jax.ShapeDtypeStruct(out_shapes[n], F32) for n in _MIXER_OUT],
        scratch_shapes=[pltpu.VMEM((HIST_ROWS + ts, d_pool), F32),
                        pltpu.VMEM((rows_s, d_pool), F32)]
        + [pltpu.VMEM(args[n].shape, BF16) for n in _MIXER_BF16],
        compiler_params=pltpu.CompilerParams(dimension_semantics=("arbitrary",),
                                             vmem_limit_bytes=VMEM_LIMIT),
        name="mixer",
    )(*ins)


def _softmax_rows(s):
    e = jnp.exp(s - jnp.max(s, axis=-1, keepdims=True))
    return e / jnp.sum(e, axis=-1, keepdims=True)


def _attend(q_heads, k_heads, v_heads, scale):
    outs = []
    for q, k, v in zip(q_heads, k_heads, v_heads):
        s = lax.dot_general(q, k.astype(BF16), (((1,), (1,)), ((), ())), preferred_element_type=F32) * scale
        outs.append(_dot(_softmax_rows(s), v))
    return outs


_XATTN_IN = ("xp", "xs", "mem", "kc", "vc", "g_mem", "w_k", "w_v", "g_xattn", "w_q", "w_o")
_XATTN_OUT = ("outp", "outs", "mk", "mv")


def _xattn_kernel(*refs, n_prompt, n_streams, Ts, hpu, scale):
    r = dict(zip(_XATTN_IN + _XATTN_OUT, refs))
    kp_ref, vp_ref, qs_ref, os_ref, kbuf, vbuf, sem = refs[len(_XATTN_IN) + len(_XATTN_OUT):]
    i = pl.program_id(0)
    hd = kp_ref.shape[-1] // MEM_HEADS
    ups = MEM_HEADS // hpu

    def unit_copies(unit, hg):
        b, slot = unit // ups, lax.rem(unit, 2)
        return [pltpu.make_async_copy(src.at[0, b, :, hg * hpu + e, :], dst.at[slot, e], sem.at[slot, j * hpu + e])
                for j, (src, dst) in enumerate(((r["kc"], kbuf), (r["vc"], vbuf))) for e in range(hpu)]

    def for_unit(unit, fn):
        for hg in range(ups):
            @pl.when(lax.rem(unit, ups) == hg)
            def _():
                for cp in unit_copies(unit, hg):
                    fn(cp)

    @pl.when(i == 0)
    def _():
        for_unit(i, lambda cp: cp.start())
        m = _rmsnorm(r["mem"][...], r["g_mem"][...]).astype(BF16)
        k = _dot(m, r["w_k"][...])
        v = _dot(m, r["w_v"][...])
        r["mk"][...] = k
        r["mv"][...] = v
        kp_ref[...] = k.astype(BF16)
        vp_ref[...] = v.astype(BF16)
        q = _dot(_rmsnorm(r["xs"][...], r["g_xattn"][...]), r["w_q"][...]).astype(BF16)
        for h in range(MEM_HEADS):
            qs_ref[h] = q[:, h * hd:(h + 1) * hd]

    @pl.when(i < n_prompt)
    def _():
        @pl.when(i + 1 < n_prompt)
        def _():
            for_unit(i + 1, lambda cp: cp.start())

        for_unit(i, lambda cp: cp.wait())

        x = r["xp"][...]
        q = _dot(_rmsnorm(x, r["g_xattn"][...]), r["w_q"][...]).astype(BF16)
        o = _attend([q[:, h * hd:(h + 1) * hd] for h in range(MEM_HEADS)],
                    [kp_ref[:, h * hd:(h + 1) * hd] for h in range(MEM_HEADS)],
                    [vp_ref[:, h * hd:(h + 1) * hd] for h in range(MEM_HEADS)], scale)
        r["outp"][...] = x + _dot(jnp.concatenate(o, axis=-1), r["w_o"][...])

        b, hg, slot = i // ups, lax.rem(i, ups), lax.rem(i, 2)
        rows = pl.ds(pl.multiple_of(b * Ts, Ts), Ts)
        heads = [hg * hpu + e for e in range(hpu)]
        o_s = _attend([qs_ref[h, rows, :] for h in heads], [kbuf[slot, e] for e in range(hpu)],
                      [vbuf[slot, e] for e in range(hpu)], scale)
        for h, o_h in zip(heads, o_s):
            os_ref[h, rows, :] = o_h.astype(BF16)

    @pl.when(i == n_prompt)
    def _():
        o = jnp.concatenate([os_ref[h] for h in range(MEM_HEADS)], axis=-1)
        r["outs"][...] = r["xs"][...] + _dot(o, r["w_o"][...])


def _xattn(xp, xs, mem, kc, vc, p, n_streams, Ts):
    S, D = xp.shape
    rows_s = xs.shape[0]
    ts = PROMPT_BLOCK
    n_prompt = S // ts
    n_mem = mem.shape[0]
    hd = D // MEM_HEADS
    hpu = n_streams * MEM_HEADS // n_prompt
    assert hpu * n_prompt == n_streams * MEM_HEADS and MEM_HEADS % hpu == 0
    args = dict(p, xp=xp, xs=xs, mem=mem, kc=kc, vc=vc)
    ins = [args[n] for n in _XATTN_IN]
    cache_spec = pl.BlockSpec(memory_space=pl.ANY)
    in_specs = [_prompt_spec(ts, D, n_prompt)]
    for n in _XATTN_IN[1:]:
        in_specs.append(cache_spec if n in ("kc", "vc") else _weight_spec(args[n].shape))
    out_shapes = dict(outp=(S, D), outs=(rows_s, D), mk=(n_mem, D), mv=(n_mem, D))
    out_specs = [_prompt_spec(ts, D, n_prompt)] + [_const_spec(out_shapes[n]) for n in _XATTN_OUT[1:]]
    return pl.pallas_call(
        functools.partial(_xattn_kernel, n_prompt=n_prompt, n_streams=n_streams, Ts=Ts, hpu=hpu, scale=hd ** -0.5),
        grid=(n_prompt + 1,),
        in_specs=in_specs,
        out_specs=out_specs,
        out_shape=[jax.ShapeDtypeStruct(out_shapes[n], F32) for n in _XATTN_OUT],
        scratch_shapes=[pltpu.VMEM((n_mem, D), BF16), pltpu.VMEM((n_mem, D), BF16),
                        pltpu.VMEM((MEM_HEADS, rows_s, hd), BF16), pltpu.VMEM((MEM_HEADS, rows_s, hd), BF16),
                        pltpu.VMEM((2, hpu, n_mem, hd), F32), pltpu.VMEM((2, hpu, n_mem, hd), F32),
                        pltpu.SemaphoreType.DMA((2, 2 * hpu))],
        compiler_params=pltpu.CompilerParams(dimension_semantics=("arbitrary",),
                                             vmem_limit_bytes=VMEM_LIMIT),
        name="xattn",
    )(*ins)


def _mlp_rows(x, g_ref, wup_ref, wdown_ref, gfin_ref):
    hm = _rmsnorm(x, g_ref[...]).astype(BF16)
    acc = x
    for j in range(wup_ref.shape[1] // FF_CHUNK):
        cols = slice(j * FF_CHUNK, (j + 1) * FF_CHUNK)
        up = _dot(hm, wup_ref[:, cols])
        acc = acc + _dot(jnp.square(jnp.maximum(up, 0.0)), wdown_ref[cols, :])
    return _rmsnorm(acc, gfin_ref[...])


def _mlp_kernel(xp_ref, xs_ref, g_ref, wup_ref, wdown_ref, gfin_ref, outp_ref, outs_ref, *, n_prompt):
    i = pl.program_id(0)

    @pl.when(i < n_prompt)
    def _():
        outp_ref[...] = _mlp_rows(xp_ref[...], g_ref, wup_ref, wdown_ref, gfin_ref)

    @pl.when(i == n_prompt)
    def _():
        outs_ref[...] = _mlp_rows(xs_ref[...], g_ref, wup_ref, wdown_ref, gfin_ref)


def _mlp(xp, xs, p):
    S, D = xp.shape
    ts = PROMPT_BLOCK
    n_prompt = S // ts
    weights = (p["g_mlp"], p["w_up"], p["w_down"], p["g_final"])
    return pl.pallas_call(
        functools.partial(_mlp_kernel, n_prompt=n_prompt),
        grid=(n_prompt + 1,),
        in_specs=[_prompt_spec(ts, D, n_prompt), _weight_spec(xs.shape)] + [_weight_spec(w.shape) for w in weights],
        out_specs=[_prompt_spec(ts, D, n_prompt), _const_spec(xs.shape)],
        out_shape=[jax.ShapeDtypeStruct(xp.shape, F32), jax.ShapeDtypeStruct(xs.shape, F32)],
        compiler_params=pltpu.CompilerParams(dimension_semantics=("arbitrary",),
                                             vmem_limit_bytes=VMEM_LIMIT),
        name="mlp",
    )(xp, xs, *weights)


def _block_diag(t):
    nhalf, gi, r, c = t.shape
    eye = jnp.eye(gi, dtype=t.dtype)
    return jnp.einsum("kirc,ij->kirjc", t, eye).reshape(nhalf, gi * r, gi * c)


def _prepare_layer(l, g_mix, w_in, w_pool, pool_scale, a_re, a_im, b_re, b_im, c_re, c_im, d_skip,
                   log_dt, w_glu, b_glu, w_out, g_xattn, g_mem, w_q, w_k, w_v, w_o, g_mlp, w_up,
                   w_down, g_final, t_max, t_ends):
    G, P = a_re.shape[1:]
    gi = G // N_HALF
    ar, ai = a_re[l].astype(F32), a_im[l].astype(F32)
    dt = jnp.exp(log_dt[l].astype(F32))[:, None]
    lam_re, lam_im = ar * dt, ai * dt

    def a_pow(kk):
        kk = kk.astype(F32)[:, None, None]
        mag = jnp.exp(kk * lam_re)
        ang = kk * lam_im
        return (mag * jnp.cos(ang)).reshape(-1, G * P), (mag * jnp.sin(ang)).reshape(-1, G * P)

    steps = jnp.arange(t_max)
    pneg_re, pneg_im = a_pow(-steps)
    ppos_re, ppos_im = a_pow(steps)
    a1_re, a1_im = a_pow(jnp.ones((1,), F32))
    pend_re, pend_im = a_pow(jnp.asarray(t_ends) - 1)

    ab_re, ab_im = a1_re.reshape(G, P), a1_im.reshape(G, P)
    den = ar * ar + ai * ai
    coef_re = ((ab_re - 1.0) * ar + ab_im * ai) / den
    coef_im = (ab_im * ar - (ab_re - 1.0) * ai) / den
    br, bi = b_re[l].astype(F32), b_im[l].astype(F32)
    bb_re = coef_re[..., None] * br - coef_im[..., None] * bi
    bb_im = coef_re[..., None] * bi + coef_im[..., None] * br

    def to_b(t):
        return _block_diag(t.reshape(N_HALF, gi, P, -1).transpose(0, 1, 3, 2))

    def to_c(t):
        return _block_diag(t.reshape(N_HALF, gi, -1, P).transpose(0, 1, 3, 2))

    bblk = jnp.concatenate([to_b(bb_re), to_b(bb_im)], axis=-1).astype(BF16)
    cblk = jnp.concatenate([to_c(c_re[l].astype(F32)), to_c(-c_im[l].astype(F32))], axis=1).astype(BF16)

    row = lambda v: v.astype(F32).reshape(1, -1)
    return dict(
        g_mix=row(g_mix[l]), w_in=w_in[l], w_pool=w_pool[l], pool_scale=row(pool_scale[l]),
        bblk=bblk, cblk=cblk,
        pneg_re=pneg_re.astype(BF16), pneg_im=pneg_im.astype(BF16),
        ppos_re=ppos_re.astype(BF16), ppos_im=ppos_im.astype(BF16),
        pend_re=pend_re, pend_im=pend_im, a1_re=a1_re, a1_im=a1_im,
        d_skip=row(d_skip[l]), w_glu=w_glu[l], b_glu=row(b_glu[l]), w_out=w_out[l],
        g_xattn=row(g_xattn[l]), g_mem=row(g_mem[l]), w_q=w_q[l], w_k=w_k[l], w_v=w_v[l], w_o=w_o[l],
        g_mlp=row(g_mlp[l]), w_up=w_up[l], w_down=w_down[l], g_final=row(g_final))


def kernel(x_prompt, x_sample, cache_mem_k, cache_mem_v, state_pool, state_ssm_re, state_ssm_im, mem_prompt, g_mix, w_in, w_pool, pool_scale, ssm_a_re, ssm_a_im, ssm_b_re, ssm_b_im, ssm_c_re, ssm_c_im, ssm_d, ssm_log_dt, w_glu, b_glu, w_out, g_xattn, g_mem, w_q, w_k, w_v, w_o, g_mlp, w_up, w_down, g_final):
    depth = g_mix.shape[0]
    assert depth == 1 and x_prompt.shape[0] == 1, "single layer, single prompt stream"
    Bp, S, D = x_prompt.shape
    Bs, Ts, _ = x_sample.shape
    n_mem = mem_prompt.shape[1]
    G, P = ssm_a_re.shape[1:]
    assert S % PROMPT_BLOCK == 0 and PROMPT_BLOCK % PROMPT_CHUNK == 0 and Ts <= PROMPT_CHUNK

    l = 0
    p = _prepare_layer(l, g_mix, w_in, w_pool, pool_scale, ssm_a_re, ssm_a_im, ssm_b_re, ssm_b_im,
                       ssm_c_re, ssm_c_im, ssm_d, ssm_log_dt, w_glu, b_glu, w_out, g_xattn, g_mem,
                       w_q, w_k, w_v, w_o, g_mlp, w_up, w_down, g_final, PROMPT_CHUNK, (PROMPT_CHUNK, Ts))

    xp1, xs1, hist_p, hre_p, him_p, hist_s, hre_s, him_s = _mixer(
        x_prompt[0], x_sample.reshape(Bs * Ts, D), state_pool[l],
        state_ssm_re[l].reshape(Bs, G * P), state_ssm_im[l].reshape(Bs, G * P), p, Bs, Ts)
    xp2, xs2, mk, mv = _xattn(xp1, xs1, mem_prompt[0], cache_mem_k[l:l + 1], cache_mem_v[l:l + 1], p, Bs, Ts)
    y_prompt, y_sample = _mlp(xp2, xs2, p)

    hd = D // MEM_HEADS
    return (y_prompt[None], y_sample.reshape(Bs, Ts, D),
            mk.reshape(1, Bp, n_mem, MEM_HEADS, hd), mv.reshape(1, Bp, n_mem, MEM_HEADS, hd),
            hist_p[1:].reshape(1, Bp, POOL_HIST, -1),
            hre_p.reshape(1, Bp, G, P), him_p.reshape(1, Bp, G, P),
            hist_s[None], hre_s.reshape(1, Bs, G, P), him_s.reshape(1, Bs, G, P))
```

```python
import functools

import jax
import jax.numpy as jnp
from jax import lax
from jax.experimental import pallas as pl
from jax.experimental.pallas import tpu as pltpu

F32 = jnp.float32
BF16 = jnp.bfloat16

EPS = 1e-6
PAST_LEN = 1024
POOL_WINDOWS = (2, 4, 8, 16)
POOL_HIST = max(POOL_WINDOWS) - 1
HIST_ROWS = 16
MEM_HEADS = 4
N_HALF = 2
S5_TILE = 256

PROMPT_BLOCK = 512
PROMPT_CHUNK = 128
FF_CHUNK = 1024
VMEM_LIMIT = 56 * 1024 * 1024


def _rmsnorm(x, g):
    return x * lax.rsqrt(jnp.mean(x * x, axis=-1, keepdims=True) + EPS) * g


def _dot(a, b):
    return jnp.dot(a.astype(BF16), b.astype(BF16), preferred_element_type=F32)


def _pool_windows(zbuf, nrows, pos, r):
    gw = r["w_pool"].shape[-1]
    outs = []
    for g, w in enumerate(POOL_WINDOWS):
        lanes = slice(g * gw, (g + 1) * gw)
        win = zbuf[pl.ds(0, HIST_ROWS + nrows), lanes]
        k = 1
        while k < w:
            win = win + pltpu.roll(win, k, 0)
            k *= 2
        cur = zbuf[pl.ds(HIST_ROWS, nrows), lanes]
        cnt = jnp.minimum(pos + 1, w).astype(F32)
        pooled = win[HIST_ROWS:, :] / cnt - cur
        outs.append(_dot(pooled, r["w_pool"][g]))
    return jnp.concatenate(outs, axis=-1) * r["pool_scale"][...]


def _cmul(ar, ai, br, bi):
    return ar * br - ai * bi, ar * bi + ai * br


def _s5_scan(u, n_chunks, T, end_row, h_carry, h_rows, r, hout_re_ref, hout_im_ref):
    hw = r["bblk"].shape[1]
    nh = r["bblk"].shape[2] // 2
    u_bf = u.astype(BF16)
    r_i = lax.broadcasted_iota(jnp.int32, (T, T), 0)
    c_i = lax.broadcasted_iota(jnp.int32, (T, T), 1)
    ltri = jnp.where(r_i >= c_i, 1.0, 0.0).astype(BF16)

    ys = []
    for k in range(N_HALF):
        uk = u_bf[:, k * hw:(k + 1) * hw]
        yk = None
        for j in range(nh // S5_TILE):
            re_c = slice(j * S5_TILE, (j + 1) * S5_TILE)
            im_c = slice(nh + j * S5_TILE, nh + (j + 1) * S5_TILE)
            st = slice(k * nh + j * S5_TILE, k * nh + (j + 1) * S5_TILE)
            bu_re = _dot(uk, r["bblk"][k, :, re_c])
            bu_im = _dot(uk, r["bblk"][k, :, im_c])
            if h_rows is None:
                carry = (h_carry[0][:, st], h_carry[1][:, st])
            h_re, h_im = [], []
            for c in range(n_chunks):
                rows = slice(c * T, (c + 1) * T)
                xr, xi = _cmul(bu_re[rows].astype(BF16), bu_im[rows].astype(BF16),
                               r["pneg_re"][0:T, st], r["pneg_im"][0:T, st])
                wr = _dot(ltri, xr)
                wi = _dot(ltri, xi)
                if h_rows is not None:
                    carry = (h_rows[0][pl.ds(c, 1), st], h_rows[1][pl.ds(c, 1), st])
                cr, ci = _cmul(r["a1_re"][:, st], r["a1_im"][:, st], carry[0], carry[1])
                wr, wi = wr + cr, wi + ci
                sr, si = _cmul(r["ppos_re"][0:T, st], r["ppos_im"][0:T, st], wr.astype(BF16), wi.astype(BF16))
                carry = _cmul(r["pend_re"][end_row:end_row + 1, st], r["pend_im"][end_row:end_row + 1, st],
                              wr[T - 1:T, :], wi[T - 1:T, :])
                if h_rows is not None:
                    hout_re_ref[pl.ds(c, 1), st] = carry[0]
                    hout_im_ref[pl.ds(c, 1), st] = carry[1]
                h_re.append(sr)
                h_im.append(si)
            if h_rows is None:
                hout_re_ref[:, st] = carry[0]
                hout_im_ref[:, st] = carry[1]
            part = (_dot(jnp.concatenate(h_re, axis=0), r["cblk"][k, re_c, :])
                    + _dot(jnp.concatenate(h_im, axis=0), r["cblk"][k, im_c, :]))
            yk = part if yk is None else yk + part
        ys.append(yk)
    return jnp.concatenate(ys, axis=-1)


def _mix_tail(x, y_pool, u, y_state, r):
    y = jax.nn.gelu(y_state + r["d_skip"][...] * u)
    y = y * jax.nn.sigmoid(_dot(y, r["w_glu"][...]) + r["b_glu"][...])
    ycat = jnp.concatenate([y_pool.astype(BF16), y.astype(BF16)], axis=-1)
    return x + _dot(ycat, r["w_out"][...])


_MIXER_IN = ("xp", "xs", "hist_in", "h0_re", "h0_im",
             "g_mix", "w_in", "w_pool", "pool_scale", "bb_re", "bb_im", "cc_re", "cc_im", "a1_re", "a1_im",
             "d_skip", "w_glu", "b_glu", "w_out")
_MIXER_OUT = ("outp", "outs", "histp", "hrep", "himp", "hists", "hres", "hims")
_MIXER_BF16 = ("w_in", "w_pool", "w_glu", "w_out")
_MIXER_GEN = ("bblk", "cblk", "pneg_re", "pneg_im", "ppos_re", "ppos_im", "pend_re", "pend_im")


def _power_table(re1, im1, T):
    SUB = 16
    assert T % SUB == 0
    n = re1.shape[-1]
    t = lax.broadcasted_iota(jnp.int32, (SUB, 1), 0)
    lo_re, lo_im = jnp.ones((SUB, n), F32), jnp.zeros((SUB, n), F32)
    cr, ci = re1, im1
    k = 1
    while k < SUB:
        bit = (t & k) != 0
        lo_re, lo_im = _cmul(lo_re, lo_im, jnp.where(bit, cr, 1.0), jnp.where(bit, ci, 0.0))
        cr, ci = _cmul(cr, ci, cr, ci)
        k *= 2
    hi_re, hi_im = [jnp.ones((1, n), F32)], [jnp.zeros((1, n), F32)]
    for _ in range(T // SUB - 1):
        nr, ni = _cmul(hi_re[-1], hi_im[-1], cr, ci)
        hi_re.append(nr)
        hi_im.append(ni)
    rep = lambda rows: jnp.concatenate([jnp.broadcast_to(v, (SUB, n)) for v in rows], axis=0)
    tile = lambda v: jnp.concatenate([v] * (T // SUB), axis=0)
    return _cmul(tile(lo_re), tile(lo_im), rep(hi_re), rep(hi_im))


def _expand_s5_params(r, gen, T, t_ends):
    are, aim = r["a1_re"][...], r["a1_im"][...]
    pr, pi = _power_table(are, aim, T)
    gen["ppos_re"][...] = pr.astype(BF16)
    gen["ppos_im"][...] = pi.astype(BF16)
    for j, te in enumerate(t_ends):
        gen["pend_re"][j:j + 1, :] = pr[te - 1:te, :]
        gen["pend_im"][j:j + 1, :] = pi[te - 1:te, :]
    den = are * are + aim * aim
    nr, ni = _power_table(are / den, -aim / den, T)
    gen["pneg_re"][...] = nr.astype(BF16)
    gen["pneg_im"][...] = ni.astype(BF16)

    n_half, rows_b, H = r["bb_re"].shape
    _, rows_c, P = r["cc_re"].shape
    nh = rows_b

    def spread(x, width):
        w = x.shape[1]
        sel = lax.rem(lax.broadcasted_iota(jnp.int32, (w, width), 1), w) == lax.broadcasted_iota(jnp.int32, (w, width), 0)
        return _dot(x, jnp.where(sel, 1.0, 0.0))

    def same_group(shape, row_size, col_size):
        return (lax.div(lax.broadcasted_iota(jnp.int32, shape, 0), row_size)
                == lax.div(lax.broadcasted_iota(jnp.int32, shape, 1), col_size))

    for k in range(n_half):
        for c, (b_src, c_src, sign) in enumerate(((r["bb_re"], r["cc_re"], 1.0), (r["bb_im"], r["cc_im"], -1.0))):
            yb = spread(b_src[k], rows_c)
            yb = jnp.where(same_group(yb.shape, P, H), yb, 0.0)
            gen["bblk"][k, :, c * nh:(c + 1) * nh] = yb.T.astype(BF16)
            yc = spread(c_src[k], rows_b)
            yc = jnp.where(same_group(yc.shape, H, P), sign * yc, 0.0)
            gen["cblk"][k, c * nh:(c + 1) * nh, :] = yc.T.astype(BF16)


def _mixer_kernel(*refs, n_prompt, n_streams, Tp, Ts):
    r = dict(zip(_MIXER_IN + _MIXER_OUT, refs))
    zbuf, ypool_ref = refs[len(_MIXER_IN) + len(_MIXER_OUT):][:2]
    n_fixed = len(_MIXER_IN) + len(_MIXER_OUT) + 2
    bf16_weights = dict(zip(_MIXER_BF16, refs[n_fixed:]))
    gen = dict(zip(_MIXER_GEN, refs[n_fixed + len(_MIXER_BF16):]))
    i = pl.program_id(0)
    ts = r["xp"].shape[0]
    d_pool = zbuf.shape[1]

    @pl.when(i == 0)
    def _():
        zbuf[0:HIST_ROWS, :] = jnp.zeros((HIST_ROWS, d_pool), F32)
        r["hrep"][...] = jnp.zeros(r["hrep"].shape, F32)
        r["himp"][...] = jnp.zeros(r["himp"].shape, F32)
        for name, ref in bf16_weights.items():
            ref[...] = r[name][...].astype(BF16)
        _expand_s5_params(r, gen, Tp, (Tp, Ts))

    r.update(bf16_weights)
    r.update(gen)

    @pl.when(i < n_prompt)
    def _():
        x = r["xp"][...]
        z = _dot(_rmsnorm(x, r["g_mix"][...]), r["w_in"][...])
        zbuf[HIST_ROWS:HIST_ROWS + ts, :] = z[:, :d_pool]
        u = z[:, d_pool:]
        pos = i * ts + lax.broadcasted_iota(jnp.int32, (ts, 1), 0)
        y_pool = _pool_windows(zbuf, ts, pos, r)
        y_state = _s5_scan(u, ts // Tp, Tp, 0, (r["hrep"][...], r["himp"][...]), None, r, r["hrep"], r["himp"])
        r["outp"][...] = _mix_tail(x, y_pool, u, y_state, r)
        tail = zbuf[ts:ts + HIST_ROWS, :]
        zbuf[0:HIST_ROWS, :] = tail
        r["histp"][...] = tail

    @pl.when(i == n_prompt)
    def _():
        ext = HIST_ROWS + Ts
        x = r["xs"][...]
        z = _dot(_rmsnorm(x, r["g_mix"][...]), r["w_in"][...])
        u = z[:, d_pool:]
        for b in range(n_streams):
            zbuf[b * ext:b * ext + 1, :] = jnp.zeros((1, d_pool), F32)
            zbuf[b * ext + 1:b * ext + HIST_ROWS, :] = r["hist_in"][b]
            zbuf[b * ext + HIST_ROWS:(b + 1) * ext, :] = z[b * Ts:(b + 1) * Ts, :d_pool]
            r["hists"][b] = zbuf[(b + 1) * ext - POOL_HIST:(b + 1) * ext, :]
        nrows = n_streams * ext - HIST_ROWS
        ridx = lax.broadcasted_iota(jnp.int32, (nrows, 1), 0)
        pos = PAST_LEN + lax.rem(ridx, ext)
        y_all = _pool_windows(zbuf, nrows, pos, r)
        for b in range(n_streams):
            ypool_ref[b * Ts:(b + 1) * Ts, :] = y_all[b * ext:b * ext + Ts, :]
        y_state = _s5_scan(u, n_streams, Ts, 1, None, (r["h0_re"], r["h0_im"]), r, r["hres"], r["hims"])
        r["outs"][...] = _mix_tail(x, ypool_ref[...], u, y_state, r)


def _const_spec(shape):
    nd = len(shape)
    return pl.BlockSpec(shape, lambda *_: (0,) * nd)


def _weight_spec(shape):
    nd = len(shape)
    return pl.BlockSpec(shape, lambda *_: (0,) * nd, pipeline_mode=pl.Buffered(1))


def _prompt_spec(ts, d, n_prompt):
    return pl.BlockSpec((ts, d), lambda i: (jnp.minimum(i, n_prompt - 1), 0))


def _mixer(xp, xs, hist_in, h0_re, h0_im, p, n_streams, Ts):
    S, D = xp.shape
    rows_s = xs.shape[0]
    ts = PROMPT_BLOCK
    n_prompt = S // ts
    d_pool = hist_in.shape[-1]
    n_state = p["a1_re"].shape[-1]
    assert n_streams * (HIST_ROWS + Ts) <= HIST_ROWS + ts and rows_s <= ts
    args = dict(p, xp=xp, xs=xs, hist_in=hist_in, h0_re=h0_re, h0_im=h0_im)
    ins = [args[n] for n in _MIXER_IN]
    in_specs = [_prompt_spec(ts, D, n_prompt)] + [_weight_spec(a.shape) for a in ins[1:]]
    n_half, nh, H = p["bb_re"].shape
    d_half = p["cc_re"].shape[1]
    gen_shapes = ([((n_half, d_half, 2 * nh), BF16), ((n_half, 2 * nh, d_half), BF16)]
                  + [((PROMPT_CHUNK, n_state), BF16)] * 4 + [((2, n_state), F32)] * 2)
    out_shapes = dict(
        outp=(S, D), outs=(rows_s, D), histp=(HIST_ROWS, d_pool), hrep=(1, n_state), himp=(1, n_state),
        hists=(n_streams, POOL_HIST, d_pool), hres=(n_streams, n_state), hims=(n_streams, n_state))
    out_specs = [_prompt_spec(ts, D, n_prompt)] + [_const_spec(out_shapes[n]) for n in _MIXER_OUT[1:]]
    return pl.pallas_call(
        functools.partial(_mixer_kernel, n_prompt=n_prompt, n_streams=n_streams, Tp=PROMPT_CHUNK, Ts=Ts),
        grid=(n_prompt + 1,),
        in_specs=in_specs,
        out_specs=out_specs,
        out_shape=[jax.ShapeDtypeStruct(out_shapes[n], F32) for n in _MIXER_OUT],
        scratch_shapes=[pltpu.VMEM((HIST_ROWS + ts, d_pool), F32),
                        pltpu.VMEM((rows_s, d_pool), F32)]
        + [pltpu.VMEM(args[n].shape, BF16) for n in _MIXER_BF16]
        + [pltpu.VMEM(shape, dtype) for shape, dtype in gen_shapes],
        compiler_params=pltpu.CompilerParams(dimension_semantics=("arbitrary",),
                                             vmem_limit_bytes=VMEM_LIMIT),
        name="mixer",
    )(*ins)


def _softmax_rows(s):
    e = jnp.exp(s - jnp.max(s, axis=-1, keepdims=True))
    return e / jnp.sum(e, axis=-1, keepdims=True)


def _attend(q_heads, k_heads, v_heads, scale):
    outs = []
    for q, k, v in zip(q_heads, k_heads, v_heads):
        s = lax.dot_general(q, k.astype(BF16), (((1,), (1,)), ((), ())), preferred_element_type=F32) * scale
        outs.append(_dot(_softmax_rows(s), v))
    return outs


_XATTN_IN = ("xp", "xs", "mem", "kc", "vc", "g_mem", "w_k", "w_v", "g_xattn", "w_q", "w_o")
_XATTN_OUT = ("outp", "outs", "mk", "mv")


def _xattn_kernel(*refs, n_prompt, n_streams, Ts, hpu, scale):
    r = dict(zip(_XATTN_IN + _XATTN_OUT, refs))
    kp_ref, vp_ref, qs_ref, os_ref, kbuf, vbuf, sem = refs[len(_XATTN_IN) + len(_XATTN_OUT):]
    i = pl.program_id(0)
    hd = kp_ref.shape[-1] // MEM_HEADS
    ups = MEM_HEADS // hpu

    def unit_copies(unit, hg):
        b, slot = unit // ups, lax.rem(unit, 2)
        return [pltpu.make_async_copy(src.at[0, b, :, hg * hpu + e, :], dst.at[slot, e], sem.at[slot, j * hpu + e])
                for j, (src, dst) in enumerate(((r["kc"], kbuf), (r["vc"], vbuf))) for e in range(hpu)]

    def for_unit(unit, fn):
        for hg in range(ups):
            @pl.when(lax.rem(unit, ups) == hg)
            def _():
                for cp in unit_copies(unit, hg):
                    fn(cp)

    @pl.when(i == 0)
    def _():
        for_unit(i, lambda cp: cp.start())
        m = _rmsnorm(r["mem"][...], r["g_mem"][...]).astype(BF16)
        k = _dot(m, r["w_k"][...])
        v = _dot(m, r["w_v"][...])
        r["mk"][...] = k
        r["mv"][...] = v
        kp_ref[...] = k.astype(BF16)
        vp_ref[...] = v.astype(BF16)
        q = _dot(_rmsnorm(r["xs"][...], r["g_xattn"][...]), r["w_q"][...]).astype(BF16)
        for h in range(MEM_HEADS):
            qs_ref[h] = q[:, h * hd:(h + 1) * hd]

    @pl.when(i < n_prompt)
    def _():
        @pl.when(i + 1 < n_prompt)
        def _():
            for_unit(i + 1, lambda cp: cp.start())

        for_unit(i, lambda cp: cp.wait())

        x = r["xp"][...]
        q = _dot(_rmsnorm(x, r["g_xattn"][...]), r["w_q"][...]).astype(BF16)
        o = _attend([q[:, h * hd:(h + 1) * hd] for h in range(MEM_HEADS)],
                    [kp_ref[:, h * hd:(h + 1) * hd] for h in range(MEM_HEADS)],
                    [vp_ref[:, h * hd:(h + 1) * hd] for h in range(MEM_HEADS)], scale)
        r["outp"][...] = x + _dot(jnp.concatenate(o, axis=-1), r["w_o"][...])

        b, hg, slot = i // ups, lax.rem(i, ups), lax.rem(i, 2)
        rows = pl.ds(pl.multiple_of(b * Ts, Ts), Ts)
        heads = [hg * hpu + e for e in range(hpu)]
        o_s = _attend([qs_ref[h, rows, :] for h in heads], [kbuf[slot, e] for e in range(hpu)],
                      [vbuf[slot, e] for e in range(hpu)], scale)
        for h, o_h in zip(heads, o_s):
            os_ref[h, rows, :] = o_h.astype(BF16)

    @pl.when(i == n_prompt)
    def _():
        o = jnp.concatenate([os_ref[h] for h in range(MEM_HEADS)], axis=-1)
        r["outs"][...] = r["xs"][...] + _dot(o, r["w_o"][...])


def _xattn(xp, xs, mem, kc, vc, p, n_streams, Ts):
    S, D = xp.shape
    rows_s = xs.shape[0]
    ts = PROMPT_BLOCK
    n_prompt = S // ts
    n_mem = mem.shape[0]
    hd = D // MEM_HEADS
    hpu = n_streams * MEM_HEADS // n_prompt
    assert hpu * n_prompt == n_streams * MEM_HEADS and MEM_HEADS % hpu == 0
    args = dict(p, xp=xp, xs=xs, mem=mem, kc=kc, vc=vc)
    ins = [args[n] for n in _XATTN_IN]
    cache_spec = pl.BlockSpec(memory_space=pl.ANY)
    in_specs = [_prompt_spec(ts, D, n_prompt)]
    for n in _XATTN_IN[1:]:
        in_specs.append(cache_spec if n in ("kc", "vc") else _weight_spec(args[n].shape))
    out_shapes = dict(outp=(S, D), outs=(rows_s, D), mk=(n_mem, D), mv=(n_mem, D))
    out_specs = [_prompt_spec(ts, D, n_prompt)] + [_const_spec(out_shapes[n]) for n in _XATTN_OUT[1:]]
    return pl.pallas_call(
        functools.partial(_xattn_kernel, n_prompt=n_prompt, n_streams=n_streams, Ts=Ts, hpu=hpu, scale=hd ** -0.5),
        grid=(n_prompt + 1,),
        in_specs=in_specs,
        out_specs=out_specs,
        out_shape=[jax.ShapeDtypeStruct(out_shapes[n], F32) for n in _XATTN_OUT],
        scratch_shapes=[pltpu.VMEM((n_mem, D), BF16), pltpu.VMEM((n_mem, D), BF16),
                        pltpu.VMEM((MEM_HEADS, rows_s, hd), BF16), pltpu.VMEM((MEM_HEADS, rows_s, hd), BF16),
                        pltpu.VMEM((2, hpu, n_mem, hd), F32), pltpu.VMEM((2, hpu, n_mem, hd), F32),
                        pltpu.SemaphoreType.DMA((2, 2 * hpu))],
        compiler_params=pltpu.CompilerParams(dimension_semantics=("arbitrary",),
                                             vmem_limit_bytes=VMEM_LIMIT),
        name="xattn",
    )(*ins)


def _mlp_rows(x, g_ref, wup_ref, wdown_ref, gfin_ref):
    hm = _rmsnorm(x, g_ref[...]).astype(BF16)
    acc = x
    for j in range(wup_ref.shape[1] // FF_CHUNK):
        cols = slice(j * FF_CHUNK, (j + 1) * FF_CHUNK)
        up = _dot(hm, wup_ref[:, cols])
        acc = acc + _dot(jnp.square(jnp.maximum(up, 0.0)), wdown_ref[cols, :])
    return _rmsnorm(acc, gfin_ref[...])


def _mlp_kernel(xp_ref, xs_ref, g_ref, wup_ref, wdown_ref, gfin_ref, outp_ref, outs_ref, *, n_prompt):
    i = pl.program_id(0)

    @pl.when(i < n_prompt)
    def _():
        outp_ref[...] = _mlp_rows(xp_ref[...], g_ref, wup_ref, wdown_ref, gfin_ref)

    @pl.when(i == n_prompt)
    def _():
        outs_ref[...] = _mlp_rows(xs_ref[...], g_ref, wup_ref, wdown_ref, gfin_ref)


def _mlp(xp, xs, p):
    S, D = xp.shape
    ts = PROMPT_BLOCK
    n_prompt = S // ts
    weights = (p["g_mlp"], p["w_up"], p["w_down"], p["g_final"])
    return pl.pallas_call(
        functools.partial(_mlp_kernel, n_prompt=n_prompt),
        grid=(n_prompt + 1,),
        in_specs=[_prompt_spec(ts, D, n_prompt), _weight_spec(xs.shape)] + [_weight_spec(w.shape) for w in weights],
        out_specs=[_prompt_spec(ts, D, n_prompt), _const_spec(xs.shape)],
        out_shape=[jax.ShapeDtypeStruct(xp.shape, F32), jax.ShapeDtypeStruct(xs.shape, F32)],
        compiler_params=pltpu.CompilerParams(dimension_semantics=("arbitrary",),
                                             vmem_limit_bytes=VMEM_LIMIT),
        name="mlp",
    )(xp, xs, *weights)


def _prepare_layer(l, g_mix, w_in, w_pool, pool_scale, a_re, a_im, b_re, b_im, c_re, c_im, d_skip,
                   log_dt, w_glu, b_glu, w_out, g_xattn, g_mem, w_q, w_k, w_v, w_o, g_mlp, w_up,
                   w_down, g_final):
    G, P = a_re.shape[1:]
    gi = G // N_HALF
    ar, ai = a_re[l].astype(F32), a_im[l].astype(F32)
    dt = jnp.exp(log_dt[l].astype(F32))[:, None]
    lam_re, lam_im = ar * dt, ai * dt

    mag = jnp.exp(lam_re)
    ab_re, ab_im = mag * jnp.cos(lam_im), mag * jnp.sin(lam_im)

    den = ar * ar + ai * ai
    coef_re = ((ab_re - 1.0) * ar + ab_im * ai) / den
    coef_im = (ab_im * ar - (ab_re - 1.0) * ai) / den
    br, bi = b_re[l].astype(F32), b_im[l].astype(F32)
    bb_re = coef_re[..., None] * br - coef_im[..., None] * bi
    bb_im = coef_re[..., None] * bi + coef_im[..., None] * br

    half = lambda t: t.astype(F32).reshape((N_HALF, gi * t.shape[1], t.shape[2]))
    row = lambda v: v.astype(F32).reshape(1, -1)
    return dict(
        g_mix=row(g_mix[l]), w_in=w_in[l], w_pool=w_pool[l], pool_scale=row(pool_scale[l]),
        bb_re=half(bb_re), bb_im=half(bb_im), cc_re=half(c_re[l]), cc_im=half(c_im[l]),
        a1_re=row(ab_re), a1_im=row(ab_im),
        d_skip=row(d_skip[l]), w_glu=w_glu[l], b_glu=row(b_glu[l]), w_out=w_out[l],
        g_xattn=row(g_xattn[l]), g_mem=row(g_mem[l]), w_q=w_q[l], w_k=w_k[l], w_v=w_v[l], w_o=w_o[l],
        g_mlp=row(g_mlp[l]), w_up=w_up[l], w_down=w_down[l], g_final=row(g_final))


def kernel(x_prompt, x_sample, cache_mem_k, cache_mem_v, state_pool, state_ssm_re, state_ssm_im, mem_prompt, g_mix, w_in, w_pool, pool_scale, ssm_a_re, ssm_a_im, ssm_b_re, ssm_b_im, ssm_c_re, ssm_c_im, ssm_d, ssm_log_dt, w_glu, b_glu, w_out, g_xattn, g_mem, w_q, w_k, w_v, w_o, g_mlp, w_up, w_down, g_final):
    depth = g_mix.shape[0]
    assert depth == 1 and x_prompt.shape[0] == 1, "single layer, single prompt stream"
    Bp, S, D = x_prompt.shape
    Bs, Ts, _ = x_sample.shape
    n_mem = mem_prompt.shape[1]
    G, P = ssm_a_re.shape[1:]
    assert S % PROMPT_BLOCK == 0 and PROMPT_BLOCK % PROMPT_CHUNK == 0 and Ts <= PROMPT_CHUNK

    l = 0
    p = _prepare_layer(l, g_mix, w_in, w_pool, pool_scale, ssm_a_re, ssm_a_im, ssm_b_re, ssm_b_im,
                       ssm_c_re, ssm_c_im, ssm_d, ssm_log_dt, w_glu, b_glu, w_out, g_xattn, g_mem,
                       w_q, w_k, w_v, w_o, g_mlp, w_up, w_down, g_final)

    xp1, xs1, hist_p, hre_p, him_p, hist_s, hre_s, him_s = _mixer(
        x_prompt[0], x_sample.reshape(Bs * Ts, D), state_pool[l],
        state_ssm_re[l].reshape(Bs, G * P), state_ssm_im[l].reshape(Bs, G * P), p, Bs, Ts)
    xp2, xs2, mk, mv = _xattn(xp1, xs1, mem_prompt[0], cache_mem_k[l:l + 1], cache_mem_v[l:l + 1], p, Bs, Ts)
    y_prompt, y_sample = _mlp(xp2, xs2, p)

    hd = D // MEM_HEADS
    return (y_prompt[None], y_sample.reshape(Bs, Ts, D),
            mk.reshape(1, Bp, n_mem, MEM_HEADS, hd), mv.reshape(1, Bp, n_mem, MEM_HEADS, hd),
            hist_p[1:].reshape(1, Bp, POOL_HIST, -1),
            hre_p.reshape(1, Bp, G, P), him_p.reshape(1, Bp, G, P),
            hist_s[None], hre_s.reshape(1, Bs, G, P), him_s.reshape(1, Bs, G, P))
```

```python
import functools

import jax
import jax.numpy as jnp
from jax import lax
from jax.experimental import pallas as pl
from jax.experimental.pallas import tpu as pltpu

F32 = jnp.float32
BF16 = jnp.bfloat16

EPS = 1e-6
PAST_LEN = 1024
POOL_WINDOWS = (2, 4, 8, 16)
POOL_HIST = max(POOL_WINDOWS) - 1
HIST_ROWS = 16
MEM_HEADS = 4
N_HALF = 2
S5_TILE = 256
S5_DTYPE = F32

PROMPT_BLOCK = 512
PROMPT_CHUNK = 128
FF_CHUNK = 1024
VMEM_LIMIT = 56 * 1024 * 1024


def _rmsnorm(x, g):
    return x * lax.rsqrt(jnp.mean(x * x, axis=-1, keepdims=True) + EPS) * g


def _dot(a, b):
    return jnp.dot(a.astype(BF16), b.astype(BF16), preferred_element_type=F32)


def _pool_windows(zbuf, nrows, pos, r):
    gw = r["w_pool"].shape[-1]
    outs = []
    for g, w in enumerate(POOL_WINDOWS):
        lanes = slice(g * gw, (g + 1) * gw)
        win = zbuf[pl.ds(0, HIST_ROWS + nrows), lanes]
        k = 1
        while k < w:
            win = win + pltpu.roll(win, k, 0)
            k *= 2
        cur = zbuf[pl.ds(HIST_ROWS, nrows), lanes]
        cnt = jnp.minimum(pos + 1, w).astype(F32)
        pooled = win[HIST_ROWS:, :] / cnt - cur
        outs.append(_dot(pooled, r["w_pool"][g]))
    return jnp.concatenate(outs, axis=-1) * r["pool_scale"][...]


def _cmul(ar, ai, br, bi):
    return ar * br - ai * bi, ar * bi + ai * br


def _s5_scan(u, n_chunks, T, end_row, h_carry, h_rows, r, hout_re_ref, hout_im_ref):
    hw = r["bblk"].shape[1]
    nh = r["bblk"].shape[2] // 2
    u_bf = u.astype(BF16)
    r_i = lax.broadcasted_iota(jnp.int32, (T, T), 0)
    c_i = lax.broadcasted_iota(jnp.int32, (T, T), 1)
    ltri = jnp.where(r_i >= c_i, 1.0, 0.0).astype(BF16)

    ys = []
    for k in range(N_HALF):
        uk = u_bf[:, k * hw:(k + 1) * hw]
        yk = None
        for j in range(nh // S5_TILE):
            re_c = slice(j * S5_TILE, (j + 1) * S5_TILE)
            im_c = slice(nh + j * S5_TILE, nh + (j + 1) * S5_TILE)
            st = slice(k * nh + j * S5_TILE, k * nh + (j + 1) * S5_TILE)
            bu_re = _dot(uk, r["bblk"][k, :, re_c])
            bu_im = _dot(uk, r["bblk"][k, :, im_c])
            if h_rows is None:
                carry = (h_carry[0][:, st], h_carry[1][:, st])
            h_re, h_im = [], []
            for c in range(n_chunks):
                rows = slice(c * T, (c + 1) * T)
                xr, xi = _cmul(bu_re[rows].astype(S5_DTYPE), bu_im[rows].astype(S5_DTYPE),
                               r["pneg_re"][0:T, st], r["pneg_im"][0:T, st])
                wr = _dot(ltri, xr)
                wi = _dot(ltri, xi)
                if h_rows is not None:
                    carry = (h_rows[0][pl.ds(c, 1), st], h_rows[1][pl.ds(c, 1), st])
                cr, ci = _cmul(r["a1_re"][:, st], r["a1_im"][:, st], carry[0], carry[1])
                wr, wi = wr + cr, wi + ci
                sr, si = _cmul(r["ppos_re"][0:T, st], r["ppos_im"][0:T, st], wr.astype(S5_DTYPE), wi.astype(S5_DTYPE))
                carry = _cmul(r["pend_re"][end_row:end_row + 1, st], r["pend_im"][end_row:end_row + 1, st],
                              wr[T - 1:T, :], wi[T - 1:T, :])
                if h_rows is not None:
                    hout_re_ref[pl.ds(c, 1), st] = carry[0]
                    hout_im_ref[pl.ds(c, 1), st] = carry[1]
                h_re.append(sr)
                h_im.append(si)
            if h_rows is None:
                hout_re_ref[:, st] = carry[0]
                hout_im_ref[:, st] = carry[1]
            part = (_dot(jnp.concatenate(h_re, axis=0), r["cblk"][k, re_c, :])
                    + _dot(jnp.concatenate(h_im, axis=0), r["cblk"][k, im_c, :]))
            yk = part if yk is None else yk + part
        ys.append(yk)
    return jnp.concatenate(ys, axis=-1)


def _mix_tail(x, y_pool, u, y_state, r):
    y = jax.nn.gelu(y_state + r["d_skip"][...] * u)
    y = y * jax.nn.sigmoid(_dot(y, r["w_glu"][...]) + r["b_glu"][...])
    ycat = jnp.concatenate([y_pool.astype(BF16), y.astype(BF16)], axis=-1)
    return x + _dot(ycat, r["w_out"][...])


_MIXER_IN = ("xp", "xs", "hist_in", "h0_re", "h0_im",
             "g_mix", "w_in", "w_pool", "pool_scale", "bb_re", "bb_im", "cc_re", "cc_im", "a1_re", "a1_im",
             "d_skip", "w_glu", "b_glu", "w_out")
_MIXER_OUT = ("outp", "outs", "histp", "hrep", "himp", "hists", "hres", "hims")
_MIXER_BF16 = ("w_in", "w_pool", "w_glu", "w_out")
_MIXER_GEN = ("bblk", "cblk", "pneg_re", "pneg_im", "ppos_re", "ppos_im", "pend_re", "pend_im")


def _power_table(re1, im1, T):
    SUB = 16
    assert T % SUB == 0
    n = re1.shape[-1]
    t = lax.broadcasted_iota(jnp.int32, (SUB, 1), 0)
    lo_re, lo_im = jnp.ones((SUB, n), F32), jnp.zeros((SUB, n), F32)
    cr, ci = re1, im1
    k = 1
    while k < SUB:
        bit = (t & k) != 0
        lo_re, lo_im = _cmul(lo_re, lo_im, jnp.where(bit, cr, 1.0), jnp.where(bit, ci, 0.0))
        cr, ci = _cmul(cr, ci, cr, ci)
        k *= 2
    hi_re, hi_im = [jnp.ones((1, n), F32)], [jnp.zeros((1, n), F32)]
    for _ in range(T // SUB - 1):
        nr, ni = _cmul(hi_re[-1], hi_im[-1], cr, ci)
        hi_re.append(nr)
        hi_im.append(ni)
    rep = lambda rows: jnp.concatenate([jnp.broadcast_to(v, (SUB, n)) for v in rows], axis=0)
    tile = lambda v: jnp.concatenate([v] * (T // SUB), axis=0)
    return _cmul(tile(lo_re), tile(lo_im), rep(hi_re), rep(hi_im))


def _expand_s5_params(r, gen, T, t_ends):
    are, aim = r["a1_re"][...], r["a1_im"][...]
    pr, pi = _power_table(are, aim, T)
    gen["ppos_re"][...] = pr.astype(S5_DTYPE)
    gen["ppos_im"][...] = pi.astype(S5_DTYPE)
    for j, te in enumerate(t_ends):
        gen["pend_re"][j:j + 1, :] = pr[te - 1:te, :]
        gen["pend_im"][j:j + 1, :] = pi[te - 1:te, :]
    den = are * are + aim * aim
    nr, ni = _power_table(are / den, -aim / den, T)
    gen["pneg_re"][...] = nr.astype(S5_DTYPE)
    gen["pneg_im"][...] = ni.astype(S5_DTYPE)

    n_half, rows_b, H = r["bb_re"].shape
    _, rows_c, P = r["cc_re"].shape
    nh = rows_b

    def spread(x, width):
        w = x.shape[1]
        sel = lax.rem(lax.broadcasted_iota(jnp.int32, (w, width), 1), w) == lax.broadcasted_iota(jnp.int32, (w, width), 0)
        return _dot(x, jnp.where(sel, 1.0, 0.0))

    def same_group(shape, row_size, col_size):
        return (lax.div(lax.broadcasted_iota(jnp.int32, shape, 0), row_size)
                == lax.div(lax.broadcasted_iota(jnp.int32, shape, 1), col_size))

    for k in range(n_half):
        for c, (b_src, c_src, sign) in enumerate(((r["bb_re"], r["cc_re"], 1.0), (r["bb_im"], r["cc_im"], -1.0))):
            yb = spread(b_src[k], rows_c)
            yb = jnp.where(same_group(yb.shape, P, H), yb, 0.0)
            gen["bblk"][k, :, c * nh:(c + 1) * nh] = yb.T.astype(BF16)
            yc = spread(c_src[k], rows_b)
            yc = jnp.where(same_group(yc.shape, H, P), sign * yc, 0.0)
            gen["cblk"][k, c * nh:(c + 1) * nh, :] = yc.T.astype(BF16)


def _mixer_kernel(*refs, n_prompt, n_streams, Tp, Ts):
    r = dict(zip(_MIXER_IN + _MIXER_OUT, refs))
    zbuf, ypool_ref = refs[len(_MIXER_IN) + len(_MIXER_OUT):][:2]
    n_fixed = len(_MIXER_IN) + len(_MIXER_OUT) + 2
    bf16_weights = dict(zip(_MIXER_BF16, refs[n_fixed:]))
    gen = dict(zip(_MIXER_GEN, refs[n_fixed + len(_MIXER_BF16):]))
    i = pl.program_id(0)
    ts = r["xp"].shape[0]
    d_pool = zbuf.shape[1]

    @pl.when(i == 0)
    def _():
        zbuf[0:HIST_ROWS, :] = jnp.zeros((HIST_ROWS, d_pool), F32)
        r["hrep"][...] = jnp.zeros(r["hrep"].shape, F32)
        r["himp"][...] = jnp.zeros(r["himp"].shape, F32)
        for name, ref in bf16_weights.items():
            ref[...] = r[name][...].astype(BF16)
        _expand_s5_params(r, gen, Tp, (Tp, Ts))

    r.update(bf16_weights)
    r.update(gen)

    @pl.when(i < n_prompt)
    def _():
        x = r["xp"][...]
        z = _dot(_rmsnorm(x, r["g_mix"][...]), r["w_in"][...])
        zbuf[HIST_ROWS:HIST_ROWS + ts, :] = z[:, :d_pool]
        u = z[:, d_pool:]
        pos = i * ts + lax.broadcasted_iota(jnp.int32, (ts, 1), 0)
        y_pool = _pool_windows(zbuf, ts, pos, r)
        y_state = _s5_scan(u, ts // Tp, Tp, 0, (r["hrep"][...], r["himp"][...]), None, r, r["hrep"], r["himp"])
        r["outp"][...] = _mix_tail(x, y_pool, u, y_state, r)
        tail = zbuf[ts:ts + HIST_ROWS, :]
        zbuf[0:HIST_ROWS, :] = tail
        r["histp"][...] = tail

    @pl.when(i == n_prompt)
    def _():
        ext = HIST_ROWS + Ts
        x = r["xs"][...]
        z = _dot(_rmsnorm(x, r["g_mix"][...]), r["w_in"][...])
        u = z[:, d_pool:]
        for b in range(n_streams):
            zbuf[b * ext:b * ext + 1, :] = jnp.zeros((1, d_pool), F32)
            zbuf[b * ext + 1:b * ext + HIST_ROWS, :] = r["hist_in"][b]
            zbuf[b * ext + HIST_ROWS:(b + 1) * ext, :] = z[b * Ts:(b + 1) * Ts, :d_pool]
            r["hists"][b] = zbuf[(b + 1) * ext - POOL_HIST:(b + 1) * ext, :]
        nrows = n_streams * ext - HIST_ROWS
        ridx = lax.broadcasted_iota(jnp.int32, (nrows, 1), 0)
        pos = PAST_LEN + lax.rem(ridx, ext)
        y_all = _pool_windows(zbuf, nrows, pos, r)
        for b in range(n_streams):
            ypool_ref[b * Ts:(b + 1) * Ts, :] = y_all[b * ext:b * ext + Ts, :]
        y_state = _s5_scan(u, n_streams, Ts, 1, None, (r["h0_re"], r["h0_im"]), r, r["hres"], r["hims"])
        r["outs"][...] = _mix_tail(x, ypool_ref[...], u, y_state, r)


def _const_spec(shape):
    nd = len(shape)
    return pl.BlockSpec(shape, lambda *_: (0,) * nd)


def _weight_spec(shape):
    nd = len(shape)
    return pl.BlockSpec(shape, lambda *_: (0,) * nd, pipeline_mode=pl.Buffered(1))


def _prompt_spec(ts, d, n_prompt):
    return pl.BlockSpec((ts, d), lambda i: (jnp.minimum(i, n_prompt - 1), 0))


def _mixer(xp, xs, hist_in, h0_re, h0_im, p, n_streams, Ts):
    S, D = xp.shape
    rows_s = xs.shape[0]
    ts = PROMPT_BLOCK
    n_prompt = S // ts
    d_pool = hist_in.shape[-1]
    n_state = p["a1_re"].shape[-1]
    assert n_streams * (HIST_ROWS + Ts) <= HIST_ROWS + ts and rows_s <= ts
    args = dict(p, xp=xp, xs=xs, hist_in=hist_in, h0_re=h0_re, h0_im=h0_im)
    ins = [args[n] for n in _MIXER_IN]
    in_specs = [_prompt_spec(ts, D, n_prompt)] + [_weight_spec(a.shape) for a in ins[1:]]
    n_half, nh, H = p["bb_re"].shape
    d_half = p["cc_re"].shape[1]
    gen_shapes = ([((n_half, d_half, 2 * nh), BF16), ((n_half, 2 * nh, d_half), BF16)]
                  + [((PROMPT_CHUNK, n_state), S5_DTYPE)] * 4 + [((2, n_state), F32)] * 2)
    out_shapes = dict(
        outp=(S, D), outs=(rows_s, D), histp=(HIST_ROWS, d_pool), hrep=(1, n_state), himp=(1, n_state),
        hists=(n_streams, POOL_HIST, d_pool), hres=(n_streams, n_state), hims=(n_streams, n_state))
    out_specs = [_prompt_spec(ts, D, n_prompt)] + [_const_spec(out_shapes[n]) for n in _MIXER_OUT[1:]]
    return pl.pallas_call(
        functools.partial(_mixer_kernel, n_prompt=n_prompt, n_streams=n_streams, Tp=PROMPT_CHUNK, Ts=Ts),
        grid=(n_prompt + 1,),
        in_specs=in_specs,
        out_specs=out_specs,
        out_shape=[jax.ShapeDtypeStruct(out_shapes[n], F32) for n in _MIXER_OUT],
        scratch_shapes=[pltpu.VMEM((HIST_ROWS + ts, d_pool), F32),
                        pltpu.VMEM((rows_s, d_pool), F32)]
        + [pltpu.VMEM(args[n].shape, BF16) for n in _MIXER_BF16]
        + [pltpu.VMEM(shape, dtype) for shape, dtype in gen_shapes],
        compiler_params=pltpu.CompilerParams(dimension_semantics=("arbitrary",),
                                             vmem_limit_bytes=VMEM_LIMIT),
        name="mixer",
    )(*ins)


def _softmax_rows(s):
    e = jnp.exp(s - jnp.max(s, axis=-1, keepdims=True))
    return e / jnp.sum(e, axis=-1, keepdims=True)


def _attend(q_heads, k_heads, v_heads, scale):
    outs = []
    for q, k, v in zip(q_heads, k_heads, v_heads):
        s = lax.dot_general(q, k.astype(BF16), (((1,), (1,)), ((), ())), preferred_element_type=F32) * scale
        outs.append(_dot(_softmax_rows(s), v))
    return outs


_XATTN_IN = ("xp", "xs", "mem", "kc", "vc", "g_mem", "w_k", "w_v", "g_xattn", "w_q", "w_o")
_XATTN_OUT = ("outp", "outs", "mk", "mv")


def _xattn_kernel(*refs, n_prompt, n_streams, Ts, hpu, scale):
    r = dict(zip(_XATTN_IN + _XATTN_OUT, refs))
    kp_ref, vp_ref, qs_ref, os_ref, kbuf, vbuf, sem = refs[len(_XATTN_IN) + len(_XATTN_OUT):]
    i = pl.program_id(0)
    hd = kp_ref.shape[-1] // MEM_HEADS
    ups = MEM_HEADS // hpu

    def unit_copies(unit, hg):
        b, slot = unit // ups, lax.rem(unit, 2)
        return [pltpu.make_async_copy(src.at[0, b, :, hg * hpu + e, :], dst.at[slot, e], sem.at[slot, j * hpu + e])
                for j, (src, dst) in enumerate(((r["kc"], kbuf), (r["vc"], vbuf))) for e in range(hpu)]

    def for_unit(unit, fn):
        for hg in range(ups):
            @pl.when(lax.rem(unit, ups) == hg)
            def _():
                for cp in unit_copies(unit, hg):
                    fn(cp)

    @pl.when(i == 0)
    def _():
        for_unit(i, lambda cp: cp.start())
        m = _rmsnorm(r["mem"][...], r["g_mem"][...]).astype(BF16)
        k = _dot(m, r["w_k"][...])
        v = _dot(m, r["w_v"][...])
        r["mk"][...] = k
        r["mv"][...] = v
        kp_ref[...] = k.astype(BF16)
        vp_ref[...] = v.astype(BF16)
        q = _dot(_rmsnorm(r["xs"][...], r["g_xattn"][...]), r["w_q"][...]).astype(BF16)
        for h in range(MEM_HEADS):
            qs_ref[h] = q[:, h * hd:(h + 1) * hd]

    @pl.when(i < n_prompt)
    def _():
        @pl.when(i + 1 < n_prompt)
        def _():
            for_unit(i + 1, lambda cp: cp.start())

        for_unit(i, lambda cp: cp.wait())

        x = r["xp"][...]
        q = _dot(_rmsnorm(x, r["g_xattn"][...]), r["w_q"][...]).astype(BF16)
        o = _attend([q[:, h * hd:(h + 1) * hd] for h in range(MEM_HEADS)],
                    [kp_ref[:, h * hd:(h + 1) * hd] for h in range(MEM_HEADS)],
                    [vp_ref[:, h * hd:(h + 1) * hd] for h in range(MEM_HEADS)], scale)
        r["outp"][...] = x + _dot(jnp.concatenate(o, axis=-1), r["w_o"][...])

        b, hg, slot = i // ups, lax.rem(i, ups), lax.rem(i, 2)
        rows = pl.ds(pl.multiple_of(b * Ts, Ts), Ts)
        heads = [hg * hpu + e for e in range(hpu)]
        o_s = _attend([qs_ref[h, rows, :] for h in heads], [kbuf[slot, e] for e in range(hpu)],
                      [vbuf[slot, e] for e in range(hpu)], scale)
        for h, o_h in zip(heads, o_s):
            os_ref[h, rows, :] = o_h.astype(BF16)

    @pl.when(i == n_prompt)
    def _():
        o = jnp.concatenate([os_ref[h] for h in range(MEM_HEADS)], axis=-1)
        r["outs"][...] = r["xs"][...] + _dot(o, r["w_o"][...])


def _xattn(xp, xs, mem, kc, vc, p, n_streams, Ts):
    S, D = xp.shape
    rows_s = xs.shape[0]
    ts = PROMPT_BLOCK
    n_prompt = S // ts
    n_mem = mem.shape[0]
    hd = D // MEM_HEADS
    hpu = n_streams * MEM_HEADS // n_prompt
    assert hpu * n_prompt == n_streams * MEM_HEADS and MEM_HEADS % hpu == 0
    args = dict(p, xp=xp, xs=xs, mem=mem, kc=kc, vc=vc)
    ins = [args[n] for n in _XATTN_IN]
    cache_spec = pl.BlockSpec(memory_space=pl.ANY)
    in_specs = [_prompt_spec(ts, D, n_prompt)]
    for n in _XATTN_IN[1:]:
        in_specs.append(cache_spec if n in ("kc", "vc") else _weight_spec(args[n].shape))
    out_shapes = dict(outp=(S, D), outs=(rows_s, D), mk=(n_mem, D), mv=(n_mem, D))
    out_specs = [_prompt_spec(ts, D, n_prompt)] + [_const_spec(out_shapes[n]) for n in _XATTN_OUT[1:]]
    return pl.pallas_call(
        functools.partial(_xattn_kernel, n_prompt=n_prompt, n_streams=n_streams, Ts=Ts, hpu=hpu, scale=hd ** -0.5),
        grid=(n_prompt + 1,),
        in_specs=in_specs,
        out_specs=out_specs,
        out_shape=[jax.ShapeDtypeStruct(out_shapes[n], F32) for n in _XATTN_OUT],
        scratch_shapes=[pltpu.VMEM((n_mem, D), BF16), pltpu.VMEM((n_mem, D), BF16),
                        pltpu.VMEM((MEM_HEADS, rows_s, hd), BF16), pltpu.VMEM((MEM_HEADS, rows_s, hd), BF16),
                        pltpu.VMEM((2, hpu, n_mem, hd), F32), pltpu.VMEM((2, hpu, n_mem, hd), F32),
                        pltpu.SemaphoreType.DMA((2, 2 * hpu))],
        compiler_params=pltpu.CompilerParams(dimension_semantics=("arbitrary",),
                                             vmem_limit_bytes=VMEM_LIMIT),
        name="xattn",
    )(*ins)


def _mlp_rows(x, g_ref, wup_ref, wdown_ref, gfin_ref):
    hm = _rmsnorm(x, g_ref[...]).astype(BF16)
    acc = x
    for j in range(wup_ref.shape[1] // FF_CHUNK):
        cols = slice(j * FF_CHUNK, (j + 1) * FF_CHUNK)
        up = _dot(hm, wup_ref[:, cols])
        acc = acc + _dot(jnp.square(jnp.maximum(up, 0.0)), wdown_ref[cols, :])
    return _rmsnorm(acc, gfin_ref[...])


def _mlp_kernel(xp_ref, xs_ref, g_ref, wup_ref, wdown_ref, gfin_ref, outp_ref, outs_ref, *, n_prompt):
    i = pl.program_id(0)

    @pl.when(i < n_prompt)
    def _():
        outp_ref[...] = _mlp_rows(xp_ref[...], g_ref, wup_ref, wdown_ref, gfin_ref)

    @pl.when(i == n_prompt)
    def _():
        outs_ref[...] = _mlp_rows(xs_ref[...], g_ref, wup_ref, wdown_ref, gfin_ref)


def _mlp(xp, xs, p):
    S, D = xp.shape
    ts = PROMPT_BLOCK
    n_prompt = S // ts
    weights = (p["g_mlp"], p["w_up"], p["w_down"], p["g_final"])
    return pl.pallas_call(
        functools.partial(_mlp_kernel, n_prompt=n_prompt),
        grid=(n_prompt + 1,),
        in_specs=[_prompt_spec(ts, D, n_prompt), _weight_spec(xs.shape)] + [_weight_spec(w.shape) for w in weights],
        out_specs=[_prompt_spec(ts, D, n_prompt), _const_spec(xs.shape)],
        out_shape=[jax.ShapeDtypeStruct(xp.shape, F32), jax.ShapeDtypeStruct(xs.shape, F32)],
        compiler_params=pltpu.CompilerParams(dimension_semantics=("arbitrary",),
                                             vmem_limit_bytes=VMEM_LIMIT),
        name="mlp",
    )(xp, xs, *weights)


def _prepare_layer(l, g_mix, w_in, w_pool, pool_scale, a_re, a_im, b_re, b_im, c_re, c_im, d_skip,
                   log_dt, w_glu, b_glu, w_out, g_xattn, g_mem, w_q, w_k, w_v, w_o, g_mlp, w_up,
                   w_down, g_final):
    G, P = a_re.shape[1:]
    gi = G // N_HALF
    ar, ai = a_re[l].astype(F32), a_im[l].astype(F32)
    dt = jnp.exp(log_dt[l].astype(F32))[:, None]
    lam_re, lam_im = ar * dt, ai * dt

    mag = jnp.exp(lam_re)
    ab_re, ab_im = mag * jnp.cos(lam_im), mag * jnp.sin(lam_im)

    den = ar * ar + ai * ai
    coef_re = ((ab_re - 1.0) * ar + ab_im * ai) / den
    coef_im = (ab_im * ar - (ab_re - 1.0) * ai) / den
    br, bi = b_re[l].astype(F32), b_im[l].astype(F32)
    bb_re = coef_re[..., None] * br - coef_im[..., None] * bi
    bb_im = coef_re[..., None] * bi + coef_im[..., None] * br

    half = lambda t: t.astype(F32).reshape((N_HALF, gi * t.shape[1], t.shape[2]))
    row = lambda v: v.astype(F32).reshape(1, -1)
    return dict(
        g_mix=row(g_mix[l]), w_in=w_in[l], w_pool=w_pool[l], pool_scale=row(pool_scale[l]),
        bb_re=half(bb_re), bb_im=half(bb_im), cc_re=half(c_re[l]), cc_im=half(c_im[l]),
        a1_re=row(ab_re), a1_im=row(ab_im),
        d_skip=row(d_skip[l]), w_glu=w_glu[l], b_glu=row(b_glu[l]), w_out=w_out[l],
        g_xattn=row(g_xattn[l]), g_mem=row(g_mem[l]), w_q=w_q[l], w_k=w_k[l], w_v=w_v[l], w_o=w_o[l],
        g_mlp=row(g_mlp[l]), w_up=w_up[l], w_down=w_down[l], g_final=row(g_final))


def kernel(x_prompt, x_sample, cache_mem_k, cache_mem_v, state_pool, state_ssm_re, state_ssm_im, mem_prompt, g_mix, w_in, w_pool, pool_scale, ssm_a_re, ssm_a_im, ssm_b_re, ssm_b_im, ssm_c_re, ssm_c_im, ssm_d, ssm_log_dt, w_glu, b_glu, w_out, g_xattn, g_mem, w_q, w_k, w_v, w_o, g_mlp, w_up, w_down, g_final):
    depth = g_mix.shape[0]
    assert depth == 1 and x_prompt.shape[0] == 1, "single layer, single prompt stream"
    Bp, S, D = x_prompt.shape
    Bs, Ts, _ = x_sample.shape
    n_mem = mem_prompt.shape[1]
    G, P = ssm_a_re.shape[1:]
    assert S % PROMPT_BLOCK == 0 and PROMPT_BLOCK % PROMPT_CHUNK == 0 and Ts <= PROMPT_CHUNK

    l = 0
    p = _prepare_layer(l, g_mix, w_in, w_pool, pool_scale, ssm_a_re, ssm_a_im, ssm_b_re, ssm_b_im,
                       ssm_c_re, ssm_c_im, ssm_d, ssm_log_dt, w_glu, b_glu, w_out, g_xattn, g_mem,
                       w_q, w_k, w_v, w_o, g_mlp, w_up, w_down, g_final)

    xp1, xs1, hist_p, hre_p, him_p, hist_s, hre_s, him_s = _mixer(
        x_prompt[0], x_sample.reshape(Bs * Ts, D), state_pool[l],
        state_ssm_re[l].reshape(Bs, G * P), state_ssm_im[l].reshape(Bs, G * P), p, Bs, Ts)
    xp2, xs2, mk, mv = _xattn(xp1, xs1, mem_prompt[0], cache_mem_k[l:l + 1], cache_mem_v[l:l + 1], p, Bs, Ts)
    y_prompt, y_sample = _mlp(xp2, xs2, p)

    hd = D // MEM_HEADS
    return (y_prompt[None], y_sample.reshape(Bs, Ts, D),
            mk.reshape(1, Bp, n_mem, MEM_HEADS, hd), mv.reshape(1, Bp, n_mem, MEM_HEADS, hd),
            hist_p[1:].reshape(1, Bp, POOL_HIST, -1),
            hre_p.reshape(1, Bp, G, P), him_p.reshape(1, Bp, G, P),
            hist_s[None], hre_s.reshape(1, Bs, G, P), him_s.reshape(1, Bs, G, P))
```

```python
import functools

import jax
import jax.numpy as jnp
from jax import lax
from jax.experimental import pallas as pl
from jax.experimental.pallas import tpu as pltpu

F32 = jnp.float32
BF16 = jnp.bfloat16

EPS = 1e-6
PAST_LEN = 1024
POOL_WINDOWS = (2, 4, 8, 16)
POOL_HIST = max(POOL_WINDOWS) - 1
HIST_ROWS = 16
MEM_HEADS = 4
N_HALF = 2
S5_TILE = 256
S5_DTYPE = F32

PROMPT_BLOCK = 512
PROMPT_CHUNK = 128
FF_CHUNK = 1024
VMEM_LIMIT = 56 * 1024 * 1024


def _rmsnorm(x, g):
    return x * lax.rsqrt(jnp.mean(x * x, axis=-1, keepdims=True) + EPS) * g


def _dot(a, b):
    return jnp.dot(a.astype(BF16), b.astype(BF16), preferred_element_type=F32)


def _pool_windows(zbuf, nrows, pos, r):
    gw = r["w_pool"].shape[-1]
    outs = []
    for g, w in enumerate(POOL_WINDOWS):
        lanes = slice(g * gw, (g + 1) * gw)
        win = zbuf[pl.ds(0, HIST_ROWS + nrows), lanes]
        k = 1
        while k < w:
            win = win + pltpu.roll(win, k, 0)
            k *= 2
        cur = zbuf[pl.ds(HIST_ROWS, nrows), lanes]
        cnt = jnp.minimum(pos + 1, w).astype(F32)
        pooled = win[HIST_ROWS:, :] / cnt - cur
        outs.append(_dot(pooled, r["w_pool"][g]))
    return jnp.concatenate(outs, axis=-1) * r["pool_scale"][...]


def _cmul(ar, ai, br, bi):
    return ar * br - ai * bi, ar * bi + ai * br


def _s5_scan(u, n_chunks, T, end_row, h_carry, h_rows, r, hout_re_ref, hout_im_ref):
    hw = r["bblk"].shape[1]
    nh = r["bblk"].shape[2] // 2
    u_bf = u.astype(BF16)
    r_i = lax.broadcasted_iota(jnp.int32, (T, T), 0)
    c_i = lax.broadcasted_iota(jnp.int32, (T, T), 1)
    ltri = jnp.where(r_i >= c_i, 1.0, 0.0).astype(BF16)

    ys = []
    for k in range(N_HALF):
        uk = u_bf[:, k * hw:(k + 1) * hw]
        yk = None
        for j in range(nh // S5_TILE):
            re_c = slice(j * S5_TILE, (j + 1) * S5_TILE)
            im_c = slice(nh + j * S5_TILE, nh + (j + 1) * S5_TILE)
            st = slice(k * nh + j * S5_TILE, k * nh + (j + 1) * S5_TILE)
            bu_re = _dot(uk, r["bblk"][k, :, re_c])
            bu_im = _dot(uk, r["bblk"][k, :, im_c])
            if h_rows is None:
                carry = (h_carry[0][:, st], h_carry[1][:, st])
            h_re, h_im = [], []
            for c in range(n_chunks):
                rows = slice(c * T, (c + 1) * T)
                xr, xi = _cmul(bu_re[rows].astype(S5_DTYPE), bu_im[rows].astype(S5_DTYPE),
                               r["pneg_re"][0:T, st], r["pneg_im"][0:T, st])
                wr = _dot(ltri, xr)
                wi = _dot(ltri, xi)
                if h_rows is not None:
                    carry = (h_rows[0][pl.ds(c, 1), st], h_rows[1][pl.ds(c, 1), st])
                cr, ci = _cmul(r["a1_re"][:, st], r["a1_im"][:, st], carry[0], carry[1])
                wr, wi = wr + cr, wi + ci
                sr, si = _cmul(r["ppos_re"][0:T, st], r["ppos_im"][0:T, st], wr.astype(S5_DTYPE), wi.astype(S5_DTYPE))
                carry = _cmul(r["pend_re"][end_row:end_row + 1, st], r["pend_im"][end_row:end_row + 1, st],
                              wr[T - 1:T, :], wi[T - 1:T, :])
                if h_rows is not None:
                    hout_re_ref[pl.ds(c, 1), st] = carry[0]
                    hout_im_ref[pl.ds(c, 1), st] = carry[1]
                h_re.append(sr)
                h_im.append(si)
            if h_rows is None:
                hout_re_ref[:, st] = carry[0]
                hout_im_ref[:, st] = carry[1]
            part = (_dot(jnp.concatenate(h_re, axis=0), r["cblk"][k, re_c, :])
                    + _dot(jnp.concatenate(h_im, axis=0), r["cblk"][k, im_c, :]))
            yk = part if yk is None else yk + part
        ys.append(yk)
    return jnp.concatenate(ys, axis=-1)


def _mix_tail(x, y_pool, u, y_state, r):
    y = jax.nn.gelu(y_state + r["d_skip"][...] * u)
    y = y * jax.nn.sigmoid(_dot(y, r["w_glu"][...]) + r["b_glu"][...])
    ycat = jnp.concatenate([y_pool.astype(BF16), y.astype(BF16)], axis=-1)
    return x + _dot(ycat, r["w_out"][...])


_MIXER_IN = ("xp", "xs", "hist_in", "h0_re", "h0_im",
             "g_mix", "w_in", "w_pool", "pool_scale", "bb_re", "bb_im", "cc_re", "cc_im", "a1_re", "a1_im",
             "d_skip", "w_glu", "b_glu", "w_out")
_MIXER_OUT = ("outp", "outs", "histp", "hrep", "himp", "hists", "hres", "hims")
_MIXER_BF16 = ("w_in", "w_pool", "w_glu", "w_out")
_MIXER_GEN = ("bblk", "cblk", "pneg_re", "pneg_im", "ppos_re", "ppos_im", "pend_re", "pend_im")


def _power_table(re1, im1, T):
    SUB = 16
    assert T % SUB == 0
    n = re1.shape[-1]
    t = lax.broadcasted_iota(jnp.int32, (SUB, 1), 0)
    lo_re, lo_im = jnp.ones((SUB, n), F32), jnp.zeros((SUB, n), F32)
    cr, ci = re1, im1
    k = 1
    while k < SUB:
        bit = (t & k) != 0
        lo_re, lo_im = _cmul(lo_re, lo_im, jnp.where(bit, cr, 1.0), jnp.where(bit, ci, 0.0))
        cr, ci = _cmul(cr, ci, cr, ci)
        k *= 2
    hi_re, hi_im = [jnp.ones((1, n), F32)], [jnp.zeros((1, n), F32)]
    for _ in range(T // SUB - 1):
        nr, ni = _cmul(hi_re[-1], hi_im[-1], cr, ci)
        hi_re.append(nr)
        hi_im.append(ni)
    rep = lambda rows: jnp.concatenate([jnp.broadcast_to(v, (SUB, n)) for v in rows], axis=0)
    tile = lambda v: jnp.concatenate([v] * (T // SUB), axis=0)
    return _cmul(tile(lo_re), tile(lo_im), rep(hi_re), rep(hi_im))


def _expand_s5_params(r, gen, T, t_ends):
    are, aim = r["a1_re"][...], r["a1_im"][...]
    pr, pi = _power_table(are, aim, T)
    gen["ppos_re"][...] = pr.astype(S5_DTYPE)
    gen["ppos_im"][...] = pi.astype(S5_DTYPE)
    for j, te in enumerate(t_ends):
        gen["pend_re"][j:j + 1, :] = pr[te - 1:te, :]
        gen["pend_im"][j:j + 1, :] = pi[te - 1:te, :]
    den = are * are + aim * aim
    nr, ni = _power_table(are / den, -aim / den, T)
    gen["pneg_re"][...] = nr.astype(S5_DTYPE)
    gen["pneg_im"][...] = ni.astype(S5_DTYPE)

    n_half, rows_b, H = r["bb_re"].shape
    _, rows_c, P = r["cc_re"].shape
    nh = rows_b

    def spread(x, width):
        w = x.shape[1]
        sel = lax.rem(lax.broadcasted_iota(jnp.int32, (w, width), 1), w) == lax.broadcasted_iota(jnp.int32, (w, width), 0)
        return _dot(x, jnp.where(sel, 1.0, 0.0))

    def same_group(shape, row_size, col_size):
        return (lax.div(lax.broadcasted_iota(jnp.int32, shape, 0), row_size)
                == lax.div(lax.broadcasted_iota(jnp.int32, shape, 1), col_size))

    for k in range(n_half):
        for c, (b_src, c_src, sign) in enumerate(((r["bb_re"], r["cc_re"], 1.0), (r["bb_im"], r["cc_im"], -1.0))):
            yb = spread(b_src[k], rows_c)
            yb = jnp.where(same_group(yb.shape, P, H), yb, 0.0)
            gen["bblk"][k, :, c * nh:(c + 1) * nh] = yb.T.astype(BF16)
            yc = spread(c_src[k], rows_b)
            yc = jnp.where(same_group(yc.shape, H, P), sign * yc, 0.0)
            gen["cblk"][k, c * nh:(c + 1) * nh, :] = yc.T.astype(BF16)


def _mixer_kernel(*refs, n_prompt, n_streams, Tp, Ts):
    r = dict(zip(_MIXER_IN + _MIXER_OUT, refs))
    zbuf, ypool_ref, hp_re, hp_im, hs_re, hs_im, h0f_re, h0f_im = refs[len(_MIXER_IN) + len(_MIXER_OUT):][:8]
    n_fixed = len(_MIXER_IN) + len(_MIXER_OUT) + 8
    bf16_weights = dict(zip(_MIXER_BF16, refs[n_fixed:]))
    gen = dict(zip(_MIXER_GEN, refs[n_fixed + len(_MIXER_BF16):]))
    i = pl.program_id(0)
    ts = r["xp"].shape[0]
    d_pool = zbuf.shape[1]

    @pl.when(i == 0)
    def _():
        zbuf[0:HIST_ROWS, :] = jnp.zeros((HIST_ROWS, d_pool), F32)
        hp_re[...] = jnp.zeros(hp_re.shape, F32)
        hp_im[...] = jnp.zeros(hp_im.shape, F32)
        for name, ref in bf16_weights.items():
            ref[...] = r[name][...].astype(BF16)
        _expand_s5_params(r, gen, Tp, (Tp, Ts))

    r.update(bf16_weights)
    r.update(gen)

    @pl.when(i < n_prompt)
    def _():
        x = r["xp"][...]
        z = _dot(_rmsnorm(x, r["g_mix"][...]), r["w_in"][...])
        zbuf[HIST_ROWS:HIST_ROWS + ts, :] = z[:, :d_pool]
        u = z[:, d_pool:]
        pos = i * ts + lax.broadcasted_iota(jnp.int32, (ts, 1), 0)
        y_pool = _pool_windows(zbuf, ts, pos, r)
        y_state = _s5_scan(u, ts // Tp, Tp, 0, (hp_re[...], hp_im[...]), None, r, hp_re, hp_im)
        r["outp"][...] = _mix_tail(x, y_pool, u, y_state, r)
        tail = zbuf[ts:ts + HIST_ROWS, :]
        zbuf[0:HIST_ROWS, :] = tail
        r["histp"][...] = tail

    @pl.when(i == n_prompt)
    def _():
        ext = HIST_ROWS + Ts
        x = r["xs"][...]
        z = _dot(_rmsnorm(x, r["g_mix"][...]), r["w_in"][...])
        u = z[:, d_pool:]
        for b in range(n_streams):
            zbuf[b * ext:b * ext + 1, :] = jnp.zeros((1, d_pool), F32)
            for t in range(POOL_HIST):
                zbuf[b * ext + 1 + t:b * ext + 2 + t, :] = r["hist_in"][t, b:b + 1, :]
            zbuf[b * ext + HIST_ROWS:(b + 1) * ext, :] = z[b * Ts:(b + 1) * Ts, :d_pool]
            for t in range(POOL_HIST):
                row = (b + 1) * ext - POOL_HIST + t
                r["hists"][t, b:b + 1, :] = zbuf[row:row + 1, :]
        nrows = n_streams * ext - HIST_ROWS
        ridx = lax.broadcasted_iota(jnp.int32, (nrows, 1), 0)
        pos = PAST_LEN + lax.rem(ridx, ext)
        y_all = _pool_windows(zbuf, nrows, pos, r)
        for b in range(n_streams):
            ypool_ref[b * Ts:(b + 1) * Ts, :] = y_all[b * ext:b * ext + Ts, :]
        G, P = r["hrep"].shape
        for src, dst in ((r["h0_re"], h0f_re), (r["h0_im"], h0f_im)):
            for b in range(n_streams):
                for g in range(G):
                    dst[b:b + 1, g * P:(g + 1) * P] = src[b * G + g:b * G + g + 1, :]
        y_state = _s5_scan(u, n_streams, Ts, 1, None, (h0f_re, h0f_im), r, hs_re, hs_im)
        r["outs"][...] = _mix_tail(x, ypool_ref[...], u, y_state, r)
        for g in range(G):
            cols = slice(g * P, (g + 1) * P)
            r["hrep"][g:g + 1, :] = hp_re[:, cols]
            r["himp"][g:g + 1, :] = hp_im[:, cols]
            for b in range(n_streams):
                r["hres"][b * G + g:b * G + g + 1, :] = hs_re[b:b + 1, cols]
                r["hims"][b * G + g:b * G + g + 1, :] = hs_im[b:b + 1, cols]


def _const_spec(shape):
    nd = len(shape)
    return pl.BlockSpec(shape, lambda *_: (0,) * nd)


def _weight_spec(shape):
    nd = len(shape)
    return pl.BlockSpec(shape, lambda *_: (0,) * nd, pipeline_mode=pl.Buffered(1))


def _prompt_spec(ts, d, n_prompt):
    return pl.BlockSpec((ts, d), lambda i: (jnp.minimum(i, n_prompt - 1), 0))


def _mixer(xp, xs, hist_in, h0_re, h0_im, p, n_streams, Ts):
    S, D = xp.shape
    rows_s = xs.shape[0]
    ts = PROMPT_BLOCK
    n_prompt = S // ts
    d_pool = hist_in.shape[-1]
    n_state = p["a1_re"].shape[-1]
    P = h0_re.shape[-1]
    G = n_state // P
    assert n_streams * (HIST_ROWS + Ts) <= HIST_ROWS + ts and rows_s <= ts
    args = dict(p, xp=xp, xs=xs, hist_in=hist_in, h0_re=h0_re, h0_im=h0_im)
    ins = [args[n] for n in _MIXER_IN]
    in_specs = [_prompt_spec(ts, D, n_prompt)] + [_weight_spec(a.shape) for a in ins[1:]]
    n_half, nh, H = p["bb_re"].shape
    d_half = p["cc_re"].shape[1]
    gen_shapes = ([((n_half, d_half, 2 * nh), BF16), ((n_half, 2 * nh, d_half), BF16)]
                  + [((PROMPT_CHUNK, n_state), S5_DTYPE)] * 4 + [((2, n_state), F32)] * 2)
    out_shapes = dict(
        outp=(S, D), outs=(rows_s, D), histp=(HIST_ROWS, d_pool), hrep=(G, P), himp=(G, P),
        hists=(POOL_HIST, n_streams, d_pool), hres=(n_streams * G, P), hims=(n_streams * G, P))
    out_specs = [_prompt_spec(ts, D, n_prompt)] + [_const_spec(out_shapes[n]) for n in _MIXER_OUT[1:]]
    return pl.pallas_call(
        functools.partial(_mixer_kernel, n_prompt=n_prompt, n_streams=n_streams, Tp=PROMPT_CHUNK, Ts=Ts),
        grid=(n_prompt + 1,),
        in_specs=in_specs,
        out_specs=out_specs,
        out_shape=[jax.ShapeDtypeStruct(out_shapes[n], F32) for n in _MIXER_OUT],
        scratch_shapes=[pltpu.VMEM((HIST_ROWS + ts, d_pool), F32),
                        pltpu.VMEM((rows_s, d_pool), F32)]
        + [pltpu.VMEM((1, n_state), F32)] * 2 + [pltpu.VMEM((n_streams, n_state), F32)] * 4
        + [pltpu.VMEM(args[n].shape, BF16) for n in _MIXER_BF16]
        + [pltpu.VMEM(shape, dtype) for shape, dtype in gen_shapes],
        compiler_params=pltpu.CompilerParams(dimension_semantics=("arbitrary",),
                                             vmem_limit_bytes=VMEM_LIMIT),
        name="mixer",
    )(*ins)


def _softmax_rows(s):
    e = jnp.exp(s - jnp.max(s, axis=-1, keepdims=True))
    return e / jnp.sum(e, axis=-1, keepdims=True)


def _attend(q_heads, k_heads, v_heads, scale):
    outs = []
    for q, k, v in zip(q_heads, k_heads, v_heads):
        s = lax.dot_general(q, k.astype(BF16), (((1,), (1,)), ((), ())), preferred_element_type=F32) * scale
        outs.append(_dot(_softmax_rows(s), v))
    return outs


_XATTN_IN = ("xp", "xs", "mem", "kc", "vc", "g_mem", "w_k", "w_v", "g_xattn", "w_q", "w_o")
_XATTN_OUT = ("outp", "outs", "mk", "mv")


def _xattn_kernel(*refs, n_prompt, n_streams, Ts, hpu, scale):
    r = dict(zip(_XATTN_IN + _XATTN_OUT, refs))
    kp_ref, vp_ref, qs_ref, os_ref, kbuf, vbuf, sem, kvf, osem = refs[len(_XATTN_IN) + len(_XATTN_OUT):]
    i = pl.program_id(0)
    hd = kp_ref.shape[-1] // MEM_HEADS
    ups = MEM_HEADS // hpu

    def unit_copies(unit, hg):
        b, slot = unit // ups, lax.rem(unit, 2)
        return [pltpu.make_async_copy(src.at[0, b, :, hg * hpu + e, :], dst.at[slot, e], sem.at[slot, j * hpu + e])
                for j, (src, dst) in enumerate(((r["kc"], kbuf), (r["vc"], vbuf))) for e in range(hpu)]

    def kv_out_copies():
        return [pltpu.make_async_copy(kvf.at[j, h], dst.at[0, 0, :, h, :], osem.at[j, h])
                for j, dst in enumerate((r["mk"], r["mv"])) for h in range(MEM_HEADS)]

    def for_unit(unit, fn):
        for hg in range(ups):
            @pl.when(lax.rem(unit, ups) == hg)
            def _():
                for cp in unit_copies(unit, hg):
                    fn(cp)

    @pl.when(i == 0)
    def _():
        for_unit(i, lambda cp: cp.start())
        m = _rmsnorm(r["mem"][...], r["g_mem"][...]).astype(BF16)
        k = _dot(m, r["w_k"][...])
        v = _dot(m, r["w_v"][...])
        for h in range(MEM_HEADS):
            kvf[0, h] = k[:, h * hd:(h + 1) * hd]
            kvf[1, h] = v[:, h * hd:(h + 1) * hd]
        for cp in kv_out_copies():
            cp.start()
        kp_ref[...] = k.astype(BF16)
        vp_ref[...] = v.astype(BF16)
        q = _dot(_rmsnorm(r["xs"][...], r["g_xattn"][...]), r["w_q"][...]).astype(BF16)
        for h in range(MEM_HEADS):
            qs_ref[h] = q[:, h * hd:(h + 1) * hd]

    @pl.when(i < n_prompt)
    def _():
        @pl.when(i + 1 < n_prompt)
        def _():
            for_unit(i + 1, lambda cp: cp.start())

        for_unit(i, lambda cp: cp.wait())

        x = r["xp"][...]
        q = _dot(_rmsnorm(x, r["g_xattn"][...]), r["w_q"][...]).astype(BF16)
        o = _attend([q[:, h * hd:(h + 1) * hd] for h in range(MEM_HEADS)],
                    [kp_ref[:, h * hd:(h + 1) * hd] for h in range(MEM_HEADS)],
                    [vp_ref[:, h * hd:(h + 1) * hd] for h in range(MEM_HEADS)], scale)
        r["outp"][...] = x + _dot(jnp.concatenate(o, axis=-1), r["w_o"][...])

        b, hg, slot = i // ups, lax.rem(i, ups), lax.rem(i, 2)
        rows = pl.ds(pl.multiple_of(b * Ts, Ts), Ts)
        heads = [hg * hpu + e for e in range(hpu)]
        o_s = _attend([qs_ref[h, rows, :] for h in heads], [kbuf[slot, e] for e in range(hpu)],
                      [vbuf[slot, e] for e in range(hpu)], scale)
        for h, o_h in zip(heads, o_s):
            os_ref[h, rows, :] = o_h.astype(BF16)

    @pl.when(i == n_prompt)
    def _():
        for cp in kv_out_copies():
            cp.wait()
        o = jnp.concatenate([os_ref[h] for h in range(MEM_HEADS)], axis=-1)
        r["outs"][...] = r["xs"][...] + _dot(o, r["w_o"][...])


def _xattn(xp, xs, mem, kc, vc, p, n_streams, Ts):
    S, D = xp.shape
    rows_s = xs.shape[0]
    ts = PROMPT_BLOCK
    n_prompt = S // ts
    n_mem = mem.shape[0]
    hd = D // MEM_HEADS
    hpu = n_streams * MEM_HEADS // n_prompt
    assert hpu * n_prompt == n_streams * MEM_HEADS and MEM_HEADS % hpu == 0
    args = dict(p, xp=xp, xs=xs, mem=mem, kc=kc, vc=vc)
    ins = [args[n] for n in _XATTN_IN]
    cache_spec = pl.BlockSpec(memory_space=pl.ANY)
    in_specs = [_prompt_spec(ts, D, n_prompt)]
    for n in _XATTN_IN[1:]:
        in_specs.append(cache_spec if n in ("kc", "vc") else _weight_spec(args[n].shape))
    out_shapes = dict(outp=(S, D), outs=(rows_s, D), mk=(1, 1, n_mem, MEM_HEADS, hd), mv=(1, 1, n_mem, MEM_HEADS, hd))
    out_specs = [_prompt_spec(ts, D, n_prompt), _const_spec(out_shapes["outs"]), cache_spec, cache_spec]
    return pl.pallas_call(
        functools.partial(_xattn_kernel, n_prompt=n_prompt, n_streams=n_streams, Ts=Ts, hpu=hpu, scale=hd ** -0.5),
        grid=(n_prompt + 1,),
        in_specs=in_specs,
        out_specs=out_specs,
        out_shape=[jax.ShapeDtypeStruct(out_shapes[n], F32) for n in _XATTN_OUT],
        scratch_shapes=[pltpu.VMEM((n_mem, D), BF16), pltpu.VMEM((n_mem, D), BF16),
                        pltpu.VMEM((MEM_HEADS, rows_s, hd), BF16), pltpu.VMEM((MEM_HEADS, rows_s, hd), BF16),
                        pltpu.VMEM((2, hpu, n_mem, hd), F32), pltpu.VMEM((2, hpu, n_mem, hd), F32),
                        pltpu.SemaphoreType.DMA((2, 2 * hpu)),
                        pltpu.VMEM((2, MEM_HEADS, n_mem, hd), F32), pltpu.SemaphoreType.DMA((2, MEM_HEADS))],
        compiler_params=pltpu.CompilerParams(dimension_semantics=("arbitrary",),
                                             vmem_limit_bytes=VMEM_LIMIT),
        name="xattn",
    )(*ins)


def _mlp_rows(x, g_ref, wup_ref, wdown_ref, gfin_ref):
    hm = _rmsnorm(x, g_ref[...]).astype(BF16)
    acc = x
    for j in range(wup_ref.shape[1] // FF_CHUNK):
        cols = slice(j * FF_CHUNK, (j + 1) * FF_CHUNK)
        up = _dot(hm, wup_ref[:, cols])
        acc = acc + _dot(jnp.square(jnp.maximum(up, 0.0)), wdown_ref[cols, :])
    return _rmsnorm(acc, gfin_ref[...])


def _mlp_kernel(xp_ref, xs_ref, g_ref, wup_ref, wdown_ref, gfin_ref, outp_ref, outs_ref, *, n_prompt):
    i = pl.program_id(0)

    @pl.when(i < n_prompt)
    def _():
        outp_ref[...] = _mlp_rows(xp_ref[...], g_ref, wup_ref, wdown_ref, gfin_ref)

    @pl.when(i == n_prompt)
    def _():
        outs_ref[...] = _mlp_rows(xs_ref[...], g_ref, wup_ref, wdown_ref, gfin_ref)


def _mlp(xp, xs, p):
    S, D = xp.shape
    ts = PROMPT_BLOCK
    n_prompt = S // ts
    weights = (p["g_mlp"], p["w_up"], p["w_down"], p["g_final"])
    return pl.pallas_call(
        functools.partial(_mlp_kernel, n_prompt=n_prompt),
        grid=(n_prompt + 1,),
        in_specs=[_prompt_spec(ts, D, n_prompt), _weight_spec(xs.shape)] + [_weight_spec(w.shape) for w in weights],
        out_specs=[_prompt_spec(ts, D, n_prompt), _const_spec(xs.shape)],
        out_shape=[jax.ShapeDtypeStruct(xp.shape, F32), jax.ShapeDtypeStruct(xs.shape, F32)],
        compiler_params=pltpu.CompilerParams(dimension_semantics=("arbitrary",),
                                             vmem_limit_bytes=VMEM_LIMIT),
        name="mlp",
    )(xp, xs, *weights)


def _prepare_layer(l, g_mix, w_in, w_pool, pool_scale, a_re, a_im, b_re, b_im, c_re, c_im, d_skip,
                   log_dt, w_glu, b_glu, w_out, g_xattn, g_mem, w_q, w_k, w_v, w_o, g_mlp, w_up,
                   w_down, g_final):
    G, P = a_re.shape[1:]
    gi = G // N_HALF
    ar, ai = a_re[l].astype(F32), a_im[l].astype(F32)
    dt = jnp.exp(log_dt[l].astype(F32))[:, None]
    lam_re, lam_im = ar * dt, ai * dt

    mag = jnp.exp(lam_re)
    ab_re, ab_im = mag * jnp.cos(lam_im), mag * jnp.sin(lam_im)

    den = ar * ar + ai * ai
    coef_re = ((ab_re - 1.0) * ar + ab_im * ai) / den
    coef_im = (ab_im * ar - (ab_re - 1.0) * ai) / den
    br, bi = b_re[l].astype(F32), b_im[l].astype(F32)
    bb_re = coef_re[..., None] * br - coef_im[..., None] * bi
    bb_im = coef_re[..., None] * bi + coef_im[..., None] * br

    half = lambda t: t.astype(F32).reshape((N_HALF, gi * t.shape[1], t.shape[2]))
    row = lambda v: v.astype(F32).reshape(1, -1)
    return dict(
        g_mix=row(g_mix[l]), w_in=w_in[l], w_pool=w_pool[l], pool_scale=row(pool_scale[l]),
        bb_re=half(bb_re), bb_im=half(bb_im), cc_re=half(c_re[l]), cc_im=half(c_im[l]),
        a1_re=row(ab_re), a1_im=row(ab_im),
        d_skip=row(d_skip[l]), w_glu=w_glu[l], b_glu=row(b_glu[l]), w_out=w_out[l],
        g_xattn=row(g_xattn[l]), g_mem=row(g_mem[l]), w_q=w_q[l], w_k=w_k[l], w_v=w_v[l], w_o=w_o[l],
        g_mlp=row(g_mlp[l]), w_up=w_up[l], w_down=w_down[l], g_final=row(g_final))


def kernel(x_prompt, x_sample, cache_mem_k, cache_mem_v, state_pool, state_ssm_re, state_ssm_im, mem_prompt, g_mix, w_in, w_pool, pool_scale, ssm_a_re, ssm_a_im, ssm_b_re, ssm_b_im, ssm_c_re, ssm_c_im, ssm_d, ssm_log_dt, w_glu, b_glu, w_out, g_xattn, g_mem, w_q, w_k, w_v, w_o, g_mlp, w_up, w_down, g_final):
    depth = g_mix.shape[0]
    assert depth == 1 and x_prompt.shape[0] == 1, "single layer, single prompt stream"
    Bp, S, D = x_prompt.shape
    Bs, Ts, _ = x_sample.shape
    n_mem = mem_prompt.shape[1]
    G, P = ssm_a_re.shape[1:]
    assert S % PROMPT_BLOCK == 0 and PROMPT_BLOCK % PROMPT_CHUNK == 0 and Ts <= PROMPT_CHUNK

    l = 0
    p = _prepare_layer(l, g_mix, w_in, w_pool, pool_scale, ssm_a_re, ssm_a_im, ssm_b_re, ssm_b_im,
                       ssm_c_re, ssm_c_im, ssm_d, ssm_log_dt, w_glu, b_glu, w_out, g_xattn, g_mem,
                       w_q, w_k, w_v, w_o, g_mlp, w_up, w_down, g_final)

    xp1, xs1, hist_p, hre_p, him_p, hist_s, hre_s, him_s = _mixer(
        x_prompt[0], x_sample.reshape(Bs * Ts, D), jnp.transpose(state_pool[l], (1, 0, 2)),
        state_ssm_re[l].reshape(Bs * G, P), state_ssm_im[l].reshape(Bs * G, P), p, Bs, Ts)
    xp2, xs2, mk, mv = _xattn(xp1, xs1, mem_prompt[0], cache_mem_k[l:l + 1], cache_mem_v[l:l + 1], p, Bs, Ts)
    y_prompt, y_sample = _mlp(xp2, xs2, p)

    hd = D // MEM_HEADS
    return (y_prompt[None], y_sample.reshape(Bs, Ts, D),
            mk, mv,
            hist_p[1:].reshape(1, Bp, POOL_HIST, -1),
            hre_p.reshape(1, Bp, G, P), him_p.reshape(1, Bp, G, P),
            jnp.transpose(hist_s, (1, 0, 2))[None], hre_s.reshape(1, Bs, G, P), him_s.reshape(1, Bs, G, P))
```

```python
import functools

import jax
import jax.numpy as jnp
from jax import lax
from jax.experimental import pallas as pl
from jax.experimental.pallas import tpu as pltpu

F32 = jnp.float32
BF16 = jnp.bfloat16

EPS = 1e-6
PAST_LEN = 1024
POOL_WINDOWS = (2, 4, 8, 16)
POOL_HIST = max(POOL_WINDOWS) - 1
HIST_ROWS = 16
MEM_HEADS = 4
N_HALF = 2
S5_TILE = 256
S5_DTYPE = F32

PROMPT_BLOCK = 512
PROMPT_CHUNK = 128
FF_CHUNK = 1024
VMEM_LIMIT = 56 * 1024 * 1024


def _rmsnorm(x, g):
    return x * lax.rsqrt(jnp.mean(x * x, axis=-1, keepdims=True) + EPS) * g


def _dot(a, b):
    return jnp.dot(a.astype(BF16), b.astype(BF16), preferred_element_type=F32)


def _pool_windows(zbuf, nrows, pos, r):
    gw = r["w_pool"].shape[-1]
    outs = []
    for g, w in enumerate(POOL_WINDOWS):
        lanes = slice(g * gw, (g + 1) * gw)
        win = zbuf[pl.ds(0, HIST_ROWS + nrows), lanes]
        k = 1
        while k < w:
            win = win + pltpu.roll(win, k, 0)
            k *= 2
        cur = zbuf[pl.ds(HIST_ROWS, nrows), lanes]
        cnt = jnp.minimum(pos + 1, w).astype(F32)
        pooled = win[HIST_ROWS:, :] / cnt - cur
        outs.append(_dot(pooled, r["w_pool"][g]))
    return jnp.concatenate(outs, axis=-1) * r["pool_scale"][...]


def _cmul(ar, ai, br, bi):
    return ar * br - ai * bi, ar * bi + ai * br


def _s5_scan(u, n_chunks, T, end_row, h_carry, h_rows, r, hout_re_ref, hout_im_ref):
    hw = r["bblk"].shape[1]
    nh = r["bblk"].shape[2] // 2
    u_bf = u.astype(BF16)
    r_i = lax.broadcasted_iota(jnp.int32, (T, T), 0)
    c_i = lax.broadcasted_iota(jnp.int32, (T, T), 1)
    ltri = jnp.where(r_i >= c_i, 1.0, 0.0).astype(BF16)

    ys = []
    for k in range(N_HALF):
        uk = u_bf[:, k * hw:(k + 1) * hw]
        yk = None
        for j in range(nh // S5_TILE):
            re_c = slice(j * S5_TILE, (j + 1) * S5_TILE)
            im_c = slice(nh + j * S5_TILE, nh + (j + 1) * S5_TILE)
            st = slice(k * nh + j * S5_TILE, k * nh + (j + 1) * S5_TILE)
            bu_re = _dot(uk, r["bblk"][k, :, re_c])
            bu_im = _dot(uk, r["bblk"][k, :, im_c])
            if h_rows is None:
                carry = (h_carry[0][:, st], h_carry[1][:, st])
            h_re, h_im = [], []
            for c in range(n_chunks):
                rows = slice(c * T, (c + 1) * T)
                xr, xi = _cmul(bu_re[rows].astype(S5_DTYPE), bu_im[rows].astype(S5_DTYPE),
                               r["pneg_re"][0:T, st], r["pneg_im"][0:T, st])
                wr = _dot(ltri, xr)
                wi = _dot(ltri, xi)
                if h_rows is not None:
                    carry = (h_rows[0][pl.ds(c, 1), st], h_rows[1][pl.ds(c, 1), st])
                cr, ci = _cmul(r["a1_re"][:, st], r["a1_im"][:, st], carry[0], carry[1])
                wr, wi = wr + cr, wi + ci
                sr, si = _cmul(r["ppos_re"][0:T, st], r["ppos_im"][0:T, st], wr.astype(S5_DTYPE), wi.astype(S5_DTYPE))
                carry = _cmul(r["pend_re"][end_row:end_row + 1, st], r["pend_im"][end_row:end_row + 1, st],
                              wr[T - 1:T, :], wi[T - 1:T, :])
                if h_rows is not None:
                    hout_re_ref[pl.ds(c, 1), st] = carry[0]
                    hout_im_ref[pl.ds(c, 1), st] = carry[1]
                h_re.append(sr)
                h_im.append(si)
            if h_rows is None:
                hout_re_ref[:, st] = carry[0]
                hout_im_ref[:, st] = carry[1]
            part = (_dot(jnp.concatenate(h_re, axis=0), r["cblk"][k, re_c, :])
                    + _dot(jnp.concatenate(h_im, axis=0), r["cblk"][k, im_c, :]))
            yk = part if yk is None else yk + part
        ys.append(yk)
    return jnp.concatenate(ys, axis=-1)


def _mix_tail(x, y_pool, u, y_state, r):
    y = jax.nn.gelu(y_state + r["d_skip"][...] * u)
    y = y * jax.nn.sigmoid(_dot(y, r["w_glu"][...]) + r["b_glu"][...])
    ycat = jnp.concatenate([y_pool.astype(BF16), y.astype(BF16)], axis=-1)
    return x + _dot(ycat, r["w_out"][...])


_MIXER_IN = ("xp", "xs", "hist_in", "h0_re", "h0_im",
             "g_mix", "w_in", "w_pool", "pool_scale", "bb_re", "bb_im", "cc_re", "cc_im", "a1_re", "a1_im",
             "d_skip", "w_glu", "b_glu", "w_out")
_MIXER_OUT = ("outp", "outs", "histp", "hrep", "himp", "hists", "hres", "hims")
_MIXER_BF16 = ("w_in", "w_pool", "w_glu", "w_out")
_MIXER_GEN = ("bblk", "cblk", "pneg_re", "pneg_im", "ppos_re", "ppos_im", "pend_re", "pend_im")


def _power_table(re1, im1, T):
    SUB = 16
    assert T % SUB == 0
    n = re1.shape[-1]
    t = lax.broadcasted_iota(jnp.int32, (SUB, 1), 0)
    lo_re, lo_im = jnp.ones((SUB, n), F32), jnp.zeros((SUB, n), F32)
    cr, ci = re1, im1
    k = 1
    while k < SUB:
        bit = (t & k) != 0
        lo_re, lo_im = _cmul(lo_re, lo_im, jnp.where(bit, cr, 1.0), jnp.where(bit, ci, 0.0))
        cr, ci = _cmul(cr, ci, cr, ci)
        k *= 2
    hi_re, hi_im = [jnp.ones((1, n), F32)], [jnp.zeros((1, n), F32)]
    for _ in range(T // SUB - 1):
        nr, ni = _cmul(hi_re[-1], hi_im[-1], cr, ci)
        hi_re.append(nr)
        hi_im.append(ni)
    rep = lambda rows: jnp.concatenate([jnp.broadcast_to(v, (SUB, n)) for v in rows], axis=0)
    tile = lambda v: jnp.concatenate([v] * (T // SUB), axis=0)
    return _cmul(tile(lo_re), tile(lo_im), rep(hi_re), rep(hi_im))


def _expand_s5_params(r, gen, T, t_ends):
    are, aim = r["a1_re"][...], r["a1_im"][...]
    pr, pi = _power_table(are, aim, T)
    gen["ppos_re"][...] = pr.astype(S5_DTYPE)
    gen["ppos_im"][...] = pi.astype(S5_DTYPE)
    for j, te in enumerate(t_ends):
        gen["pend_re"][j:j + 1, :] = pr[te - 1:te, :]
        gen["pend_im"][j:j + 1, :] = pi[te - 1:te, :]
    den = are * are + aim * aim
    nr, ni = _power_table(are / den, -aim / den, T)
    gen["pneg_re"][...] = nr.astype(S5_DTYPE)
    gen["pneg_im"][...] = ni.astype(S5_DTYPE)

    n_half, rows_b, H = r["bb_re"].shape
    _, rows_c, P = r["cc_re"].shape
    nh = rows_b

    def spread(x, width):
        w = x.shape[1]
        sel = lax.rem(lax.broadcasted_iota(jnp.int32, (w, width), 1), w) == lax.broadcasted_iota(jnp.int32, (w, width), 0)
        return _dot(x, jnp.where(sel, 1.0, 0.0))

    def same_group(shape, row_size, col_size):
        return (lax.div(lax.broadcasted_iota(jnp.int32, shape, 0), row_size)
                == lax.div(lax.broadcasted_iota(jnp.int32, shape, 1), col_size))

    for k in range(n_half):
        for c, (b_src, c_src, sign) in enumerate(((r["bb_re"], r["cc_re"], 1.0), (r["bb_im"], r["cc_im"], -1.0))):
            yb = spread(b_src[k], rows_c)
            yb = jnp.where(same_group(yb.shape, P, H), yb, 0.0)
            gen["bblk"][k, :, c * nh:(c + 1) * nh] = yb.T.astype(BF16)
            yc = spread(c_src[k], rows_b)
            yc = jnp.where(same_group(yc.shape, H, P), sign * yc, 0.0)
            gen["cblk"][k, c * nh:(c + 1) * nh, :] = yc.T.astype(BF16)


def _mixer_kernel(*refs, n_prompt, n_streams, Tp, Ts):
    r = dict(zip(_MIXER_IN + _MIXER_OUT, refs))
    zbuf, ypool_ref, hp_re, hp_im, hs_re, hs_im, h0f_re, h0f_im = refs[len(_MIXER_IN) + len(_MIXER_OUT):][:8]
    n_fixed = len(_MIXER_IN) + len(_MIXER_OUT) + 8
    bf16_weights = dict(zip(_MIXER_BF16, refs[n_fixed:]))
    gen = dict(zip(_MIXER_GEN, refs[n_fixed + len(_MIXER_BF16):]))
    i = pl.program_id(0)
    ts = r["xp"].shape[0]
    d_pool = zbuf.shape[1]

    @pl.when(i == 0)
    def _():
        zbuf[0:HIST_ROWS, :] = jnp.zeros((HIST_ROWS, d_pool), F32)
        hp_re[...] = jnp.zeros(hp_re.shape, F32)
        hp_im[...] = jnp.zeros(hp_im.shape, F32)
        for name, ref in bf16_weights.items():
            ref[...] = r[name][...].astype(BF16)
        _expand_s5_params(r, gen, Tp, (Tp, Ts))

    r.update(bf16_weights)
    r.update(gen)

    @pl.when(i < n_prompt)
    def _():
        x = r["xp"][...]
        z = _dot(_rmsnorm(x, r["g_mix"][...]), r["w_in"][...])
        zbuf[HIST_ROWS:HIST_ROWS + ts, :] = z[:, :d_pool]
        u = z[:, d_pool:]
        pos = i * ts + lax.broadcasted_iota(jnp.int32, (ts, 1), 0)
        y_pool = _pool_windows(zbuf, ts, pos, r)
        y_state = _s5_scan(u, ts // Tp, Tp, 0, (hp_re[...], hp_im[...]), None, r, hp_re, hp_im)
        r["outp"][...] = _mix_tail(x, y_pool, u, y_state, r)
        tail = zbuf[ts:ts + HIST_ROWS, :]
        zbuf[0:HIST_ROWS, :] = tail
        r["histp"][...] = tail

    @pl.when(i == n_prompt)
    def _():
        ext = HIST_ROWS + Ts
        x = r["xs"][...]
        z = _dot(_rmsnorm(x, r["g_mix"][...]), r["w_in"][...])
        u = z[:, d_pool:]
        for b in range(n_streams):
            zbuf[b * ext:b * ext + 1, :] = jnp.zeros((1, d_pool), F32)
            for t in range(POOL_HIST):
                zbuf[b * ext + 1 + t:b * ext + 2 + t, :] = r["hist_in"][t, b:b + 1, :]
            zbuf[b * ext + HIST_ROWS:(b + 1) * ext, :] = z[b * Ts:(b + 1) * Ts, :d_pool]
            for t in range(POOL_HIST):
                row = (b + 1) * ext - POOL_HIST + t
                r["hists"][t, b:b + 1, :] = zbuf[row:row + 1, :]
        nrows = n_streams * ext - HIST_ROWS
        ridx = lax.broadcasted_iota(jnp.int32, (nrows, 1), 0)
        pos = PAST_LEN + lax.rem(ridx, ext)
        y_all = _pool_windows(zbuf, nrows, pos, r)
        for b in range(n_streams):
            ypool_ref[b * Ts:(b + 1) * Ts, :] = y_all[b * ext:b * ext + Ts, :]
        G, P = r["hrep"].shape
        for src, dst in ((r["h0_re"], h0f_re), (r["h0_im"], h0f_im)):
            for b in range(n_streams):
                for g in range(G):
                    dst[b:b + 1, g * P:(g + 1) * P] = src[b * G + g:b * G + g + 1, :]
        y_state = _s5_scan(u, n_streams, Ts, 1, None, (h0f_re, h0f_im), r, hs_re, hs_im)
        r["outs"][...] = _mix_tail(x, ypool_ref[...], u, y_state, r)
        for g in range(G):
            cols = slice(g * P, (g + 1) * P)
            r["hrep"][g:g + 1, :] = hp_re[:, cols]
            r["himp"][g:g + 1, :] = hp_im[:, cols]
            for b in range(n_streams):
                r["hres"][b * G + g:b * G + g + 1, :] = hs_re[b:b + 1, cols]
                r["hims"][b * G + g:b * G + g + 1, :] = hs_im[b:b + 1, cols]


def _const_spec(shape):
    nd = len(shape)
    return pl.BlockSpec(shape, lambda *_: (0,) * nd)


def _weight_spec(shape):
    nd = len(shape)
    return pl.BlockSpec(shape, lambda *_: (0,) * nd, pipeline_mode=pl.Buffered(1))


def _prompt_spec(ts, d, n_prompt):
    return pl.BlockSpec((ts, d), lambda i: (jnp.minimum(i, n_prompt - 1), 0))


def _mixer(xp, xs, hist_in, h0_re, h0_im, p, n_streams, Ts):
    S, D = xp.shape
    rows_s = xs.shape[0]
    ts = PROMPT_BLOCK
    n_prompt = S // ts
    d_pool = hist_in.shape[-1]
    n_state = p["a1_re"].shape[-1]
    P = h0_re.shape[-1]
    G = n_state // P
    assert n_streams * (HIST_ROWS + Ts) <= HIST_ROWS + ts and rows_s <= ts
    args = dict(p, xp=xp, xs=xs, hist_in=hist_in, h0_re=h0_re, h0_im=h0_im)
    ins = [args[n] for n in _MIXER_IN]
    in_specs = [_prompt_spec(ts, D, n_prompt)] + [_weight_spec(a.shape) for a in ins[1:]]
    n_half, nh, H = p["bb_re"].shape
    d_half = p["cc_re"].shape[1]
    gen_shapes = ([((n_half, d_half, 2 * nh), BF16), ((n_half, 2 * nh, d_half), BF16)]
                  + [((PROMPT_CHUNK, n_state), S5_DTYPE)] * 4 + [((2, n_state), F32)] * 2)
    out_shapes = dict(
        outp=(S, D), outs=(rows_s, D), histp=(HIST_ROWS, d_pool), hrep=(G, P), himp=(G, P),
        hists=(POOL_HIST, n_streams, d_pool), hres=(n_streams * G, P), hims=(n_streams * G, P))
    out_specs = [_prompt_spec(ts, D, n_prompt)] + [_const_spec(out_shapes[n]) for n in _MIXER_OUT[1:]]
    return pl.pallas_call(
        functools.partial(_mixer_kernel, n_prompt=n_prompt, n_streams=n_streams, Tp=PROMPT_CHUNK, Ts=Ts),
        grid=(n_prompt + 1,),
        in_specs=in_specs,
        out_specs=out_specs,
        out_shape=[jax.ShapeDtypeStruct(out_shapes[n], F32) for n in _MIXER_OUT],
        scratch_shapes=[pltpu.VMEM((HIST_ROWS + ts, d_pool), F32),
                        pltpu.VMEM((rows_s, d_pool), F32)]
        + [pltpu.VMEM((1, n_state), F32)] * 2 + [pltpu.VMEM((n_streams, n_state), F32)] * 4
        + [pltpu.VMEM(args[n].shape, BF16) for n in _MIXER_BF16]
        + [pltpu.VMEM(shape, dtype) for shape, dtype in gen_shapes],
        compiler_params=pltpu.CompilerParams(dimension_semantics=("arbitrary",),
                                             vmem_limit_bytes=VMEM_LIMIT),
        name="mixer",
    )(*ins)


def _attend(q_heads, k_heads, v_heads, scale):
    outs = []
    for q, k, v in zip(q_heads, k_heads, v_heads):
        s = lax.dot_general(q, k.astype(BF16), (((1,), (1,)), ((), ())), preferred_element_type=F32) * scale
        e = jnp.exp(s - jnp.max(s, axis=-1, keepdims=True))
        outs.append(_dot(e, v) / jnp.sum(e, axis=-1, keepdims=True))
    return outs


_XATTN_IN = ("xp", "xs", "mem", "kc", "vc", "g_mem", "w_k", "w_v", "g_xattn", "w_q", "w_o")
_XATTN_OUT = ("outp", "outs", "mk", "mv")


def _xattn_kernel(*refs, n_prompt, n_streams, Ts, hpu, scale):
    r = dict(zip(_XATTN_IN + _XATTN_OUT, refs))
    kp_ref, vp_ref, qs_ref, os_ref, kbuf, vbuf, sem, kvf, osem = refs[len(_XATTN_IN) + len(_XATTN_OUT):]
    i = pl.program_id(0)
    hd = kp_ref.shape[-1] // MEM_HEADS
    ups = MEM_HEADS // hpu

    def unit_copies(unit, hg):
        b, slot = unit // ups, lax.rem(unit, 2)
        return [pltpu.make_async_copy(src.at[0, b, :, hg * hpu + e, :], dst.at[slot, e], sem.at[slot, j * hpu + e])
                for j, (src, dst) in enumerate(((r["kc"], kbuf), (r["vc"], vbuf))) for e in range(hpu)]

    def kv_out_copies():
        return [pltpu.make_async_copy(kvf.at[j, h], dst.at[0, 0, :, h, :], osem.at[j, h])
                for j, dst in enumerate((r["mk"], r["mv"])) for h in range(MEM_HEADS)]

    def for_unit(unit, fn):
        for hg in range(ups):
            @pl.when(lax.rem(unit, ups) == hg)
            def _():
                for cp in unit_copies(unit, hg):
                    fn(cp)

    @pl.when(i == 0)
    def _():
        for_unit(i, lambda cp: cp.start())
        m = _rmsnorm(r["mem"][...], r["g_mem"][...]).astype(BF16)
        k = _dot(m, r["w_k"][...])
        v = _dot(m, r["w_v"][...])
        for h in range(MEM_HEADS):
            kvf[0, h] = k[:, h * hd:(h + 1) * hd]
            kvf[1, h] = v[:, h * hd:(h + 1) * hd]
        for cp in kv_out_copies():
            cp.start()
        kp_ref[...] = k.astype(BF16)
        vp_ref[...] = v.astype(BF16)
        q = _dot(_rmsnorm(r["xs"][...], r["g_xattn"][...]), r["w_q"][...]).astype(BF16)
        for h in range(MEM_HEADS):
            qs_ref[h] = q[:, h * hd:(h + 1) * hd]

    @pl.when(i < n_prompt)
    def _():
        @pl.when(i + 1 < n_prompt)
        def _():
            for_unit(i + 1, lambda cp: cp.start())

        for_unit(i, lambda cp: cp.wait())

        x = r["xp"][...]
        q = _dot(_rmsnorm(x, r["g_xattn"][...]), r["w_q"][...]).astype(BF16)
        o = _attend([q[:, h * hd:(h + 1) * hd] for h in range(MEM_HEADS)],
                    [kp_ref[:, h * hd:(h + 1) * hd] for h in range(MEM_HEADS)],
                    [vp_ref[:, h * hd:(h + 1) * hd] for h in range(MEM_HEADS)], scale)
        r["outp"][...] = x + _dot(jnp.concatenate(o, axis=-1), r["w_o"][...])

        b, hg, slot = i // ups, lax.rem(i, ups), lax.rem(i, 2)
        rows = pl.ds(pl.multiple_of(b * Ts, Ts), Ts)
        heads = [hg * hpu + e for e in range(hpu)]
        o_s = _attend([qs_ref[h, rows, :] for h in heads], [kbuf[slot, e] for e in range(hpu)],
                      [vbuf[slot, e] for e in range(hpu)], scale)
        for h, o_h in zip(heads, o_s):
            os_ref[h, rows, :] = o_h.astype(BF16)

    @pl.when(i == n_prompt)
    def _():
        for cp in kv_out_copies():
            cp.wait()
        o = jnp.concatenate([os_ref[h] for h in range(MEM_HEADS)], axis=-1)
        r["outs"][...] = r["xs"][...] + _dot(o, r["w_o"][...])


def _xattn(xp, xs, mem, kc, vc, p, n_streams, Ts):
    S, D = xp.shape
    rows_s = xs.shape[0]
    ts = PROMPT_BLOCK
    n_prompt = S // ts
    n_mem = mem.shape[0]
    hd = D // MEM_HEADS
    hpu = n_streams * MEM_HEADS // n_prompt
    assert hpu * n_prompt == n_streams * MEM_HEADS and MEM_HEADS % hpu == 0
    args = dict(p, xp=xp, xs=xs, mem=mem, kc=kc, vc=vc)
    ins = [args[n] for n in _XATTN_IN]
    cache_spec = pl.BlockSpec(memory_space=pl.ANY)
    in_specs = [_prompt_spec(ts, D, n_prompt)]
    for n in _XATTN_IN[1:]:
        in_specs.append(cache_spec if n in ("kc", "vc") else _weight_spec(args[n].shape))
    out_shapes = dict(outp=(S, D), outs=(rows_s, D), mk=(1, 1, n_mem, MEM_HEADS, hd), mv=(1, 1, n_mem, MEM_HEADS, hd))
    out_specs = [_prompt_spec(ts, D, n_prompt), _const_spec(out_shapes["outs"]), cache_spec, cache_spec]
    return pl.pallas_call(
        functools.partial(_xattn_kernel, n_prompt=n_prompt, n_streams=n_streams, Ts=Ts, hpu=hpu, scale=hd ** -0.5),
        grid=(n_prompt + 1,),
        in_specs=in_specs,
        out_specs=out_specs,
        out_shape=[jax.ShapeDtypeStruct(out_shapes[n], F32) for n in _XATTN_OUT],
        scratch_shapes=[pltpu.VMEM((n_mem, D), BF16), pltpu.VMEM((n_mem, D), BF16),
                        pltpu.VMEM((MEM_HEADS, rows_s, hd), BF16), pltpu.VMEM((MEM_HEADS, rows_s, hd), BF16),
                        pltpu.VMEM((2, hpu, n_mem, hd), F32), pltpu.VMEM((2, hpu, n_mem, hd), F32),
                        pltpu.SemaphoreType.DMA((2, 2 * hpu)),
                        pltpu.VMEM((2, MEM_HEADS, n_mem, hd), F32), pltpu.SemaphoreType.DMA((2, MEM_HEADS))],
        compiler_params=pltpu.CompilerParams(dimension_semantics=("arbitrary",),
                                             vmem_limit_bytes=VMEM_LIMIT),
        name="xattn",
    )(*ins)


def _mlp_rows(x, g_ref, wup_ref, wdown_ref, gfin_ref):
    hm = _rmsnorm(x, g_ref[...]).astype(BF16)
    acc = x
    for j in range(wup_ref.shape[1] // FF_CHUNK):
        cols = slice(j * FF_CHUNK, (j + 1) * FF_CHUNK)
        up = _dot(hm, wup_ref[:, cols])
        acc = acc + _dot(jnp.square(jnp.maximum(up, 0.0)), wdown_ref[cols, :])
    return _rmsnorm(acc, gfin_ref[...])


def _mlp_kernel(xp_ref, xs_ref, g_ref, wup_ref, wdown_ref, gfin_ref, outp_ref, outs_ref, *, n_prompt):
    i = pl.program_id(0)

    @pl.when(i < n_prompt)
    def _():
        outp_ref[...] = _mlp_rows(xp_ref[...], g_ref, wup_ref, wdown_ref, gfin_ref)

    @pl.when(i == n_prompt)
    def _():
        outs_ref[...] = _mlp_rows(xs_ref[...], g_ref, wup_ref, wdown_ref, gfin_ref)


def _mlp(xp, xs, p):
    S, D = xp.shape
    ts = PROMPT_BLOCK
    n_prompt = S // ts
    weights = (p["g_mlp"], p["w_up"], p["w_down"], p["g_final"])
    return pl.pallas_call(
        functools.partial(_mlp_kernel, n_prompt=n_prompt),
        grid=(n_prompt + 1,),
        in_specs=[_prompt_spec(ts, D, n_prompt), _weight_spec(xs.shape)] + [_weight_spec(w.shape) for w in weights],
        out_specs=[_prompt_spec(ts, D, n_prompt), _const_spec(xs.shape)],
        out_shape=[jax.ShapeDtypeStruct(xp.shape, F32), jax.ShapeDtypeStruct(xs.shape, F32)],
        compiler_params=pltpu.CompilerParams(dimension_semantics=("arbitrary",),
                                             vmem_limit_bytes=VMEM_LIMIT),
        name="mlp",
    )(xp, xs, *weights)


def _prepare_layer(l, g_mix, w_in, w_pool, pool_scale, a_re, a_im, b_re, b_im, c_re, c_im, d_skip,
                   log_dt, w_glu, b_glu, w_out, g_xattn, g_mem, w_q, w_k, w_v, w_o, g_mlp, w_up,
                   w_down, g_final):
    G, P = a_re.shape[1:]
    gi = G // N_HALF
    ar, ai = a_re[l].astype(F32), a_im[l].astype(F32)
    dt = jnp.exp(log_dt[l].astype(F32))[:, None]
    lam_re, lam_im = ar * dt, ai * dt

    mag = jnp.exp(lam_re)
    ab_re, ab_im = mag * jnp.cos(lam_im), mag * jnp.sin(lam_im)

    den = ar * ar + ai * ai
    coef_re = ((ab_re - 1.0) * ar + ab_im * ai) / den
    coef_im = (ab_im * ar - (ab_re - 1.0) * ai) / den
    br, bi = b_re[l].astype(F32), b_im[l].astype(F32)
    bb_re = coef_re[..., None] * br - coef_im[..., None] * bi
    bb_im = coef_re[..., None] * bi + coef_im[..., None] * br

    half = lambda t: t.astype(F32).reshape((N_HALF, gi * t.shape[1], t.shape[2]))
    row = lambda v: v.astype(F32).reshape(1, -1)
    return dict(
        g_mix=row(g_mix[l]), w_in=w_in[l], w_pool=w_pool[l], pool_scale=row(pool_scale[l]),
        bb_re=half(bb_re), bb_im=half(bb_im), cc_re=half(c_re[l]), cc_im=half(c_im[l]),
        a1_re=row(ab_re), a1_im=row(ab_im),
        d_skip=row(d_skip[l]), w_glu=w_glu[l], b_glu=row(b_glu[l]), w_out=w_out[l],
        g_xattn=row(g_xattn[l]), g_mem=row(g_mem[l]), w_q=w_q[l], w_k=w_k[l], w_v=w_v[l], w_o=w_o[l],
        g_mlp=row(g_mlp[l]), w_up=w_up[l], w_down=w_down[l], g_final=row(g_final))


def kernel(x_prompt, x_sample, cache_mem_k, cache_mem_v, state_pool, state_ssm_re, state_ssm_im, mem_prompt, g_mix, w_in, w_pool, pool_scale, ssm_a_re, ssm_a_im, ssm_b_re, ssm_b_im, ssm_c_re, ssm_c_im, ssm_d, ssm_log_dt, w_glu, b_glu, w_out, g_xattn, g_mem, w_q, w_k, w_v, w_o, g_mlp, w_up, w_down, g_final):
    depth = g_mix.shape[0]
    assert depth == 1 and x_prompt.shape[0] == 1, "single layer, single prompt stream"
    Bp, S, D = x_prompt.shape
    Bs, Ts, _ = x_sample.shape
    n_mem = mem_prompt.shape[1]
    G, P = ssm_a_re.shape[1:]
    assert S % PROMPT_BLOCK == 0 and PROMPT_BLOCK % PROMPT_CHUNK == 0 and Ts <= PROMPT_CHUNK

    l = 0
    p = _prepare_layer(l, g_mix, w_in, w_pool, pool_scale, ssm_a_re, ssm_a_im, ssm_b_re, ssm_b_im,
                       ssm_c_re, ssm_c_im, ssm_d, ssm_log_dt, w_glu, b_glu, w_out, g_xattn, g_mem,
                       w_q, w_k, w_v, w_o, g_mlp, w_up, w_down, g_final)

    xp1, xs1, hist_p, hre_p, him_p, hist_s, hre_s, him_s = _mixer(
        x_prompt[0], x_sample.reshape(Bs * Ts, D), jnp.transpose(state_pool[l], (1, 0, 2)),
        state_ssm_re[l].reshape(Bs * G, P), state_ssm_im[l].reshape(Bs * G, P), p, Bs, Ts)
    xp2, xs2, mk, mv = _xattn(xp1, xs1, mem_prompt[0], cache_mem_k[l:l + 1], cache_mem_v[l:l + 1], p, Bs, Ts)
    y_prompt, y_sample = _mlp(xp2, xs2, p)

    hd = D // MEM_HEADS
    return (y_prompt[None], y_sample.reshape(Bs, Ts, D),
            mk, mv,
            hist_p[1:].reshape(1, Bp, POOL_HIST, -1),
            hre_p.reshape(1, Bp, G, P), him_p.reshape(1, Bp, G, P),
            jnp.transpose(hist_s, (1, 0, 2))[None], hre_s.reshape(1, Bs, G, P), him_s.reshape(1, Bs, G, P))
```

```python
import functools

import jax
import jax.numpy as jnp
from jax import lax
from jax.experimental import pallas as pl
from jax.experimental.pallas import tpu as pltpu

F32 = jnp.float32
BF16 = jnp.bfloat16

EPS = 1e-6
PAST_LEN = 1024
POOL_WINDOWS = (2, 4, 8, 16)
POOL_HIST = max(POOL_WINDOWS) - 1
HIST_ROWS = 16
MEM_HEADS = 4
N_HALF = 2
S5_TILE = 256
S5_DTYPE = F32

PROMPT_BLOCK = 512
PROMPT_CHUNK = 128
FF_CHUNK = 1024
VMEM_LIMIT = 56 * 1024 * 1024


def _rmsnorm(x, g):
    return x * lax.rsqrt(jnp.mean(x * x, axis=-1, keepdims=True) + EPS) * g


def _dot(a, b):
    return jnp.dot(a.astype(BF16), b.astype(BF16), preferred_element_type=F32)


def _inv_rms(x):
    return lax.rsqrt(jnp.mean(x * x, axis=-1, keepdims=True) + EPS)


def _norm_dot(x, g, w):
    return _dot(x * g, w) * _inv_rms(x)


def _pool_windows(zbuf, nrows, pos, r):
    gw = r["w_pool"].shape[-1]
    outs = []
    for g, w in enumerate(POOL_WINDOWS):
        lanes = slice(g * gw, (g + 1) * gw)
        win = zbuf[pl.ds(0, HIST_ROWS + nrows), lanes]
        k = 1
        while k < w:
            win = win + pltpu.roll(win, k, 0)
            k *= 2
        cur = zbuf[pl.ds(HIST_ROWS, nrows), lanes]
        cnt = jnp.minimum(pos + 1, w).astype(F32)
        pooled = win[HIST_ROWS:, :] / cnt - cur
        outs.append(_dot(pooled, r["w_pool"][g]))
    return jnp.concatenate(outs, axis=-1) * r["pool_scale"][...]


def _cmul(ar, ai, br, bi):
    return ar * br - ai * bi, ar * bi + ai * br


def _s5_scan(u, n_chunks, T, end_row, h_carry, h_rows, r, hout_re_ref, hout_im_ref):
    hw = r["bblk"].shape[1]
    nh = r["bblk"].shape[2] // 2
    u_bf = u.astype(BF16)
    r_i = lax.broadcasted_iota(jnp.int32, (T, T), 0)
    c_i = lax.broadcasted_iota(jnp.int32, (T, T), 1)
    ltri = jnp.where(r_i >= c_i, 1.0, 0.0).astype(BF16)

    ys = []
    for k in range(N_HALF):
        uk = u_bf[:, k * hw:(k + 1) * hw]
        yk = None
        for j in range(nh // S5_TILE):
            re_c = slice(j * S5_TILE, (j + 1) * S5_TILE)
            im_c = slice(nh + j * S5_TILE, nh + (j + 1) * S5_TILE)
            st = slice(k * nh + j * S5_TILE, k * nh + (j + 1) * S5_TILE)
            bu_re = _dot(uk, r["bblk"][k, :, re_c])
            bu_im = _dot(uk, r["bblk"][k, :, im_c])
            if h_rows is None:
                carry = (h_carry[0][:, st], h_carry[1][:, st])
            h_re, h_im = [], []
            for c in range(n_chunks):
                rows = slice(c * T, (c + 1) * T)
                xr, xi = _cmul(bu_re[rows].astype(S5_DTYPE), bu_im[rows].astype(S5_DTYPE),
                               r["pneg_re"][0:T, st], r["pneg_im"][0:T, st])
                wr = _dot(ltri, xr)
                wi = _dot(ltri, xi)
                if h_rows is not None:
                    carry = (h_rows[0][pl.ds(c, 1), st], h_rows[1][pl.ds(c, 1), st])
                cr, ci = _cmul(r["a1_re"][:, st], r["a1_im"][:, st], carry[0], carry[1])
                wr, wi = wr + cr, wi + ci
                sr, si = _cmul(r["ppos_re"][0:T, st], r["ppos_im"][0:T, st], wr.astype(S5_DTYPE), wi.astype(S5_DTYPE))
                carry = _cmul(r["pend_re"][end_row:end_row + 1, st], r["pend_im"][end_row:end_row + 1, st],
                              wr[T - 1:T, :], wi[T - 1:T, :])
                if h_rows is not None:
                    hout_re_ref[pl.ds(c, 1), st] = carry[0]
                    hout_im_ref[pl.ds(c, 1), st] = carry[1]
                h_re.append(sr)
                h_im.append(si)
            if h_rows is None:
                hout_re_ref[:, st] = carry[0]
                hout_im_ref[:, st] = carry[1]
            part = (_dot(jnp.concatenate(h_re, axis=0), r["cblk"][k, re_c, :])
                    + _dot(jnp.concatenate(h_im, axis=0), r["cblk"][k, im_c, :]))
            yk = part if yk is None else yk + part
        ys.append(yk)
    return jnp.concatenate(ys, axis=-1)


def _mix_tail(x, y_pool, u, y_state, r):
    y = jax.nn.gelu(y_state + r["d_skip"][...] * u)
    y = y * jax.nn.sigmoid(_dot(y, r["w_glu"][...]) + r["b_glu"][...])
    ycat = jnp.concatenate([y_pool.astype(BF16), y.astype(BF16)], axis=-1)
    return x + _dot(ycat, r["w_out"][...])


_MIXER_IN = ("xp", "xs", "hist_in", "h0_re", "h0_im",
             "g_mix", "w_in", "w_pool", "pool_scale", "bb_re", "bb_im", "cc_re", "cc_im", "a1_re", "a1_im",
             "d_skip", "w_glu", "b_glu", "w_out")
_MIXER_OUT = ("outp", "outs", "histp", "hrep", "himp", "hists", "hres", "hims")
_MIXER_BF16 = ("w_in", "w_pool", "w_glu", "w_out")
_MIXER_GEN = ("bblk", "cblk", "pneg_re", "pneg_im", "ppos_re", "ppos_im", "pend_re", "pend_im")


def _power_table(re1, im1, T):
    SUB = 16
    assert T % SUB == 0
    n = re1.shape[-1]
    t = lax.broadcasted_iota(jnp.int32, (SUB, 1), 0)
    lo_re, lo_im = jnp.ones((SUB, n), F32), jnp.zeros((SUB, n), F32)
    cr, ci = re1, im1
    k = 1
    while k < SUB:
        bit = (t & k) != 0
        lo_re, lo_im = _cmul(lo_re, lo_im, jnp.where(bit, cr, 1.0), jnp.where(bit, ci, 0.0))
        cr, ci = _cmul(cr, ci, cr, ci)
        k *= 2
    hi_re, hi_im = [jnp.ones((1, n), F32)], [jnp.zeros((1, n), F32)]
    for _ in range(T // SUB - 1):
        nr, ni = _cmul(hi_re[-1], hi_im[-1], cr, ci)
        hi_re.append(nr)
        hi_im.append(ni)
    rep = lambda rows: jnp.concatenate([jnp.broadcast_to(v, (SUB, n)) for v in rows], axis=0)
    tile = lambda v: jnp.concatenate([v] * (T // SUB), axis=0)
    return _cmul(tile(lo_re), tile(lo_im), rep(hi_re), rep(hi_im))


def _expand_s5_params(r, gen, T, t_ends):
    are, aim = r["a1_re"][...], r["a1_im"][...]
    pr, pi = _power_table(are, aim, T)
    gen["ppos_re"][...] = pr.astype(S5_DTYPE)
    gen["ppos_im"][...] = pi.astype(S5_DTYPE)
    for j, te in enumerate(t_ends):
        gen["pend_re"][j:j + 1, :] = pr[te - 1:te, :]
        gen["pend_im"][j:j + 1, :] = pi[te - 1:te, :]
    den = are * are + aim * aim
    nr, ni = _power_table(are / den, -aim / den, T)
    gen["pneg_re"][...] = nr.astype(S5_DTYPE)
    gen["pneg_im"][...] = ni.astype(S5_DTYPE)

    n_half, rows_b, H = r["bb_re"].shape
    _, rows_c, P = r["cc_re"].shape
    nh = rows_b

    def spread(x, width):
        w = x.shape[1]
        sel = lax.rem(lax.broadcasted_iota(jnp.int32, (w, width), 1), w) == lax.broadcasted_iota(jnp.int32, (w, width), 0)
        return _dot(x, jnp.where(sel, 1.0, 0.0))

    def same_group(shape, row_size, col_size):
        return (lax.div(lax.broadcasted_iota(jnp.int32, shape, 0), row_size)
                == lax.div(lax.broadcasted_iota(jnp.int32, shape, 1), col_size))

    for k in range(n_half):
        for c, (b_src, c_src, sign) in enumerate(((r["bb_re"], r["cc_re"], 1.0), (r["bb_im"], r["cc_im"], -1.0))):
            yb = spread(b_src[k], rows_c)
            yb = jnp.where(same_group(yb.shape, P, H), yb, 0.0)
            gen["bblk"][k, :, c * nh:(c + 1) * nh] = yb.T.astype(BF16)
            yc = spread(c_src[k], rows_b)
            yc = jnp.where(same_group(yc.shape, H, P), sign * yc, 0.0)
            gen["cblk"][k, c * nh:(c + 1) * nh, :] = yc.T.astype(BF16)


def _mixer_kernel(*refs, n_prompt, n_streams, Tp, Ts):
    r = dict(zip(_MIXER_IN + _MIXER_OUT, refs))
    zbuf, ypool_ref, hp_re, hp_im, hs_re, hs_im, h0f_re, h0f_im = refs[len(_MIXER_IN) + len(_MIXER_OUT):][:8]
    n_fixed = len(_MIXER_IN) + len(_MIXER_OUT) + 8
    bf16_weights = dict(zip(_MIXER_BF16, refs[n_fixed:]))
    gen = dict(zip(_MIXER_GEN, refs[n_fixed + len(_MIXER_BF16):]))
    i = pl.program_id(0)
    ts = r["xp"].shape[0]
    d_pool = zbuf.shape[1]

    @pl.when(i == 0)
    def _():
        zbuf[0:HIST_ROWS, :] = jnp.zeros((HIST_ROWS, d_pool), F32)
        hp_re[...] = jnp.zeros(hp_re.shape, F32)
        hp_im[...] = jnp.zeros(hp_im.shape, F32)
        for name, ref in bf16_weights.items():
            ref[...] = r[name][...].astype(BF16)
        _expand_s5_params(r, gen, Tp, (Tp, Ts))

    r.update(bf16_weights)
    r.update(gen)

    @pl.when(i < n_prompt)
    def _():
        x = r["xp"][...]
        z = _norm_dot(x, r["g_mix"][...], r["w_in"][...])
        zbuf[HIST_ROWS:HIST_ROWS + ts, :] = z[:, :d_pool]
        u = z[:, d_pool:]
        pos = i * ts + lax.broadcasted_iota(jnp.int32, (ts, 1), 0)
        y_pool = _pool_windows(zbuf, ts, pos, r)
        y_state = _s5_scan(u, ts // Tp, Tp, 0, (hp_re[...], hp_im[...]), None, r, hp_re, hp_im)
        r["outp"][...] = _mix_tail(x, y_pool, u, y_state, r)
        tail = zbuf[ts:ts + HIST_ROWS, :]
        zbuf[0:HIST_ROWS, :] = tail
        r["histp"][...] = tail

    @pl.when(i == n_prompt)
    def _():
        ext = HIST_ROWS + Ts
        x = r["xs"][...]
        z = _norm_dot(x, r["g_mix"][...], r["w_in"][...])
        u = z[:, d_pool:]
        for b in range(n_streams):
            zbuf[b * ext:b * ext + 1, :] = jnp.zeros((1, d_pool), F32)
            for t in range(POOL_HIST):
                zbuf[b * ext + 1 + t:b * ext + 2 + t, :] = r["hist_in"][t, b:b + 1, :]
            zbuf[b * ext + HIST_ROWS:(b + 1) * ext, :] = z[b * Ts:(b + 1) * Ts, :d_pool]
            for t in range(POOL_HIST):
                row = (b + 1) * ext - POOL_HIST + t
                r["hists"][t, b:b + 1, :] = zbuf[row:row + 1, :]
        nrows = n_streams * ext - HIST_ROWS
        ridx = lax.broadcasted_iota(jnp.int32, (nrows, 1), 0)
        pos = PAST_LEN + lax.rem(ridx, ext)
        y_all = _pool_windows(zbuf, nrows, pos, r)
        for b in range(n_streams):
            ypool_ref[b * Ts:(b + 1) * Ts, :] = y_all[b * ext:b * ext + Ts, :]
        G, P = r["hrep"].shape
        for src, dst in ((r["h0_re"], h0f_re), (r["h0_im"], h0f_im)):
            for b in range(n_streams):
                for g in range(G):
                    dst[b:b + 1, g * P:(g + 1) * P] = src[b * G + g:b * G + g + 1, :]
        y_state = _s5_scan(u, n_streams, Ts, 1, None, (h0f_re, h0f_im), r, hs_re, hs_im)
        r["outs"][...] = _mix_tail(x, ypool_ref[...], u, y_state, r)
        for g in range(G):
            cols = slice(g * P, (g + 1) * P)
            r["hrep"][g:g + 1, :] = hp_re[:, cols]
            r["himp"][g:g + 1, :] = hp_im[:, cols]
            for b in range(n_streams):
                r["hres"][b * G + g:b * G + g + 1, :] = hs_re[b:b + 1, cols]
                r["hims"][b * G + g:b * G + g + 1, :] = hs_im[b:b + 1, cols]


def _const_spec(shape):
    nd = len(shape)
    return pl.BlockSpec(shape, lambda *_: (0,) * nd)


def _weight_spec(shape):
    nd = len(shape)
    return pl.BlockSpec(shape, lambda *_: (0,) * nd, pipeline_mode=pl.Buffered(1))


def _prompt_spec(ts, d, n_prompt):
    return pl.BlockSpec((ts, d), lambda i: (jnp.minimum(i, n_prompt - 1), 0))


def _mixer(xp, xs, hist_in, h0_re, h0_im, p, n_streams, Ts):
    S, D = xp.shape
    rows_s = xs.shape[0]
    ts = PROMPT_BLOCK
    n_prompt = S // ts
    d_pool = hist_in.shape[-1]
    n_state = p["a1_re"].shape[-1]
    P = h0_re.shape[-1]
    G = n_state // P
    assert n_streams * (HIST_ROWS + Ts) <= HIST_ROWS + ts and rows_s <= ts
    args = dict(p, xp=xp, xs=xs, hist_in=hist_in, h0_re=h0_re, h0_im=h0_im)
    ins = [args[n] for n in _MIXER_IN]
    in_specs = [_prompt_spec(ts, D, n_prompt)] + [_weight_spec(a.shape) for a in ins[1:]]
    n_half, nh, H = p["bb_re"].shape
    d_half = p["cc_re"].shape[1]
    gen_shapes = ([((n_half, d_half, 2 * nh), BF16), ((n_half, 2 * nh, d_half), BF16)]
                  + [((PROMPT_CHUNK, n_state), S5_DTYPE)] * 4 + [((2, n_state), F32)] * 2)
    out_shapes = dict(
        outp=(S, D), outs=(rows_s, D), histp=(HIST_ROWS, d_pool), hrep=(G, P), himp=(G, P),
        hists=(POOL_HIST, n_streams, d_pool), hres=(n_streams * G, P), hims=(n_streams * G, P))
    out_specs = [_prompt_spec(ts, D, n_prompt)] + [_const_spec(out_shapes[n]) for n in _MIXER_OUT[1:]]
    return pl.pallas_call(
        functools.partial(_mixer_kernel, n_prompt=n_prompt, n_streams=n_streams, Tp=PROMPT_CHUNK, Ts=Ts),
        grid=(n_prompt + 1,),
        in_specs=in_specs,
        out_specs=out_specs,
        out_shape=[jax.ShapeDtypeStruct(out_shapes[n], F32) for n in _MIXER_OUT],
        scratch_shapes=[pltpu.VMEM((HIST_ROWS + ts, d_pool), F32),
                        pltpu.VMEM((rows_s, d_pool), F32)]
        + [pltpu.VMEM((1, n_state), F32)] * 2 + [pltpu.VMEM((n_streams, n_state), F32)] * 4
        + [pltpu.VMEM(args[n].shape, BF16) for n in _MIXER_BF16]
        + [pltpu.VMEM(shape, dtype) for shape, dtype in gen_shapes],
        compiler_params=pltpu.CompilerParams(dimension_semantics=("arbitrary",),
                                             vmem_limit_bytes=VMEM_LIMIT),
        name="mixer",
    )(*ins)


def _attend(q_heads, k_heads, v_heads, scale):
    outs = []
    for q, k, v in zip(q_heads, k_heads, v_heads):
        s = lax.dot_general(q, k.astype(BF16), (((1,), (1,)), ((), ())), preferred_element_type=F32) * scale
        e = jnp.exp(s - jnp.max(s, axis=-1, keepdims=True))
        outs.append(_dot(e, v) / jnp.sum(e, axis=-1, keepdims=True))
    return outs


_XATTN_IN = ("xp", "xs", "mem", "kc", "vc", "g_mem", "w_k", "w_v", "g_xattn", "w_q", "w_o")
_XATTN_OUT = ("outp", "outs", "mk", "mv")


def _xattn_kernel(*refs, n_prompt, n_streams, Ts, hpu, scale):
    r = dict(zip(_XATTN_IN + _XATTN_OUT, refs))
    kp_ref, vp_ref, qs_ref, os_ref, kbuf, vbuf, sem, kvf, osem = refs[len(_XATTN_IN) + len(_XATTN_OUT):]
    i = pl.program_id(0)
    hd = kp_ref.shape[-1] // MEM_HEADS
    ups = MEM_HEADS // hpu

    def unit_copies(unit, hg):
        b, slot = unit // ups, lax.rem(unit, 2)
        return [pltpu.make_async_copy(src.at[0, b, :, hg * hpu + e, :], dst.at[slot, e], sem.at[slot, j * hpu + e])
                for j, (src, dst) in enumerate(((r["kc"], kbuf), (r["vc"], vbuf))) for e in range(hpu)]

    def kv_out_copies():
        return [pltpu.make_async_copy(kvf.at[j, h], dst.at[0, 0, :, h, :], osem.at[j, h])
                for j, dst in enumerate((r["mk"], r["mv"])) for h in range(MEM_HEADS)]

    def for_unit(unit, fn):
        for hg in range(ups):
            @pl.when(lax.rem(unit, ups) == hg)
            def _():
                for cp in unit_copies(unit, hg):
                    fn(cp)

    @pl.when(i == 0)
    def _():
        for_unit(i, lambda cp: cp.start())
        mem = r["mem"][...]
        inv = _inv_rms(mem)
        mg = (mem * r["g_mem"][...]).astype(BF16)
        k = _dot(mg, r["w_k"][...]) * inv
        v = _dot(mg, r["w_v"][...]) * inv
        for h in range(MEM_HEADS):
            kvf[0, h] = k[:, h * hd:(h + 1) * hd]
            kvf[1, h] = v[:, h * hd:(h + 1) * hd]
        for cp in kv_out_copies():
            cp.start()
        kp_ref[...] = k.astype(BF16)
        vp_ref[...] = v.astype(BF16)
        q = _norm_dot(r["xs"][...], r["g_xattn"][...], r["w_q"][...]).astype(BF16)
        for h in range(MEM_HEADS):
            qs_ref[h] = q[:, h * hd:(h + 1) * hd]

    @pl.when(i < n_prompt)
    def _():
        @pl.when(i + 1 < n_prompt)
        def _():
            for_unit(i + 1, lambda cp: cp.start())

        for_unit(i, lambda cp: cp.wait())

        x = r["xp"][...]
        q = _norm_dot(x, r["g_xattn"][...], r["w_q"][...]).astype(BF16)
        o = _attend([q[:, h * hd:(h + 1) * hd] for h in range(MEM_HEADS)],
                    [kp_ref[:, h * hd:(h + 1) * hd] for h in range(MEM_HEADS)],
                    [vp_ref[:, h * hd:(h + 1) * hd] for h in range(MEM_HEADS)], scale)
        r["outp"][...] = x + _dot(jnp.concatenate(o, axis=-1), r["w_o"][...])

        b, hg, slot = i // ups, lax.rem(i, ups), lax.rem(i, 2)
        rows = pl.ds(pl.multiple_of(b * Ts, Ts), Ts)
        heads = [hg * hpu + e for e in range(hpu)]
        o_s = _attend([qs_ref[h, rows, :] for h in heads], [kbuf[slot, e] for e in range(hpu)],
                      [vbuf[slot, e] for e in range(hpu)], scale)
        for h, o_h in zip(heads, o_s):
            os_ref[h, rows, :] = o_h.astype(BF16)

    @pl.when(i == n_prompt)
    def _():
        for cp in kv_out_copies():
            cp.wait()
        o = jnp.concatenate([os_ref[h] for h in range(MEM_HEADS)], axis=-1)
        r["outs"][...] = r["xs"][...] + _dot(o, r["w_o"][...])


def _xattn(xp, xs, mem, kc, vc, p, n_streams, Ts):
    S, D = xp.shape
    rows_s = xs.shape[0]
    ts = PROMPT_BLOCK
    n_prompt = S // ts
    n_mem = mem.shape[0]
    hd = D // MEM_HEADS
    hpu = n_streams * MEM_HEADS // n_prompt
    assert hpu * n_prompt == n_streams * MEM_HEADS and MEM_HEADS % hpu == 0
    args = dict(p, xp=xp, xs=xs, mem=mem, kc=kc, vc=vc)
    ins = [args[n] for n in _XATTN_IN]
    cache_spec = pl.BlockSpec(memory_space=pl.ANY)
    in_specs = [_prompt_spec(ts, D, n_prompt)]
    for n in _XATTN_IN[1:]:
        in_specs.append(cache_spec if n in ("kc", "vc") else _weight_spec(args[n].shape))
    out_shapes = dict(outp=(S, D), outs=(rows_s, D), mk=(1, 1, n_mem, MEM_HEADS, hd), mv=(1, 1, n_mem, MEM_HEADS, hd))
    out_specs = [_prompt_spec(ts, D, n_prompt), _const_spec(out_shapes["outs"]), cache_spec, cache_spec]
    return pl.pallas_call(
        functools.partial(_xattn_kernel, n_prompt=n_prompt, n_streams=n_streams, Ts=Ts, hpu=hpu, scale=hd ** -0.5),
        grid=(n_prompt + 1,),
        in_specs=in_specs,
        out_specs=out_specs,
        out_shape=[jax.ShapeDtypeStruct(out_shapes[n], F32) for n in _XATTN_OUT],
        scratch_shapes=[pltpu.VMEM((n_mem, D), BF16), pltpu.VMEM((n_mem, D), BF16),
                        pltpu.VMEM((MEM_HEADS, rows_s, hd), BF16), pltpu.VMEM((MEM_HEADS, rows_s, hd), BF16),
                        pltpu.VMEM((2, hpu, n_mem, hd), F32), pltpu.VMEM((2, hpu, n_mem, hd), F32),
                        pltpu.SemaphoreType.DMA((2, 2 * hpu)),
                        pltpu.VMEM((2, MEM_HEADS, n_mem, hd), F32), pltpu.SemaphoreType.DMA((2, MEM_HEADS))],
        compiler_params=pltpu.CompilerParams(dimension_semantics=("arbitrary",),
                                             vmem_limit_bytes=VMEM_LIMIT),
        name="xattn",
    )(*ins)


def _mlp_rows(x, g_ref, wup_ref, wdown_ref, gfin_ref):
    hg = (x * g_ref[...]).astype(BF16)
    acc = None
    for j in range(wup_ref.shape[1] // FF_CHUNK):
        cols = slice(j * FF_CHUNK, (j + 1) * FF_CHUNK)
        up = _dot(hg, wup_ref[:, cols])
        part = _dot(jnp.square(jnp.maximum(up, 0.0)), wdown_ref[cols, :])
        acc = part if acc is None else acc + part
    return _rmsnorm(x + acc * jnp.square(_inv_rms(x)), gfin_ref[...])


def _mlp_kernel(xp_ref, xs_ref, g_ref, wup_ref, wdown_ref, gfin_ref, outp_ref, outs_ref, *, n_prompt):
    i = pl.program_id(0)

    @pl.when(i < n_prompt)
    def _():
        outp_ref[...] = _mlp_rows(xp_ref[...], g_ref, wup_ref, wdown_ref, gfin_ref)

    @pl.when(i == n_prompt)
    def _():
        outs_ref[...] = _mlp_rows(xs_ref[...], g_ref, wup_ref, wdown_ref, gfin_ref)


def _mlp(xp, xs, p):
    S, D = xp.shape
    ts = PROMPT_BLOCK
    n_prompt = S // ts
    weights = (p["g_mlp"], p["w_up"], p["w_down"], p["g_final"])
    return pl.pallas_call(
        functools.partial(_mlp_kernel, n_prompt=n_prompt),
        grid=(n_prompt + 1,),
        in_specs=[_prompt_spec(ts, D, n_prompt), _weight_spec(xs.shape)] + [_weight_spec(w.shape) for w in weights],
        out_specs=[_prompt_spec(ts, D, n_prompt), _const_spec(xs.shape)],
        out_shape=[jax.ShapeDtypeStruct(xp.shape, F32), jax.ShapeDtypeStruct(xs.shape, F32)],
        compiler_params=pltpu.CompilerParams(dimension_semantics=("arbitrary",),
                                             vmem_limit_bytes=VMEM_LIMIT),
        name="mlp",
    )(xp, xs, *weights)


def _prepare_layer(l, g_mix, w_in, w_pool, pool_scale, a_re, a_im, b_re, b_im, c_re, c_im, d_skip,
                   log_dt, w_glu, b_glu, w_out, g_xattn, g_mem, w_q, w_k, w_v, w_o, g_mlp, w_up,
                   w_down, g_final):
    G, P = a_re.shape[1:]
    gi = G // N_HALF
    ar, ai = a_re[l].astype(F32), a_im[l].astype(F32)
    dt = jnp.exp(log_dt[l].astype(F32))[:, None]
    lam_re, lam_im = ar * dt, ai * dt

    mag = jnp.exp(lam_re)
    ab_re, ab_im = mag * jnp.cos(lam_im), mag * jnp.sin(lam_im)

    den = ar * ar + ai * ai
    coef_re = ((ab_re - 1.0) * ar + ab_im * ai) / den
    coef_im = (ab_im * ar - (ab_re - 1.0) * ai) / den
    br, bi = b_re[l].astype(F32), b_im[l].astype(F32)
    bb_re = coef_re[..., None] * br - coef_im[..., None] * bi
    bb_im = coef_re[..., None] * bi + coef_im[..., None] * br

    half = lambda t: t.astype(F32).reshape((N_HALF, gi * t.shape[1], t.shape[2]))
    row = lambda v: v.astype(F32).reshape(1, -1)
    return dict(
        g_mix=row(g_mix[l]), w_in=w_in[l], w_pool=w_pool[l], pool_scale=row(pool_scale[l]),
        bb_re=half(bb_re), bb_im=half(bb_im), cc_re=half(c_re[l]), cc_im=half(c_im[l]),
        a1_re=row(ab_re), a1_im=row(ab_im),
        d_skip=row(d_skip[l]), w_glu=w_glu[l], b_glu=row(b_glu[l]), w_out=w_out[l],
        g_xattn=row(g_xattn[l]), g_mem=row(g_mem[l]), w_q=w_q[l], w_k=w_k[l], w_v=w_v[l], w_o=w_o[l],
        g_mlp=row(g_mlp[l]), w_up=w_up[l], w_down=w_down[l], g_final=row(g_final))


def kernel(x_prompt, x_sample, cache_mem_k, cache_mem_v, state_pool, state_ssm_re, state_ssm_im, mem_prompt, g_mix, w_in, w_pool, pool_scale, ssm_a_re, ssm_a_im, ssm_b_re, ssm_b_im, ssm_c_re, ssm_c_im, ssm_d, ssm_log_dt, w_glu, b_glu, w_out, g_xattn, g_mem, w_q, w_k, w_v, w_o, g_mlp, w_up, w_down, g_final):
    depth = g_mix.shape[0]
    assert depth == 1 and x_prompt.shape[0] == 1, "single layer, single prompt stream"
    Bp, S, D = x_prompt.shape
    Bs, Ts, _ = x_sample.shape
    n_mem = mem_prompt.shape[1]
    G, P = ssm_a_re.shape[1:]
    assert S % PROMPT_BLOCK == 0 and PROMPT_BLOCK % PROMPT_CHUNK == 0 and Ts <= PROMPT_CHUNK

    l = 0
    p = _prepare_layer(l, g_mix, w_in, w_pool, pool_scale, ssm_a_re, ssm_a_im, ssm_b_re, ssm_b_im,
                       ssm_c_re, ssm_c_im, ssm_d, ssm_log_dt, w_glu, b_glu, w_out, g_xattn, g_mem,
                       w_q, w_k, w_v, w_o, g_mlp, w_up, w_down, g_final)

    xp1, xs1, hist_p, hre_p, him_p, hist_s, hre_s, him_s = _mixer(
        x_prompt[0], x_sample.reshape(Bs * Ts, D), jnp.transpose(state_pool[l], (1, 0, 2)),
        state_ssm_re[l].reshape(Bs * G, P), state_ssm_im[l].reshape(Bs * G, P), p, Bs, Ts)
    xp2, xs2, mk, mv = _xattn(xp1, xs1, mem_prompt[0], cache_mem_k[l:l + 1], cache_mem_v[l:l + 1], p, Bs, Ts)
    y_prompt, y_sample = _mlp(xp2, xs2, p)

    hd = D // MEM_HEADS
    return (y_prompt[None], y_sample.reshape(Bs, Ts, D),
            mk, mv,
            hist_p[1:].reshape(1, Bp, POOL_HIST, -1),
            hre_p.reshape(1, Bp, G, P), him_p.reshape(1, Bp, G, P),
            jnp.transpose(hist_s, (1, 0, 2))[None], hre_s.reshape(1, Bs, G, P), him_s.reshape(1, Bs, G, P))
```

```python
import functools

import jax
import jax.numpy as jnp
from jax import lax
from jax.experimental import pallas as pl
from jax.experimental.pallas import tpu as pltpu

F32 = jnp.float32
BF16 = jnp.bfloat16

EPS = 1e-6
PAST_LEN = 1024
POOL_WINDOWS = (2, 4, 8, 16)
POOL_HIST = max(POOL_WINDOWS) - 1
HIST_ROWS = 16
MEM_HEADS = 4
N_HALF = 2
S5_TILE = 256
S5_DTYPE = F32

PROMPT_BLOCK = 512
PROMPT_CHUNK = 128
FF_CHUNK = 1024
VMEM_LIMIT = 56 * 1024 * 1024


def _rmsnorm(x, g):
    return x * lax.rsqrt(jnp.mean(x * x, axis=-1, keepdims=True) + EPS) * g


def _dot(a, b):
    return jnp.dot(a.astype(BF16), b.astype(BF16), preferred_element_type=F32)


def _inv_rms(x):
    return lax.rsqrt(jnp.mean(x * x, axis=-1, keepdims=True) + EPS)


def _norm_dot(x, g, w):
    return _dot(x * g, w) * _inv_rms(x)


def _pool_windows(zbuf, nrows, pos, r):
    gw = r["w_pool"].shape[-1]
    outs = []
    for g, w in enumerate(POOL_WINDOWS):
        lanes = slice(g * gw, (g + 1) * gw)
        win = zbuf[pl.ds(0, HIST_ROWS + nrows), lanes]
        k = 1
        while k < w:
            win = win + pltpu.roll(win, k, 0)
            k *= 2
        cur = zbuf[pl.ds(HIST_ROWS, nrows), lanes]
        cnt = jnp.minimum(pos + 1, w).astype(F32)
        pooled = win[HIST_ROWS:, :] / cnt - cur
        outs.append(_dot(pooled, r["w_pool"][g]))
    return jnp.concatenate(outs, axis=-1) * r["pool_scale"][...]


def _cmul(ar, ai, br, bi):
    return ar * br - ai * bi, ar * bi + ai * br


def _s5_scan(u, n_chunks, T, end_row, h_carry, h_rows, r, hout_re_ref, hout_im_ref):
    hw = r["bblk"].shape[1]
    nh = r["bblk"].shape[2] // 2
    u_bf = u.astype(BF16)
    r_i = lax.broadcasted_iota(jnp.int32, (T, T), 0)
    c_i = lax.broadcasted_iota(jnp.int32, (T, T), 1)
    ltri = jnp.where(r_i >= c_i, 1.0, 0.0).astype(BF16)

    ys = []
    for k in range(N_HALF):
        uk = u_bf[:, k * hw:(k + 1) * hw]
        yk = None
        for j in range(nh // S5_TILE):
            re_c = slice(j * S5_TILE, (j + 1) * S5_TILE)
            im_c = slice(nh + j * S5_TILE, nh + (j + 1) * S5_TILE)
            st = slice(k * nh + j * S5_TILE, k * nh + (j + 1) * S5_TILE)
            bu_re = _dot(uk, r["bblk"][k, :, re_c])
            bu_im = _dot(uk, r["bblk"][k, :, im_c])
            if h_rows is None:
                carry = (h_carry[0][:, st], h_carry[1][:, st])
            h_re, h_im = [], []
            for c in range(n_chunks):
                rows = slice(c * T, (c + 1) * T)
                xr, xi = _cmul(bu_re[rows].astype(S5_DTYPE), bu_im[rows].astype(S5_DTYPE),
                               r["pneg_re"][0:T, st], r["pneg_im"][0:T, st])
                wr = _dot(ltri, xr)
                wi = _dot(ltri, xi)
                if h_rows is not None:
                    carry = (h_rows[0][pl.ds(c, 1), st], h_rows[1][pl.ds(c, 1), st])
                cr, ci = _cmul(r["a1_re"][:, st], r["a1_im"][:, st], carry[0], carry[1])
                wr, wi = wr + cr, wi + ci
                sr, si = _cmul(r["ppos_re"][0:T, st], r["ppos_im"][0:T, st], wr.astype(S5_DTYPE), wi.astype(S5_DTYPE))
                carry = _cmul(r["pend_re"][end_row:end_row + 1, st], r["pend_im"][end_row:end_row + 1, st],
                              wr[T - 1:T, :], wi[T - 1:T, :])
                if h_rows is not None:
                    hout_re_ref[pl.ds(c, 1), st] = carry[0]
                    hout_im_ref[pl.ds(c, 1), st] = carry[1]
                h_re.append(sr)
                h_im.append(si)
            if h_rows is None:
                hout_re_ref[:, st] = carry[0]
                hout_im_ref[:, st] = carry[1]
            part = (_dot(jnp.concatenate(h_re, axis=0), r["cblk"][k, re_c, :])
                    + _dot(jnp.concatenate(h_im, axis=0), r["cblk"][k, im_c, :]))
            yk = part if yk is None else yk + part
        ys.append(yk)
    return jnp.concatenate(ys, axis=-1)


def _mix_tail(x, y_pool, u, y_state, r):
    y = jax.nn.gelu(y_state + r["d_skip"][...] * u)
    y = y * jax.nn.sigmoid(_dot(y, r["w_glu"][...]) + r["b_glu"][...])
    ycat = jnp.concatenate([y_pool.astype(BF16), y.astype(BF16)], axis=-1)
    return x + _dot(ycat, r["w_out"][...])


_MIXER_IN = ("xp", "xs", "hist_in", "h0_re", "h0_im",
             "g_mix", "w_in", "w_pool", "pool_scale", "bb_re", "bb_im", "cc_re", "cc_im", "a1_re", "a1_im",
             "d_skip", "w_glu", "b_glu", "w_out")
_MIXER_OUT = ("outp", "outs", "histp", "hrep", "himp", "hists", "hres", "hims")
_MIXER_BF16 = ("w_in", "w_pool", "w_glu", "w_out")
_MIXER_GEN = ("bblk", "cblk", "pneg_re", "pneg_im", "ppos_re", "ppos_im", "pend_re", "pend_im")


def _power_table(re1, im1, T):
    SUB = 16
    assert T % SUB == 0
    n = re1.shape[-1]
    t = lax.broadcasted_iota(jnp.int32, (SUB, 1), 0)
    lo_re, lo_im = jnp.ones((SUB, n), F32), jnp.zeros((SUB, n), F32)
    cr, ci = re1, im1
    k = 1
    while k < SUB:
        bit = (t & k) != 0
        lo_re, lo_im = _cmul(lo_re, lo_im, jnp.where(bit, cr, 1.0), jnp.where(bit, ci, 0.0))
        cr, ci = _cmul(cr, ci, cr, ci)
        k *= 2
    hi_re, hi_im = [jnp.ones((1, n), F32)], [jnp.zeros((1, n), F32)]
    for _ in range(T // SUB - 1):
        nr, ni = _cmul(hi_re[-1], hi_im[-1], cr, ci)
        hi_re.append(nr)
        hi_im.append(ni)
    rep = lambda rows: jnp.concatenate([jnp.broadcast_to(v, (SUB, n)) for v in rows], axis=0)
    tile = lambda v: jnp.concatenate([v] * (T // SUB), axis=0)
    return _cmul(tile(lo_re), tile(lo_im), rep(hi_re), rep(hi_im))


def _expand_s5_params(r, gen, T, t_ends):
    are, aim = r["a1_re"][...], r["a1_im"][...]
    pr, pi = _power_table(are, aim, T)
    gen["ppos_re"][...] = pr.astype(S5_DTYPE)
    gen["ppos_im"][...] = pi.astype(S5_DTYPE)
    for j, te in enumerate(t_ends):
        gen["pend_re"][j:j + 1, :] = pr[te - 1:te, :]
        gen["pend_im"][j:j + 1, :] = pi[te - 1:te, :]
    den = are * are + aim * aim
    nr, ni = _power_table(are / den, -aim / den, T)
    gen["pneg_re"][...] = nr.astype(S5_DTYPE)
    gen["pneg_im"][...] = ni.astype(S5_DTYPE)

    n_half, rows_b, H = r["bb_re"].shape
    _, rows_c, P = r["cc_re"].shape
    nh = rows_b

    def spread(x, width):
        w = x.shape[1]
        sel = lax.rem(lax.broadcasted_iota(jnp.int32, (w, width), 1), w) == lax.broadcasted_iota(jnp.int32, (w, width), 0)
        return _dot(x, jnp.where(sel, 1.0, 0.0))

    def same_group(shape, row_size, col_size):
        return (lax.div(lax.broadcasted_iota(jnp.int32, shape, 0), row_size)
                == lax.div(lax.broadcasted_iota(jnp.int32, shape, 1), col_size))

    for k in range(n_half):
        for c, (b_src, c_src, sign) in enumerate(((r["bb_re"], r["cc_re"], 1.0), (r["bb_im"], r["cc_im"], -1.0))):
            yb = spread(b_src[k], rows_c)
            yb = jnp.where(same_group(yb.shape, P, H), yb, 0.0)
            gen["bblk"][k, :, c * nh:(c + 1) * nh] = yb.T.astype(BF16)
            yc = spread(c_src[k], rows_b)
            yc = jnp.where(same_group(yc.shape, H, P), sign * yc, 0.0)
            gen["cblk"][k, c * nh:(c + 1) * nh, :] = yc.T.astype(BF16)


def _mixer_kernel(*refs, n_prompt, n_streams, Tp, Ts):
    r = dict(zip(_MIXER_IN + _MIXER_OUT, refs))
    zbuf, ypool_ref, hp_re, hp_im, hs_re, hs_im, h0f_re, h0f_im = refs[len(_MIXER_IN) + len(_MIXER_OUT):][:8]
    n_fixed = len(_MIXER_IN) + len(_MIXER_OUT) + 8
    bf16_weights = dict(zip(_MIXER_BF16, refs[n_fixed:]))
    gen = dict(zip(_MIXER_GEN, refs[n_fixed + len(_MIXER_BF16):]))
    i = pl.program_id(0)
    ts = r["xp"].shape[0]
    d_pool = zbuf.shape[1]

    @pl.when(i == 0)
    def _():
        zbuf[0:HIST_ROWS, :] = jnp.zeros((HIST_ROWS, d_pool), F32)
        hp_re[...] = jnp.zeros(hp_re.shape, F32)
        hp_im[...] = jnp.zeros(hp_im.shape, F32)
        for name, ref in bf16_weights.items():
            ref[...] = r[name][...].astype(BF16)
        _expand_s5_params(r, gen, Tp, (Tp, Ts))

    r.update(bf16_weights)
    r.update(gen)

    @pl.when(i < n_prompt)
    def _():
        x = r["xp"][...]
        z = _dot(_rmsnorm(x, r["g_mix"][...]), r["w_in"][...])
        zbuf[HIST_ROWS:HIST_ROWS + ts, :] = z[:, :d_pool]
        u = z[:, d_pool:]
        pos = i * ts + lax.broadcasted_iota(jnp.int32, (ts, 1), 0)
        y_pool = _pool_windows(zbuf, ts, pos, r)
        y_state = _s5_scan(u, ts // Tp, Tp, 0, (hp_re[...], hp_im[...]), None, r, hp_re, hp_im)
        r["outp"][...] = _mix_tail(x, y_pool, u, y_state, r)
        tail = zbuf[ts:ts + HIST_ROWS, :]
        zbuf[0:HIST_ROWS, :] = tail
        r["histp"][...] = tail

    @pl.when(i == n_prompt)
    def _():
        ext = HIST_ROWS + Ts
        x = r["xs"][...]
        z = _dot(_rmsnorm(x, r["g_mix"][...]), r["w_in"][...])
        u = z[:, d_pool:]
        for b in range(n_streams):
            zbuf[b * ext:b * ext + 1, :] = jnp.zeros((1, d_pool), F32)
            for t in range(POOL_HIST):
                zbuf[b * ext + 1 + t:b * ext + 2 + t, :] = r["hist_in"][t, b:b + 1, :]
            zbuf[b * ext + HIST_ROWS:(b + 1) * ext, :] = z[b * Ts:(b + 1) * Ts, :d_pool]
            for t in range(POOL_HIST):
                row = (b + 1) * ext - POOL_HIST + t
                r["hists"][t, b:b + 1, :] = zbuf[row:row + 1, :]
        nrows = n_streams * ext - HIST_ROWS
        ridx = lax.broadcasted_iota(jnp.int32, (nrows, 1), 0)
        pos = PAST_LEN + lax.rem(ridx, ext)
        y_all = _pool_windows(zbuf, nrows, pos, r)
        for b in range(n_streams):
            ypool_ref[b * Ts:(b + 1) * Ts, :] = y_all[b * ext:b * ext + Ts, :]
        G, P = r["hrep"].shape
        for src, dst in ((r["h0_re"], h0f_re), (r["h0_im"], h0f_im)):
            for b in range(n_streams):
                for g in range(G):
                    dst[b:b + 1, g * P:(g + 1) * P] = src[b * G + g:b * G + g + 1, :]
        y_state = _s5_scan(u, n_streams, Ts, 1, None, (h0f_re, h0f_im), r, hs_re, hs_im)
        r["outs"][...] = _mix_tail(x, ypool_ref[...], u, y_state, r)
        for g in range(G):
            cols = slice(g * P, (g + 1) * P)
            r["hrep"][g:g + 1, :] = hp_re[:, cols]
            r["himp"][g:g + 1, :] = hp_im[:, cols]
            for b in range(n_streams):
                r["hres"][b * G + g:b * G + g + 1, :] = hs_re[b:b + 1, cols]
                r["hims"][b * G + g:b * G + g + 1, :] = hs_im[b:b + 1, cols]


def _const_spec(shape):
    nd = len(shape)
    return pl.BlockSpec(shape, lambda *_: (0,) * nd)


def _weight_spec(shape):
    nd = len(shape)
    return pl.BlockSpec(shape, lambda *_: (0,) * nd, pipeline_mode=pl.Buffered(1))


def _prompt_spec(ts, d, n_prompt):
    return pl.BlockSpec((ts, d), lambda i: (jnp.minimum(i, n_prompt - 1), 0))


def _mixer(xp, xs, hist_in, h0_re, h0_im, p, n_streams, Ts):
    S, D = xp.shape
    rows_s = xs.shape[0]
    ts = PROMPT_BLOCK
    n_prompt = S // ts
    d_pool = hist_in.shape[-1]
    n_state = p["a1_re"].shape[-1]
    P = h0_re.shape[-1]
    G = n_state // P
    assert n_streams * (HIST_ROWS + Ts) <= HIST_ROWS + ts and rows_s <= ts
    args = dict(p, xp=xp, xs=xs, hist_in=hist_in, h0_re=h0_re, h0_im=h0_im)
    ins = [args[n] for n in _MIXER_IN]
    in_specs = [_prompt_spec(ts, D, n_prompt)] + [_weight_spec(a.shape) for a in ins[1:]]
    n_half, nh, H = p["bb_re"].shape
    d_half = p["cc_re"].shape[1]
    gen_shapes = ([((n_half, d_half, 2 * nh), BF16), ((n_half, 2 * nh, d_half), BF16)]
                  + [((PROMPT_CHUNK, n_state), S5_DTYPE)] * 4 + [((2, n_state), F32)] * 2)
    out_shapes = dict(
        outp=(S, D), outs=(rows_s, D), histp=(HIST_ROWS, d_pool), hrep=(G, P), himp=(G, P),
        hists=(POOL_HIST, n_streams, d_pool), hres=(n_streams * G, P), hims=(n_streams * G, P))
    out_specs = [_prompt_spec(ts, D, n_prompt)] + [_const_spec(out_shapes[n]) for n in _MIXER_OUT[1:]]
    return pl.pallas_call(
        functools.partial(_mixer_kernel, n_prompt=n_prompt, n_streams=n_streams, Tp=PROMPT_CHUNK, Ts=Ts),
        grid=(n_prompt + 1,),
        in_specs=in_specs,
        out_specs=out_specs,
        out_shape=[jax.ShapeDtypeStruct(out_shapes[n], F32) for n in _MIXER_OUT],
        scratch_shapes=[pltpu.VMEM((HIST_ROWS + ts, d_pool), F32),
                        pltpu.VMEM((rows_s, d_pool), F32)]
        + [pltpu.VMEM((1, n_state), F32)] * 2 + [pltpu.VMEM((n_streams, n_state), F32)] * 4
        + [pltpu.VMEM(args[n].shape, BF16) for n in _MIXER_BF16]
        + [pltpu.VMEM(shape, dtype) for shape, dtype in gen_shapes],
        compiler_params=pltpu.CompilerParams(dimension_semantics=("arbitrary",),
                                             vmem_limit_bytes=VMEM_LIMIT),
        name="mixer",
    )(*ins)


def _attend(q_heads, k_heads, v_heads, scale):
    outs = []
    for q, k, v in zip(q_heads, k_heads, v_heads):
        s = lax.dot_general(q, k.astype(BF16), (((1,), (1,)), ((), ())), preferred_element_type=F32) * scale
        e = jnp.exp(s - jnp.max(s, axis=-1, keepdims=True))
        outs.append(_dot(e, v) / jnp.sum(e, axis=-1, keepdims=True))
    return outs


_XATTN_IN = ("xp", "xs", "mem", "kc", "vc", "g_mem", "w_k", "w_v", "g_xattn", "w_q", "w_o")
_XATTN_OUT = ("outp", "outs", "mk", "mv")


def _xattn_kernel(*refs, n_prompt, n_streams, Ts, hpu, scale):
    r = dict(zip(_XATTN_IN + _XATTN_OUT, refs))
    kp_ref, vp_ref, qs_ref, os_ref, kbuf, vbuf, sem, kvf, osem = refs[len(_XATTN_IN) + len(_XATTN_OUT):]
    i = pl.program_id(0)
    hd = kp_ref.shape[-1] // MEM_HEADS
    ups = MEM_HEADS // hpu

    def unit_copies(unit, hg):
        b, slot = unit // ups, lax.rem(unit, 2)
        return [pltpu.make_async_copy(src.at[0, b, :, hg * hpu + e, :], dst.at[slot, e], sem.at[slot, j * hpu + e])
                for j, (src, dst) in enumerate(((r["kc"], kbuf), (r["vc"], vbuf))) for e in range(hpu)]

    def kv_out_copies():
        return [pltpu.make_async_copy(kvf.at[j, h], dst.at[0, 0, :, h, :], osem.at[j, h])
                for j, dst in enumerate((r["mk"], r["mv"])) for h in range(MEM_HEADS)]

    def for_unit(unit, fn):
        for hg in range(ups):
            @pl.when(lax.rem(unit, ups) == hg)
            def _():
                for cp in unit_copies(unit, hg):
                    fn(cp)

    @pl.when(i == 0)
    def _():
        for_unit(i, lambda cp: cp.start())
        mem = r["mem"][...]
        inv = _inv_rms(mem)
        mg = (mem * r["g_mem"][...]).astype(BF16)
        k = _dot(mg, r["w_k"][...]) * inv
        v = _dot(mg, r["w_v"][...]) * inv
        for h in range(MEM_HEADS):
            kvf[0, h] = k[:, h * hd:(h + 1) * hd]
            kvf[1, h] = v[:, h * hd:(h + 1) * hd]
        for cp in kv_out_copies():
            cp.start()
        kp_ref[...] = k.astype(BF16)
        vp_ref[...] = v.astype(BF16)
        q = _norm_dot(r["xs"][...], r["g_xattn"][...], r["w_q"][...]).astype(BF16)
        for h in range(MEM_HEADS):
            qs_ref[h] = q[:, h * hd:(h + 1) * hd]

    @pl.when(i < n_prompt)
    def _():
        @pl.when(i + 1 < n_prompt)
        def _():
            for_unit(i + 1, lambda cp: cp.start())

        for_unit(i, lambda cp: cp.wait())

        x = r["xp"][...]
        q = _norm_dot(x, r["g_xattn"][...], r["w_q"][...]).astype(BF16)
        o = _attend([q[:, h * hd:(h + 1) * hd] for h in range(MEM_HEADS)],
                    [kp_ref[:, h * hd:(h + 1) * hd] for h in range(MEM_HEADS)],
                    [vp_ref[:, h * hd:(h + 1) * hd] for h in range(MEM_HEADS)], scale)
        r["outp"][...] = x + _dot(jnp.concatenate(o, axis=-1), r["w_o"][...])

        b, hg, slot = i // ups, lax.rem(i, ups), lax.rem(i, 2)
        rows = pl.ds(pl.multiple_of(b * Ts, Ts), Ts)
        heads = [hg * hpu + e for e in range(hpu)]
        o_s = _attend([qs_ref[h, rows, :] for h in heads], [kbuf[slot, e] for e in range(hpu)],
                      [vbuf[slot, e] for e in range(hpu)], scale)
        for h, o_h in zip(heads, o_s):
            os_ref[h, rows, :] = o_h.astype(BF16)

    @pl.when(i == n_prompt)
    def _():
        for cp in kv_out_copies():
            cp.wait()
        o = jnp.concatenate([os_ref[h] for h in range(MEM_HEADS)], axis=-1)
        r["outs"][...] = r["xs"][...] + _dot(o, r["w_o"][...])


def _xattn(xp, xs, mem, kc, vc, p, n_streams, Ts):
    S, D = xp.shape
    rows_s = xs.shape[0]
    ts = PROMPT_BLOCK
    n_prompt = S // ts
    n_mem = mem.shape[0]
    hd = D // MEM_HEADS
    hpu = n_streams * MEM_HEADS // n_prompt
    assert hpu * n_prompt == n_streams * MEM_HEADS and MEM_HEADS % hpu == 0
    args = dict(p, xp=xp, xs=xs, mem=mem, kc=kc, vc=vc)
    ins = [args[n] for n in _XATTN_IN]
    cache_spec = pl.BlockSpec(memory_space=pl.ANY)
    in_specs = [_prompt_spec(ts, D, n_prompt)]
    for n in _XATTN_IN[1:]:
        in_specs.append(cache_spec if n in ("kc", "vc") else _weight_spec(args[n].shape))
    out_shapes = dict(outp=(S, D), outs=(rows_s, D), mk=(1, 1, n_mem, MEM_HEADS, hd), mv=(1, 1, n_mem, MEM_HEADS, hd))
    out_specs = [_prompt_spec(ts, D, n_prompt), _const_spec(out_shapes["outs"]), cache_spec, cache_spec]
    return pl.pallas_call(
        functools.partial(_xattn_kernel, n_prompt=n_prompt, n_streams=n_streams, Ts=Ts, hpu=hpu, scale=hd ** -0.5),
        grid=(n_prompt + 1,),
        in_specs=in_specs,
        out_specs=out_specs,
        out_shape=[jax.ShapeDtypeStruct(out_shapes[n], F32) for n in _XATTN_OUT],
        scratch_shapes=[pltpu.VMEM((n_mem, D), BF16), pltpu.VMEM((n_mem, D), BF16),
                        pltpu.VMEM((MEM_HEADS, rows_s, hd), BF16), pltpu.VMEM((MEM_HEADS, rows_s, hd), BF16),
                        pltpu.VMEM((2, hpu, n_mem, hd), F32), pltpu.VMEM((2, hpu, n_mem, hd), F32),
                        pltpu.SemaphoreType.DMA((2, 2 * hpu)),
                        pltpu.VMEM((2, MEM_HEADS, n_mem, hd), F32), pltpu.SemaphoreType.DMA((2, MEM_HEADS))],
        compiler_params=pltpu.CompilerParams(dimension_semantics=("arbitrary",),
                                             vmem_limit_bytes=VMEM_LIMIT),
        name="xattn",
    )(*ins)


def _mlp_rows(x, g_ref, wup_ref, wdown_ref, gfin_ref):
    hg = (x * g_ref[...]).astype(BF16)
    acc = None
    for j in range(wup_ref.shape[1] // FF_CHUNK):
        cols = slice(j * FF_CHUNK, (j + 1) * FF_CHUNK)
        up = _dot(hg, wup_ref[:, cols])
        part = _dot(jnp.square(jnp.maximum(up, 0.0)), wdown_ref[cols, :])
        acc = part if acc is None else acc + part
    return _rmsnorm(x + acc * jnp.square(_inv_rms(x)), gfin_ref[...])


def _mlp_kernel(xp_ref, xs_ref, g_ref, wup_ref, wdown_ref, gfin_ref, outp_ref, outs_ref, *, n_prompt):
    i = pl.program_id(0)

    @pl.when(i < n_prompt)
    def _():
        outp_ref[...] = _mlp_rows(xp_ref[...], g_ref, wup_ref, wdown_ref, gfin_ref)

    @pl.when(i == n_prompt)
    def _():
        outs_ref[...] = _mlp_rows(xs_ref[...], g_ref, wup_ref, wdown_ref, gfin_ref)


def _mlp(xp, xs, p):
    S, D = xp.shape
    ts = PROMPT_BLOCK
    n_prompt = S // ts
    weights = (p["g_mlp"], p["w_up"], p["w_down"], p["g_final"])
    return pl.pallas_call(
        functools.partial(_mlp_kernel, n_prompt=n_prompt),
        grid=(n_prompt + 1,),
        in_specs=[_prompt_spec(ts, D, n_prompt), _weight_spec(xs.shape)] + [_weight_spec(w.shape) for w in weights],
        out_specs=[_prompt_spec(ts, D, n_prompt), _const_spec(xs.shape)],
        out_shape=[jax.ShapeDtypeStruct(xp.shape, F32), jax.ShapeDtypeStruct(xs.shape, F32)],
        compiler_params=pltpu.CompilerParams(dimension_semantics=("arbitrary",),
                                             vmem_limit_bytes=VMEM_LIMIT),
        name="mlp",
    )(xp, xs, *weights)


def _prepare_layer(l, g_mix, w_in, w_pool, pool_scale, a_re, a_im, b_re, b_im, c_re, c_im, d_skip,
                   log_dt, w_glu, b_glu, w_out, g_xattn, g_mem, w_q, w_k, w_v, w_o, g_mlp, w_up,
                   w_down, g_final):
    G, P = a_re.shape[1:]
    gi = G // N_HALF
    ar, ai = a_re[l].astype(F32), a_im[l].astype(F32)
    dt = jnp.exp(log_dt[l].astype(F32))[:, None]
    lam_re, lam_im = ar * dt, ai * dt

    mag = jnp.exp(lam_re)
    ab_re, ab_im = mag * jnp.cos(lam_im), mag * jnp.sin(lam_im)

    den = ar * ar + ai * ai
    coef_re = ((ab_re - 1.0) * ar + ab_im * ai) / den
    coef_im = (ab_im * ar - (ab_re - 1.0) * ai) / den
    br, bi = b_re[l].astype(F32), b_im[l].astype(F32)
    bb_re = coef_re[..., None] * br - coef_im[..., None] * bi
    bb_im = coef_re[..., None] * bi + coef_im[..., None] * br

    half = lambda t: t.astype(F32).reshape((N_HALF, gi * t.shape[1], t.shape[2]))
    row = lambda v: v.astype(F32).reshape(1, -1)
    return dict(
        g_mix=row(g_mix[l]), w_in=w_in[l], w_pool=w_pool[l], pool_scale=row(pool_scale[l]),
        bb_re=half(bb_re), bb_im=half(bb_im), cc_re=half(c_re[l]), cc_im=half(c_im[l]),
        a1_re=row(ab_re), a1_im=row(ab_im),
        d_skip=row(d_skip[l]), w_glu=w_glu[l], b_glu=row(b_glu[l]), w_out=w_out[l],
        g_xattn=row(g_xattn[l]), g_mem=row(g_mem[l]), w_q=w_q[l], w_k=w_k[l], w_v=w_v[l], w_o=w_o[l],
        g_mlp=row(g_mlp[l]), w_up=w_up[l], w_down=w_down[l], g_final=row(g_final))


def kernel(x_prompt, x_sample, cache_mem_k, cache_mem_v, state_pool, state_ssm_re, state_ssm_im, mem_prompt, g_mix, w_in, w_pool, pool_scale, ssm_a_re, ssm_a_im, ssm_b_re, ssm_b_im, ssm_c_re, ssm_c_im, ssm_d, ssm_log_dt, w_glu, b_glu, w_out, g_xattn, g_mem, w_q, w_k, w_v, w_o, g_mlp, w_up, w_down, g_final):
    depth = g_mix.shape[0]
    assert depth == 1 and x_prompt.shape[0] == 1, "single layer, single prompt stream"
    Bp, S, D = x_prompt.shape
    Bs, Ts, _ = x_sample.shape
    n_mem = mem_prompt.shape[1]
    G, P = ssm_a_re.shape[1:]
    assert S % PROMPT_BLOCK == 0 and PROMPT_BLOCK % PROMPT_CHUNK == 0 and Ts <= PROMPT_CHUNK

    l = 0
    p = _prepare_layer(l, g_mix, w_in, w_pool, pool_scale, ssm_a_re, ssm_a_im, ssm_b_re, ssm_b_im,
                       ssm_c_re, ssm_c_im, ssm_d, ssm_log_dt, w_glu, b_glu, w_out, g_xattn, g_mem,
                       w_q, w_k, w_v, w_o, g_mlp, w_up, w_down, g_final)

    xp1, xs1, hist_p, hre_p, him_p, hist_s, hre_s, him_s = _mixer(
        x_prompt[0], x_sample.reshape(Bs * Ts, D), jnp.transpose(state_pool[l], (1, 0, 2)),
        state_ssm_re[l].reshape(Bs * G, P), state_ssm_im[l].reshape(Bs * G, P), p, Bs, Ts)
    xp2, xs2, mk, mv = _xattn(xp1, xs1, mem_prompt[0], cache_mem_k[l:l + 1], cache_mem_v[l:l + 1], p, Bs, Ts)
    y_prompt, y_sample = _mlp(xp2, xs2, p)

    hd = D // MEM_HEADS
    return (y_prompt[None], y_sample.reshape(Bs, Ts, D),
            mk, mv,
            hist_p[1:].reshape(1, Bp, POOL_HIST, -1),
            hre_p.reshape(1, Bp, G, P), him_p.reshape(1, Bp, G, P),
            jnp.transpose(hist_s, (1, 0, 2))[None], hre_s.reshape(1, Bs, G, P), him_s.reshape(1, Bs, G, P))
```

```python
import functools

import jax
import jax.numpy as jnp
from jax import lax
from jax.experimental import pallas as pl
from jax.experimental.pallas import tpu as pltpu

F32 = jnp.float32
BF16 = jnp.bfloat16

EPS = 1e-6
PAST_LEN = 1024
POOL_WINDOWS = (2, 4, 8, 16)
POOL_HIST = max(POOL_WINDOWS) - 1
HIST_ROWS = 16
MEM_HEADS = 4
N_HALF = 2
S5_TILE = 256
S5_DTYPE = F32

PROMPT_BLOCK = 512
PROMPT_CHUNK = 128
FF_CHUNK = 1024
VMEM_LIMIT = 56 * 1024 * 1024


def _rmsnorm(x, g):
    return x * lax.rsqrt(jnp.mean(x * x, axis=-1, keepdims=True) + EPS) * g


def _dot(a, b):
    return jnp.dot(a.astype(BF16), b.astype(BF16), preferred_element_type=F32)


def _inv_rms(x):
    return lax.rsqrt(jnp.mean(x * x, axis=-1, keepdims=True) + EPS)


def _norm_dot(x, g, w):
    return _dot(x * g, w) * _inv_rms(x)


def _pool_windows(zbuf, nrows, pos, r):
    gw = r["w_pool"].shape[-1]
    outs = []
    for g, w in enumerate(POOL_WINDOWS):
        lanes = slice(g * gw, (g + 1) * gw)
        win = zbuf[pl.ds(0, HIST_ROWS + nrows), lanes]
        k = 1
        while k < w:
            win = win + pltpu.roll(win, k, 0)
            k *= 2
        cur = zbuf[pl.ds(HIST_ROWS, nrows), lanes]
        cnt = jnp.minimum(pos + 1, w).astype(F32)
        pooled = win[HIST_ROWS:, :] / cnt - cur
        outs.append(_dot(pooled, r["w_pool"][g]))
    return jnp.concatenate(outs, axis=-1) * r["pool_scale"][...]


def _cmul(ar, ai, br, bi):
    return ar * br - ai * bi, ar * bi + ai * br


def _s5_scan(u, n_chunks, T, end_row, h_carry, h_rows, r, hout_re_ref, hout_im_ref):
    hw = r["bblk"].shape[1]
    nh = r["bblk"].shape[2] // 2
    u_bf = u.astype(BF16)
    r_i = lax.broadcasted_iota(jnp.int32, (T, T), 0)
    c_i = lax.broadcasted_iota(jnp.int32, (T, T), 1)
    ltri = jnp.where(r_i >= c_i, 1.0, 0.0).astype(BF16)

    ys = []
    for k in range(N_HALF):
        uk = u_bf[:, k * hw:(k + 1) * hw]
        yk = None
        for j in range(nh // S5_TILE):
            re_c = slice(j * S5_TILE, (j + 1) * S5_TILE)
            im_c = slice(nh + j * S5_TILE, nh + (j + 1) * S5_TILE)
            st = slice(k * nh + j * S5_TILE, k * nh + (j + 1) * S5_TILE)
            bu_re = _dot(uk, r["bblk"][k, :, re_c])
            bu_im = _dot(uk, r["bblk"][k, :, im_c])
            if h_rows is None:
                carry = (h_carry[0][:, st], h_carry[1][:, st])
            h_re, h_im = [], []
            for c in range(n_chunks):
                rows = slice(c * T, (c + 1) * T)
                xr, xi = _cmul(bu_re[rows].astype(S5_DTYPE), bu_im[rows].astype(S5_DTYPE),
                               r["pneg_re"][0:T, st], r["pneg_im"][0:T, st])
                wr = _dot(ltri, xr)
                wi = _dot(ltri, xi)
                if h_rows is not None:
                    carry = (h_rows[0][pl.ds(c, 1), st], h_rows[1][pl.ds(c, 1), st])
                cr, ci = _cmul(r["a1_re"][:, st], r["a1_im"][:, st], carry[0], carry[1])
                wr, wi = wr + cr, wi + ci
                sr, si = _cmul(r["ppos_re"][0:T, st], r["ppos_im"][0:T, st], wr.astype(S5_DTYPE), wi.astype(S5_DTYPE))
                carry = _cmul(r["pend_re"][end_row:end_row + 1, st], r["pend_im"][end_row:end_row + 1, st],
                              wr[T - 1:T, :], wi[T - 1:T, :])
                if h_rows is not None:
                    hout_re_ref[pl.ds(c, 1), st] = carry[0]
                    hout_im_ref[pl.ds(c, 1), st] = carry[1]
                h_re.append(sr)
                h_im.append(si)
            if h_rows is None:
                hout_re_ref[:, st] = carry[0]
                hout_im_ref[:, st] = carry[1]
            part = (_dot(jnp.concatenate(h_re, axis=0), r["cblk"][k, re_c, :])
                    + _dot(jnp.concatenate(h_im, axis=0), r["cblk"][k, im_c, :]))
            yk = part if yk is None else yk + part
        ys.append(yk)
    return jnp.concatenate(ys, axis=-1)


def _mix_tail(x, y_pool, u, y_state, r):
    y = jax.nn.gelu(y_state + r["d_skip"][...] * u)
    y = y * jax.nn.sigmoid(_dot(y, r["w_glu"][...]) + r["b_glu"][...])
    ycat = jnp.concatenate([y_pool.astype(BF16), y.astype(BF16)], axis=-1)
    return x + _dot(ycat, r["w_out"][...])


_MIXER_IN = ("xp", "xs", "hist_in", "h0_re", "h0_im",
             "g_mix", "w_in", "w_pool", "pool_scale", "bb_re", "bb_im", "cc_re", "cc_im", "a1_re", "a1_im",
             "d_skip", "w_glu", "b_glu", "w_out")
_MIXER_OUT = ("outp", "outs", "histp", "hrep", "himp", "hists", "hres", "hims")
_MIXER_BF16 = ("w_in", "w_pool", "w_glu", "w_out")
_MIXER_GEN = ("bblk", "cblk", "pneg_re", "pneg_im", "ppos_re", "ppos_im", "pend_re", "pend_im")


def _power_table(re1, im1, T):
    SUB = 16
    assert T % SUB == 0
    n = re1.shape[-1]
    t = lax.broadcasted_iota(jnp.int32, (SUB, 1), 0)
    lo_re, lo_im = jnp.ones((SUB, n), F32), jnp.zeros((SUB, n), F32)
    cr, ci = re1, im1
    k = 1
    while k < SUB:
        bit = (t & k) != 0
        lo_re, lo_im = _cmul(lo_re, lo_im, jnp.where(bit, cr, 1.0), jnp.where(bit, ci, 0.0))
        cr, ci = _cmul(cr, ci, cr, ci)
        k *= 2
    hi_re, hi_im = [jnp.ones((1, n), F32)], [jnp.zeros((1, n), F32)]
    for _ in range(T // SUB - 1):
        nr, ni = _cmul(hi_re[-1], hi_im[-1], cr, ci)
        hi_re.append(nr)
        hi_im.append(ni)
    rep = lambda rows: jnp.concatenate([jnp.broadcast_to(v, (SUB, n)) for v in rows], axis=0)
    tile = lambda v: jnp.concatenate([v] * (T // SUB), axis=0)
    return _cmul(tile(lo_re), tile(lo_im), rep(hi_re), rep(hi_im))


def _expand_s5_params(r, gen, T, t_ends):
    are, aim = r["a1_re"][...], r["a1_im"][...]
    pr, pi = _power_table(are, aim, T)
    gen["ppos_re"][...] = pr.astype(S5_DTYPE)
    gen["ppos_im"][...] = pi.astype(S5_DTYPE)
    for j, te in enumerate(t_ends):
        gen["pend_re"][j:j + 1, :] = pr[te - 1:te, :]
        gen["pend_im"][j:j + 1, :] = pi[te - 1:te, :]
    den = are * are + aim * aim
    nr, ni = _power_table(are / den, -aim / den, T)
    gen["pneg_re"][...] = nr.astype(S5_DTYPE)
    gen["pneg_im"][...] = ni.astype(S5_DTYPE)

    n_half, rows_b, H = r["bb_re"].shape
    _, rows_c, P = r["cc_re"].shape
    nh = rows_b

    def spread(x, width):
        w = x.shape[1]
        sel = lax.rem(lax.broadcasted_iota(jnp.int32, (w, width), 1), w) == lax.broadcasted_iota(jnp.int32, (w, width), 0)
        return _dot(x, jnp.where(sel, 1.0, 0.0))

    def same_group(shape, row_size, col_size):
        return (lax.div(lax.broadcasted_iota(jnp.int32, shape, 0), row_size)
                == lax.div(lax.broadcasted_iota(jnp.int32, shape, 1), col_size))

    for k in range(n_half):
        for c, (b_src, c_src, sign) in enumerate(((r["bb_re"], r["cc_re"], 1.0), (r["bb_im"], r["cc_im"], -1.0))):
            yb = spread(b_src[k], rows_c)
            yb = jnp.where(same_group(yb.shape, P, H), yb, 0.0)
            gen["bblk"][k, :, c * nh:(c + 1) * nh] = yb.T.astype(BF16)
            yc = spread(c_src[k], rows_b)
            yc = jnp.where(same_group(yc.shape, H, P), sign * yc, 0.0)
            gen["cblk"][k, c * nh:(c + 1) * nh, :] = yc.T.astype(BF16)


def _mixer_kernel(*refs, n_prompt, n_streams, Tp, Ts):
    r = dict(zip(_MIXER_IN + _MIXER_OUT, refs))
    zbuf, ypool_ref, hp_re, hp_im, hs_re, hs_im, h0f_re, h0f_im = refs[len(_MIXER_IN) + len(_MIXER_OUT):][:8]
    n_fixed = len(_MIXER_IN) + len(_MIXER_OUT) + 8
    bf16_weights = dict(zip(_MIXER_BF16, refs[n_fixed:]))
    gen = dict(zip(_MIXER_GEN, refs[n_fixed + len(_MIXER_BF16):]))
    i = pl.program_id(0)
    ts = r["xp"].shape[0]
    d_pool = zbuf.shape[1]

    @pl.when(i == 0)
    def _():
        zbuf[0:HIST_ROWS, :] = jnp.zeros((HIST_ROWS, d_pool), F32)
        hp_re[...] = jnp.zeros(hp_re.shape, F32)
        hp_im[...] = jnp.zeros(hp_im.shape, F32)
        for name, ref in bf16_weights.items():
            ref[...] = r[name][...].astype(BF16)
        _expand_s5_params(r, gen, Tp, (Tp, Ts))

    r.update(bf16_weights)
    r.update(gen)

    @pl.when(i < n_prompt)
    def _():
        x = r["xp"][...]
        z = _dot(_rmsnorm(x, r["g_mix"][...]), r["w_in"][...])
        zbuf[HIST_ROWS:HIST_ROWS + ts, :] = z[:, :d_pool]
        u = z[:, d_pool:]
        pos = i * ts + lax.broadcasted_iota(jnp.int32, (ts, 1), 0)
        y_pool = _pool_windows(zbuf, ts, pos, r)
        y_state = _s5_scan(u, ts // Tp, Tp, 0, (hp_re[...], hp_im[...]), None, r, hp_re, hp_im)
        r["outp"][...] = _mix_tail(x, y_pool, u, y_state, r)
        tail = zbuf[ts:ts + HIST_ROWS, :]
        zbuf[0:HIST_ROWS, :] = tail
        r["histp"][...] = tail

    @pl.when(i == n_prompt)
    def _():
        ext = HIST_ROWS + Ts
        x = r["xs"][...]
        z = _dot(_rmsnorm(x, r["g_mix"][...]), r["w_in"][...])
        u = z[:, d_pool:]
        for b in range(n_streams):
            zbuf[b * ext:b * ext + 1, :] = jnp.zeros((1, d_pool), F32)
            for t in range(POOL_HIST):
                zbuf[b * ext + 1 + t:b * ext + 2 + t, :] = r["hist_in"][t, b:b + 1, :]
            zbuf[b * ext + HIST_ROWS:(b + 1) * ext, :] = z[b * Ts:(b + 1) * Ts, :d_pool]
            for t in range(POOL_HIST):
                row = (b + 1) * ext - POOL_HIST + t
                r["hists"][t, b:b + 1, :] = zbuf[row:row + 1, :]
        nrows = n_streams * ext - HIST_ROWS
        ridx = lax.broadcasted_iota(jnp.int32, (nrows, 1), 0)
        pos = PAST_LEN + lax.rem(ridx, ext)
        y_all = _pool_windows(zbuf, nrows, pos, r)
        for b in range(n_streams):
            ypool_ref[b * Ts:(b + 1) * Ts, :] = y_all[b * ext:b * ext + Ts, :]
        G, P = r["hrep"].shape
        for src, dst in ((r["h0_re"], h0f_re), (r["h0_im"], h0f_im)):
            for b in range(n_streams):
                for g in range(G):
                    dst[b:b + 1, g * P:(g + 1) * P] = src[b * G + g:b * G + g + 1, :]
        y_state = _s5_scan(u, n_streams, Ts, 1, None, (h0f_re, h0f_im), r, hs_re, hs_im)
        r["outs"][...] = _mix_tail(x, ypool_ref[...], u, y_state, r)
        for g in range(G):
            cols = slice(g * P, (g + 1) * P)
            r["hrep"][g:g + 1, :] = hp_re[:, cols]
            r["himp"][g:g + 1, :] = hp_im[:, cols]
            for b in range(n_streams):
                r["hres"][b * G + g:b * G + g + 1, :] = hs_re[b:b + 1, cols]
                r["hims"][b * G + g:b * G + g + 1, :] = hs_im[b:b + 1, cols]


def _const_spec(shape):
    nd = len(shape)
    return pl.BlockSpec(shape, lambda *_: (0,) * nd)


def _weight_spec(shape):
    nd = len(shape)
    return pl.BlockSpec(shape, lambda *_: (0,) * nd, pipeline_mode=pl.Buffered(1))


def _prompt_spec(ts, d, n_prompt):
    return pl.BlockSpec((ts, d), lambda i: (jnp.minimum(i, n_prompt - 1), 0))


def _mixer(xp, xs, hist_in, h0_re, h0_im, p, n_streams, Ts):
    S, D = xp.shape
    rows_s = xs.shape[0]
    ts = PROMPT_BLOCK
    n_prompt = S // ts
    d_pool = hist_in.shape[-1]
    n_state = p["a1_re"].shape[-1]
    P = h0_re.shape[-1]
    G = n_state // P
    assert n_streams * (HIST_ROWS + Ts) <= HIST_ROWS + ts and rows_s <= ts
    args = dict(p, xp=xp, xs=xs, hist_in=hist_in, h0_re=h0_re, h0_im=h0_im)
    ins = [args[n] for n in _MIXER_IN]
    in_specs = [_prompt_spec(ts, D, n_prompt)] + [_weight_spec(a.shape) for a in ins[1:]]
    n_half, nh, H = p["bb_re"].shape
    d_half = p["cc_re"].shape[1]
    gen_shapes = ([((n_half, d_half, 2 * nh), BF16), ((n_half, 2 * nh, d_half), BF16)]
                  + [((PROMPT_CHUNK, n_state), S5_DTYPE)] * 4 + [((2, n_state), F32)] * 2)
    out_shapes = dict(
        outp=(S, D), outs=(rows_s, D), histp=(HIST_ROWS, d_pool), hrep=(G, P), himp=(G, P),
        hists=(POOL_HIST, n_streams, d_pool), hres=(n_streams * G, P), hims=(n_streams * G, P))
    out_specs = [_prompt_spec(ts, D, n_prompt)] + [_const_spec(out_shapes[n]) for n in _MIXER_OUT[1:]]
    return pl.pallas_call(
        functools.partial(_mixer_kernel, n_prompt=n_prompt, n_streams=n_streams, Tp=PROMPT_CHUNK, Ts=Ts),
        grid=(n_prompt + 1,),
        in_specs=in_specs,
        out_specs=out_specs,
        out_shape=[jax.ShapeDtypeStruct(out_shapes[n], F32) for n in _MIXER_OUT],
        scratch_shapes=[pltpu.VMEM((HIST_ROWS + ts, d_pool), F32),
                        pltpu.VMEM((rows_s, d_pool), F32)]
        + [pltpu.VMEM((1, n_state), F32)] * 2 + [pltpu.VMEM((n_streams, n_state), F32)] * 4
        + [pltpu.VMEM(args[n].shape, BF16) for n in _MIXER_BF16]
        + [pltpu.VMEM(shape, dtype) for shape, dtype in gen_shapes],
        compiler_params=pltpu.CompilerParams(dimension_semantics=("arbitrary",),
                                             vmem_limit_bytes=VMEM_LIMIT),
        name="mixer",
    )(*ins)


def _attend(q_heads, k_heads, v_heads, scale):
    outs = []
    for q, k, v in zip(q_heads, k_heads, v_heads):
        s = lax.dot_general(q, k.astype(BF16), (((1,), (1,)), ((), ())), preferred_element_type=F32) * scale
        e = jnp.exp(s - jnp.max(s, axis=-1, keepdims=True))
        outs.append(_dot(e, v) / jnp.sum(e, axis=-1, keepdims=True))
    return outs


_XATTN_IN = ("xp", "xs", "mem", "kc", "vc", "g_mem", "w_k", "w_v", "g_xattn", "w_q", "w_o")
_XATTN_OUT = ("outp", "outs", "mk", "mv")


def _xattn_kernel(*refs, n_prompt, n_streams, Ts, hpu, scale):
    r = dict(zip(_XATTN_IN + _XATTN_OUT, refs))
    kp_ref, vp_ref, qs_ref, os_ref, kbuf, vbuf, sem, kvf, osem = refs[len(_XATTN_IN) + len(_XATTN_OUT):]
    i = pl.program_id(0)
    hd = kp_ref.shape[-1] // MEM_HEADS
    ups = MEM_HEADS // hpu

    def unit_copies(unit, hg):
        b, slot = unit // ups, lax.rem(unit, 2)
        return [pltpu.make_async_copy(src.at[0, b, :, hg * hpu + e, :], dst.at[slot, e], sem.at[slot, j * hpu + e])
                for j, (src, dst) in enumerate(((r["kc"], kbuf), (r["vc"], vbuf))) for e in range(hpu)]

    def kv_out_copies():
        return [pltpu.make_async_copy(kvf.at[j, h], dst.at[0, 0, :, h, :], osem.at[j, h])
                for j, dst in enumerate((r["mk"], r["mv"])) for h in range(MEM_HEADS)]

    def for_unit(unit, fn):
        for hg in range(ups):
            @pl.when(lax.rem(unit, ups) == hg)
            def _():
                for cp in unit_copies(unit, hg):
                    fn(cp)

    @pl.when(i == 0)
    def _():
        for_unit(i, lambda cp: cp.start())
        mem = r["mem"][...]
        inv = _inv_rms(mem)
        mg = (mem * r["g_mem"][...]).astype(BF16)
        k = _dot(mg, r["w_k"][...]) * inv
        v = _dot(mg, r["w_v"][...]) * inv
        for h in range(MEM_HEADS):
            kvf[0, h] = k[:, h * hd:(h + 1) * hd]
            kvf[1, h] = v[:, h * hd:(h + 1) * hd]
        for cp in kv_out_copies():
            cp.start()
        kp_ref[...] = k.astype(BF16)
        vp_ref[...] = v.astype(BF16)
        q = _norm_dot(r["xs"][...], r["g_xattn"][...], r["w_q"][...]).astype(BF16)
        for h in range(MEM_HEADS):
            qs_ref[h] = q[:, h * hd:(h + 1) * hd]

    @pl.when(i < n_prompt)
    def _():
        @pl.when(i + 1 < n_prompt)
        def _():
            for_unit(i + 1, lambda cp: cp.start())

        for_unit(i, lambda cp: cp.wait())

        x = r["xp"][...]
        q = _norm_dot(x, r["g_xattn"][...], r["w_q"][...]).astype(BF16)
        o = _attend([q[:, h * hd:(h + 1) * hd] for h in range(MEM_HEADS)],
                    [kp_ref[:, h * hd:(h + 1) * hd] for h in range(MEM_HEADS)],
                    [vp_ref[:, h * hd:(h + 1) * hd] for h in range(MEM_HEADS)], scale)
        r["outp"][...] = x + _dot(jnp.concatenate(o, axis=-1), r["w_o"][...])

        b, hg, slot = i // ups, lax.rem(i, ups), lax.rem(i, 2)
        rows = pl.ds(pl.multiple_of(b * Ts, Ts), Ts)
        heads = [hg * hpu + e for e in range(hpu)]
        o_s = _attend([qs_ref[h, rows, :] for h in heads], [kbuf[slot, e] for e in range(hpu)],
                      [vbuf[slot, e] for e in range(hpu)], scale)
        for h, o_h in zip(heads, o_s):
            os_ref[h, rows, :] = o_h.astype(BF16)

    @pl.when(i == n_prompt)
    def _():
        for cp in kv_out_copies():
            cp.wait()
        o = jnp.concatenate([os_ref[h] for h in range(MEM_HEADS)], axis=-1)
        r["outs"][...] = r["xs"][...] + _dot(o, r["w_o"][...])


def _xattn(xp, xs, mem, kc, vc, p, n_streams, Ts):
    S, D = xp.shape
    rows_s = xs.shape[0]
    ts = PROMPT_BLOCK
    n_prompt = S // ts
    n_mem = mem.shape[0]
    hd = D // MEM_HEADS
    hpu = n_streams * MEM_HEADS // n_prompt
    assert hpu * n_prompt == n_streams * MEM_HEADS and MEM_HEADS % hpu == 0
    args = dict(p, xp=xp, xs=xs, mem=mem, kc=kc, vc=vc)
    ins = [args[n] for n in _XATTN_IN]
    cache_spec = pl.BlockSpec(memory_space=pl.ANY)
    in_specs = [_prompt_spec(ts, D, n_prompt)]
    for n in _XATTN_IN[1:]:
        in_specs.append(cache_spec if n in ("kc", "vc") else _weight_spec(args[n].shape))
    out_shapes = dict(outp=(S, D), outs=(rows_s, D), mk=(1, 1, n_mem, MEM_HEADS, hd), mv=(1, 1, n_mem, MEM_HEADS, hd))
    out_specs = [_prompt_spec(ts, D, n_prompt), _const_spec(out_shapes["outs"]), cache_spec, cache_spec]
    return pl.pallas_call(
        functools.partial(_xattn_kernel, n_prompt=n_prompt, n_streams=n_streams, Ts=Ts, hpu=hpu, scale=hd ** -0.5),
        grid=(n_prompt + 1,),
        in_specs=in_specs,
        out_specs=out_specs,
        out_shape=[jax.ShapeDtypeStruct(out_shapes[n], F32) for n in _XATTN_OUT],
        scratch_shapes=[pltpu.VMEM((n_mem, D), BF16), pltpu.VMEM((n_mem, D), BF16),
                        pltpu.VMEM((MEM_HEADS, rows_s, hd), BF16), pltpu.VMEM((MEM_HEADS, rows_s, hd), BF16),
                        pltpu.VMEM((2, hpu, n_mem, hd), F32), pltpu.VMEM((2, hpu, n_mem, hd), F32),
                        pltpu.SemaphoreType.DMA((2, 2 * hpu)),
                        pltpu.VMEM((2, MEM_HEADS, n_mem, hd), F32), pltpu.SemaphoreType.DMA((2, MEM_HEADS))],
        compiler_params=pltpu.CompilerParams(dimension_semantics=("arbitrary",),
                                             vmem_limit_bytes=VMEM_LIMIT),
        name="xattn",
    )(*ins)


def _mlp_rows(x, g_ref, wup_ref, wdown_ref, gfin_ref, before_chunk=None):
    hg = (x * g_ref[...]).astype(BF16)
    acc = None
    for j in range(wup_ref.shape[1] // FF_CHUNK):
        cols = slice(j * FF_CHUNK, (j + 1) * FF_CHUNK)
        if before_chunk is not None:
            before_chunk(j, 0)
        up = _dot(hg, wup_ref[:, cols])
        if before_chunk is not None:
            before_chunk(j, 1)
        part = _dot(jnp.square(jnp.maximum(up, 0.0)), wdown_ref[cols, :])
        acc = part if acc is None else acc + part
    return _rmsnorm(x + acc * jnp.square(_inv_rms(x)), gfin_ref[...])


def _mlp_kernel(xp_ref, xs_ref, g_ref, wup_hbm, wdown_hbm, gfin_ref, outp_ref, outs_ref, wup_ref, wdown_ref, sem,
                *, n_prompt):
    i = pl.program_id(0)
    n_ff = wup_ref.shape[1] // FF_CHUNK

    def weight_copy(j, which):
        cols = pl.ds(j * FF_CHUNK, FF_CHUNK)
        if which == 0:
            return pltpu.make_async_copy(wup_hbm.at[:, cols], wup_ref.at[:, cols], sem.at[0, j])
        return pltpu.make_async_copy(wdown_hbm.at[cols, :], wdown_ref.at[cols, :], sem.at[1, j])

    @pl.when(i == 0)
    def _():
        for j in range(n_ff):
            for which in range(2):
                weight_copy(j, which).start()
        outp_ref[...] = _mlp_rows(xp_ref[...], g_ref, wup_ref, wdown_ref, gfin_ref,
                                  before_chunk=lambda j, which: weight_copy(j, which).wait())

    @pl.when(jnp.logical_and(i > 0, i < n_prompt))
    def _():
        outp_ref[...] = _mlp_rows(xp_ref[...], g_ref, wup_ref, wdown_ref, gfin_ref)

    @pl.when(i == n_prompt)
    def _():
        outs_ref[...] = _mlp_rows(xs_ref[...], g_ref, wup_ref, wdown_ref, gfin_ref)


def _mlp(xp, xs, p):
    S, D = xp.shape
    ts = PROMPT_BLOCK
    n_prompt = S // ts
    assert n_prompt >= 1
    any_spec = pl.BlockSpec(memory_space=pl.ANY)
    n_ff = p["w_up"].shape[1] // FF_CHUNK
    return pl.pallas_call(
        functools.partial(_mlp_kernel, n_prompt=n_prompt),
        grid=(n_prompt + 1,),
        in_specs=[_prompt_spec(ts, D, n_prompt), _weight_spec(xs.shape), _weight_spec(p["g_mlp"].shape),
                  any_spec, any_spec, _weight_spec(p["g_final"].shape)],
        out_specs=[_prompt_spec(ts, D, n_prompt), _const_spec(xs.shape)],
        out_shape=[jax.ShapeDtypeStruct(xp.shape, F32), jax.ShapeDtypeStruct(xs.shape, F32)],
        scratch_shapes=[pltpu.VMEM(p["w_up"].shape, F32), pltpu.VMEM(p["w_down"].shape, F32),
                        pltpu.SemaphoreType.DMA((2, n_ff))],
        compiler_params=pltpu.CompilerParams(dimension_semantics=("arbitrary",),
                                             vmem_limit_bytes=VMEM_LIMIT),
        name="mlp",
    )(xp, xs, p["g_mlp"], p["w_up"], p["w_down"], p["g_final"])


def _prepare_layer(l, g_mix, w_in, w_pool, pool_scale, a_re, a_im, b_re, b_im, c_re, c_im, d_skip,
                   log_dt, w_glu, b_glu, w_out, g_xattn, g_mem, w_q, w_k, w_v, w_o, g_mlp, w_up,
                   w_down, g_final):
    G, P = a_re.shape[1:]
    gi = G // N_HALF
    ar, ai = a_re[l].astype(F32), a_im[l].astype(F32)
    dt = jnp.exp(log_dt[l].astype(F32))[:, None]
    lam_re, lam_im = ar * dt, ai * dt

    mag = jnp.exp(lam_re)
    ab_re, ab_im = mag * jnp.cos(lam_im), mag * jnp.sin(lam_im)

    den = ar * ar + ai * ai
    coef_re = ((ab_re - 1.0) * ar + ab_im * ai) / den
    coef_im = (ab_im * ar - (ab_re - 1.0) * ai) / den
    br, bi = b_re[l].astype(F32), b_im[l].astype(F32)
    bb_re = coef_re[..., None] * br - coef_im[..., None] * bi
    bb_im = coef_re[..., None] * bi + coef_im[..., None] * br

    half = lambda t: t.astype(F32).reshape((N_HALF, gi * t.shape[1], t.shape[2]))
    row = lambda v: v.astype(F32).reshape(1, -1)
    return dict(
        g_mix=row(g_mix[l]), w_in=w_in[l], w_pool=w_pool[l], pool_scale=row(pool_scale[l]),
        bb_re=half(bb_re), bb_im=half(bb_im), cc_re=half(c_re[l]), cc_im=half(c_im[l]),
        a1_re=row(ab_re), a1_im=row(ab_im),
        d_skip=row(d_skip[l]), w_glu=w_glu[l], b_glu=row(b_glu[l]), w_out=w_out[l],
        g_xattn=row(g_xattn[l]), g_mem=row(g_mem[l]), w_q=w_q[l], w_k=w_k[l], w_v=w_v[l], w_o=w_o[l],
        g_mlp=row(g_mlp[l]), w_up=w_up[l], w_down=w_down[l], g_final=row(g_final))


def kernel(x_prompt, x_sample, cache_mem_k, cache_mem_v, state_pool, state_ssm_re, state_ssm_im, mem_prompt, g_mix, w_in, w_pool, pool_scale, ssm_a_re, ssm_a_im, ssm_b_re, ssm_b_im, ssm_c_re, ssm_c_im, ssm_d, ssm_log_dt, w_glu, b_glu, w_out, g_xattn, g_mem, w_q, w_k, w_v, w_o, g_mlp, w_up, w_down, g_final):
    depth = g_mix.shape[0]
    assert depth == 1 and x_prompt.shape[0] == 1, "single layer, single prompt stream"
    Bp, S, D = x_prompt.shape
    Bs, Ts, _ = x_sample.shape
    n_mem = mem_prompt.shape[1]
    G, P = ssm_a_re.shape[1:]
    assert S % PROMPT_BLOCK == 0 and PROMPT_BLOCK % PROMPT_CHUNK == 0 and Ts <= PROMPT_CHUNK

    l = 0
    p = _prepare_layer(l, g_mix, w_in, w_pool, pool_scale, ssm_a_re, ssm_a_im, ssm_b_re, ssm_b_im,
                       ssm_c_re, ssm_c_im, ssm_d, ssm_log_dt, w_glu, b_glu, w_out, g_xattn, g_mem,
                       w_q, w_k, w_v, w_o, g_mlp, w_up, w_down, g_final)

    xp1, xs1, hist_p, hre_p, him_p, hist_s, hre_s, him_s = _mixer(
        x_prompt[0], x_sample.reshape(Bs * Ts, D), jnp.transpose(state_pool[l], (1, 0, 2)),
        state_ssm_re[l].reshape(Bs * G, P), state_ssm_im[l].reshape(Bs * G, P), p, Bs, Ts)
    xp2, xs2, mk, mv = _xattn(xp1, xs1, mem_prompt[0], cache_mem_k[l:l + 1], cache_mem_v[l:l + 1], p, Bs, Ts)
    y_prompt, y_sample = _mlp(xp2, xs2, p)

    hd = D // MEM_HEADS
    return (y_prompt[None], y_sample.reshape(Bs, Ts, D),
            mk, mv,
            hist_p[1:].reshape(1, Bp, POOL_HIST, -1),
            hre_p.reshape(1, Bp, G, P), him_p.reshape(1, Bp, G, P),
            jnp.transpose(hist_s, (1, 0, 2))[None], hre_s.reshape(1, Bs, G, P), him_s.reshape(1, Bs, G, P))
```

```python
import functools

import jax
import jax.numpy as jnp
from jax import lax
from jax.experimental import pallas as pl
from jax.experimental.pallas import tpu as pltpu

F32 = jnp.float32
BF16 = jnp.bfloat16

EPS = 1e-6
PAST_LEN = 1024
POOL_WINDOWS = (2, 4, 8, 16)
POOL_HIST = max(POOL_WINDOWS) - 1
HIST_ROWS = 16
MEM_HEADS = 4
N_HALF = 2
S5_TILE = 256
S5_DTYPE = F32

PROMPT_BLOCK = 512
PROMPT_CHUNK = 64
FF_CHUNK = 1024
VMEM_LIMIT = 56 * 1024 * 1024


def _rmsnorm(x, g):
    return x * lax.rsqrt(jnp.mean(x * x, axis=-1, keepdims=True) + EPS) * g


def _dot(a, b):
    return jnp.dot(a.astype(BF16), b.astype(BF16), preferred_element_type=F32)


def _inv_rms(x):
    return lax.rsqrt(jnp.mean(x * x, axis=-1, keepdims=True) + EPS)


def _norm_dot(x, g, w):
    return _dot(x * g, w) * _inv_rms(x)


def _pool_windows(zbuf, nrows, pos, r):
    gw = r["w_pool"].shape[-1]
    outs = []
    for g, w in enumerate(POOL_WINDOWS):
        lanes = slice(g * gw, (g + 1) * gw)
        win = zbuf[pl.ds(0, HIST_ROWS + nrows), lanes]
        k = 1
        while k < w:
            win = win + pltpu.roll(win, k, 0)
            k *= 2
        cur = zbuf[pl.ds(HIST_ROWS, nrows), lanes]
        cnt = jnp.minimum(pos + 1, w).astype(F32)
        pooled = win[HIST_ROWS:, :] / cnt - cur
        outs.append(_dot(pooled, r["w_pool"][g]))
    return jnp.concatenate(outs, axis=-1) * r["pool_scale"][...]


def _cmul(ar, ai, br, bi):
    return ar * br - ai * bi, ar * bi + ai * br


def _s5_scan(u, n_chunks, T, end_row, h_carry, h_rows, r, hout_re_ref, hout_im_ref):
    hw = r["bblk"].shape[1]
    nh = r["bblk"].shape[2] // 2
    u_bf = u.astype(BF16)
    r_i = lax.broadcasted_iota(jnp.int32, (T, T), 0)
    c_i = lax.broadcasted_iota(jnp.int32, (T, T), 1)
    ltri = jnp.where(r_i >= c_i, 1.0, 0.0).astype(BF16)

    ys = []
    for k in range(N_HALF):
        uk = u_bf[:, k * hw:(k + 1) * hw]
        yk = None
        for j in range(nh // S5_TILE):
            re_c = slice(j * S5_TILE, (j + 1) * S5_TILE)
            im_c = slice(nh + j * S5_TILE, nh + (j + 1) * S5_TILE)
            st = slice(k * nh + j * S5_TILE, k * nh + (j + 1) * S5_TILE)
            bu_re = _dot(uk, r["bblk"][k, :, re_c])
            bu_im = _dot(uk, r["bblk"][k, :, im_c])
            if h_rows is None:
                carry = (h_carry[0][:, st], h_carry[1][:, st])
            h_re, h_im = [], []
            for c in range(n_chunks):
                rows = slice(c * T, (c + 1) * T)
                xr, xi = _cmul(bu_re[rows].astype(S5_DTYPE), bu_im[rows].astype(S5_DTYPE),
                               r["pneg_re"][0:T, st], r["pneg_im"][0:T, st])
                wr = _dot(ltri, xr)
                wi = _dot(ltri, xi)
                if h_rows is not None:
                    carry = (h_rows[0][pl.ds(c, 1), st], h_rows[1][pl.ds(c, 1), st])
                cr, ci = _cmul(r["a1_re"][:, st], r["a1_im"][:, st], carry[0], carry[1])
                wr, wi = wr + cr, wi + ci
                sr, si = _cmul(r["ppos_re"][0:T, st], r["ppos_im"][0:T, st], wr.astype(S5_DTYPE), wi.astype(S5_DTYPE))
                carry = _cmul(r["pend_re"][end_row:end_row + 1, st], r["pend_im"][end_row:end_row + 1, st],
                              wr[T - 1:T, :], wi[T - 1:T, :])
                if h_rows is not None:
                    hout_re_ref[pl.ds(c, 1), st] = carry[0]
                    hout_im_ref[pl.ds(c, 1), st] = carry[1]
                h_re.append(sr)
                h_im.append(si)
            if h_rows is None:
                hout_re_ref[:, st] = carry[0]
                hout_im_ref[:, st] = carry[1]
            part = (_dot(jnp.concatenate(h_re, axis=0), r["cblk"][k, re_c, :])
                    + _dot(jnp.concatenate(h_im, axis=0), r["cblk"][k, im_c, :]))
            yk = part if yk is None else yk + part
        ys.append(yk)
    return jnp.concatenate(ys, axis=-1)


def _mix_tail(x, y_pool, u, y_state, r):
    y = jax.nn.gelu(y_state + r["d_skip"][...] * u)
    y = y * jax.nn.sigmoid(_dot(y, r["w_glu"][...]) + r["b_glu"][...])
    ycat = jnp.concatenate([y_pool.astype(BF16), y.astype(BF16)], axis=-1)
    return x + _dot(ycat, r["w_out"][...])


_MIXER_IN = ("xp", "xs", "hist_in", "h0_re", "h0_im",
             "g_mix", "w_in", "w_pool", "pool_scale", "bb_re", "bb_im", "cc_re", "cc_im", "a1_re", "a1_im",
             "d_skip", "w_glu", "b_glu", "w_out")
_MIXER_OUT = ("outp", "outs", "histp", "hrep", "himp", "hists", "hres", "hims")
_MIXER_BF16 = ("w_in", "w_pool", "w_glu", "w_out")
_MIXER_GEN = ("bblk", "cblk", "pneg_re", "pneg_im", "ppos_re", "ppos_im", "pend_re", "pend_im")


def _power_table(re1, im1, T):
    SUB = 16
    assert T % SUB == 0
    n = re1.shape[-1]
    t = lax.broadcasted_iota(jnp.int32, (SUB, 1), 0)
    lo_re, lo_im = jnp.ones((SUB, n), F32), jnp.zeros((SUB, n), F32)
    cr, ci = re1, im1
    k = 1
    while k < SUB:
        bit = (t & k) != 0
        lo_re, lo_im = _cmul(lo_re, lo_im, jnp.where(bit, cr, 1.0), jnp.where(bit, ci, 0.0))
        cr, ci = _cmul(cr, ci, cr, ci)
        k *= 2
    hi_re, hi_im = [jnp.ones((1, n), F32)], [jnp.zeros((1, n), F32)]
    for _ in range(T // SUB - 1):
        nr, ni = _cmul(hi_re[-1], hi_im[-1], cr, ci)
        hi_re.append(nr)
        hi_im.append(ni)
    rep = lambda rows: jnp.concatenate([jnp.broadcast_to(v, (SUB, n)) for v in rows], axis=0)
    tile = lambda v: jnp.concatenate([v] * (T // SUB), axis=0)
    return _cmul(tile(lo_re), tile(lo_im), rep(hi_re), rep(hi_im))


def _expand_s5_params(r, gen, T, t_ends):
    are, aim = r["a1_re"][...], r["a1_im"][...]
    pr, pi = _power_table(are, aim, T)
    gen["ppos_re"][...] = pr.astype(S5_DTYPE)
    gen["ppos_im"][...] = pi.astype(S5_DTYPE)
    for j, te in enumerate(t_ends):
        gen["pend_re"][j:j + 1, :] = pr[te - 1:te, :]
        gen["pend_im"][j:j + 1, :] = pi[te - 1:te, :]
    den = are * are + aim * aim
    nr, ni = _power_table(are / den, -aim / den, T)
    gen["pneg_re"][...] = nr.astype(S5_DTYPE)
    gen["pneg_im"][...] = ni.astype(S5_DTYPE)

    n_half, rows_b, H = r["bb_re"].shape
    _, rows_c, P = r["cc_re"].shape
    nh = rows_b

    def spread(x, width):
        w = x.shape[1]
        sel = lax.rem(lax.broadcasted_iota(jnp.int32, (w, width), 1), w) == lax.broadcasted_iota(jnp.int32, (w, width), 0)
        return _dot(x, jnp.where(sel, 1.0, 0.0))

    def same_group(shape, row_size, col_size):
        return (lax.div(lax.broadcasted_iota(jnp.int32, shape, 0), row_size)
                == lax.div(lax.broadcasted_iota(jnp.int32, shape, 1), col_size))

    for k in range(n_half):
        for c, (b_src, c_src, sign) in enumerate(((r["bb_re"], r["cc_re"], 1.0), (r["bb_im"], r["cc_im"], -1.0))):
            yb = spread(b_src[k], rows_c)
            yb = jnp.where(same_group(yb.shape, P, H), yb, 0.0)
            gen["bblk"][k, :, c * nh:(c + 1) * nh] = yb.T.astype(BF16)
            yc = spread(c_src[k], rows_b)
            yc = jnp.where(same_group(yc.shape, H, P), sign * yc, 0.0)
            gen["cblk"][k, c * nh:(c + 1) * nh, :] = yc.T.astype(BF16)


def _mixer_kernel(*refs, n_prompt, n_streams, Tp, Ts):
    r = dict(zip(_MIXER_IN + _MIXER_OUT, refs))
    zbuf, ypool_ref, hp_re, hp_im, hs_re, hs_im, h0f_re, h0f_im = refs[len(_MIXER_IN) + len(_MIXER_OUT):][:8]
    n_fixed = len(_MIXER_IN) + len(_MIXER_OUT) + 8
    bf16_weights = dict(zip(_MIXER_BF16, refs[n_fixed:]))
    gen = dict(zip(_MIXER_GEN, refs[n_fixed + len(_MIXER_BF16):]))
    i = pl.program_id(0)
    ts = r["xp"].shape[0]
    d_pool = zbuf.shape[1]

    @pl.when(i == 0)
    def _():
        zbuf[0:HIST_ROWS, :] = jnp.zeros((HIST_ROWS, d_pool), F32)
        hp_re[...] = jnp.zeros(hp_re.shape, F32)
        hp_im[...] = jnp.zeros(hp_im.shape, F32)
        for name, ref in bf16_weights.items():
            ref[...] = r[name][...].astype(BF16)
        _expand_s5_params(r, gen, Tp, (Tp, Ts))

    r.update(bf16_weights)
    r.update(gen)

    @pl.when(i < n_prompt)
    def _():
        x = r["xp"][...]
        z = _dot(_rmsnorm(x, r["g_mix"][...]), r["w_in"][...])
        zbuf[HIST_ROWS:HIST_ROWS + ts, :] = z[:, :d_pool]
        u = z[:, d_pool:]
        pos = i * ts + lax.broadcasted_iota(jnp.int32, (ts, 1), 0)
        y_pool = _pool_windows(zbuf, ts, pos, r)
        y_state = _s5_scan(u, ts // Tp, Tp, 0, (hp_re[...], hp_im[...]), None, r, hp_re, hp_im)
        r["outp"][...] = _mix_tail(x, y_pool, u, y_state, r)
        tail = zbuf[ts:ts + HIST_ROWS, :]
        zbuf[0:HIST_ROWS, :] = tail
        r["histp"][...] = tail

    @pl.when(i == n_prompt)
    def _():
        ext = HIST_ROWS + Ts
        x = r["xs"][...]
        z = _dot(_rmsnorm(x, r["g_mix"][...]), r["w_in"][...])
        u = z[:, d_pool:]
        for b in range(n_streams):
            zbuf[b * ext:b * ext + 1, :] = jnp.zeros((1, d_pool), F32)
            for t in range(POOL_HIST):
                zbuf[b * ext + 1 + t:b * ext + 2 + t, :] = r["hist_in"][t, b:b + 1, :]
            zbuf[b * ext + HIST_ROWS:(b + 1) * ext, :] = z[b * Ts:(b + 1) * Ts, :d_pool]
            for t in range(POOL_HIST):
                row = (b + 1) * ext - POOL_HIST + t
                r["hists"][t, b:b + 1, :] = zbuf[row:row + 1, :]
        nrows = n_streams * ext - HIST_ROWS
        ridx = lax.broadcasted_iota(jnp.int32, (nrows, 1), 0)
        pos = PAST_LEN + lax.rem(ridx, ext)
        y_all = _pool_windows(zbuf, nrows, pos, r)
        for b in range(n_streams):
            ypool_ref[b * Ts:(b + 1) * Ts, :] = y_all[b * ext:b * ext + Ts, :]
        G, P = r["hrep"].shape
        for src, dst in ((r["h0_re"], h0f_re), (r["h0_im"], h0f_im)):
            for b in range(n_streams):
                for g in range(G):
                    dst[b:b + 1, g * P:(g + 1) * P] = src[b * G + g:b * G + g + 1, :]
        y_state = _s5_scan(u, n_streams, Ts, 1, None, (h0f_re, h0f_im), r, hs_re, hs_im)
        r["outs"][...] = _mix_tail(x, ypool_ref[...], u, y_state, r)
        for g in range(G):
            cols = slice(g * P, (g + 1) * P)
            r["hrep"][g:g + 1, :] = hp_re[:, cols]
            r["himp"][g:g + 1, :] = hp_im[:, cols]
            for b in range(n_streams):
                r["hres"][b * G + g:b * G + g + 1, :] = hs_re[b:b + 1, cols]
                r["hims"][b * G + g:b * G + g + 1, :] = hs_im[b:b + 1, cols]


def _const_spec(shape):
    nd = len(shape)
    return pl.BlockSpec(shape, lambda *_: (0,) * nd)


def _weight_spec(shape):
    nd = len(shape)
    return pl.BlockSpec(shape, lambda *_: (0,) * nd, pipeline_mode=pl.Buffered(1))


def _prompt_spec(ts, d, n_prompt):
    return pl.BlockSpec((ts, d), lambda i: (jnp.minimum(i, n_prompt - 1), 0))


def _mixer(xp, xs, hist_in, h0_re, h0_im, p, n_streams, Ts):
    S, D = xp.shape
    rows_s = xs.shape[0]
    ts = PROMPT_BLOCK
    n_prompt = S // ts
    d_pool = hist_in.shape[-1]
    n_state = p["a1_re"].shape[-1]
    P = h0_re.shape[-1]
    G = n_state // P
    assert n_streams * (HIST_ROWS + Ts) <= HIST_ROWS + ts and rows_s <= ts
    args = dict(p, xp=xp, xs=xs, hist_in=hist_in, h0_re=h0_re, h0_im=h0_im)
    ins = [args[n] for n in _MIXER_IN]
    in_specs = [_prompt_spec(ts, D, n_prompt)] + [_weight_spec(a.shape) for a in ins[1:]]
    n_half, nh, H = p["bb_re"].shape
    d_half = p["cc_re"].shape[1]
    gen_shapes = ([((n_half, d_half, 2 * nh), BF16), ((n_half, 2 * nh, d_half), BF16)]
                  + [((PROMPT_CHUNK, n_state), S5_DTYPE)] * 4 + [((2, n_state), F32)] * 2)
    out_shapes = dict(
        outp=(S, D), outs=(rows_s, D), histp=(HIST_ROWS, d_pool), hrep=(G, P), himp=(G, P),
        hists=(POOL_HIST, n_streams, d_pool), hres=(n_streams * G, P), hims=(n_streams * G, P))
    out_specs = [_prompt_spec(ts, D, n_prompt)] + [_const_spec(out_shapes[n]) for n in _MIXER_OUT[1:]]
    return pl.pallas_call(
        functools.partial(_mixer_kernel, n_prompt=n_prompt, n_streams=n_streams, Tp=PROMPT_CHUNK, Ts=Ts),
        grid=(n_prompt + 1,),
        in_specs=in_specs,
        out_specs=out_specs,
        out_shape=[jax.ShapeDtypeStruct(out_shapes[n], F32) for n in _MIXER_OUT],
        scratch_shapes=[pltpu.VMEM((HIST_ROWS + ts, d_pool), F32),
                        pltpu.VMEM((rows_s, d_pool), F32)]
        + [pltpu.VMEM((1, n_state), F32)] * 2 + [pltpu.VMEM((n_streams, n_state), F32)] * 4
        + [pltpu.VMEM(args[n].shape, BF16) for n in _MIXER_BF16]
        + [pltpu.VMEM(shape, dtype) for shape, dtype in gen_shapes],
        compiler_params=pltpu.CompilerParams(dimension_semantics=("arbitrary",),
                                             vmem_limit_bytes=VMEM_LIMIT),
        name="mixer",
    )(*ins)


def _attend(q_heads, k_heads, v_heads, scale):
    outs = []
    for q, k, v in zip(q_heads, k_heads, v_heads):
        s = lax.dot_general(q, k.astype(BF16), (((1,), (1,)), ((), ())), preferred_element_type=F32) * scale
        e = jnp.exp(s - jnp.max(s, axis=-1, keepdims=True))
        outs.append(_dot(e, v) / jnp.sum(e, axis=-1, keepdims=True))
    return outs


_XATTN_IN = ("xp", "xs", "mem", "kc", "vc", "g_mem", "w_k", "w_v", "g_xattn", "w_q", "w_o")
_XATTN_OUT = ("outp", "outs", "mk", "mv")


def _xattn_kernel(*refs, n_prompt, n_streams, Ts, hpu, scale):
    r = dict(zip(_XATTN_IN + _XATTN_OUT, refs))
    kp_ref, vp_ref, qs_ref, os_ref, kbuf, vbuf, sem, kvf, osem = refs[len(_XATTN_IN) + len(_XATTN_OUT):]
    i = pl.program_id(0)
    hd = kp_ref.shape[-1] // MEM_HEADS
    ups = MEM_HEADS // hpu

    def unit_copies(unit, hg):
        b, slot = unit // ups, lax.rem(unit, 2)
        return [pltpu.make_async_copy(src.at[0, b, :, hg * hpu + e, :], dst.at[slot, e], sem.at[slot, j * hpu + e])
                for j, (src, dst) in enumerate(((r["kc"], kbuf), (r["vc"], vbuf))) for e in range(hpu)]

    def kv_out_copies():
        return [pltpu.make_async_copy(kvf.at[j, h], dst.at[0, 0, :, h, :], osem.at[j, h])
                for j, dst in enumerate((r["mk"], r["mv"])) for h in range(MEM_HEADS)]

    def for_unit(unit, fn):
        for hg in range(ups):
            @pl.when(lax.rem(unit, ups) == hg)
            def _():
                for cp in unit_copies(unit, hg):
                    fn(cp)

    @pl.when(i == 0)
    def _():
        for_unit(i, lambda cp: cp.start())
        mem = r["mem"][...]
        inv = _inv_rms(mem)
        mg = (mem * r["g_mem"][...]).astype(BF16)
        k = _dot(mg, r["w_k"][...]) * inv
        v = _dot(mg, r["w_v"][...]) * inv
        for h in range(MEM_HEADS):
            kvf[0, h] = k[:, h * hd:(h + 1) * hd]
            kvf[1, h] = v[:, h * hd:(h + 1) * hd]
        for cp in kv_out_copies():
            cp.start()
        kp_ref[...] = k.astype(BF16)
        vp_ref[...] = v.astype(BF16)
        q = _norm_dot(r["xs"][...], r["g_xattn"][...], r["w_q"][...]).astype(BF16)
        for h in range(MEM_HEADS):
            qs_ref[h] = q[:, h * hd:(h + 1) * hd]

    @pl.when(i < n_prompt)
    def _():
        @pl.when(i + 1 < n_prompt)
        def _():
            for_unit(i + 1, lambda cp: cp.start())

        for_unit(i, lambda cp: cp.wait())

        x = r["xp"][...]
        q = _norm_dot(x, r["g_xattn"][...], r["w_q"][...]).astype(BF16)
        o = _attend([q[:, h * hd:(h + 1) * hd] for h in range(MEM_HEADS)],
                    [kp_ref[:, h * hd:(h + 1) * hd] for h in range(MEM_HEADS)],
                    [vp_ref[:, h * hd:(h + 1) * hd] for h in range(MEM_HEADS)], scale)
        r["outp"][...] = x + _dot(jnp.concatenate(o, axis=-1), r["w_o"][...])

        b, hg, slot = i // ups, lax.rem(i, ups), lax.rem(i, 2)
        rows = pl.ds(pl.multiple_of(b * Ts, Ts), Ts)
        heads = [hg * hpu + e for e in range(hpu)]
        o_s = _attend([qs_ref[h, rows, :] for h in heads], [kbuf[slot, e] for e in range(hpu)],
                      [vbuf[slot, e] for e in range(hpu)], scale)
        for h, o_h in zip(heads, o_s):
            os_ref[h, rows, :] = o_h.astype(BF16)

    @pl.when(i == n_prompt)
    def _():
        for cp in kv_out_copies():
            cp.wait()
        o = jnp.concatenate([os_ref[h] for h in range(MEM_HEADS)], axis=-1)
        r["outs"][...] = r["xs"][...] + _dot(o, r["w_o"][...])


def _xattn(xp, xs, mem, kc, vc, p, n_streams, Ts):
    S, D = xp.shape
    rows_s = xs.shape[0]
    ts = PROMPT_BLOCK
    n_prompt = S // ts
    n_mem = mem.shape[0]
    hd = D // MEM_HEADS
    hpu = n_streams * MEM_HEADS // n_prompt
    assert hpu * n_prompt == n_streams * MEM_HEADS and MEM_HEADS % hpu == 0
    args = dict(p, xp=xp, xs=xs, mem=mem, kc=kc, vc=vc)
    ins = [args[n] for n in _XATTN_IN]
    cache_spec = pl.BlockSpec(memory_space=pl.ANY)
    in_specs = [_prompt_spec(ts, D, n_prompt)]
    for n in _XATTN_IN[1:]:
        in_specs.append(cache_spec if n in ("kc", "vc") else _weight_spec(args[n].shape))
    out_shapes = dict(outp=(S, D), outs=(rows_s, D), mk=(1, 1, n_mem, MEM_HEADS, hd), mv=(1, 1, n_mem, MEM_HEADS, hd))
    out_specs = [_prompt_spec(ts, D, n_prompt), _const_spec(out_shapes["outs"]), cache_spec, cache_spec]
    return pl.pallas_call(
        functools.partial(_xattn_kernel, n_prompt=n_prompt, n_streams=n_streams, Ts=Ts, hpu=hpu, scale=hd ** -0.5),
        grid=(n_prompt + 1,),
        in_specs=in_specs,
        out_specs=out_specs,
        out_shape=[jax.ShapeDtypeStruct(out_shapes[n], F32) for n in _XATTN_OUT],
        scratch_shapes=[pltpu.VMEM((n_mem, D), BF16), pltpu.VMEM((n_mem, D), BF16),
                        pltpu.VMEM((MEM_HEADS, rows_s, hd), BF16), pltpu.VMEM((MEM_HEADS, rows_s, hd), BF16),
                        pltpu.VMEM((2, hpu, n_mem, hd), F32), pltpu.VMEM((2, hpu, n_mem, hd), F32),
                        pltpu.SemaphoreType.DMA((2, 2 * hpu)),
                        pltpu.VMEM((2, MEM_HEADS, n_mem, hd), F32), pltpu.SemaphoreType.DMA((2, MEM_HEADS))],
        compiler_params=pltpu.CompilerParams(dimension_semantics=("arbitrary",),
                                             vmem_limit_bytes=VMEM_LIMIT),
        name="xattn",
    )(*ins)


def _mlp_rows(x, g_ref, wup_ref, wdown_ref, gfin_ref):
    hg = (x * g_ref[...]).astype(BF16)
    acc = None
    for j in range(wup_ref.shape[1] // FF_CHUNK):
        cols = slice(j * FF_CHUNK, (j + 1) * FF_CHUNK)
        up = _dot(hg, wup_ref[:, cols])
        part = _dot(jnp.square(jnp.maximum(up, 0.0)), wdown_ref[cols, :])
        acc = part if acc is None else acc + part
    return _rmsnorm(x + acc * jnp.square(_inv_rms(x)), gfin_ref[...])


def _mlp_kernel(xp_ref, xs_ref, g_ref, wup_ref, wdown_ref, gfin_ref, outp_ref, outs_ref, *, n_prompt):
    i = pl.program_id(0)

    @pl.when(i < n_prompt)
    def _():
        outp_ref[...] = _mlp_rows(xp_ref[...], g_ref, wup_ref, wdown_ref, gfin_ref)

    @pl.when(i == n_prompt)
    def _():
        outs_ref[...] = _mlp_rows(xs_ref[...], g_ref, wup_ref, wdown_ref, gfin_ref)


def _mlp(xp, xs, p):
    S, D = xp.shape
    ts = PROMPT_BLOCK
    n_prompt = S // ts
    weights = (p["g_mlp"], p["w_up"], p["w_down"], p["g_final"])
    return pl.pallas_call(
        functools.partial(_mlp_kernel, n_prompt=n_prompt),
        grid=(n_prompt + 1,),
        in_specs=[_prompt_spec(ts, D, n_prompt), _weight_spec(xs.shape)] + [_weight_spec(w.shape) for w in weights],
        out_specs=[_prompt_spec(ts, D, n_prompt), _const_spec(xs.shape)],
        out_shape=[jax.ShapeDtypeStruct(xp.shape, F32), jax.ShapeDtypeStruct(xs.shape, F32)],
        compiler_params=pltpu.CompilerParams(dimension_semantics=("arbitrary",),
                                             vmem_limit_bytes=VMEM_LIMIT),
        name="mlp",
    )(xp, xs, *weights)


def _prepare_layer(l, g_mix, w_in, w_pool, pool_scale, a_re, a_im, b_re, b_im, c_re, c_im, d_skip,
                   log_dt, w_glu, b_glu, w_out, g_xattn, g_mem, w_q, w_k, w_v, w_o, g_mlp, w_up,
                   w_down, g_final):
    G, P = a_re.shape[1:]
    gi = G // N_HALF
    ar, ai = a_re[l].astype(F32), a_im[l].astype(F32)
    dt = jnp.exp(log_dt[l].astype(F32))[:, None]
    lam_re, lam_im = ar * dt, ai * dt

    mag = jnp.exp(lam_re)
    ab_re, ab_im = mag * jnp.cos(lam_im), mag * jnp.sin(lam_im)

    den = ar * ar + ai * ai
    coef_re = ((ab_re - 1.0) * ar + ab_im * ai) / den
    coef_im = (ab_im * ar - (ab_re - 1.0) * ai) / den
    br, bi = b_re[l].astype(F32), b_im[l].astype(F32)
    bb_re = coef_re[..., None] * br - coef_im[..., None] * bi
    bb_im = coef_re[..., None] * bi + coef_im[..., None] * br

    half = lambda t: t.astype(F32).reshape((N_HALF, gi * t.shape[1], t.shape[2]))
    row = lambda v: v.astype(F32).reshape(1, -1)
    return dict(
        g_mix=row(g_mix[l]), w_in=w_in[l], w_pool=w_pool[l], pool_scale=row(pool_scale[l]),
        bb_re=half(bb_re), bb_im=half(bb_im), cc_re=half(c_re[l]), cc_im=half(c_im[l]),
        a1_re=row(ab_re), a1_im=row(ab_im),
        d_skip=row(d_skip[l]), w_glu=w_glu[l], b_glu=row(b_glu[l]), w_out=w_out[l],
        g_xattn=row(g_xattn[l]), g_mem=row(g_mem[l]), w_q=w_q[l], w_k=w_k[l], w_v=w_v[l], w_o=w_o[l],
        g_mlp=row(g_mlp[l]), w_up=w_up[l], w_down=w_down[l], g_final=row(g_final))


def kernel(x_prompt, x_sample, cache_mem_k, cache_mem_v, state_pool, state_ssm_re, state_ssm_im, mem_prompt, g_mix, w_in, w_pool, pool_scale, ssm_a_re, ssm_a_im, ssm_b_re, ssm_b_im, ssm_c_re, ssm_c_im, ssm_d, ssm_log_dt, w_glu, b_glu, w_out, g_xattn, g_mem, w_q, w_k, w_v, w_o, g_mlp, w_up, w_down, g_final):
    depth = g_mix.shape[0]
    assert depth == 1 and x_prompt.shape[0] == 1, "single layer, single prompt stream"
    Bp, S, D = x_prompt.shape
    Bs, Ts, _ = x_sample.shape
    n_mem = mem_prompt.shape[1]
    G, P = ssm_a_re.shape[1:]
    assert S % PROMPT_BLOCK == 0 and PROMPT_BLOCK % PROMPT_CHUNK == 0 and Ts <= PROMPT_CHUNK

    l = 0
    p = _prepare_layer(l, g_mix, w_in, w_pool, pool_scale, ssm_a_re, ssm_a_im, ssm_b_re, ssm_b_im,
                       ssm_c_re, ssm_c_im, ssm_d, ssm_log_dt, w_glu, b_glu, w_out, g_xattn, g_mem,
                       w_q, w_k, w_v, w_o, g_mlp, w_up, w_down, g_final)

    xp1, xs1, hist_p, hre_p, him_p, hist_s, hre_s, him_s = _mixer(
        x_prompt[0], x_sample.reshape(Bs * Ts, D), jnp.transpose(state_pool[l], (1, 0, 2)),
        state_ssm_re[l].reshape(Bs * G, P), state_ssm_im[l].reshape(Bs * G, P), p, Bs, Ts)
    xp2, xs2, mk, mv = _xattn(xp1, xs1, mem_prompt[0], cache_mem_k[l:l + 1], cache_mem_v[l:l + 1], p, Bs, Ts)
    y_prompt, y_sample = _mlp(xp2, xs2, p)

    hd = D // MEM_HEADS
    return (y_prompt[None], y_sample.reshape(Bs, Ts, D),
            mk, mv,
            hist_p[1:].reshape(1, Bp, POOL_HIST, -1),
            hre_p.reshape(1, Bp, G, P), him_p.reshape(1, Bp, G, P),
            jnp.transpose(hist_s, (1, 0, 2))[None], hre_s.reshape(1, Bs, G, P), him_s.reshape(1, Bs, G, P))
```

```python
import functools

import jax
import jax.numpy as jnp
from jax import lax
from jax.experimental import pallas as pl
from jax.experimental.pallas import tpu as pltpu

F32 = jnp.float32
BF16 = jnp.bfloat16

EPS = 1e-6
PAST_LEN = 1024
POOL_WINDOWS = (2, 4, 8, 16)
POOL_HIST = max(POOL_WINDOWS) - 1
HIST_ROWS = 16
MEM_HEADS = 4
N_HALF = 2
S5_TILE = 256
S5_DTYPE = F32

PROMPT_BLOCK = 512
PROMPT_CHUNK = 256
FF_CHUNK = 1024
VMEM_LIMIT = 56 * 1024 * 1024


def _rmsnorm(x, g):
    return x * lax.rsqrt(jnp.mean(x * x, axis=-1, keepdims=True) + EPS) * g


def _dot(a, b):
    return jnp.dot(a.astype(BF16), b.astype(BF16), preferred_element_type=F32)


def _inv_rms(x):
    return lax.rsqrt(jnp.mean(x * x, axis=-1, keepdims=True) + EPS)


def _norm_dot(x, g, w):
    return _dot(x * g, w) * _inv_rms(x)


def _pool_windows(zbuf, nrows, pos, r):
    gw = r["w_pool"].shape[-1]
    outs = []
    for g, w in enumerate(POOL_WINDOWS):
        lanes = slice(g * gw, (g + 1) * gw)
        win = zbuf[pl.ds(0, HIST_ROWS + nrows), lanes]
        k = 1
        while k < w:
            win = win + pltpu.roll(win, k, 0)
            k *= 2
        cur = zbuf[pl.ds(HIST_ROWS, nrows), lanes]
        cnt = jnp.minimum(pos + 1, w).astype(F32)
        pooled = win[HIST_ROWS:, :] / cnt - cur
        outs.append(_dot(pooled, r["w_pool"][g]))
    return jnp.concatenate(outs, axis=-1) * r["pool_scale"][...]


def _cmul(ar, ai, br, bi):
    return ar * br - ai * bi, ar * bi + ai * br


def _s5_scan(u, n_chunks, T, end_row, h_carry, h_rows, r, hout_re_ref, hout_im_ref):
    hw = r["bblk"].shape[1]
    nh = r["bblk"].shape[2] // 2
    u_bf = u.astype(BF16)
    r_i = lax.broadcasted_iota(jnp.int32, (T, T), 0)
    c_i = lax.broadcasted_iota(jnp.int32, (T, T), 1)
    ltri = jnp.where(r_i >= c_i, 1.0, 0.0).astype(BF16)

    ys = []
    for k in range(N_HALF):
        uk = u_bf[:, k * hw:(k + 1) * hw]
        yk = None
        for j in range(nh // S5_TILE):
            re_c = slice(j * S5_TILE, (j + 1) * S5_TILE)
            im_c = slice(nh + j * S5_TILE, nh + (j + 1) * S5_TILE)
            st = slice(k * nh + j * S5_TILE, k * nh + (j + 1) * S5_TILE)
            bu_re = _dot(uk, r["bblk"][k, :, re_c])
            bu_im = _dot(uk, r["bblk"][k, :, im_c])
            if h_rows is None:
                carry = (h_carry[0][:, st], h_carry[1][:, st])
            h_re, h_im = [], []
            for c in range(n_chunks):
                rows = slice(c * T, (c + 1) * T)
                xr, xi = _cmul(bu_re[rows].astype(S5_DTYPE), bu_im[rows].astype(S5_DTYPE),
                               r["pneg_re"][0:T, st], r["pneg_im"][0:T, st])
                wr = _dot(ltri, xr)
                wi = _dot(ltri, xi)
                if h_rows is not None:
                    carry = (h_rows[0][pl.ds(c, 1), st], h_rows[1][pl.ds(c, 1), st])
                cr, ci = _cmul(r["a1_re"][:, st], r["a1_im"][:, st], carry[0], carry[1])
                wr, wi = wr + cr, wi + ci
                sr, si = _cmul(r["ppos_re"][0:T, st], r["ppos_im"][0:T, st], wr.astype(S5_DTYPE), wi.astype(S5_DTYPE))
                carry = _cmul(r["pend_re"][end_row:end_row + 1, st], r["pend_im"][end_row:end_row + 1, st],
                              wr[T - 1:T, :], wi[T - 1:T, :])
                if h_rows is not None:
                    hout_re_ref[pl.ds(c, 1), st] = carry[0]
                    hout_im_ref[pl.ds(c, 1), st] = carry[1]
                h_re.append(sr)
                h_im.append(si)
            if h_rows is None:
                hout_re_ref[:, st] = carry[0]
                hout_im_ref[:, st] = carry[1]
            part = (_dot(jnp.concatenate(h_re, axis=0), r["cblk"][k, re_c, :])
                    + _dot(jnp.concatenate(h_im, axis=0), r["cblk"][k, im_c, :]))
            yk = part if yk is None else yk + part
        ys.append(yk)
    return jnp.concatenate(ys, axis=-1)


def _mix_tail(x, y_pool, u, y_state, r):
    y = jax.nn.gelu(y_state + r["d_skip"][...] * u)
    y = y * jax.nn.sigmoid(_dot(y, r["w_glu"][...]) + r["b_glu"][...])
    ycat = jnp.concatenate([y_pool.astype(BF16), y.astype(BF16)], axis=-1)
    return x + _dot(ycat, r["w_out"][...])


_MIXER_IN = ("xp", "xs", "hist_in", "h0_re", "h0_im",
             "g_mix", "w_in", "w_pool", "pool_scale", "bb_re", "bb_im", "cc_re", "cc_im", "a1_re", "a1_im",
             "d_skip", "w_glu", "b_glu", "w_out")
_MIXER_OUT = ("outp", "outs", "histp", "hrep", "himp", "hists", "hres", "hims")
_MIXER_BF16 = ("w_in", "w_pool", "w_glu", "w_out")
_MIXER_GEN = ("bblk", "cblk", "pneg_re", "pneg_im", "ppos_re", "ppos_im", "pend_re", "pend_im")


def _power_table(re1, im1, T):
    SUB = 16
    assert T % SUB == 0
    n = re1.shape[-1]
    t = lax.broadcasted_iota(jnp.int32, (SUB, 1), 0)
    lo_re, lo_im = jnp.ones((SUB, n), F32), jnp.zeros((SUB, n), F32)
    cr, ci = re1, im1
    k = 1
    while k < SUB:
        bit = (t & k) != 0
        lo_re, lo_im = _cmul(lo_re, lo_im, jnp.where(bit, cr, 1.0), jnp.where(bit, ci, 0.0))
        cr, ci = _cmul(cr, ci, cr, ci)
        k *= 2
    hi_re, hi_im = [jnp.ones((1, n), F32)], [jnp.zeros((1, n), F32)]
    for _ in range(T // SUB - 1):
        nr, ni = _cmul(hi_re[-1], hi_im[-1], cr, ci)
        hi_re.append(nr)
        hi_im.append(ni)
    rep = lambda rows: jnp.concatenate([jnp.broadcast_to(v, (SUB, n)) for v in rows], axis=0)
    tile = lambda v: jnp.concatenate([v] * (T // SUB), axis=0)
    return _cmul(tile(lo_re), tile(lo_im), rep(hi_re), rep(hi_im))


def _expand_s5_params(r, gen, T, t_ends):
    are, aim = r["a1_re"][...], r["a1_im"][...]
    pr, pi = _power_table(are, aim, T)
    gen["ppos_re"][...] = pr.astype(S5_DTYPE)
    gen["ppos_im"][...] = pi.astype(S5_DTYPE)
    for j, te in enumerate(t_ends):
        gen["pend_re"][j:j + 1, :] = pr[te - 1:te, :]
        gen["pend_im"][j:j + 1, :] = pi[te - 1:te, :]
    den = are * are + aim * aim
    nr, ni = _power_table(are / den, -aim / den, T)
    gen["pneg_re"][...] = nr.astype(S5_DTYPE)
    gen["pneg_im"][...] = ni.astype(S5_DTYPE)

    n_half, rows_b, H = r["bb_re"].shape
    _, rows_c, P = r["cc_re"].shape
    nh = rows_b

    def spread(x, width):
        w = x.shape[1]
        sel = lax.rem(lax.broadcasted_iota(jnp.int32, (w, width), 1), w) == lax.broadcasted_iota(jnp.int32, (w, width), 0)
        return _dot(x, jnp.where(sel, 1.0, 0.0))

    def same_group(shape, row_size, col_size):
        return (lax.div(lax.broadcasted_iota(jnp.int32, shape, 0), row_size)
                == lax.div(lax.broadcasted_iota(jnp.int32, shape, 1), col_size))

    for k in range(n_half):
        for c, (b_src, c_src, sign) in enumerate(((r["bb_re"], r["cc_re"], 1.0), (r["bb_im"], r["cc_im"], -1.0))):
            yb = spread(b_src[k], rows_c)
            yb = jnp.where(same_group(yb.shape, P, H), yb, 0.0)
            gen["bblk"][k, :, c * nh:(c + 1) * nh] = yb.T.astype(BF16)
            yc = spread(c_src[k], rows_b)
            yc = jnp.where(same_group(yc.shape, H, P), sign * yc, 0.0)
            gen["cblk"][k, c * nh:(c + 1) * nh, :] = yc.T.astype(BF16)


def _mixer_kernel(*refs, n_prompt, n_streams, Tp, Ts):
    r = dict(zip(_MIXER_IN + _MIXER_OUT, refs))
    zbuf, ypool_ref, hp_re, hp_im, hs_re, hs_im, h0f_re, h0f_im = refs[len(_MIXER_IN) + len(_MIXER_OUT):][:8]
    n_fixed = len(_MIXER_IN) + len(_MIXER_OUT) + 8
    bf16_weights = dict(zip(_MIXER_BF16, refs[n_fixed:]))
    gen = dict(zip(_MIXER_GEN, refs[n_fixed + len(_MIXER_BF16):]))
    i = pl.program_id(0)
    ts = r["xp"].shape[0]
    d_pool = zbuf.shape[1]

    @pl.when(i == 0)
    def _():
        zbuf[0:HIST_ROWS, :] = jnp.zeros((HIST_ROWS, d_pool), F32)
        hp_re[...] = jnp.zeros(hp_re.shape, F32)
        hp_im[...] = jnp.zeros(hp_im.shape, F32)
        for name, ref in bf16_weights.items():
            ref[...] = r[name][...].astype(BF16)
        _expand_s5_params(r, gen, Tp, (Tp, Ts))

    r.update(bf16_weights)
    r.update(gen)

    @pl.when(i < n_prompt)
    def _():
        x = r["xp"][...]
        z = _dot(_rmsnorm(x, r["g_mix"][...]), r["w_in"][...])
        zbuf[HIST_ROWS:HIST_ROWS + ts, :] = z[:, :d_pool]
        u = z[:, d_pool:]
        pos = i * ts + lax.broadcasted_iota(jnp.int32, (ts, 1), 0)
        y_pool = _pool_windows(zbuf, ts, pos, r)
        y_state = _s5_scan(u, ts // Tp, Tp, 0, (hp_re[...], hp_im[...]), None, r, hp_re, hp_im)
        r["outp"][...] = _mix_tail(x, y_pool, u, y_state, r)
        tail = zbuf[ts:ts + HIST_ROWS, :]
        zbuf[0:HIST_ROWS, :] = tail
        r["histp"][...] = tail

    @pl.when(i == n_prompt)
    def _():
        ext = HIST_ROWS + Ts
        x = r["xs"][...]
        z = _dot(_rmsnorm(x, r["g_mix"][...]), r["w_in"][...])
        u = z[:, d_pool:]
        for b in range(n_streams):
            zbuf[b * ext:b * ext + 1, :] = jnp.zeros((1, d_pool), F32)
            for t in range(POOL_HIST):
                zbuf[b * ext + 1 + t:b * ext + 2 + t, :] = r["hist_in"][t, b:b + 1, :]
            zbuf[b * ext + HIST_ROWS:(b + 1) * ext, :] = z[b * Ts:(b + 1) * Ts, :d_pool]
            for t in range(POOL_HIST):
                row = (b + 1) * ext - POOL_HIST + t
                r["hists"][t, b:b + 1, :] = zbuf[row:row + 1, :]
        nrows = n_streams * ext - HIST_ROWS
        ridx = lax.broadcasted_iota(jnp.int32, (nrows, 1), 0)
        pos = PAST_LEN + lax.rem(ridx, ext)
        y_all = _pool_windows(zbuf, nrows, pos, r)
        for b in range(n_streams):
            ypool_ref[b * Ts:(b + 1) * Ts, :] = y_all[b * ext:b * ext + Ts, :]
        G, P = r["hrep"].shape
        for src, dst in ((r["h0_re"], h0f_re), (r["h0_im"], h0f_im)):
            for b in range(n_streams):
                for g in range(G):
                    dst[b:b + 1, g * P:(g + 1) * P] = src[b * G + g:b * G + g + 1, :]
        y_state = _s5_scan(u, n_streams, Ts, 1, None, (h0f_re, h0f_im), r, hs_re, hs_im)
        r["outs"][...] = _mix_tail(x, ypool_ref[...], u, y_state, r)
        for g in range(G):
            cols = slice(g * P, (g + 1) * P)
            r["hrep"][g:g + 1, :] = hp_re[:, cols]
            r["himp"][g:g + 1, :] = hp_im[:, cols]
            for b in range(n_streams):
                r["hres"][b * G + g:b * G + g + 1, :] = hs_re[b:b + 1, cols]
                r["hims"][b * G + g:b * G + g + 1, :] = hs_im[b:b + 1, cols]


def _const_spec(shape):
    nd = len(shape)
    return pl.BlockSpec(shape, lambda *_: (0,) * nd)


def _weight_spec(shape):
    nd = len(shape)
    return pl.BlockSpec(shape, lambda *_: (0,) * nd, pipeline_mode=pl.Buffered(1))


def _prompt_spec(ts, d, n_prompt):
    return pl.BlockSpec((ts, d), lambda i: (jnp.minimum(i, n_prompt - 1), 0))


def _mixer(xp, xs, hist_in, h0_re, h0_im, p, n_streams, Ts):
    S, D = xp.shape
    rows_s = xs.shape[0]
    ts = PROMPT_BLOCK
    n_prompt = S // ts
    d_pool = hist_in.shape[-1]
    n_state = p["a1_re"].shape[-1]
    P = h0_re.shape[-1]
    G = n_state // P
    assert n_streams * (HIST_ROWS + Ts) <= HIST_ROWS + ts and rows_s <= ts
    args = dict(p, xp=xp, xs=xs, hist_in=hist_in, h0_re=h0_re, h0_im=h0_im)
    ins = [args[n] for n in _MIXER_IN]
    in_specs = [_prompt_spec(ts, D, n_prompt)] + [_weight_spec(a.shape) for a in ins[1:]]
    n_half, nh, H = p["bb_re"].shape
    d_half = p["cc_re"].shape[1]
    gen_shapes = ([((n_half, d_half, 2 * nh), BF16), ((n_half, 2 * nh, d_half), BF16)]
                  + [((PROMPT_CHUNK, n_state), S5_DTYPE)] * 4 + [((2, n_state), F32)] * 2)
    out_shapes = dict(
        outp=(S, D), outs=(rows_s, D), histp=(HIST_ROWS, d_pool), hrep=(G, P), himp=(G, P),
        hists=(POOL_HIST, n_streams, d_pool), hres=(n_streams * G, P), hims=(n_streams * G, P))
    out_specs = [_prompt_spec(ts, D, n_prompt)] + [_const_spec(out_shapes[n]) for n in _MIXER_OUT[1:]]
    return pl.pallas_call(
        functools.partial(_mixer_kernel, n_prompt=n_prompt, n_streams=n_streams, Tp=PROMPT_CHUNK, Ts=Ts),
        grid=(n_prompt + 1,),
        in_specs=in_specs,
        out_specs=out_specs,
        out_shape=[jax.ShapeDtypeStruct(out_shapes[n], F32) for n in _MIXER_OUT],
        scratch_shapes=[pltpu.VMEM((HIST_ROWS + ts, d_pool), F32),
                        pltpu.VMEM((rows_s, d_pool), F32)]
        + [pltpu.VMEM((1, n_state), F32)] * 2 + [pltpu.VMEM((n_streams, n_state), F32)] * 4
        + [pltpu.VMEM(args[n].shape, BF16) for n in _MIXER_BF16]
        + [pltpu.VMEM(shape, dtype) for shape, dtype in gen_shapes],
        compiler_params=pltpu.CompilerParams(dimension_semantics=("arbitrary",),
                                             vmem_limit_bytes=VMEM_LIMIT),
        name="mixer",
    )(*ins)


def _attend(q_heads, k_heads, v_heads, scale):
    outs = []
    for q, k, v in zip(q_heads, k_heads, v_heads):
        s = lax.dot_general(q, k.astype(BF16), (((1,), (1,)), ((), ())), preferred_element_type=F32) * scale
        e = jnp.exp(s - jnp.max(s, axis=-1, keepdims=True))
        outs.append(_dot(e, v) / jnp.sum(e, axis=-1, keepdims=True))
    return outs


_XATTN_IN = ("xp", "xs", "mem", "kc", "vc", "g_mem", "w_k", "w_v", "g_xattn", "w_q", "w_o")
_XATTN_OUT = ("outp", "outs", "mk", "mv")


def _xattn_kernel(*refs, n_prompt, n_streams, Ts, hpu, scale):
    r = dict(zip(_XATTN_IN + _XATTN_OUT, refs))
    kp_ref, vp_ref, qs_ref, os_ref, kbuf, vbuf, sem, kvf, osem = refs[len(_XATTN_IN) + len(_XATTN_OUT):]
    i = pl.program_id(0)
    hd = kp_ref.shape[-1] // MEM_HEADS
    ups = MEM_HEADS // hpu

    def unit_copies(unit, hg):
        b, slot = unit // ups, lax.rem(unit, 2)
        return [pltpu.make_async_copy(src.at[0, b, :, hg * hpu + e, :], dst.at[slot, e], sem.at[slot, j * hpu + e])
                for j, (src, dst) in enumerate(((r["kc"], kbuf), (r["vc"], vbuf))) for e in range(hpu)]

    def kv_out_copies():
        return [pltpu.make_async_copy(kvf.at[j, h], dst.at[0, 0, :, h, :], osem.at[j, h])
                for j, dst in enumerate((r["mk"], r["mv"])) for h in range(MEM_HEADS)]

    def for_unit(unit, fn):
        for hg in range(ups):
            @pl.when(lax.rem(unit, ups) == hg)
            def _():
                for cp in unit_copies(unit, hg):
                    fn(cp)

    @pl.when(i == 0)
    def _():
        for_unit(i, lambda cp: cp.start())
        mem = r["mem"][...]
        inv = _inv_rms(mem)
        mg = (mem * r["g_mem"][...]).astype(BF16)
        k = _dot(mg, r["w_k"][...]) * inv
        v = _dot(mg, r["w_v"][...]) * inv
        for h in range(MEM_HEADS):
            kvf[0, h] = k[:, h * hd:(h + 1) * hd]
            kvf[1, h] = v[:, h * hd:(h + 1) * hd]
        for cp in kv_out_copies():
            cp.start()
        kp_ref[...] = k.astype(BF16)
        vp_ref[...] = v.astype(BF16)
        q = _norm_dot(r["xs"][...], r["g_xattn"][...], r["w_q"][...]).astype(BF16)
        for h in range(MEM_HEADS):
            qs_ref[h] = q[:, h * hd:(h + 1) * hd]

    @pl.when(i < n_prompt)
    def _():
        @pl.when(i + 1 < n_prompt)
        def _():
            for_unit(i + 1, lambda cp: cp.start())

        for_unit(i, lambda cp: cp.wait())

        x = r["xp"][...]
        q = _norm_dot(x, r["g_xattn"][...], r["w_q"][...]).astype(BF16)
        o = _attend([q[:, h * hd:(h + 1) * hd] for h in range(MEM_HEADS)],
                    [kp_ref[:, h * hd:(h + 1) * hd] for h in range(MEM_HEADS)],
                    [vp_ref[:, h * hd:(h + 1) * hd] for h in range(MEM_HEADS)], scale)
        r["outp"][...] = x + _dot(jnp.concatenate(o, axis=-1), r["w_o"][...])

        b, hg, slot = i // ups, lax.rem(i, ups), lax.rem(i, 2)
        rows = pl.ds(pl.multiple_of(b * Ts, Ts), Ts)
        heads = [hg * hpu + e for e in range(hpu)]
        o_s = _attend([qs_ref[h, rows, :] for h in heads], [kbuf[slot, e] for e in range(hpu)],
                      [vbuf[slot, e] for e in range(hpu)], scale)
        for h, o_h in zip(heads, o_s):
            os_ref[h, rows, :] = o_h.astype(BF16)

    @pl.when(i == n_prompt)
    def _():
        for cp in kv_out_copies():
            cp.wait()
        o = jnp.concatenate([os_ref[h] for h in range(MEM_HEADS)], axis=-1)
        r["outs"][...] = r["xs"][...] + _dot(o, r["w_o"][...])


def _xattn(xp, xs, mem, kc, vc, p, n_streams, Ts):
    S, D = xp.shape
    rows_s = xs.shape[0]
    ts = PROMPT_BLOCK
    n_prompt = S // ts
    n_mem = mem.shape[0]
    hd = D // MEM_HEADS
    hpu = n_streams * MEM_HEADS // n_prompt
    assert hpu * n_prompt == n_streams * MEM_HEADS and MEM_HEADS % hpu == 0
    args = dict(p, xp=xp, xs=xs, mem=mem, kc=kc, vc=vc)
    ins = [args[n] for n in _XATTN_IN]
    cache_spec = pl.BlockSpec(memory_space=pl.ANY)
    in_specs = [_prompt_spec(ts, D, n_prompt)]
    for n in _XATTN_IN[1:]:
        in_specs.append(cache_spec if n in ("kc", "vc") else _weight_spec(args[n].shape))
    out_shapes = dict(outp=(S, D), outs=(rows_s, D), mk=(1, 1, n_mem, MEM_HEADS, hd), mv=(1, 1, n_mem, MEM_HEADS, hd))
    out_specs = [_prompt_spec(ts, D, n_prompt), _const_spec(out_shapes["outs"]), cache_spec, cache_spec]
    return pl.pallas_call(
        functools.partial(_xattn_kernel, n_prompt=n_prompt, n_streams=n_streams, Ts=Ts, hpu=hpu, scale=hd ** -0.5),
        grid=(n_prompt + 1,),
        in_specs=in_specs,
        out_specs=out_specs,
        out_shape=[jax.ShapeDtypeStruct(out_shapes[n], F32) for n in _XATTN_OUT],
        scratch_shapes=[pltpu.VMEM((n_mem, D), BF16), pltpu.VMEM((n_mem, D), BF16),
                        pltpu.VMEM((MEM_HEADS, rows_s, hd), BF16), pltpu.VMEM((MEM_HEADS, rows_s, hd), BF16),
                        pltpu.VMEM((2, hpu, n_mem, hd), F32), pltpu.VMEM((2, hpu, n_mem, hd), F32),
                        pltpu.SemaphoreType.DMA((2, 2 * hpu)),
                        pltpu.VMEM((2, MEM_HEADS, n_mem, hd), F32), pltpu.SemaphoreType.DMA((2, MEM_HEADS))],
        compiler_params=pltpu.CompilerParams(dimension_semantics=("arbitrary",),
                                             vmem_limit_bytes=VMEM_LIMIT),
        name="xattn",
    )(*ins)


def _mlp_rows(x, g_ref, wup_ref, wdown_ref, gfin_ref):
    hg = (x * g_ref[...]).astype(BF16)
    acc = None
    for j in range(wup_ref.shape[1] // FF_CHUNK):
        cols = slice(j * FF_CHUNK, (j + 1) * FF_CHUNK)
        up = _dot(hg, wup_ref[:, cols])
        part = _dot(jnp.square(jnp.maximum(up, 0.0)), wdown_ref[cols, :])
        acc = part if acc is None else acc + part
    return _rmsnorm(x + acc * jnp.square(_inv_rms(x)), gfin_ref[...])


def _mlp_kernel(xp_ref, xs_ref, g_ref, wup_ref, wdown_ref, gfin_ref, outp_ref, outs_ref, *, n_prompt):
    i = pl.program_id(0)

    @pl.when(i < n_prompt)
    def _():
        outp_ref[...] = _mlp_rows(xp_ref[...], g_ref, wup_ref, wdown_ref, gfin_ref)

    @pl.when(i == n_prompt)
    def _():
        outs_ref[...] = _mlp_rows(xs_ref[...], g_ref, wup_ref, wdown_ref, gfin_ref)


def _mlp(xp, xs, p):
    S, D = xp.shape
    ts = PROMPT_BLOCK
    n_prompt = S // ts
    weights = (p["g_mlp"], p["w_up"], p["w_down"], p["g_final"])
    return pl.pallas_call(
        functools.partial(_mlp_kernel, n_prompt=n_prompt),
        grid=(n_prompt + 1,),
        in_specs=[_prompt_spec(ts, D, n_prompt), _weight_spec(xs.shape)] + [_weight_spec(w.shape) for w in weights],
        out_specs=[_prompt_spec(ts, D, n_prompt), _const_spec(xs.shape)],
        out_shape=[jax.ShapeDtypeStruct(xp.shape, F32), jax.ShapeDtypeStruct(xs.shape, F32)],
        compiler_params=pltpu.CompilerParams(dimension_semantics=("arbitrary",),
                                             vmem_limit_bytes=VMEM_LIMIT),
        name="mlp",
    )(xp, xs, *weights)


def _prepare_layer(l, g_mix, w_in, w_pool, pool_scale, a_re, a_im, b_re, b_im, c_re, c_im, d_skip,
                   log_dt, w_glu, b_glu, w_out, g_xattn, g_mem, w_q, w_k, w_v, w_o, g_mlp, w_up,
                   w_down, g_final):
    G, P = a_re.shape[1:]
    gi = G // N_HALF
    ar, ai = a_re[l].astype(F32), a_im[l].astype(F32)
    dt = jnp.exp(log_dt[l].astype(F32))[:, None]
    lam_re, lam_im = ar * dt, ai * dt

    mag = jnp.exp(lam_re)
    ab_re, ab_im = mag * jnp.cos(lam_im), mag * jnp.sin(lam_im)

    den = ar * ar + ai * ai
    coef_re = ((ab_re - 1.0) * ar + ab_im * ai) / den
    coef_im = (ab_im * ar - (ab_re - 1.0) * ai) / den
    br, bi = b_re[l].astype(F32), b_im[l].astype(F32)
    bb_re = coef_re[..., None] * br - coef_im[..., None] * bi
    bb_im = coef_re[..., None] * bi + coef_im[..., None] * br

    half = lambda t: t.astype(F32).reshape((N_HALF, gi * t.shape[1], t.shape[2]))
    row = lambda v: v.astype(F32).reshape(1, -1)
    return dict(
        g_mix=row(g_mix[l]), w_in=w_in[l], w_pool=w_pool[l], pool_scale=row(pool_scale[l]),
        bb_re=half(bb_re), bb_im=half(bb_im), cc_re=half(c_re[l]), cc_im=half(c_im[l]),
        a1_re=row(ab_re), a1_im=row(ab_im),
        d_skip=row(d_skip[l]), w_glu=w_glu[l], b_glu=row(b_glu[l]), w_out=w_out[l],
        g_xattn=row(g_xattn[l]), g_mem=row(g_mem[l]), w_q=w_q[l], w_k=w_k[l], w_v=w_v[l], w_o=w_o[l],
        g_mlp=row(g_mlp[l]), w_up=w_up[l], w_down=w_down[l], g_final=row(g_final))


def kernel(x_prompt, x_sample, cache_mem_k, cache_mem_v, state_pool, state_ssm_re, state_ssm_im, mem_prompt, g_mix, w_in, w_pool, pool_scale, ssm_a_re, ssm_a_im, ssm_b_re, ssm_b_im, ssm_c_re, ssm_c_im, ssm_d, ssm_log_dt, w_glu, b_glu, w_out, g_xattn, g_mem, w_q, w_k, w_v, w_o, g_mlp, w_up, w_down, g_final):
    depth = g_mix.shape[0]
    assert depth == 1 and x_prompt.shape[0] == 1, "single layer, single prompt stream"
    Bp, S, D = x_prompt.shape
    Bs, Ts, _ = x_sample.shape
    n_mem = mem_prompt.shape[1]
    G, P = ssm_a_re.shape[1:]
    assert S % PROMPT_BLOCK == 0 and PROMPT_BLOCK % PROMPT_CHUNK == 0 and Ts <= PROMPT_CHUNK

    l = 0
    p = _prepare_layer(l, g_mix, w_in, w_pool, pool_scale, ssm_a_re, ssm_a_im, ssm_b_re, ssm_b_im,
                       ssm_c_re, ssm_c_im, ssm_d, ssm_log_dt, w_glu, b_glu, w_out, g_xattn, g_mem,
                       w_q, w_k, w_v, w_o, g_mlp, w_up, w_down, g_final)

    xp1, xs1, hist_p, hre_p, him_p, hist_s, hre_s, him_s = _mixer(
        x_prompt[0], x_sample.reshape(Bs * Ts, D), jnp.transpose(state_pool[l], (1, 0, 2)),
        state_ssm_re[l].reshape(Bs * G, P), state_ssm_im[l].reshape(Bs * G, P), p, Bs, Ts)
    xp2, xs2, mk, mv = _xattn(xp1, xs1, mem_prompt[0], cache_mem_k[l:l + 1], cache_mem_v[l:l + 1], p, Bs, Ts)
    y_prompt, y_sample = _mlp(xp2, xs2, p)

    hd = D // MEM_HEADS
    return (y_prompt[None], y_sample.reshape(Bs, Ts, D),
            mk, mv,
            hist_p[1:].reshape(1, Bp, POOL_HIST, -1),
            hre_p.reshape(1, Bp, G, P), him_p.reshape(1, Bp, G, P),
            jnp.transpose(hist_s, (1, 0, 2))[None], hre_s.reshape(1, Bs, G, P), him_s.reshape(1, Bs, G, P))
```

```python
import functools

import jax
import jax.numpy as jnp
from jax import lax
from jax.experimental import pallas as pl
from jax.experimental.pallas import tpu as pltpu

F32 = jnp.float32
BF16 = jnp.bfloat16

EPS = 1e-6
PAST_LEN = 1024
POOL_WINDOWS = (2, 4, 8, 16)
POOL_HIST = max(POOL_WINDOWS) - 1
HIST_ROWS = 16
MEM_HEADS = 4
N_HALF = 2
S5_TILE = 256
S5_DTYPE = F32

PROMPT_BLOCK = 512
PROMPT_CHUNK = 128
FF_CHUNK = 4096
VMEM_LIMIT = 56 * 1024 * 1024


def _rmsnorm(x, g):
    return x * lax.rsqrt(jnp.mean(x * x, axis=-1, keepdims=True) + EPS) * g


def _dot(a, b):
    return jnp.dot(a.astype(BF16), b.astype(BF16), preferred_element_type=F32)


def _inv_rms(x):
    return lax.rsqrt(jnp.mean(x * x, axis=-1, keepdims=True) + EPS)


def _norm_dot(x, g, w):
    return _dot(x * g, w) * _inv_rms(x)


def _pool_windows(zbuf, nrows, pos, r):
    gw = r["w_pool"].shape[-1]
    outs = []
    for g, w in enumerate(POOL_WINDOWS):
        lanes = slice(g * gw, (g + 1) * gw)
        win = zbuf[pl.ds(0, HIST_ROWS + nrows), lanes]
        k = 1
        while k < w:
            win = win + pltpu.roll(win, k, 0)
            k *= 2
        cur = zbuf[pl.ds(HIST_ROWS, nrows), lanes]
        cnt = jnp.minimum(pos + 1, w).astype(F32)
        pooled = win[HIST_ROWS:, :] / cnt - cur
        outs.append(_dot(pooled, r["w_pool"][g]))
    return jnp.concatenate(outs, axis=-1) * r["pool_scale"][...]


def _cmul(ar, ai, br, bi):
    return ar * br - ai * bi, ar * bi + ai * br


def _s5_scan(u, n_chunks, T, end_row, h_carry, h_rows, r, hout_re_ref, hout_im_ref):
    hw = r["bblk"].shape[1]
    nh = r["bblk"].shape[2] // 2
    u_bf = u.astype(BF16)
    r_i = lax.broadcasted_iota(jnp.int32, (T, T), 0)
    c_i = lax.broadcasted_iota(jnp.int32, (T, T), 1)
    ltri = jnp.where(r_i >= c_i, 1.0, 0.0).astype(BF16)

    ys = []
    for k in range(N_HALF):
        uk = u_bf[:, k * hw:(k + 1) * hw]
        yk = None
        for j in range(nh // S5_TILE):
            re_c = slice(j * S5_TILE, (j + 1) * S5_TILE)
            im_c = slice(nh + j * S5_TILE, nh + (j + 1) * S5_TILE)
            st = slice(k * nh + j * S5_TILE, k * nh + (j + 1) * S5_TILE)
            bu_re = _dot(uk, r["bblk"][k, :, re_c])
            bu_im = _dot(uk, r["bblk"][k, :, im_c])
            if h_rows is None:
                carry = (h_carry[0][:, st], h_carry[1][:, st])
            h_re, h_im = [], []
            for c in range(n_chunks):
                rows = slice(c * T, (c + 1) * T)
                xr, xi = _cmul(bu_re[rows].astype(S5_DTYPE), bu_im[rows].astype(S5_DTYPE),
                               r["pneg_re"][0:T, st], r["pneg_im"][0:T, st])
                wr = _dot(ltri, xr)
                wi = _dot(ltri, xi)
                if h_rows is not None:
                    carry = (h_rows[0][pl.ds(c, 1), st], h_rows[1][pl.ds(c, 1), st])
                cr, ci = _cmul(r["a1_re"][:, st], r["a1_im"][:, st], carry[0], carry[1])
                wr, wi = wr + cr, wi + ci
                sr, si = _cmul(r["ppos_re"][0:T, st], r["ppos_im"][0:T, st], wr.astype(S5_DTYPE), wi.astype(S5_DTYPE))
                carry = _cmul(r["pend_re"][end_row:end_row + 1, st], r["pend_im"][end_row:end_row + 1, st],
                              wr[T - 1:T, :], wi[T - 1:T, :])
                if h_rows is not None:
                    hout_re_ref[pl.ds(c, 1), st] = carry[0]
                    hout_im_ref[pl.ds(c, 1), st] = carry[1]
                h_re.append(sr)
                h_im.append(si)
            if h_rows is None:
                hout_re_ref[:, st] = carry[0]
                hout_im_ref[:, st] = carry[1]
            part = (_dot(jnp.concatenate(h_re, axis=0), r["cblk"][k, re_c, :])
                    + _dot(jnp.concatenate(h_im, axis=0), r["cblk"][k, im_c, :]))
            yk = part if yk is None else yk + part
        ys.append(yk)
    return jnp.concatenate(ys, axis=-1)


def _mix_tail(x, y_pool, u, y_state, r):
    y = jax.nn.gelu(y_state + r["d_skip"][...] * u)
    y = y * jax.nn.sigmoid(_dot(y, r["w_glu"][...]) + r["b_glu"][...])
    ycat = jnp.concatenate([y_pool.astype(BF16), y.astype(BF16)], axis=-1)
    return x + _dot(ycat, r["w_out"][...])


_MIXER_IN = ("xp", "xs", "hist_in", "h0_re", "h0_im",
             "g_mix", "w_in", "w_pool", "pool_scale", "bb_re", "bb_im", "cc_re", "cc_im", "a1_re", "a1_im",
             "d_skip", "w_glu", "b_glu", "w_out")
_MIXER_OUT = ("outp", "outs", "histp", "hrep", "himp", "hists", "hres", "hims")
_MIXER_BF16 = ("w_in", "w_pool", "w_glu", "w_out")
_MIXER_GEN = ("bblk", "cblk", "pneg_re", "pneg_im", "ppos_re", "ppos_im", "pend_re", "pend_im")


def _power_table(re1, im1, T):
    SUB = 16
    assert T % SUB == 0
    n = re1.shape[-1]
    t = lax.broadcasted_iota(jnp.int32, (SUB, 1), 0)
    lo_re, lo_im = jnp.ones((SUB, n), F32), jnp.zeros((SUB, n), F32)
    cr, ci = re1, im1
    k = 1
    while k < SUB:
        bit = (t & k) != 0
        lo_re, lo_im = _cmul(lo_re, lo_im, jnp.where(bit, cr, 1.0), jnp.where(bit, ci, 0.0))
        cr, ci = _cmul(cr, ci, cr, ci)
        k *= 2
    hi_re, hi_im = [jnp.ones((1, n), F32)], [jnp.zeros((1, n), F32)]
    for _ in range(T // SUB - 1):
        nr, ni = _cmul(hi_re[-1], hi_im[-1], cr, ci)
        hi_re.append(nr)
        hi_im.append(ni)
    rep = lambda rows: jnp.concatenate([jnp.broadcast_to(v, (SUB, n)) for v in rows], axis=0)
    tile = lambda v: jnp.concatenate([v] * (T // SUB), axis=0)
    return _cmul(tile(lo_re), tile(lo_im), rep(hi_re), rep(hi_im))


def _expand_s5_params(r, gen, T, t_ends):
    are, aim = r["a1_re"][...], r["a1_im"][...]
    pr, pi = _power_table(are, aim, T)
    gen["ppos_re"][...] = pr.astype(S5_DTYPE)
    gen["ppos_im"][...] = pi.astype(S5_DTYPE)
    for j, te in enumerate(t_ends):
        gen["pend_re"][j:j + 1, :] = pr[te - 1:te, :]
        gen["pend_im"][j:j + 1, :] = pi[te - 1:te, :]
    den = are * are + aim * aim
    nr, ni = _power_table(are / den, -aim / den, T)
    gen["pneg_re"][...] = nr.astype(S5_DTYPE)
    gen["pneg_im"][...] = ni.astype(S5_DTYPE)

    n_half, rows_b, H = r["bb_re"].shape
    _, rows_c, P = r["cc_re"].shape
    nh = rows_b

    def spread(x, width):
        w = x.shape[1]
        sel = lax.rem(lax.broadcasted_iota(jnp.int32, (w, width), 1), w) == lax.broadcasted_iota(jnp.int32, (w, width), 0)
        return _dot(x, jnp.where(sel, 1.0, 0.0))

    def same_group(shape, row_size, col_size):
        return (lax.div(lax.broadcasted_iota(jnp.int32, shape, 0), row_size)
                == lax.div(lax.broadcasted_iota(jnp.int32, shape, 1), col_size))

    for k in range(n_half):
        for c, (b_src, c_src, sign) in enumerate(((r["bb_re"], r["cc_re"], 1.0), (r["bb_im"], r["cc_im"], -1.0))):
            yb = spread(b_src[k], rows_c)
            yb = jnp.where(same_group(yb.shape, P, H), yb, 0.0)
            gen["bblk"][k, :, c * nh:(c + 1) * nh] = yb.T.astype(BF16)
            yc = spread(c_src[k], rows_b)
            yc = jnp.where(same_group(yc.shape, H, P), sign * yc, 0.0)
            gen["cblk"][k, c * nh:(c + 1) * nh, :] = yc.T.astype(BF16)


def _mixer_kernel(*refs, n_prompt, n_streams, Tp, Ts):
    r = dict(zip(_MIXER_IN + _MIXER_OUT, refs))
    zbuf, ypool_ref, hp_re, hp_im, hs_re, hs_im, h0f_re, h0f_im = refs[len(_MIXER_IN) + len(_MIXER_OUT):][:8]
    n_fixed = len(_MIXER_IN) + len(_MIXER_OUT) + 8
    bf16_weights = dict(zip(_MIXER_BF16, refs[n_fixed:]))
    gen = dict(zip(_MIXER_GEN, refs[n_fixed + len(_MIXER_BF16):]))
    i = pl.program_id(0)
    ts = r["xp"].shape[0]
    d_pool = zbuf.shape[1]

    @pl.when(i == 0)
    def _():
        zbuf[0:HIST_ROWS, :] = jnp.zeros((HIST_ROWS, d_pool), F32)
        hp_re[...] = jnp.zeros(hp_re.shape, F32)
        hp_im[...] = jnp.zeros(hp_im.shape, F32)
        for name, ref in bf16_weights.items():
            ref[...] = r[name][...].astype(BF16)
        _expand_s5_params(r, gen, Tp, (Tp, Ts))

    r.update(bf16_weights)
    r.update(gen)

    @pl.when(i < n_prompt)
    def _():
        x = r["xp"][...]
        z = _dot(_rmsnorm(x, r["g_mix"][...]), r["w_in"][...])
        zbuf[HIST_ROWS:HIST_ROWS + ts, :] = z[:, :d_pool]
        u = z[:, d_pool:]
        pos = i * ts + lax.broadcasted_iota(jnp.int32, (ts, 1), 0)
        y_pool = _pool_windows(zbuf, ts, pos, r)
        y_state = _s5_scan(u, ts // Tp, Tp, 0, (hp_re[...], hp_im[...]), None, r, hp_re, hp_im)
        r["outp"][...] = _mix_tail(x, y_pool, u, y_state, r)
        tail = zbuf[ts:ts + HIST_ROWS, :]
        zbuf[0:HIST_ROWS, :] = tail
        r["histp"][...] = tail

    @pl.when(i == n_prompt)
    def _():
        ext = HIST_ROWS + Ts
        x = r["xs"][...]
        z = _dot(_rmsnorm(x, r["g_mix"][...]), r["w_in"][...])
        u = z[:, d_pool:]
        for b in range(n_streams):
            zbuf[b * ext:b * ext + 1, :] = jnp.zeros((1, d_pool), F32)
            for t in range(POOL_HIST):
                zbuf[b * ext + 1 + t:b * ext + 2 + t, :] = r["hist_in"][t, b:b + 1, :]
            zbuf[b * ext + HIST_ROWS:(b + 1) * ext, :] = z[b * Ts:(b + 1) * Ts, :d_pool]
            for t in range(POOL_HIST):
                row = (b + 1) * ext - POOL_HIST + t
                r["hists"][t, b:b + 1, :] = zbuf[row:row + 1, :]
        nrows = n_streams * ext - HIST_ROWS
        ridx = lax.broadcasted_iota(jnp.int32, (nrows, 1), 0)
        pos = PAST_LEN + lax.rem(ridx, ext)
        y_all = _pool_windows(zbuf, nrows, pos, r)
        for b in range(n_streams):
            ypool_ref[b * Ts:(b + 1) * Ts, :] = y_all[b * ext:b * ext + Ts, :]
        G, P = r["hrep"].shape
        for src, dst in ((r["h0_re"], h0f_re), (r["h0_im"], h0f_im)):
            for b in range(n_streams):
                for g in range(G):
                    dst[b:b + 1, g * P:(g + 1) * P] = src[b * G + g:b * G + g + 1, :]
        y_state = _s5_scan(u, n_streams, Ts, 1, None, (h0f_re, h0f_im), r, hs_re, hs_im)
        r["outs"][...] = _mix_tail(x, ypool_ref[...], u, y_state, r)
        for g in range(G):
            cols = slice(g * P, (g + 1) * P)
            r["hrep"][g:g + 1, :] = hp_re[:, cols]
            r["himp"][g:g + 1, :] = hp_im[:, cols]
            for b in range(n_streams):
                r["hres"][b * G + g:b * G + g + 1, :] = hs_re[b:b + 1, cols]
                r["hims"][b * G + g:b * G + g + 1, :] = hs_im[b:b + 1, cols]


def _const_spec(shape):
    nd = len(shape)
    return pl.BlockSpec(shape, lambda *_: (0,) * nd)


def _weight_spec(shape):
    nd = len(shape)
    return pl.BlockSpec(shape, lambda *_: (0,) * nd, pipeline_mode=pl.Buffered(1))


def _prompt_spec(ts, d, n_prompt):
    return pl.BlockSpec((ts, d), lambda i: (jnp.minimum(i, n_prompt - 1), 0))


def _mixer(xp, xs, hist_in, h0_re, h0_im, p, n_streams, Ts):
    S, D = xp.shape
    rows_s = xs.shape[0]
    ts = PROMPT_BLOCK
    n_prompt = S // ts
    d_pool = hist_in.shape[-1]
    n_state = p["a1_re"].shape[-1]
    P = h0_re.shape[-1]
    G = n_state // P
    assert n_streams * (HIST_ROWS + Ts) <= HIST_ROWS + ts and rows_s <= ts
    args = dict(p, xp=xp, xs=xs, hist_in=hist_in, h0_re=h0_re, h0_im=h0_im)
    ins = [args[n] for n in _MIXER_IN]
    in_specs = [_prompt_spec(ts, D, n_prompt)] + [_weight_spec(a.shape) for a in ins[1:]]
    n_half, nh, H = p["bb_re"].shape
    d_half = p["cc_re"].shape[1]
    gen_shapes = ([((n_half, d_half, 2 * nh), BF16), ((n_half, 2 * nh, d_half), BF16)]
                  + [((PROMPT_CHUNK, n_state), S5_DTYPE)] * 4 + [((2, n_state), F32)] * 2)
    out_shapes = dict(
        outp=(S, D), outs=(rows_s, D), histp=(HIST_ROWS, d_pool), hrep=(G, P), himp=(G, P),
        hists=(POOL_HIST, n_streams, d_pool), hres=(n_streams * G, P), hims=(n_streams * G, P))
    out_specs = [_prompt_spec(ts, D, n_prompt)] + [_const_spec(out_shapes[n]) for n in _MIXER_OUT[1:]]
    return pl.pallas_call(
        functools.partial(_mixer_kernel, n_prompt=n_prompt, n_streams=n_streams, Tp=PROMPT_CHUNK, Ts=Ts),
        grid=(n_prompt + 1,),
        in_specs=in_specs,
        out_specs=out_specs,
        out_shape=[jax.ShapeDtypeStruct(out_shapes[n], F32) for n in _MIXER_OUT],
        scratch_shapes=[pltpu.VMEM((HIST_ROWS + ts, d_pool), F32),
                        pltpu.VMEM((rows_s, d_pool), F32)]
        + [pltpu.VMEM((1, n_state), F32)] * 2 + [pltpu.VMEM((n_streams, n_state), F32)] * 4
        + [pltpu.VMEM(args[n].shape, BF16) for n in _MIXER_BF16]
        + [pltpu.VMEM(shape, dtype) for shape, dtype in gen_shapes],
        compiler_params=pltpu.CompilerParams(dimension_semantics=("arbitrary",),
                                             vmem_limit_bytes=VMEM_LIMIT),
        name="mixer",
    )(*ins)


def _attend(q_heads, k_heads, v_heads, scale):
    outs = []
    for q, k, v in zip(q_heads, k_heads, v_heads):
        s = lax.dot_general(q, k.astype(BF16), (((1,), (1,)), ((), ())), preferred_element_type=F32) * scale
        e = jnp.exp(s - jnp.max(s, axis=-1, keepdims=True))
        outs.append(_dot(e, v) / jnp.sum(e, axis=-1, keepdims=True))
    return outs


_XATTN_IN = ("xp", "xs", "mem", "kc", "vc", "g_mem", "w_k", "w_v", "g_xattn", "w_q", "w_o")
_XATTN_OUT = ("outp", "outs", "mk", "mv")


def _xattn_kernel(*refs, n_prompt, n_streams, Ts, hpu, scale):
    r = dict(zip(_XATTN_IN + _XATTN_OUT, refs))
    kp_ref, vp_ref, qs_ref, os_ref, kbuf, vbuf, sem, kvf, osem = refs[len(_XATTN_IN) + len(_XATTN_OUT):]
    i = pl.program_id(0)
    hd = kp_ref.shape[-1] // MEM_HEADS
    ups = MEM_HEADS // hpu

    def unit_copies(unit, hg):
        b, slot = unit // ups, lax.rem(unit, 2)
        return [pltpu.make_async_copy(src.at[0, b, :, hg * hpu + e, :], dst.at[slot, e], sem.at[slot, j * hpu + e])
                for j, (src, dst) in enumerate(((r["kc"], kbuf), (r["vc"], vbuf))) for e in range(hpu)]

    def kv_out_copies():
        return [pltpu.make_async_copy(kvf.at[j, h], dst.at[0, 0, :, h, :], osem.at[j, h])
                for j, dst in enumerate((r["mk"], r["mv"])) for h in range(MEM_HEADS)]

    def for_unit(unit, fn):
        for hg in range(ups):
            @pl.when(lax.rem(unit, ups) == hg)
            def _():
                for cp in unit_copies(unit, hg):
                    fn(cp)

    @pl.when(i == 0)
    def _():
        for_unit(i, lambda cp: cp.start())
        mem = r["mem"][...]
        inv = _inv_rms(mem)
        mg = (mem * r["g_mem"][...]).astype(BF16)
        k = _dot(mg, r["w_k"][...]) * inv
        v = _dot(mg, r["w_v"][...]) * inv
        for h in range(MEM_HEADS):
            kvf[0, h] = k[:, h * hd:(h + 1) * hd]
            kvf[1, h] = v[:, h * hd:(h + 1) * hd]
        for cp in kv_out_copies():
            cp.start()
        kp_ref[...] = k.astype(BF16)
        vp_ref[...] = v.astype(BF16)
        q = _norm_dot(r["xs"][...], r["g_xattn"][...], r["w_q"][...]).astype(BF16)
        for h in range(MEM_HEADS):
            qs_ref[h] = q[:, h * hd:(h + 1) * hd]

    @pl.when(i < n_prompt)
    def _():
        @pl.when(i + 1 < n_prompt)
        def _():
            for_unit(i + 1, lambda cp: cp.start())

        for_unit(i, lambda cp: cp.wait())

        x = r["xp"][...]
        q = _norm_dot(x, r["g_xattn"][...], r["w_q"][...]).astype(BF16)
        o = _attend([q[:, h * hd:(h + 1) * hd] for h in range(MEM_HEADS)],
                    [kp_ref[:, h * hd:(h + 1) * hd] for h in range(MEM_HEADS)],
                    [vp_ref[:, h * hd:(h + 1) * hd] for h in range(MEM_HEADS)], scale)
        r["outp"][...] = x + _dot(jnp.concatenate(o, axis=-1), r["w_o"][...])

        b, hg, slot = i // ups, lax.rem(i, ups), lax.rem(i, 2)
        rows = pl.ds(pl.multiple_of(b * Ts, Ts), Ts)
        heads = [hg * hpu + e for e in range(hpu)]
        o_s = _attend([qs_ref[h, rows, :] for h in heads], [kbuf[slot, e] for e in range(hpu)],
                      [vbuf[slot, e] for e in range(hpu)], scale)
        for h, o_h in zip(heads, o_s):
            os_ref[h, rows, :] = o_h.astype(BF16)

    @pl.when(i == n_prompt)
    def _():
        for cp in kv_out_copies():
            cp.wait()
        o = jnp.concatenate([os_ref[h] for h in range(MEM_HEADS)], axis=-1)
        r["outs"][...] = r["xs"][...] + _dot(o, r["w_o"][...])


def _xattn(xp, xs, mem, kc, vc, p, n_streams, Ts):
    S, D = xp.shape
    rows_s = xs.shape[0]
    ts = PROMPT_BLOCK
    n_prompt = S // ts
    n_mem = mem.shape[0]
    hd = D // MEM_HEADS
    hpu = n_streams * MEM_HEADS // n_prompt
    assert hpu * n_prompt == n_streams * MEM_HEADS and MEM_HEADS % hpu == 0
    args = dict(p, xp=xp, xs=xs, mem=mem, kc=kc, vc=vc)
    ins = [args[n] for n in _XATTN_IN]
    cache_spec = pl.BlockSpec(memory_space=pl.ANY)
    in_specs = [_prompt_spec(ts, D, n_prompt)]
    for n in _XATTN_IN[1:]:
        in_specs.append(cache_spec if n in ("kc", "vc") else _weight_spec(args[n].shape))
    out_shapes = dict(outp=(S, D), outs=(rows_s, D), mk=(1, 1, n_mem, MEM_HEADS, hd), mv=(1, 1, n_mem, MEM_HEADS, hd))
    out_specs = [_prompt_spec(ts, D, n_prompt), _const_spec(out_shapes["outs"]), cache_spec, cache_spec]
    return pl.pallas_call(
        functools.partial(_xattn_kernel, n_prompt=n_prompt, n_streams=n_streams, Ts=Ts, hpu=hpu, scale=hd ** -0.5),
        grid=(n_prompt + 1,),
        in_specs=in_specs,
        out_specs=out_specs,
        out_shape=[jax.ShapeDtypeStruct(out_shapes[n], F32) for n in _XATTN_OUT],
        scratch_shapes=[pltpu.VMEM((n_mem, D), BF16), pltpu.VMEM((n_mem, D), BF16),
                        pltpu.VMEM((MEM_HEADS, rows_s, hd), BF16), pltpu.VMEM((MEM_HEADS, rows_s, hd), BF16),
                        pltpu.VMEM((2, hpu, n_mem, hd), F32), pltpu.VMEM((2, hpu, n_mem, hd), F32),
                        pltpu.SemaphoreType.DMA((2, 2 * hpu)),
                        pltpu.VMEM((2, MEM_HEADS, n_mem, hd), F32), pltpu.SemaphoreType.DMA((2, MEM_HEADS))],
        compiler_params=pltpu.CompilerParams(dimension_semantics=("arbitrary",),
                                             vmem_limit_bytes=VMEM_LIMIT),
        name="xattn",
    )(*ins)


def _mlp_rows(x, g_ref, wup_ref, wdown_ref, gfin_ref):
    hg = (x * g_ref[...]).astype(BF16)
    acc = None
    for j in range(wup_ref.shape[1] // FF_CHUNK):
        cols = slice(j * FF_CHUNK, (j + 1) * FF_CHUNK)
        up = _dot(hg, wup_ref[:, cols])
        part = _dot(jnp.square(jnp.maximum(up, 0.0)), wdown_ref[cols, :])
        acc = part if acc is None else acc + part
    return _rmsnorm(x + acc * jnp.square(_inv_rms(x)), gfin_ref[...])


def _mlp_kernel(xp_ref, xs_ref, g_ref, wup_ref, wdown_ref, gfin_ref, outp_ref, outs_ref, *, n_prompt):
    i = pl.program_id(0)

    @pl.when(i < n_prompt)
    def _():
        outp_ref[...] = _mlp_rows(xp_ref[...], g_ref, wup_ref, wdown_ref, gfin_ref)

    @pl.when(i == n_prompt)
    def _():
        outs_ref[...] = _mlp_rows(xs_ref[...], g_ref, wup_ref, wdown_ref, gfin_ref)


def _mlp(xp, xs, p):
    S, D = xp.shape
    ts = PROMPT_BLOCK
    n_prompt = S // ts
    weights = (p["g_mlp"], p["w_up"], p["w_down"], p["g_final"])
    return pl.pallas_call(
        functools.partial(_mlp_kernel, n_prompt=n_prompt),
        grid=(n_prompt + 1,),
        in_specs=[_prompt_spec(ts, D, n_prompt), _weight_spec(xs.shape)] + [_weight_spec(w.shape) for w in weights],
        out_specs=[_prompt_spec(ts, D, n_prompt), _const_spec(xs.shape)],
        out_shape=[jax.ShapeDtypeStruct(xp.shape, F32), jax.ShapeDtypeStruct(xs.shape, F32)],
        compiler_params=pltpu.CompilerParams(dimension_semantics=("arbitrary",),
                                             vmem_limit_bytes=VMEM_LIMIT),
        name="mlp",
    )(xp, xs, *weights)


def _prepare_layer(l, g_mix, w_in, w_pool, pool_scale, a_re, a_im, b_re, b_im, c_re, c_im, d_skip,
                   log_dt, w_glu, b_glu, w_out, g_xattn, g_mem, w_q, w_k, w_v, w_o, g_mlp, w_up,
                   w_down, g_final):
    G, P = a_re.shape[1:]
    gi = G // N_HALF
    ar, ai = a_re[l].astype(F32), a_im[l].astype(F32)
    dt = jnp.exp(log_dt[l].astype(F32))[:, None]
    lam_re, lam_im = ar * dt, ai * dt

    mag = jnp.exp(lam_re)
    ab_re, ab_im = mag * jnp.cos(lam_im), mag * jnp.sin(lam_im)

    den = ar * ar + ai * ai
    coef_re = ((ab_re - 1.0) * ar + ab_im * ai) / den
    coef_im = (ab_im * ar - (ab_re - 1.0) * ai) / den
    br, bi = b_re[l].astype(F32), b_im[l].astype(F32)
    bb_re = coef_re[..., None] * br - coef_im[..., None] * bi
    bb_im = coef_re[..., None] * bi + coef_im[..., None] * br

    half = lambda t: t.astype(F32).reshape((N_HALF, gi * t.shape[1], t.shape[2]))
    row = lambda v: v.astype(F32).reshape(1, -1)
    return dict(
        g_mix=row(g_mix[l]), w_in=w_in[l], w_pool=w_pool[l], pool_scale=row(pool_scale[l]),
        bb_re=half(bb_re), bb_im=half(bb_im), cc_re=half(c_re[l]), cc_im=half(c_im[l]),
        a1_re=row(ab_re), a1_im=row(ab_im),
        d_skip=row(d_skip[l]), w_glu=w_glu[l], b_glu=row(b_glu[l]), w_out=w_out[l],
        g_xattn=row(g_xattn[l]), g_mem=row(g_mem[l]), w_q=w_q[l], w_k=w_k[l], w_v=w_v[l], w_o=w_o[l],
        g_mlp=row(g_mlp[l]), w_up=w_up[l], w_down=w_down[l], g_final=row(g_final))


def kernel(x_prompt, x_sample, cache_mem_k, cache_mem_v, state_pool, state_ssm_re, state_ssm_im, mem_prompt, g_mix, w_in, w_pool, pool_scale, ssm_a_re, ssm_a_im, ssm_b_re, ssm_b_im, ssm_c_re, ssm_c_im, ssm_d, ssm_log_dt, w_glu, b_glu, w_out, g_xattn, g_mem, w_q, w_k, w_v, w_o, g_mlp, w_up, w_down, g_final):
    depth = g_mix.shape[0]
    assert depth == 1 and x_prompt.shape[0] == 1, "single layer, single prompt stream"
    Bp, S, D = x_prompt.shape
    Bs, Ts, _ = x_sample.shape
    n_mem = mem_prompt.shape[1]
    G, P = ssm_a_re.shape[1:]
    assert S % PROMPT_BLOCK == 0 and PROMPT_BLOCK % PROMPT_CHUNK == 0 and Ts <= PROMPT_CHUNK

    l = 0
    p = _prepare_layer(l, g_mix, w_in, w_pool, pool_scale, ssm_a_re, ssm_a_im, ssm_b_re, ssm_b_im,
                       ssm_c_re, ssm_c_im, ssm_d, ssm_log_dt, w_glu, b_glu, w_out, g_xattn, g_mem,
                       w_q, w_k, w_v, w_o, g_mlp, w_up, w_down, g_final)

    xp1, xs1, hist_p, hre_p, him_p, hist_s, hre_s, him_s = _mixer(
        x_prompt[0], x_sample.reshape(Bs * Ts, D), jnp.transpose(state_pool[l], (1, 0, 2)),
        state_ssm_re[l].reshape(Bs * G, P), state_ssm_im[l].reshape(Bs * G, P), p, Bs, Ts)
    xp2, xs2, mk, mv = _xattn(xp1, xs1, mem_prompt[0], cache_mem_k[l:l + 1], cache_mem_v[l:l + 1], p, Bs, Ts)
    y_prompt, y_sample = _mlp(xp2, xs2, p)

    hd = D // MEM_HEADS
    return (y_prompt[None], y_sample.reshape(Bs, Ts, D),
            mk, mv,
            hist_p[1:].reshape(1, Bp, POOL_HIST, -1),
            hre_p.reshape(1, Bp, G, P), him_p.reshape(1, Bp, G, P),
            jnp.transpose(hist_s, (1, 0, 2))[None], hre_s.reshape(1, Bs, G, P), him_s.reshape(1, Bs, G, P))
```

```python
import functools

import jax
import jax.numpy as jnp
from jax import lax
from jax.experimental import pallas as pl
from jax.experimental.pallas import tpu as pltpu

F32 = jnp.float32
BF16 = jnp.bfloat16

EPS = 1e-6
PAST_LEN = 1024
POOL_WINDOWS = (2, 4, 8, 16)
POOL_HIST = max(POOL_WINDOWS) - 1
HIST_ROWS = 16
MEM_HEADS = 4
N_HALF = 2
S5_TILE = 256
S5_DTYPE = F32

PROMPT_BLOCK = 512
PROMPT_CHUNK = 128
FF_CHUNK = 2048
VMEM_LIMIT = 56 * 1024 * 1024


def _rmsnorm(x, g):
    return x * lax.rsqrt(jnp.mean(x * x, axis=-1, keepdims=True) + EPS) * g


def _dot(a, b):
    return jnp.dot(a.astype(BF16), b.astype(BF16), preferred_element_type=F32)


def _inv_rms(x):
    return lax.rsqrt(jnp.mean(x * x, axis=-1, keepdims=True) + EPS)


def _norm_dot(x, g, w):
    return _dot(x * g, w) * _inv_rms(x)


def _pool_windows(zbuf, nrows, pos, r):
    gw = r["w_pool"].shape[-1]
    outs = []
    for g, w in enumerate(POOL_WINDOWS):
        lanes = slice(g * gw, (g + 1) * gw)
        win = zbuf[pl.ds(0, HIST_ROWS + nrows), lanes]
        k = 1
        while k < w:
            win = win + pltpu.roll(win, k, 0)
            k *= 2
        cur = zbuf[pl.ds(HIST_ROWS, nrows), lanes]
        cnt = jnp.minimum(pos + 1, w).astype(F32)
        pooled = win[HIST_ROWS:, :] / cnt - cur
        outs.append(_dot(pooled, r["w_pool"][g]))
    return jnp.concatenate(outs, axis=-1) * r["pool_scale"][...]


def _cmul(ar, ai, br, bi):
    return ar * br - ai * bi, ar * bi + ai * br


def _s5_scan(u, n_chunks, T, end_row, h_carry, h_rows, r, hout_re_ref, hout_im_ref):
    hw = r["bblk"].shape[1]
    nh = r["bblk"].shape[2] // 2
    u_bf = u.astype(BF16)
    r_i = lax.broadcasted_iota(jnp.int32, (T, T), 0)
    c_i = lax.broadcasted_iota(jnp.int32, (T, T), 1)
    ltri = jnp.where(r_i >= c_i, 1.0, 0.0).astype(BF16)

    ys = []
    for k in range(N_HALF):
        uk = u_bf[:, k * hw:(k + 1) * hw]
        yk = None
        for j in range(nh // S5_TILE):
            re_c = slice(j * S5_TILE, (j + 1) * S5_TILE)
            im_c = slice(nh + j * S5_TILE, nh + (j + 1) * S5_TILE)
            st = slice(k * nh + j * S5_TILE, k * nh + (j + 1) * S5_TILE)
            bu_re = _dot(uk, r["bblk"][k, :, re_c])
            bu_im = _dot(uk, r["bblk"][k, :, im_c])
            if h_rows is None:
                carry = (h_carry[0][:, st], h_carry[1][:, st])
            h_re, h_im = [], []
            for c in range(n_chunks):
                rows = slice(c * T, (c + 1) * T)
                xr, xi = _cmul(bu_re[rows].astype(S5_DTYPE), bu_im[rows].astype(S5_DTYPE),
                               r["pneg_re"][0:T, st], r["pneg_im"][0:T, st])
                wr = _dot(ltri, xr)
                wi = _dot(ltri, xi)
                if h_rows is not None:
                    carry = (h_rows[0][pl.ds(c, 1), st], h_rows[1][pl.ds(c, 1), st])
                cr, ci = _cmul(r["a1_re"][:, st], r["a1_im"][:, st], carry[0], carry[1])
                wr, wi = wr + cr, wi + ci
                sr, si = _cmul(r["ppos_re"][0:T, st], r["ppos_im"][0:T, st], wr.astype(S5_DTYPE), wi.astype(S5_DTYPE))
                carry = _cmul(r["pend_re"][end_row:end_row + 1, st], r["pend_im"][end_row:end_row + 1, st],
                              wr[T - 1:T, :], wi[T - 1:T, :])
                if h_rows is not None:
                    hout_re_ref[pl.ds(c, 1), st] = carry[0]
                    hout_im_ref[pl.ds(c, 1), st] = carry[1]
                h_re.append(sr)
                h_im.append(si)
            if h_rows is None:
                hout_re_ref[:, st] = carry[0]
                hout_im_ref[:, st] = carry[1]
            part = (_dot(jnp.concatenate(h_re, axis=0), r["cblk"][k, re_c, :])
                    + _dot(jnp.concatenate(h_im, axis=0), r["cblk"][k, im_c, :]))
            yk = part if yk is None else yk + part
        ys.append(yk)
    return jnp.concatenate(ys, axis=-1)


def _mix_tail(x, y_pool, u, y_state, r):
    y = jax.nn.gelu(y_state + r["d_skip"][...] * u)
    y = y * jax.nn.sigmoid(_dot(y, r["w_glu"][...]) + r["b_glu"][...])
    ycat = jnp.concatenate([y_pool.astype(BF16), y.astype(BF16)], axis=-1)
    return x + _dot(ycat, r["w_out"][...])


_MIXER_IN = ("xp", "xs", "hist_in", "h0_re", "h0_im",
             "g_mix", "w_in", "w_pool", "pool_scale", "bb_re", "bb_im", "cc_re", "cc_im", "a1_re", "a1_im",
             "d_skip", "w_glu", "b_glu", "w_out")
_MIXER_OUT = ("outp", "outs", "histp", "hrep", "himp", "hists", "hres", "hims")
_MIXER_BF16 = ("w_in", "w_pool", "w_glu", "w_out")
_MIXER_GEN = ("bblk", "cblk", "pneg_re", "pneg_im", "ppos_re", "ppos_im", "pend_re", "pend_im")


def _power_table(re1, im1, T):
    SUB = 16
    assert T % SUB == 0
    n = re1.shape[-1]
    t = lax.broadcasted_iota(jnp.int32, (SUB, 1), 0)
    lo_re, lo_im = jnp.ones((SUB, n), F32), jnp.zeros((SUB, n), F32)
    cr, ci = re1, im1
    k = 1
    while k < SUB:
        bit = (t & k) != 0
        lo_re, lo_im = _cmul(lo_re, lo_im, jnp.where(bit, cr, 1.0), jnp.where(bit, ci, 0.0))
        cr, ci = _cmul(cr, ci, cr, ci)
        k *= 2
    hi_re, hi_im = [jnp.ones((1, n), F32)], [jnp.zeros((1, n), F32)]
    for _ in range(T // SUB - 1):
        nr, ni = _cmul(hi_re[-1], hi_im[-1], cr, ci)
        hi_re.append(nr)
        hi_im.append(ni)
    rep = lambda rows: jnp.concatenate([jnp.broadcast_to(v, (SUB, n)) for v in rows], axis=0)
    tile = lambda v: jnp.concatenate([v] * (T // SUB), axis=0)
    return _cmul(tile(lo_re), tile(lo_im), rep(hi_re), rep(hi_im))


def _expand_s5_params(r, gen, T, t_ends):
    are, aim = r["a1_re"][...], r["a1_im"][...]
    pr, pi = _power_table(are, aim, T)
    gen["ppos_re"][...] = pr.astype(S5_DTYPE)
    gen["ppos_im"][...] = pi.astype(S5_DTYPE)
    for j, te in enumerate(t_ends):
        gen["pend_re"][j:j + 1, :] = pr[te - 1:te, :]
        gen["pend_im"][j:j + 1, :] = pi[te - 1:te, :]
    den = are * are + aim * aim
    nr, ni = _power_table(are / den, -aim / den, T)
    gen["pneg_re"][...] = nr.astype(S5_DTYPE)
    gen["pneg_im"][...] = ni.astype(S5_DTYPE)

    n_half, rows_b, H = r["bb_re"].shape
    _, rows_c, P = r["cc_re"].shape
    nh = rows_b

    def spread(x, width):
        w = x.shape[1]
        sel = lax.rem(lax.broadcasted_iota(jnp.int32, (w, width), 1), w) == lax.broadcasted_iota(jnp.int32, (w, width), 0)
        return _dot(x, jnp.where(sel, 1.0, 0.0))

    def same_group(shape, row_size, col_size):
        return (lax.div(lax.broadcasted_iota(jnp.int32, shape, 0), row_size)
                == lax.div(lax.broadcasted_iota(jnp.int32, shape, 1), col_size))

    for k in range(n_half):
        for c, (b_src, c_src, sign) in enumerate(((r["bb_re"], r["cc_re"], 1.0), (r["bb_im"], r["cc_im"], -1.0))):
            yb = spread(b_src[k], rows_c)
            yb = jnp.where(same_group(yb.shape, P, H), yb, 0.0)
            gen["bblk"][k, :, c * nh:(c + 1) * nh] = yb.T.astype(BF16)
            yc = spread(c_src[k], rows_b)
            yc = jnp.where(same_group(yc.shape, H, P), sign * yc, 0.0)
            gen["cblk"][k, c * nh:(c + 1) * nh, :] = yc.T.astype(BF16)


def _mixer_kernel(*refs, n_prompt, n_streams, Tp, Ts):
    r = dict(zip(_MIXER_IN + _MIXER_OUT, refs))
    zbuf, ypool_ref, hp_re, hp_im, hs_re, hs_im, h0f_re, h0f_im = refs[len(_MIXER_IN) + len(_MIXER_OUT):][:8]
    n_fixed = len(_MIXER_IN) + len(_MIXER_OUT) + 8
    bf16_weights = dict(zip(_MIXER_BF16, refs[n_fixed:]))
    gen = dict(zip(_MIXER_GEN, refs[n_fixed + len(_MIXER_BF16):]))
    i = pl.program_id(0)
    ts = r["xp"].shape[0]
    d_pool = zbuf.shape[1]

    @pl.when(i == 0)
    def _():
        zbuf[0:HIST_ROWS, :] = jnp.zeros((HIST_ROWS, d_pool), F32)
        hp_re[...] = jnp.zeros(hp_re.shape, F32)
        hp_im[...] = jnp.zeros(hp_im.shape, F32)
        for name, ref in bf16_weights.items():
            ref[...] = r[name][...].astype(BF16)
        _expand_s5_params(r, gen, Tp, (Tp, Ts))

    r.update(bf16_weights)
    r.update(gen)

    @pl.when(i < n_prompt)
    def _():
        x = r["xp"][...]
        z = _dot(_rmsnorm(x, r["g_mix"][...]), r["w_in"][...])
        zbuf[HIST_ROWS:HIST_ROWS + ts, :] = z[:, :d_pool]
        u = z[:, d_pool:]
        pos = i * ts + lax.broadcasted_iota(jnp.int32, (ts, 1), 0)
        y_pool = _pool_windows(zbuf, ts, pos, r)
        y_state = _s5_scan(u, ts // Tp, Tp, 0, (hp_re[...], hp_im[...]), None, r, hp_re, hp_im)
        r["outp"][...] = _mix_tail(x, y_pool, u, y_state, r)
        tail = zbuf[ts:ts + HIST_ROWS, :]
        zbuf[0:HIST_ROWS, :] = tail
        r["histp"][...] = tail

    @pl.when(i == n_prompt)
    def _():
        ext = HIST_ROWS + Ts
        x = r["xs"][...]
        z = _dot(_rmsnorm(x, r["g_mix"][...]), r["w_in"][...])
        u = z[:, d_pool:]
        for b in range(n_streams):
            zbuf[b * ext:b * ext + 1, :] = jnp.zeros((1, d_pool), F32)
            for t in range(POOL_HIST):
                zbuf[b * ext + 1 + t:b * ext + 2 + t, :] = r["hist_in"][t, b:b + 1, :]
            zbuf[b * ext + HIST_ROWS:(b + 1) * ext, :] = z[b * Ts:(b + 1) * Ts, :d_pool]
            for t in range(POOL_HIST):
                row = (b + 1) * ext - POOL_HIST + t
                r["hists"][t, b:b + 1, :] = zbuf[row:row + 1, :]
        nrows = n_streams * ext - HIST_ROWS
        ridx = lax.broadcasted_iota(jnp.int32, (nrows, 1), 0)
        pos = PAST_LEN + lax.rem(ridx, ext)
        y_all = _pool_windows(zbuf, nrows, pos, r)
        for b in range(n_streams):
            ypool_ref[b * Ts:(b + 1) * Ts, :] = y_all[b * ext:b * ext + Ts, :]
        G, P = r["hrep"].shape
        for src, dst in ((r["h0_re"], h0f_re), (r["h0_im"], h0f_im)):
            for b in range(n_streams):
                for g in range(G):
                    dst[b:b + 1, g * P:(g + 1) * P] = src[b * G + g:b * G + g + 1, :]
        y_state = _s5_scan(u, n_streams, Ts, 1, None, (h0f_re, h0f_im), r, hs_re, hs_im)
        r["outs"][...] = _mix_tail(x, ypool_ref[...], u, y_state, r)
        for g in range(G):
            cols = slice(g * P, (g + 1) * P)
            r["hrep"][g:g + 1, :] = hp_re[:, cols]
            r["himp"][g:g + 1, :] = hp_im[:, cols]
            for b in range(n_streams):
                r["hres"][b * G + g:b * G + g + 1, :] = hs_re[b:b + 1, cols]
                r["hims"][b * G + g:b * G + g + 1, :] = hs_im[b:b + 1, cols]


def _const_spec(shape):
    nd = len(shape)
    return pl.BlockSpec(shape, lambda *_: (0,) * nd)


def _weight_spec(shape):
    nd = len(shape)
    return pl.BlockSpec(shape, lambda *_: (0,) * nd, pipeline_mode=pl.Buffered(1))


def _prompt_spec(ts, d, n_prompt):
    return pl.BlockSpec((ts, d), lambda i: (jnp.minimum(i, n_prompt - 1), 0))


def _mixer(xp, xs, hist_in, h0_re, h0_im, p, n_streams, Ts):
    S, D = xp.shape
    rows_s = xs.shape[0]
    ts = PROMPT_BLOCK
    n_prompt = S // ts
    d_pool = hist_in.shape[-1]
    n_state = p["a1_re"].shape[-1]
    P = h0_re.shape[-1]
    G = n_state // P
    assert n_streams * (HIST_ROWS + Ts) <= HIST_ROWS + ts and rows_s <= ts
    args = dict(p, xp=xp, xs=xs, hist_in=hist_in, h0_re=h0_re, h0_im=h0_im)
    ins = [args[n] for n in _MIXER_IN]
    in_specs = [_prompt_spec(ts, D, n_prompt)] + [_weight_spec(a.shape) for a in ins[1:]]
    n_half, nh, H = p["bb_re"].shape
    d_half = p["cc_re"].shape[1]
    gen_shapes = ([((n_half, d_half, 2 * nh), BF16), ((n_half, 2 * nh, d_half), BF16)]
                  + [((PROMPT_CHUNK, n_state), S5_DTYPE)] * 4 + [((2, n_state), F32)] * 2)
    out_shapes = dict(
        outp=(S, D), outs=(rows_s, D), histp=(HIST_ROWS, d_pool), hrep=(G, P), himp=(G, P),
        hists=(POOL_HIST, n_streams, d_pool), hres=(n_streams * G, P), hims=(n_streams * G, P))
    out_specs = [_prompt_spec(ts, D, n_prompt)] + [_const_spec(out_shapes[n]) for n in _MIXER_OUT[1:]]
    return pl.pallas_call(
        functools.partial(_mixer_kernel, n_prompt=n_prompt, n_streams=n_streams, Tp=PROMPT_CHUNK, Ts=Ts),
        grid=(n_prompt + 1,),
        in_specs=in_specs,
        out_specs=out_specs,
        out_shape=[jax.ShapeDtypeStruct(out_shapes[n], F32) for n in _MIXER_OUT],
        scratch_shapes=[pltpu.VMEM((HIST_ROWS + ts, d_pool), F32),
                        pltpu.VMEM((rows_s, d_pool), F32)]
        + [pltpu.VMEM((1, n_state), F32)] * 2 + [pltpu.VMEM((n_streams, n_state), F32)] * 4
        + [pltpu.VMEM(args[n].shape, BF16) for n in _MIXER_BF16]
        + [pltpu.VMEM(shape, dtype) for shape, dtype in gen_shapes],
        compiler_params=pltpu.CompilerParams(dimension_semantics=("arbitrary",),
                                             vmem_limit_bytes=VMEM_LIMIT),
        name="mixer",
    )(*ins)


def _attend(q_heads, k_heads, v_heads, scale):
    outs = []
    for q, k, v in zip(q_heads, k_heads, v_heads):
        s = lax.dot_general(q, k.astype(BF16), (((1,), (1,)), ((), ())), preferred_element_type=F32) * scale
        e = jnp.exp(s - jnp.max(s, axis=-1, keepdims=True))
        outs.append(_dot(e, v) / jnp.sum(e, axis=-1, keepdims=True))
    return outs


_XATTN_IN = ("xp", "xs", "mem", "kc", "vc", "g_mem", "w_k", "w_v", "g_xattn", "w_q", "w_o")
_XATTN_OUT = ("outp", "outs", "mk", "mv")


def _xattn_kernel(*refs, n_prompt, n_streams, Ts, hpu, scale):
    r = dict(zip(_XATTN_IN + _XATTN_OUT, refs))
    kp_ref, vp_ref, qs_ref, os_ref, kbuf, vbuf, sem, kvf, osem = refs[len(_XATTN_IN) + len(_XATTN_OUT):]
    i = pl.program_id(0)
    hd = kp_ref.shape[-1] // MEM_HEADS
    ups = MEM_HEADS // hpu

    def unit_copies(unit, hg):
        b, slot = unit // ups, lax.rem(unit, 2)
        return [pltpu.make_async_copy(src.at[0, b, :, hg * hpu + e, :], dst.at[slot, e], sem.at[slot, j * hpu + e])
                for j, (src, dst) in enumerate(((r["kc"], kbuf), (r["vc"], vbuf))) for e in range(hpu)]

    def kv_out_copies():
        return [pltpu.make_async_copy(kvf.at[j, h], dst.at[0, 0, :, h, :], osem.at[j, h])
                for j, dst in enumerate((r["mk"], r["mv"])) for h in range(MEM_HEADS)]

    def for_unit(unit, fn):
        for hg in range(ups):
            @pl.when(lax.rem(unit, ups) == hg)
            def _():
                for cp in unit_copies(unit, hg):
                    fn(cp)

    @pl.when(i == 0)
    def _():
        for_unit(i, lambda cp: cp.start())
        mem = r["mem"][...]
        inv = _inv_rms(mem)
        mg = (mem * r["g_mem"][...]).astype(BF16)
        k = _dot(mg, r["w_k"][...]) * inv
        v = _dot(mg, r["w_v"][...]) * inv
        for h in range(MEM_HEADS):
            kvf[0, h] = k[:, h * hd:(h + 1) * hd]
            kvf[1, h] = v[:, h * hd:(h + 1) * hd]
        for cp in kv_out_copies():
            cp.start()
        kp_ref[...] = k.astype(BF16)
        vp_ref[...] = v.astype(BF16)
        q = _norm_dot(r["xs"][...], r["g_xattn"][...], r["w_q"][...]).astype(BF16)
        for h in range(MEM_HEADS):
            qs_ref[h] = q[:, h * hd:(h + 1) * hd]

    @pl.when(i < n_prompt)
    def _():
        @pl.when(i + 1 < n_prompt)
        def _():
            for_unit(i + 1, lambda cp: cp.start())

        for_unit(i, lambda cp: cp.wait())

        x = r["xp"][...]
        q = _norm_dot(x, r["g_xattn"][...], r["w_q"][...]).astype(BF16)
        o = _attend([q[:, h * hd:(h + 1) * hd] for h in range(MEM_HEADS)],
                    [kp_ref[:, h * hd:(h + 1) * hd] for h in range(MEM_HEADS)],
                    [vp_ref[:, h * hd:(h + 1) * hd] for h in range(MEM_HEADS)], scale)
        r["outp"][...] = x + _dot(jnp.concatenate(o, axis=-1), r["w_o"][...])

        b, hg, slot = i // ups, lax.rem(i, ups), lax.rem(i, 2)
        rows = pl.ds(pl.multiple_of(b * Ts, Ts), Ts)
        heads = [hg * hpu + e for e in range(hpu)]
        o_s = _attend([qs_ref[h, rows, :] for h in heads], [kbuf[slot, e] for e in range(hpu)],
                      [vbuf[slot, e] for e in range(hpu)], scale)
        for h, o_h in zip(heads, o_s):
            os_ref[h, rows, :] = o_h.astype(BF16)

    @pl.when(i == n_prompt)
    def _():
        for cp in kv_out_copies():
            cp.wait()
        o = jnp.concatenate([os_ref[h] for h in range(MEM_HEADS)], axis=-1)
        r["outs"][...] = r["xs"][...] + _dot(o, r["w_o"][...])


def _xattn(xp, xs, mem, kc, vc, p, n_streams, Ts):
    S, D = xp.shape
    rows_s = xs.shape[0]
    ts = 1024
    n_prompt = S // ts
    n_mem = mem.shape[0]
    hd = D // MEM_HEADS
    hpu = n_streams * MEM_HEADS // n_prompt
    assert hpu * n_prompt == n_streams * MEM_HEADS and MEM_HEADS % hpu == 0
    args = dict(p, xp=xp, xs=xs, mem=mem, kc=kc, vc=vc)
    ins = [args[n] for n in _XATTN_IN]
    cache_spec = pl.BlockSpec(memory_space=pl.ANY)
    in_specs = [_prompt_spec(ts, D, n_prompt)]
    for n in _XATTN_IN[1:]:
        in_specs.append(cache_spec if n in ("kc", "vc") else _weight_spec(args[n].shape))
    out_shapes = dict(outp=(S, D), outs=(rows_s, D), mk=(1, 1, n_mem, MEM_HEADS, hd), mv=(1, 1, n_mem, MEM_HEADS, hd))
    out_specs = [_prompt_spec(ts, D, n_prompt), _const_spec(out_shapes["outs"]), cache_spec, cache_spec]
    return pl.pallas_call(
        functools.partial(_xattn_kernel, n_prompt=n_prompt, n_streams=n_streams, Ts=Ts, hpu=hpu, scale=hd ** -0.5),
        grid=(n_prompt + 1,),
        in_specs=in_specs,
        out_specs=out_specs,
        out_shape=[jax.ShapeDtypeStruct(out_shapes[n], F32) for n in _XATTN_OUT],
        scratch_shapes=[pltpu.VMEM((n_mem, D), BF16), pltpu.VMEM((n_mem, D), BF16),
                        pltpu.VMEM((MEM_HEADS, rows_s, hd), BF16), pltpu.VMEM((MEM_HEADS, rows_s, hd), BF16),
                        pltpu.VMEM((2, hpu, n_mem, hd), F32), pltpu.VMEM((2, hpu, n_mem, hd), F32),
                        pltpu.SemaphoreType.DMA((2, 2 * hpu)),
                        pltpu.VMEM((2, MEM_HEADS, n_mem, hd), F32), pltpu.SemaphoreType.DMA((2, MEM_HEADS))],
        compiler_params=pltpu.CompilerParams(dimension_semantics=("arbitrary",),
                                             vmem_limit_bytes=VMEM_LIMIT),
        name="xattn",
    )(*ins)


def _mlp_rows(x, g_ref, wup_ref, wdown_ref, gfin_ref):
    hg = (x * g_ref[...]).astype(BF16)
    acc = None
    for j in range(wup_ref.shape[1] // FF_CHUNK):
        cols = slice(j * FF_CHUNK, (j + 1) * FF_CHUNK)
        up = _dot(hg, wup_ref[:, cols])
        part = _dot(jnp.square(jnp.maximum(up, 0.0)), wdown_ref[cols, :])
        acc = part if acc is None else acc + part
    return _rmsnorm(x + acc * jnp.square(_inv_rms(x)), gfin_ref[...])


def _mlp_kernel(xp_ref, xs_ref, g_ref, wup_ref, wdown_ref, gfin_ref, outp_ref, outs_ref, *, n_prompt):
    i = pl.program_id(0)

    @pl.when(i < n_prompt)
    def _():
        outp_ref[...] = _mlp_rows(xp_ref[...], g_ref, wup_ref, wdown_ref, gfin_ref)

    @pl.when(i == n_prompt)
    def _():
        outs_ref[...] = _mlp_rows(xs_ref[...], g_ref, wup_ref, wdown_ref, gfin_ref)


def _mlp(xp, xs, p):
    S, D = xp.shape
    ts = PROMPT_BLOCK
    n_prompt = S // ts
    weights = (p["g_mlp"], p["w_up"], p["w_down"], p["g_final"])
    return pl.pallas_call(
        functools.partial(_mlp_kernel, n_prompt=n_prompt),
        grid=(n_prompt + 1,),
        in_specs=[_prompt_spec(ts, D, n_prompt), _weight_spec(xs.shape)] + [_weight_spec(w.shape) for w in weights],
        out_specs=[_prompt_spec(ts, D, n_prompt), _const_spec(xs.shape)],
        out_shape=[jax.ShapeDtypeStruct(xp.shape, F32), jax.ShapeDtypeStruct(xs.shape, F32)],
        compiler_params=pltpu.CompilerParams(dimension_semantics=("arbitrary",),
                                             vmem_limit_bytes=VMEM_LIMIT),
        name="mlp",
    )(xp, xs, *weights)


def _prepare_layer(l, g_mix, w_in, w_pool, pool_scale, a_re, a_im, b_re, b_im, c_re, c_im, d_skip,
                   log_dt, w_glu, b_glu, w_out, g_xattn, g_mem, w_q, w_k, w_v, w_o, g_mlp, w_up,
                   w_down, g_final):
    G, P = a_re.shape[1:]
    gi = G // N_HALF
    ar, ai = a_re[l].astype(F32), a_im[l].astype(F32)
    dt = jnp.exp(log_dt[l].astype(F32))[:, None]
    lam_re, lam_im = ar * dt, ai * dt

    mag = jnp.exp(lam_re)
    ab_re, ab_im = mag * jnp.cos(lam_im), mag * jnp.sin(lam_im)

    den = ar * ar + ai * ai
    coef_re = ((ab_re - 1.0) * ar + ab_im * ai) / den
    coef_im = (ab_im * ar - (ab_re - 1.0) * ai) / den
    br, bi = b_re[l].astype(F32), b_im[l].astype(F32)
    bb_re = coef_re[..., None] * br - coef_im[..., None] * bi
    bb_im = coef_re[..., None] * bi + coef_im[..., None] * br

    half = lambda t: t.astype(F32).reshape((N_HALF, gi * t.shape[1], t.shape[2]))
    row = lambda v: v.astype(F32).reshape(1, -1)
    return dict(
        g_mix=row(g_mix[l]), w_in=w_in[l], w_pool=w_pool[l], pool_scale=row(pool_scale[l]),
        bb_re=half(bb_re), bb_im=half(bb_im), cc_re=half(c_re[l]), cc_im=half(c_im[l]),
        a1_re=row(ab_re), a1_im=row(ab_im),
        d_skip=row(d_skip[l]), w_glu=w_glu[l], b_glu=row(b_glu[l]), w_out=w_out[l],
        g_xattn=row(g_xattn[l]), g_mem=row(g_mem[l]), w_q=w_q[l], w_k=w_k[l], w_v=w_v[l], w_o=w_o[l],
        g_mlp=row(g_mlp[l]), w_up=w_up[l], w_down=w_down[l], g_final=row(g_final))


def kernel(x_prompt, x_sample, cache_mem_k, cache_mem_v, state_pool, state_ssm_re, state_ssm_im, mem_prompt, g_mix, w_in, w_pool, pool_scale, ssm_a_re, ssm_a_im, ssm_b_re, ssm_b_im, ssm_c_re, ssm_c_im, ssm_d, ssm_log_dt, w_glu, b_glu, w_out, g_xattn, g_mem, w_q, w_k, w_v, w_o, g_mlp, w_up, w_down, g_final):
    depth = g_mix.shape[0]
    assert depth == 1 and x_prompt.shape[0] == 1, "single layer, single prompt stream"
    Bp, S, D = x_prompt.shape
    Bs, Ts, _ = x_sample.shape
    n_mem = mem_prompt.shape[1]
    G, P = ssm_a_re.shape[1:]
    assert S % PROMPT_BLOCK == 0 and PROMPT_BLOCK % PROMPT_CHUNK == 0 and Ts <= PROMPT_CHUNK

    l = 0
    p = _prepare_layer(l, g_mix, w_in, w_pool, pool_scale, ssm_a_re, ssm_a_im, ssm_b_re, ssm_b_im,
                       ssm_c_re, ssm_c_im, ssm_d, ssm_log_dt, w_glu, b_glu, w_out, g_xattn, g_mem,
                       w_q, w_k, w_v, w_o, g_mlp, w_up, w_down, g_final)

    xp1, xs1, hist_p, hre_p, him_p, hist_s, hre_s, him_s = _mixer(
        x_prompt[0], x_sample.reshape(Bs * Ts, D), jnp.transpose(state_pool[l], (1, 0, 2)),
        state_ssm_re[l].reshape(Bs * G, P), state_ssm_im[l].reshape(Bs * G, P), p, Bs, Ts)
    xp2, xs2, mk, mv = _xattn(xp1, xs1, mem_prompt[0], cache_mem_k[l:l + 1], cache_mem_v[l:l + 1], p, Bs, Ts)
    y_prompt, y_sample = _mlp(xp2, xs2, p)

    hd = D // MEM_HEADS
    return (y_prompt[None], y_sample.reshape(Bs, Ts, D),
            mk, mv,
            hist_p[1:].reshape(1, Bp, POOL_HIST, -1),
            hre_p.reshape(1, Bp, G, P), him_p.reshape(1, Bp, G, P),
            jnp.transpose(hist_s, (1, 0, 2))[None], hre_s.reshape(1, Bs, G, P), him_s.reshape(1, Bs, G, P))
```

```python
import functools

import jax
import jax.numpy as jnp
from jax import lax
from jax.experimental import pallas as pl
from jax.experimental.pallas import tpu as pltpu

F32 = jnp.float32
BF16 = jnp.bfloat16

EPS = 1e-6
PAST_LEN = 1024
POOL_WINDOWS = (2, 4, 8, 16)
POOL_HIST = max(POOL_WINDOWS) - 1
HIST_ROWS = 16
MEM_HEADS = 4
N_HALF = 2
S5_TILE = 256

PROMPT_BLOCK = 512
XATTN_BLOCK = 1024
PROMPT_CHUNK = 128
FF_CHUNK = 2048
VMEM_LIMIT = 56 * 1024 * 1024


def _inv_rms(x):
    return lax.rsqrt(jnp.mean(x * x, axis=-1, keepdims=True) + EPS)


def _rmsnorm(x, g):
    return x * _inv_rms(x) * g


def _dot(a, b):
    return jnp.dot(a.astype(BF16), b.astype(BF16), preferred_element_type=F32)


def _norm_dot(x, g, w):
    return _dot(x * g, w) * _inv_rms(x)


def _pool_windows(zbuf, nrows, pos, r):
    gw = r["w_pool"].shape[-1]
    outs = []
    for g, w in enumerate(POOL_WINDOWS):
        lanes = slice(g * gw, (g + 1) * gw)
        win = zbuf[pl.ds(0, HIST_ROWS + nrows), lanes]
        k = 1
        while k < w:
            win = win + pltpu.roll(win, k, 0)
            k *= 2
        cur = zbuf[pl.ds(HIST_ROWS, nrows), lanes]
        cnt = jnp.minimum(pos + 1, w).astype(F32)
        pooled = win[HIST_ROWS:, :] / cnt - cur
        outs.append(_dot(pooled, r["w_pool"][g]))
    return jnp.concatenate(outs, axis=-1) * r["pool_scale"][...]


def _cmul(ar, ai, br, bi):
    return ar * br - ai * bi, ar * bi + ai * br


def _s5_scan(u, n_chunks, T, h_carry, h_rows, r, hout_re_ref, hout_im_ref):
    hw = r["bblk"].shape[1]
    nh = r["bblk"].shape[2] // 2
    u_bf = u.astype(BF16)
    r_i = lax.broadcasted_iota(jnp.int32, (T, T), 0)
    c_i = lax.broadcasted_iota(jnp.int32, (T, T), 1)
    ltri = jnp.where(r_i >= c_i, 1.0, 0.0).astype(BF16)

    ys = []
    for k in range(N_HALF):
        uk = u_bf[:, k * hw:(k + 1) * hw]
        yk = None
        for j in range(nh // S5_TILE):
            re_c = slice(j * S5_TILE, (j + 1) * S5_TILE)
            im_c = slice(nh + j * S5_TILE, nh + (j + 1) * S5_TILE)
            st = slice(k * nh + j * S5_TILE, k * nh + (j + 1) * S5_TILE)
            bu_re = _dot(uk, r["bblk"][k, :, re_c])
            bu_im = _dot(uk, r["bblk"][k, :, im_c])
            if h_rows is None:
                carry = (h_carry[0][:, st], h_carry[1][:, st])
            h_re, h_im = [], []
            for c in range(n_chunks):
                rows = slice(c * T, (c + 1) * T)
                xr, xi = _cmul(bu_re[rows], bu_im[rows], r["pneg_re"][0:T, st], r["pneg_im"][0:T, st])
                wr = _dot(ltri, xr)
                wi = _dot(ltri, xi)
                if h_rows is not None:
                    carry = (h_rows[0][pl.ds(c, 1), st], h_rows[1][pl.ds(c, 1), st])
                cr, ci = _cmul(r["a1_re"][:, st], r["a1_im"][:, st], carry[0], carry[1])
                sr, si = _cmul(r["ppos_re"][0:T, st], r["ppos_im"][0:T, st], wr + cr, wi + ci)
                carry = (sr[T - 1:T, :], si[T - 1:T, :])
                if h_rows is not None:
                    hout_re_ref[pl.ds(c, 1), st] = carry[0]
                    hout_im_ref[pl.ds(c, 1), st] = carry[1]
                h_re.append(sr)
                h_im.append(si)
            if h_rows is None:
                hout_re_ref[:, st] = carry[0]
                hout_im_ref[:, st] = carry[1]
            part = (_dot(jnp.concatenate(h_re, axis=0), r["cblk"][k, re_c, :])
                    + _dot(jnp.concatenate(h_im, axis=0), r["cblk"][k, im_c, :]))
            yk = part if yk is None else yk + part
        ys.append(yk)
    return jnp.concatenate(ys, axis=-1)


def _mix_tail(x, y_pool, u, y_state, r):
    y = jax.nn.gelu(y_state + r["d_skip"][...] * u)
    y = y * jax.nn.sigmoid(_dot(y, r["w_glu"][...]) + r["b_glu"][...])
    ycat = jnp.concatenate([y_pool.astype(BF16), y.astype(BF16)], axis=-1)
    return x + _dot(ycat, r["w_out"][...])


_MIXER_IN = ("xp", "xs", "hist_in", "h0_re", "h0_im",
             "g_mix", "w_in", "w_pool", "pool_scale", "bb_re", "bb_im", "cc_re", "cc_im", "a1_re", "a1_im",
             "d_skip", "w_glu", "b_glu", "w_out")
_MIXER_OUT = ("outp", "outs", "histp", "hrep", "himp", "hists", "hres", "hims")
_MIXER_BF16 = ("w_in", "w_pool", "w_glu", "w_out")
_MIXER_GEN = ("bblk", "cblk", "pneg_re", "pneg_im", "ppos_re", "ppos_im")


def _power_table(re1, im1, T):
    SUB = 16
    assert T % SUB == 0
    n = re1.shape[-1]
    t = lax.broadcasted_iota(jnp.int32, (SUB, 1), 0)
    lo_re, lo_im = jnp.ones((SUB, n), F32), jnp.zeros((SUB, n), F32)
    cr, ci = re1, im1
    k = 1
    while k < SUB:
        bit = (t & k) != 0
        lo_re, lo_im = _cmul(lo_re, lo_im, jnp.where(bit, cr, 1.0), jnp.where(bit, ci, 0.0))
        cr, ci = _cmul(cr, ci, cr, ci)
        k *= 2
    hi_re, hi_im = [jnp.ones((1, n), F32)], [jnp.zeros((1, n), F32)]
    for _ in range(T // SUB - 1):
        nr, ni = _cmul(hi_re[-1], hi_im[-1], cr, ci)
        hi_re.append(nr)
        hi_im.append(ni)
    rep = lambda rows: jnp.concatenate([jnp.broadcast_to(v, (SUB, n)) for v in rows], axis=0)
    tile = lambda v: jnp.concatenate([v] * (T // SUB), axis=0)
    return _cmul(tile(lo_re), tile(lo_im), rep(hi_re), rep(hi_im))


def _expand_s5_params(r, gen, T):
    are, aim = r["a1_re"][...], r["a1_im"][...]
    gen["ppos_re"][...], gen["ppos_im"][...] = _power_table(are, aim, T)
    den = are * are + aim * aim
    gen["pneg_re"][...], gen["pneg_im"][...] = _power_table(are / den, -aim / den, T)

    n_half, rows_b, H = r["bb_re"].shape
    _, rows_c, P = r["cc_re"].shape
    nh = rows_b

    def spread(x, width):
        w = x.shape[1]
        sel = lax.rem(lax.broadcasted_iota(jnp.int32, (w, width), 1), w) == lax.broadcasted_iota(jnp.int32, (w, width), 0)
        return _dot(x, jnp.where(sel, 1.0, 0.0))

    def same_group(shape, row_size, col_size):
        return (lax.div(lax.broadcasted_iota(jnp.int32, shape, 0), row_size)
                == lax.div(lax.broadcasted_iota(jnp.int32, shape, 1), col_size))

    for k in range(n_half):
        for c, (b_src, c_src, sign) in enumerate(((r["bb_re"], r["cc_re"], 1.0), (r["bb_im"], r["cc_im"], -1.0))):
            yb = spread(b_src[k], rows_c)
            yb = jnp.where(same_group(yb.shape, P, H), yb, 0.0)
            gen["bblk"][k, :, c * nh:(c + 1) * nh] = yb.T.astype(BF16)
            yc = spread(c_src[k], rows_b)
            yc = jnp.where(same_group(yc.shape, H, P), sign * yc, 0.0)
            gen["cblk"][k, c * nh:(c + 1) * nh, :] = yc.T.astype(BF16)


def _mixer_kernel(*refs, n_prompt, n_streams, Tp, Ts):
    r = dict(zip(_MIXER_IN + _MIXER_OUT, refs))
    zbuf, ypool_ref, hp_re, hp_im, hs_re, hs_im, h0f_re, h0f_im = refs[len(_MIXER_IN) + len(_MIXER_OUT):][:8]
    n_fixed = len(_MIXER_IN) + len(_MIXER_OUT) + 8
    bf16_weights = dict(zip(_MIXER_BF16, refs[n_fixed:]))
    gen = dict(zip(_MIXER_GEN, refs[n_fixed + len(_MIXER_BF16):]))
    i = pl.program_id(0)
    ts = r["xp"].shape[0]
    d_pool = zbuf.shape[1]

    @pl.when(i == 0)
    def _():
        zbuf[0:HIST_ROWS, :] = jnp.zeros((HIST_ROWS, d_pool), F32)
        hp_re[...] = jnp.zeros(hp_re.shape, F32)
        hp_im[...] = jnp.zeros(hp_im.shape, F32)
        for name, ref in bf16_weights.items():
            ref[...] = r[name][...].astype(BF16)
        _expand_s5_params(r, gen, Tp)

    r.update(bf16_weights)
    r.update(gen)

    @pl.when(i < n_prompt)
    def _():
        x = r["xp"][...]
        z = _dot(_rmsnorm(x, r["g_mix"][...]), r["w_in"][...])
        zbuf[HIST_ROWS:HIST_ROWS + ts, :] = z[:, :d_pool]
        u = z[:, d_pool:]
        pos = i * ts + lax.broadcasted_iota(jnp.int32, (ts, 1), 0)
        y_pool = _pool_windows(zbuf, ts, pos, r)
        y_state = _s5_scan(u, ts // Tp, Tp, (hp_re[...], hp_im[...]), None, r, hp_re, hp_im)
        r["outp"][...] = _mix_tail(x, y_pool, u, y_state, r)
        tail = zbuf[ts:ts + HIST_ROWS, :]
        zbuf[0:HIST_ROWS, :] = tail
        r["histp"][...] = tail

    @pl.when(i == n_prompt)
    def _():
        ext = HIST_ROWS + Ts
        x = r["xs"][...]
        z = _dot(_rmsnorm(x, r["g_mix"][...]), r["w_in"][...])
        u = z[:, d_pool:]
        for b in range(n_streams):
            zbuf[b * ext:b * ext + 1, :] = jnp.zeros((1, d_pool), F32)
            for t in range(POOL_HIST):
                zbuf[b * ext + 1 + t:b * ext + 2 + t, :] = r["hist_in"][t, b:b + 1, :]
            zbuf[b * ext + HIST_ROWS:(b + 1) * ext, :] = z[b * Ts:(b + 1) * Ts, :d_pool]
            for t in range(POOL_HIST):
                row = (b + 1) * ext - POOL_HIST + t
                r["hists"][t, b:b + 1, :] = zbuf[row:row + 1, :]
        nrows = n_streams * ext - HIST_ROWS
        ridx = lax.broadcasted_iota(jnp.int32, (nrows, 1), 0)
        pos = PAST_LEN + lax.rem(ridx, ext)
        y_all = _pool_windows(zbuf, nrows, pos, r)
        for b in range(n_streams):
            ypool_ref[b * Ts:(b + 1) * Ts, :] = y_all[b * ext:b * ext + Ts, :]
        G, P = r["hrep"].shape
        for src, dst in ((r["h0_re"], h0f_re), (r["h0_im"], h0f_im)):
            for b in range(n_streams):
                for g in range(G):
                    dst[b:b + 1, g * P:(g + 1) * P] = src[b * G + g:b * G + g + 1, :]
        y_state = _s5_scan(u, n_streams, Ts, None, (h0f_re, h0f_im), r, hs_re, hs_im)
        r["outs"][...] = _mix_tail(x, ypool_ref[...], u, y_state, r)
        for g in range(G):
            cols = slice(g * P, (g + 1) * P)
            r["hrep"][g:g + 1, :] = hp_re[:, cols]
            r["himp"][g:g + 1, :] = hp_im[:, cols]
            for b in range(n_streams):
                r["hres"][b * G + g:b * G + g + 1, :] = hs_re[b:b + 1, cols]
                r["hims"][b * G + g:b * G + g + 1, :] = hs_im[b:b + 1, cols]


def _const_spec(shape):
    nd = len(shape)
    return pl.BlockSpec(shape, lambda *_: (0,) * nd)


def _weight_spec(shape):
    nd = len(shape)
    return pl.BlockSpec(shape, lambda *_: (0,) * nd, pipeline_mode=pl.Buffered(1))


def _prompt_spec(ts, d, n_prompt):
    return pl.BlockSpec((ts, d), lambda i: (jnp.minimum(i, n_prompt - 1), 0))


def _mixer(xp, xs, hist_in, h0_re, h0_im, p, n_streams, Ts):
    S, D = xp.shape
    rows_s = xs.shape[0]
    ts = PROMPT_BLOCK
    n_prompt = S // ts
    d_pool = hist_in.shape[-1]
    n_state = p["a1_re"].shape[-1]
    P = h0_re.shape[-1]
    G = n_state // P
    assert n_streams * (HIST_ROWS + Ts) <= HIST_ROWS + ts and rows_s <= ts
    args = dict(p, xp=xp, xs=xs, hist_in=hist_in, h0_re=h0_re, h0_im=h0_im)
    ins = [args[n] for n in _MIXER_IN]
    in_specs = [_prompt_spec(ts, D, n_prompt)] + [_weight_spec(a.shape) for a in ins[1:]]
    n_half, nh, H = p["bb_re"].shape
    d_half = p["cc_re"].shape[1]
    gen_shapes = ([((n_half, d_half, 2 * nh), BF16), ((n_half, 2 * nh, d_half), BF16)]
                  + [((PROMPT_CHUNK, n_state), F32)] * 4)
    out_shapes = dict(
        outp=(S, D), outs=(rows_s, D), histp=(HIST_ROWS, d_pool), hrep=(G, P), himp=(G, P),
        hists=(POOL_HIST, n_streams, d_pool), hres=(n_streams * G, P), hims=(n_streams * G, P))
    out_specs = [_prompt_spec(ts, D, n_prompt)] + [_const_spec(out_shapes[n]) for n in _MIXER_OUT[1:]]
    return pl.pallas_call(
        functools.partial(_mixer_kernel, n_prompt=n_prompt, n_streams=n_streams, Tp=PROMPT_CHUNK, Ts=Ts),
        grid=(n_prompt + 1,),
        in_specs=in_specs,
        out_specs=out_specs,
        out_shape=[jax.ShapeDtypeStruct(out_shapes[n], F32) for n in _MIXER_OUT],
        scratch_shapes=[pltpu.VMEM((HIST_ROWS + ts, d_pool), F32),
                        pltpu.VMEM((rows_s, d_pool), F32)]
        + [pltpu.VMEM((1, n_state), F32)] * 2 + [pltpu.VMEM((n_streams, n_state), F32)] * 4
        + [pltpu.VMEM(args[n].shape, BF16) for n in _MIXER_BF16]
        + [pltpu.VMEM(shape, dtype) for shape, dtype in gen_shapes],
        compiler_params=pltpu.CompilerParams(dimension_semantics=("arbitrary",),
                                             vmem_limit_bytes=VMEM_LIMIT),
        name="mixer",
    )(*ins)


def _attend(q_heads, k_heads, v_heads, scale):
    outs = []
    for q, k, v in zip(q_heads, k_heads, v_heads):
        s = lax.dot_general(q, k.astype(BF16), (((1,), (1,)), ((), ())), preferred_element_type=F32) * scale
        e = jnp.exp(s - jnp.max(s, axis=-1, keepdims=True))
        outs.append(_dot(e, v) / jnp.sum(e, axis=-1, keepdims=True))
    return outs


_XATTN_IN = ("xp", "xs", "mem", "kc", "vc", "g_mem", "w_k", "w_v", "g_xattn", "w_q", "w_o")
_XATTN_OUT = ("outp", "outs", "mk", "mv")


def _xattn_kernel(*refs, n_prompt, n_streams, Ts, hpu, scale):
    r = dict(zip(_XATTN_IN + _XATTN_OUT, refs))
    kp_ref, vp_ref, qs_ref, os_ref, kbuf, vbuf, sem, kvf, osem = refs[len(_XATTN_IN) + len(_XATTN_OUT):]
    i = pl.program_id(0)
    hd = kp_ref.shape[-1] // MEM_HEADS
    ups = MEM_HEADS // hpu

    def unit_copies(unit, hg):
        b, slot = unit // ups, lax.rem(unit, 2)
        return [pltpu.make_async_copy(src.at[0, b, :, hg * hpu + e, :], dst.at[slot, e], sem.at[slot, j * hpu + e])
                for j, (src, dst) in enumerate(((r["kc"], kbuf), (r["vc"], vbuf))) for e in range(hpu)]

    def kv_out_copies():
        return [pltpu.make_async_copy(kvf.at[j, h], dst.at[0, 0, :, h, :], osem.at[j, h])
                for j, dst in enumerate((r["mk"], r["mv"])) for h in range(MEM_HEADS)]

    def for_unit(unit, fn):
        for hg in range(ups):
            @pl.when(lax.rem(unit, ups) == hg)
            def _():
                for cp in unit_copies(unit, hg):
                    fn(cp)

    @pl.when(i == 0)
    def _():
        for_unit(i, lambda cp: cp.start())
        mem = r["mem"][...]
        inv = _inv_rms(mem)
        mg = (mem * r["g_mem"][...]).astype(BF16)
        k = _dot(mg, r["w_k"][...]) * inv
        v = _dot(mg, r["w_v"][...]) * inv
        for h in range(MEM_HEADS):
            kvf[0, h] = k[:, h * hd:(h + 1) * hd]
            kvf[1, h] = v[:, h * hd:(h + 1) * hd]
        for cp in kv_out_copies():
            cp.start()
        kp_ref[...] = k.astype(BF16)
        vp_ref[...] = v.astype(BF16)
        q = _norm_dot(r["xs"][...], r["g_xattn"][...], r["w_q"][...]).astype(BF16)
        for h in range(MEM_HEADS):
            qs_ref[h] = q[:, h * hd:(h + 1) * hd]

    @pl.when(i < n_prompt)
    def _():
        @pl.when(i + 1 < n_prompt)
        def _():
            for_unit(i + 1, lambda cp: cp.start())

        for_unit(i, lambda cp: cp.wait())

        x = r["xp"][...]
        q = _norm_dot(x, r["g_xattn"][...], r["w_q"][...]).astype(BF16)
        o = _attend([q[:, h * hd:(h + 1) * hd] for h in range(MEM_HEADS)],
                    [kp_ref[:, h * hd:(h + 1) * hd] for h in range(MEM_HEADS)],
                    [vp_ref[:, h * hd:(h + 1) * hd] for h in range(MEM_HEADS)], scale)
        r["outp"][...] = x + _dot(jnp.concatenate(o, axis=-1), r["w_o"][...])

        b, hg, slot = i // ups, lax.rem(i, ups), lax.rem(i, 2)
        rows = pl.ds(pl.multiple_of(b * Ts, Ts), Ts)
        heads = [hg * hpu + e for e in range(hpu)]
        o_s = _attend([qs_ref[h, rows, :] for h in heads], [kbuf[slot, e] for e in range(hpu)],
                      [vbuf[slot, e] for e in range(hpu)], scale)
        for h, o_h in zip(heads, o_s):
            os_ref[h, rows, :] = o_h.astype(BF16)

    @pl.when(i == n_prompt)
    def _():
        for cp in kv_out_copies():
            cp.wait()
        o = jnp.concatenate([os_ref[h] for h in range(MEM_HEADS)], axis=-1)
        r["outs"][...] = r["xs"][...] + _dot(o, r["w_o"][...])


def _xattn(xp, xs, mem, kc, vc, p, n_streams, Ts):
    S, D = xp.shape
    rows_s = xs.shape[0]
    ts = XATTN_BLOCK
    n_prompt = S // ts
    n_mem = mem.shape[0]
    hd = D // MEM_HEADS
    hpu = n_streams * MEM_HEADS // n_prompt
    assert hpu * n_prompt == n_streams * MEM_HEADS and MEM_HEADS % hpu == 0
    args = dict(p, xp=xp, xs=xs, mem=mem, kc=kc, vc=vc)
    ins = [args[n] for n in _XATTN_IN]
    cache_spec = pl.BlockSpec(memory_space=pl.ANY)
    in_specs = [_prompt_spec(ts, D, n_prompt)]
    for n in _XATTN_IN[1:]:
        in_specs.append(cache_spec if n in ("kc", "vc") else _weight_spec(args[n].shape))
    out_shapes = dict(outp=(S, D), outs=(rows_s, D), mk=(1, 1, n_mem, MEM_HEADS, hd), mv=(1, 1, n_mem, MEM_HEADS, hd))
    out_specs = [_prompt_spec(ts, D, n_prompt), _const_spec(out_shapes["outs"]), cache_spec, cache_spec]
    return pl.pallas_call(
        functools.partial(_xattn_kernel, n_prompt=n_prompt, n_streams=n_streams, Ts=Ts, hpu=hpu, scale=hd ** -0.5),
        grid=(n_prompt + 1,),
        in_specs=in_specs,
        out_specs=out_specs,
        out_shape=[jax.ShapeDtypeStruct(out_shapes[n], F32) for n in _XATTN_OUT],
        scratch_shapes=[pltpu.VMEM((n_mem, D), BF16), pltpu.VMEM((n_mem, D), BF16),
                        pltpu.VMEM((MEM_HEADS, rows_s, hd), BF16), pltpu.VMEM((MEM_HEADS, rows_s, hd), BF16),
                        pltpu.VMEM((2, hpu, n_mem, hd), F32), pltpu.VMEM((2, hpu, n_mem, hd), F32),
                        pltpu.SemaphoreType.DMA((2, 2 * hpu)),
                        pltpu.VMEM((2, MEM_HEADS, n_mem, hd), F32), pltpu.SemaphoreType.DMA((2, MEM_HEADS))],
        compiler_params=pltpu.CompilerParams(dimension_semantics=("arbitrary",),
                                             vmem_limit_bytes=VMEM_LIMIT),
        name="xattn",
    )(*ins)


def _mlp_rows(x, g_ref, wup_ref, wdown_ref, gfin_ref):
    hg = (x * g_ref[...]).astype(BF16)
    acc = None
    for j in range(wup_ref.shape[1] // FF_CHUNK):
        cols = slice(j * FF_CHUNK, (j + 1) * FF_CHUNK)
        up = _dot(hg, wup_ref[:, cols])
        part = _dot(jnp.square(jnp.maximum(up, 0.0)), wdown_ref[cols, :])
        acc = part if acc is None else acc + part
    return _rmsnorm(x + acc * jnp.square(_inv_rms(x)), gfin_ref[...])


def _mlp_kernel(xp_ref, xs_ref, g_ref, wup_ref, wdown_ref, gfin_ref, outp_ref, outs_ref, *, n_prompt):
    i = pl.program_id(0)

    @pl.when(i < n_prompt)
    def _():
        outp_ref[...] = _mlp_rows(xp_ref[...], g_ref, wup_ref, wdown_ref, gfin_ref)

    @pl.when(i == n_prompt)
    def _():
        outs_ref[...] = _mlp_rows(xs_ref[...], g_ref, wup_ref, wdown_ref, gfin_ref)


def _mlp(xp, xs, p):
    S, D = xp.shape
    ts = PROMPT_BLOCK
    n_prompt = S // ts
    weights = (p["g_mlp"], p["w_up"], p["w_down"], p["g_final"])
    return pl.pallas_call(
        functools.partial(_mlp_kernel, n_prompt=n_prompt),
        grid=(n_prompt + 1,),
        in_specs=[_prompt_spec(ts, D, n_prompt), _weight_spec(xs.shape)] + [_weight_spec(w.shape) for w in weights],
        out_specs=[_prompt_spec(ts, D, n_prompt), _const_spec(xs.shape)],
        out_shape=[jax.ShapeDtypeStruct(xp.shape, F32), jax.ShapeDtypeStruct(xs.shape, F32)],
        compiler_params=pltpu.CompilerParams(dimension_semantics=("arbitrary",),
                                             vmem_limit_bytes=VMEM_LIMIT),
        name="mlp",
    )(xp, xs, *weights)


def _prepare_layer(l, g_mix, w_in, w_pool, pool_scale, a_re, a_im, b_re, b_im, c_re, c_im, d_skip,
                   log_dt, w_glu, b_glu, w_out, g_xattn, g_mem, w_q, w_k, w_v, w_o, g_mlp, w_up,
                   w_down, g_final):
    G, P = a_re.shape[1:]
    gi = G // N_HALF
    ar, ai = a_re[l].astype(F32), a_im[l].astype(F32)
    dt = jnp.exp(log_dt[l].astype(F32))[:, None]
    lam_re, lam_im = ar * dt, ai * dt

    mag = jnp.exp(lam_re)
    ab_re, ab_im = mag * jnp.cos(lam_im), mag * jnp.sin(lam_im)

    den = ar * ar + ai * ai
    coef_re = ((ab_re - 1.0) * ar + ab_im * ai) / den
    coef_im = (ab_im * ar - (ab_re - 1.0) * ai) / den
    br, bi = b_re[l].astype(F32), b_im[l].astype(F32)
    bb_re = coef_re[..., None] * br - coef_im[..., None] * bi
    bb_im = coef_re[..., None] * bi + coef_im[..., None] * br

    half = lambda t: t.astype(F32).reshape((N_HALF, gi * t.shape[1], t.shape[2]))
    row = lambda v: v.astype(F32).reshape(1, -1)
    return dict(
        g_mix=row(g_mix[l]), w_in=w_in[l], w_pool=w_pool[l], pool_scale=row(pool_scale[l]),
        bb_re=half(bb_re), bb_im=half(bb_im), cc_re=half(c_re[l]), cc_im=half(c_im[l]),
        a1_re=row(ab_re), a1_im=row(ab_im),
        d_skip=row(d_skip[l]), w_glu=w_glu[l], b_glu=row(b_glu[l]), w_out=w_out[l],
        g_xattn=row(g_xattn[l]), g_mem=row(g_mem[l]), w_q=w_q[l], w_k=w_k[l], w_v=w_v[l], w_o=w_o[l],
        g_mlp=row(g_mlp[l]), w_up=w_up[l], w_down=w_down[l], g_final=row(g_final))


def kernel(x_prompt, x_sample, cache_mem_k, cache_mem_v, state_pool, state_ssm_re, state_ssm_im, mem_prompt, g_mix, w_in, w_pool, pool_scale, ssm_a_re, ssm_a_im, ssm_b_re, ssm_b_im, ssm_c_re, ssm_c_im, ssm_d, ssm_log_dt, w_glu, b_glu, w_out, g_xattn, g_mem, w_q, w_k, w_v, w_o, g_mlp, w_up, w_down, g_final):
    depth = g_mix.shape[0]
    assert depth == 1 and x_prompt.shape[0] == 1, "single layer, single prompt stream"
    Bp, S, D = x_prompt.shape
    Bs, Ts, _ = x_sample.shape
    n_mem = mem_prompt.shape[1]
    G, P = ssm_a_re.shape[1:]
    assert S % PROMPT_BLOCK == 0 and S % XATTN_BLOCK == 0
    assert PROMPT_BLOCK % PROMPT_CHUNK == 0 and Ts <= PROMPT_CHUNK

    l = 0
    p = _prepare_layer(l, g_mix, w_in, w_pool, pool_scale, ssm_a_re, ssm_a_im, ssm_b_re, ssm_b_im,
                       ssm_c_re, ssm_c_im, ssm_d, ssm_log_dt, w_glu, b_glu, w_out, g_xattn, g_mem,
                       w_q, w_k, w_v, w_o, g_mlp, w_up, w_down, g_final)

    xp1, xs1, hist_p, hre_p, him_p, hist_s, hre_s, him_s = _mixer(
        x_prompt[0], x_sample.reshape(Bs * Ts, D), jnp.transpose(state_pool[l], (1, 0, 2)),
        state_ssm_re[l].reshape(Bs * G, P), state_ssm_im[l].reshape(Bs * G, P), p, Bs, Ts)
    xp2, xs2, mk, mv = _xattn(xp1, xs1, mem_prompt[0], cache_mem_k[l:l + 1], cache_mem_v[l:l + 1], p, Bs, Ts)
    y_prompt, y_sample = _mlp(xp2, xs2, p)

    return (y_prompt[None], y_sample.reshape(Bs, Ts, D), mk, mv,
            hist_p[1:].reshape(1, Bp, POOL_HIST, -1),
            hre_p.reshape(1, Bp, G, P), him_p.reshape(1, Bp, G, P),
            jnp.transpose(hist_s, (1, 0, 2))[None], hre_s.reshape(1, Bs, G, P), him_s.reshape(1, Bs, G, P))
```

```python
import functools

import jax
import jax.numpy as jnp
from jax import lax
from jax.experimental import pallas as pl
from jax.experimental.pallas import tpu as pltpu

F32 = jnp.float32
BF16 = jnp.bfloat16

EPS = 1e-6
PAST_LEN = 1024
POOL_WINDOWS = (2, 4, 8, 16)
POOL_HIST = max(POOL_WINDOWS) - 1
HIST_ROWS = 16
MEM_HEADS = 4
N_HALF = 2
S5_TILE = 256

PROMPT_BLOCK = 512
XATTN_BLOCK = 1024
PROMPT_CHUNK = 128
FF_CHUNK = 2048
VMEM_LIMIT = 56 * 1024 * 1024


def _inv_rms(x):
    return lax.rsqrt(jnp.mean(x * x, axis=-1, keepdims=True) + EPS)


def _rmsnorm(x, g):
    return x * _inv_rms(x) * g


def _dot(a, b):
    return jnp.dot(a.astype(BF16), b.astype(BF16), preferred_element_type=F32)


def _norm_dot(x, g, w):
    return _dot(x * g, w) * _inv_rms(x)


def _pool_windows(zbuf, nrows, pos, r):
    gw = r["w_pool"].shape[-1]
    outs = []
    for g, w in enumerate(POOL_WINDOWS):
        lanes = slice(g * gw, (g + 1) * gw)
        win = zbuf[pl.ds(0, HIST_ROWS + nrows), lanes]
        k = 1
        while k < w:
            win = win + pltpu.roll(win, k, 0)
            k *= 2
        cur = zbuf[pl.ds(HIST_ROWS, nrows), lanes]
        cnt = jnp.minimum(pos + 1, w).astype(F32)
        pooled = win[HIST_ROWS:, :] / cnt - cur
        outs.append(_dot(pooled, r["w_pool"][g]))
    return jnp.concatenate(outs, axis=-1) * r["pool_scale"][...]


def _cmul(ar, ai, br, bi):
    return ar * br - ai * bi, ar * bi + ai * br


def _s5_scan(u, n_chunks, T, h_carry, h_rows, r, hout_re_ref, hout_im_ref):
    hw = r["bblk"].shape[1]
    nh = r["bblk"].shape[2] // 2
    u_bf = u.astype(BF16)
    r_i = lax.broadcasted_iota(jnp.int32, (T, T), 0)
    c_i = lax.broadcasted_iota(jnp.int32, (T, T), 1)
    ltri = jnp.where(r_i >= c_i, 1.0, 0.0).astype(BF16)

    ys = []
    for k in range(N_HALF):
        uk = u_bf[:, k * hw:(k + 1) * hw]
        yk = None
        for j in range(nh // S5_TILE):
            both = slice(2 * j * S5_TILE, 2 * (j + 1) * S5_TILE)
            st = slice(k * nh + j * S5_TILE, k * nh + (j + 1) * S5_TILE)
            bu = _dot(uk, r["bblk"][k, :, both])
            bu_re, bu_im = bu[:, :S5_TILE], bu[:, S5_TILE:]
            if h_rows is None:
                carry = (h_carry[0][:, st], h_carry[1][:, st])
            h_re, h_im = [], []
            for c in range(n_chunks):
                rows = slice(c * T, (c + 1) * T)
                xr, xi = _cmul(bu_re[rows], bu_im[rows], r["pneg_re"][0:T, st], r["pneg_im"][0:T, st])
                w = _dot(ltri, jnp.concatenate([xr.astype(BF16), xi.astype(BF16)], axis=1))
                wr, wi = w[:, :S5_TILE], w[:, S5_TILE:]
                if h_rows is not None:
                    carry = (h_rows[0][pl.ds(c, 1), st], h_rows[1][pl.ds(c, 1), st])
                cr, ci = _cmul(r["a1_re"][:, st], r["a1_im"][:, st], carry[0], carry[1])
                sr, si = _cmul(r["ppos_re"][0:T, st], r["ppos_im"][0:T, st], wr + cr, wi + ci)
                carry = (sr[T - 1:T, :], si[T - 1:T, :])
                if h_rows is not None:
                    hout_re_ref[pl.ds(c, 1), st] = carry[0]
                    hout_im_ref[pl.ds(c, 1), st] = carry[1]
                h_re.append(sr.astype(BF16))
                h_im.append(si.astype(BF16))
            if h_rows is None:
                hout_re_ref[:, st] = carry[0]
                hout_im_ref[:, st] = carry[1]
            h = jnp.concatenate([jnp.concatenate(h_re, axis=0), jnp.concatenate(h_im, axis=0)], axis=1)
            part = _dot(h, r["cblk"][k, both, :])
            yk = part if yk is None else yk + part
        ys.append(yk)
    return jnp.concatenate(ys, axis=-1)


def _mix_tail(x, y_pool, u, y_state, r):
    y = jax.nn.gelu(y_state + r["d_skip"][...] * u)
    y = y * jax.nn.sigmoid(_dot(y, r["w_glu"][...]) + r["b_glu"][...])
    ycat = jnp.concatenate([y_pool.astype(BF16), y.astype(BF16)], axis=-1)
    return x + _dot(ycat, r["w_out"][...])


_MIXER_IN = ("xp", "xs", "hist_in", "h0_re", "h0_im",
             "g_mix", "w_in", "w_pool", "pool_scale", "bb_re", "bb_im", "cc_re", "cc_im", "a1_re", "a1_im",
             "d_skip", "w_glu", "b_glu", "w_out")
_MIXER_OUT = ("outp", "outs", "histp", "hrep", "himp", "hists", "hres", "hims")
_MIXER_BF16 = ("w_in", "w_pool", "w_glu", "w_out")
_MIXER_GEN = ("bblk", "cblk", "pneg_re", "pneg_im", "ppos_re", "ppos_im")


def _power_table(re1, im1, T):
    SUB = 16
    assert T % SUB == 0
    n = re1.shape[-1]
    t = lax.broadcasted_iota(jnp.int32, (SUB, 1), 0)
    lo_re, lo_im = jnp.ones((SUB, n), F32), jnp.zeros((SUB, n), F32)
    cr, ci = re1, im1
    k = 1
    while k < SUB:
        bit = (t & k) != 0
        lo_re, lo_im = _cmul(lo_re, lo_im, jnp.where(bit, cr, 1.0), jnp.where(bit, ci, 0.0))
        cr, ci = _cmul(cr, ci, cr, ci)
        k *= 2
    hi_re, hi_im = [jnp.ones((1, n), F32)], [jnp.zeros((1, n), F32)]
    for _ in range(T // SUB - 1):
        nr, ni = _cmul(hi_re[-1], hi_im[-1], cr, ci)
        hi_re.append(nr)
        hi_im.append(ni)
    rep = lambda rows: jnp.concatenate([jnp.broadcast_to(v, (SUB, n)) for v in rows], axis=0)
    tile = lambda v: jnp.concatenate([v] * (T // SUB), axis=0)
    return _cmul(tile(lo_re), tile(lo_im), rep(hi_re), rep(hi_im))


def _expand_s5_params(r, gen, T):
    are, aim = r["a1_re"][...], r["a1_im"][...]
    gen["ppos_re"][...], gen["ppos_im"][...] = _power_table(are, aim, T)
    den = are * are + aim * aim
    gen["pneg_re"][...], gen["pneg_im"][...] = _power_table(are / den, -aim / den, T)

    n_half, rows_b, H = r["bb_re"].shape
    _, rows_c, P = r["cc_re"].shape
    nh = rows_b

    def spread(x, width):
        w = x.shape[1]
        sel = lax.rem(lax.broadcasted_iota(jnp.int32, (w, width), 1), w) == lax.broadcasted_iota(jnp.int32, (w, width), 0)
        return _dot(x, jnp.where(sel, 1.0, 0.0))

    def same_group(shape, row_size, col_size):
        return (lax.div(lax.broadcasted_iota(jnp.int32, shape, 0), row_size)
                == lax.div(lax.broadcasted_iota(jnp.int32, shape, 1), col_size))

    for k in range(n_half):
        for c, (b_src, c_src, sign) in enumerate(((r["bb_re"], r["cc_re"], 1.0), (r["bb_im"], r["cc_im"], -1.0))):
            yb = spread(b_src[k], rows_c)
            yb = jnp.where(same_group(yb.shape, P, H), yb, 0.0)
            ybt = yb.T.astype(BF16)
            for j in range(nh // S5_TILE):
                gen["bblk"][k, :, (2 * j + c) * S5_TILE:(2 * j + c + 1) * S5_TILE] = ybt[:, j * S5_TILE:(j + 1) * S5_TILE]
            yc = spread(c_src[k], rows_b)
            yc = jnp.where(same_group(yc.shape, H, P), sign * yc, 0.0)
            yct = yc.T.astype(BF16)
            for j in range(nh // S5_TILE):
                gen["cblk"][k, (2 * j + c) * S5_TILE:(2 * j + c + 1) * S5_TILE, :] = yct[j * S5_TILE:(j + 1) * S5_TILE, :]


def _mixer_kernel(*refs, n_prompt, n_streams, Tp, Ts):
    r = dict(zip(_MIXER_IN + _MIXER_OUT, refs))
    zbuf, ypool_ref, hp_re, hp_im, hs_re, hs_im, h0f_re, h0f_im = refs[len(_MIXER_IN) + len(_MIXER_OUT):][:8]
    n_fixed = len(_MIXER_IN) + len(_MIXER_OUT) + 8
    bf16_weights = dict(zip(_MIXER_BF16, refs[n_fixed:]))
    gen = dict(zip(_MIXER_GEN, refs[n_fixed + len(_MIXER_BF16):]))
    i = pl.program_id(0)
    ts = r["xp"].shape[0]
    d_pool = zbuf.shape[1]

    @pl.when(i == 0)
    def _():
        zbuf[0:HIST_ROWS, :] = jnp.zeros((HIST_ROWS, d_pool), F32)
        hp_re[...] = jnp.zeros(hp_re.shape, F32)
        hp_im[...] = jnp.zeros(hp_im.shape, F32)
        for name, ref in bf16_weights.items():
            ref[...] = r[name][...].astype(BF16)
        _expand_s5_params(r, gen, Tp)

    r.update(bf16_weights)
    r.update(gen)

    @pl.when(i < n_prompt)
    def _():
        x = r["xp"][...]
        z = _dot(_rmsnorm(x, r["g_mix"][...]), r["w_in"][...])
        zbuf[HIST_ROWS:HIST_ROWS + ts, :] = z[:, :d_pool]
        u = z[:, d_pool:]
        pos = i * ts + lax.broadcasted_iota(jnp.int32, (ts, 1), 0)
        y_pool = _pool_windows(zbuf, ts, pos, r)
        y_state = _s5_scan(u, ts // Tp, Tp, (hp_re[...], hp_im[...]), None, r, hp_re, hp_im)
        r["outp"][...] = _mix_tail(x, y_pool, u, y_state, r)
        tail = zbuf[ts:ts + HIST_ROWS, :]
        zbuf[0:HIST_ROWS, :] = tail
        r["histp"][...] = tail

    @pl.when(i == n_prompt)
    def _():
        ext = HIST_ROWS + Ts
        x = r["xs"][...]
        z = _dot(_rmsnorm(x, r["g_mix"][...]), r["w_in"][...])
        u = z[:, d_pool:]
        for b in range(n_streams):
            zbuf[b * ext:b * ext + 1, :] = jnp.zeros((1, d_pool), F32)
            for t in range(POOL_HIST):
                zbuf[b * ext + 1 + t:b * ext + 2 + t, :] = r["hist_in"][t, b:b + 1, :]
            zbuf[b * ext + HIST_ROWS:(b + 1) * ext, :] = z[b * Ts:(b + 1) * Ts, :d_pool]
            for t in range(POOL_HIST):
                row = (b + 1) * ext - POOL_HIST + t
                r["hists"][t, b:b + 1, :] = zbuf[row:row + 1, :]
        nrows = n_streams * ext - HIST_ROWS
        ridx = lax.broadcasted_iota(jnp.int32, (nrows, 1), 0)
        pos = PAST_LEN + lax.rem(ridx, ext)
        y_all = _pool_windows(zbuf, nrows, pos, r)
        for b in range(n_streams):
            ypool_ref[b * Ts:(b + 1) * Ts, :] = y_all[b * ext:b * ext + Ts, :]
        G, P = r["hrep"].shape
        for src, dst in ((r["h0_re"], h0f_re), (r["h0_im"], h0f_im)):
            for b in range(n_streams):
                for g in range(G):
                    dst[b:b + 1, g * P:(g + 1) * P] = src[b * G + g:b * G + g + 1, :]
        y_state = _s5_scan(u, n_streams, Ts, None, (h0f_re, h0f_im), r, hs_re, hs_im)
        r["outs"][...] = _mix_tail(x, ypool_ref[...], u, y_state, r)
        for g in range(G):
            cols = slice(g * P, (g + 1) * P)
            r["hrep"][g:g + 1, :] = hp_re[:, cols]
            r["himp"][g:g + 1, :] = hp_im[:, cols]
            for b in range(n_streams):
                r["hres"][b * G + g:b * G + g + 1, :] = hs_re[b:b + 1, cols]
                r["hims"][b * G + g:b * G + g + 1, :] = hs_im[b:b + 1, cols]


def _const_spec(shape):
    nd = len(shape)
    return pl.BlockSpec(shape, lambda *_: (0,) * nd)


def _weight_spec(shape):
    nd = len(shape)
    return pl.BlockSpec(shape, lambda *_: (0,) * nd, pipeline_mode=pl.Buffered(1))


def _prompt_spec(ts, d, n_prompt):
    return pl.BlockSpec((ts, d), lambda i: (jnp.minimum(i, n_prompt - 1), 0))


def _mixer(xp, xs, hist_in, h0_re, h0_im, p, n_streams, Ts):
    S, D = xp.shape
    rows_s = xs.shape[0]
    ts = PROMPT_BLOCK
    n_prompt = S // ts
    d_pool = hist_in.shape[-1]
    n_state = p["a1_re"].shape[-1]
    P = h0_re.shape[-1]
    G = n_state // P
    assert n_streams * (HIST_ROWS + Ts) <= HIST_ROWS + ts and rows_s <= ts
    args = dict(p, xp=xp, xs=xs, hist_in=hist_in, h0_re=h0_re, h0_im=h0_im)
    ins = [args[n] for n in _MIXER_IN]
    in_specs = [_prompt_spec(ts, D, n_prompt)] + [_weight_spec(a.shape) for a in ins[1:]]
    n_half, nh, H = p["bb_re"].shape
    d_half = p["cc_re"].shape[1]
    gen_shapes = ([((n_half, d_half, 2 * nh), BF16), ((n_half, 2 * nh, d_half), BF16)]
                  + [((PROMPT_CHUNK, n_state), F32)] * 4)
    out_shapes = dict(
        outp=(S, D), outs=(rows_s, D), histp=(HIST_ROWS, d_pool), hrep=(G, P), himp=(G, P),
        hists=(POOL_HIST, n_streams, d_pool), hres=(n_streams * G, P), hims=(n_streams * G, P))
    out_specs = [_prompt_spec(ts, D, n_prompt)] + [_const_spec(out_shapes[n]) for n in _MIXER_OUT[1:]]
    return pl.pallas_call(
        functools.partial(_mixer_kernel, n_prompt=n_prompt, n_streams=n_streams, Tp=PROMPT_CHUNK, Ts=Ts),
        grid=(n_prompt + 1,),
        in_specs=in_specs,
        out_specs=out_specs,
        out_shape=[jax.ShapeDtypeStruct(out_shapes[n], F32) for n in _MIXER_OUT],
        scratch_shapes=[pltpu.VMEM((HIST_ROWS + ts, d_pool), F32),
                        pltpu.VMEM((rows_s, d_pool), F32)]
        + [pltpu.VMEM((1, n_state), F32)] * 2 + [pltpu.VMEM((n_streams, n_state), F32)] * 4
        + [pltpu.VMEM(args[n].shape, BF16) for n in _MIXER_BF16]
        + [pltpu.VMEM(shape, dtype) for shape, dtype in gen_shapes],
        compiler_params=pltpu.CompilerParams(dimension_semantics=("arbitrary",),
                                             vmem_limit_bytes=VMEM_LIMIT),
        name="mixer",
    )(*ins)


def _attend(q_heads, k_heads, v_heads, scale):
    outs = []
    for q, k, v in zip(q_heads, k_heads, v_heads):
        s = lax.dot_general(q, k.astype(BF16), (((1,), (1,)), ((), ())), preferred_element_type=F32) * scale
        e = jnp.exp(s - jnp.max(s, axis=-1, keepdims=True))
        outs.append(_dot(e, v) / jnp.sum(e, axis=-1, keepdims=True))
    return outs


_XATTN_IN = ("xp", "xs", "mem", "kc", "vc", "g_mem", "w_k", "w_v", "g_xattn", "w_q", "w_o")
_XATTN_OUT = ("outp", "outs", "mk", "mv")


def _xattn_kernel(*refs, n_prompt, n_streams, Ts, hpu, scale):
    r = dict(zip(_XATTN_IN + _XATTN_OUT, refs))
    kp_ref, vp_ref, qs_ref, os_ref, kbuf, vbuf, sem, kvf, osem = refs[len(_XATTN_IN) + len(_XATTN_OUT):]
    i = pl.program_id(0)
    hd = kp_ref.shape[-1] // MEM_HEADS
    ups = MEM_HEADS // hpu

    def unit_copies(unit, hg):
        b, slot = unit // ups, lax.rem(unit, 2)
        return [pltpu.make_async_copy(src.at[0, b, :, hg * hpu + e, :], dst.at[slot, e], sem.at[slot, j * hpu + e])
                for j, (src, dst) in enumerate(((r["kc"], kbuf), (r["vc"], vbuf))) for e in range(hpu)]

    def kv_out_copies():
        return [pltpu.make_async_copy(kvf.at[j, h], dst.at[0, 0, :, h, :], osem.at[j, h])
                for j, dst in enumerate((r["mk"], r["mv"])) for h in range(MEM_HEADS)]

    def for_unit(unit, fn):
        for hg in range(ups):
            @pl.when(lax.rem(unit, ups) == hg)
            def _():
                for cp in unit_copies(unit, hg):
                    fn(cp)

    @pl.when(i == 0)
    def _():
        for_unit(i, lambda cp: cp.start())
        mem = r["mem"][...]
        inv = _inv_rms(mem)
        mg = (mem * r["g_mem"][...]).astype(BF16)
        k = _dot(mg, r["w_k"][...]) * inv
        v = _dot(mg, r["w_v"][...]) * inv
        for h in range(MEM_HEADS):
            kvf[0, h] = k[:, h * hd:(h + 1) * hd]
            kvf[1, h] = v[:, h * hd:(h + 1) * hd]
        for cp in kv_out_copies():
            cp.start()
        kp_ref[...] = k.astype(BF16)
        vp_ref[...] = v.astype(BF16)
        q = _norm_dot(r["xs"][...], r["g_xattn"][...], r["w_q"][...]).astype(BF16)
        for h in range(MEM_HEADS):
            qs_ref[h] = q[:, h * hd:(h + 1) * hd]

    @pl.when(i < n_prompt)
    def _():
        @pl.when(i + 1 < n_prompt)
        def _():
            for_unit(i + 1, lambda cp: cp.start())

        for_unit(i, lambda cp: cp.wait())

        x = r["xp"][...]
        q = _norm_dot(x, r["g_xattn"][...], r["w_q"][...]).astype(BF16)
        o = _attend([q[:, h * hd:(h + 1) * hd] for h in range(MEM_HEADS)],
                    [kp_ref[:, h * hd:(h + 1) * hd] for h in range(MEM_HEADS)],
                    [vp_ref[:, h * hd:(h + 1) * hd] for h in range(MEM_HEADS)], scale)
        r["outp"][...] = x + _dot(jnp.concatenate(o, axis=-1), r["w_o"][...])

        b, hg, slot = i // ups, lax.rem(i, ups), lax.rem(i, 2)
        rows = pl.ds(pl.multiple_of(b * Ts, Ts), Ts)
        heads = [hg * hpu + e for e in range(hpu)]
        o_s = _attend([qs_ref[h, rows, :] for h in heads], [kbuf[slot, e] for e in range(hpu)],
                      [vbuf[slot, e] for e in range(hpu)], scale)
        for h, o_h in zip(heads, o_s):
            os_ref[h, rows, :] = o_h.astype(BF16)

    @pl.when(i == n_prompt)
    def _():
        for cp in kv_out_copies():
            cp.wait()
        o = jnp.concatenate([os_ref[h] for h in range(MEM_HEADS)], axis=-1)
        r["outs"][...] = r["xs"][...] + _dot(o, r["w_o"][...])


def _xattn(xp, xs, mem, kc, vc, p, n_streams, Ts):
    S, D = xp.shape
    rows_s = xs.shape[0]
    ts = XATTN_BLOCK
    n_prompt = S // ts
    n_mem = mem.shape[0]
    hd = D // MEM_HEADS
    hpu = n_streams * MEM_HEADS // n_prompt
    assert hpu * n_prompt == n_streams * MEM_HEADS and MEM_HEADS % hpu == 0
    args = dict(p, xp=xp, xs=xs, mem=mem, kc=kc, vc=vc)
    ins = [args[n] for n in _XATTN_IN]
    cache_spec = pl.BlockSpec(memory_space=pl.ANY)
    in_specs = [_prompt_spec(ts, D, n_prompt)]
    for n in _XATTN_IN[1:]:
        in_specs.append(cache_spec if n in ("kc", "vc") else _weight_spec(args[n].shape))
    out_shapes = dict(outp=(S, D), outs=(rows_s, D), mk=(1, 1, n_mem, MEM_HEADS, hd), mv=(1, 1, n_mem, MEM_HEADS, hd))
    out_specs = [_prompt_spec(ts, D, n_prompt), _const_spec(out_shapes["outs"]), cache_spec, cache_spec]
    return pl.pallas_call(
        functools.partial(_xattn_kernel, n_prompt=n_prompt, n_streams=n_streams, Ts=Ts, hpu=hpu, scale=hd ** -0.5),
        grid=(n_prompt + 1,),
        in_specs=in_specs,
        out_specs=out_specs,
        out_shape=[jax.ShapeDtypeStruct(out_shapes[n], F32) for n in _XATTN_OUT],
        scratch_shapes=[pltpu.VMEM((n_mem, D), BF16), pltpu.VMEM((n_mem, D), BF16),
                        pltpu.VMEM((MEM_HEADS, rows_s, hd), BF16), pltpu.VMEM((MEM_HEADS, rows_s, hd), BF16),
                        pltpu.VMEM((2, hpu, n_mem, hd), F32), pltpu.VMEM((2, hpu, n_mem, hd), F32),
                        pltpu.SemaphoreType.DMA((2, 2 * hpu)),
                        pltpu.VMEM((2, MEM_HEADS, n_mem, hd), F32), pltpu.SemaphoreType.DMA((2, MEM_HEADS))],
        compiler_params=pltpu.CompilerParams(dimension_semantics=("arbitrary",),
                                             vmem_limit_bytes=VMEM_LIMIT),
        name="xattn",
    )(*ins)


def _mlp_rows(x, g_ref, wup_ref, wdown_ref, gfin_ref):
    hg = (x * g_ref[...]).astype(BF16)
    acc = None
    for j in range(wup_ref.shape[1] // FF_CHUNK):
        cols = slice(j * FF_CHUNK, (j + 1) * FF_CHUNK)
        up = _dot(hg, wup_ref[:, cols])
        part = _dot(jnp.square(jnp.maximum(up, 0.0)), wdown_ref[cols, :])
        acc = part if acc is None else acc + part
    return _rmsnorm(x + acc * jnp.square(_inv_rms(x)), gfin_ref[...])


def _mlp_kernel(xp_ref, xs_ref, g_ref, wup_ref, wdown_ref, gfin_ref, outp_ref, outs_ref, *, n_prompt):
    i = pl.program_id(0)

    @pl.when(i < n_prompt)
    def _():
        outp_ref[...] = _mlp_rows(xp_ref[...], g_ref, wup_ref, wdown_ref, gfin_ref)

    @pl.when(i == n_prompt)
    def _():
        outs_ref[...] = _mlp_rows(xs_ref[...], g_ref, wup_ref, wdown_ref, gfin_ref)


def _mlp(xp, xs, p):
    S, D = xp.shape
    ts = PROMPT_BLOCK
    n_prompt = S // ts
    weights = (p["g_mlp"], p["w_up"], p["w_down"], p["g_final"])
    return pl.pallas_call(
        functools.partial(_mlp_kernel, n_prompt=n_prompt),
        grid=(n_prompt + 1,),
        in_specs=[_prompt_spec(ts, D, n_prompt), _weight_spec(xs.shape)] + [_weight_spec(w.shape) for w in weights],
        out_specs=[_prompt_spec(ts, D, n_prompt), _const_spec(xs.shape)],
        out_shape=[jax.ShapeDtypeStruct(xp.shape, F32), jax.ShapeDtypeStruct(xs.shape, F32)],
        compiler_params=pltpu.CompilerParams(dimension_semantics=("arbitrary",),
                                             vmem_limit_bytes=VMEM_LIMIT),
        name="mlp",
    )(xp, xs, *weights)


def _prepare_layer(l, g_mix, w_in, w_pool, pool_scale, a_re, a_im, b_re, b_im, c_re, c_im, d_skip,
                   log_dt, w_glu, b_glu, w_out, g_xattn, g_mem, w_q, w_k, w_v, w_o, g_mlp, w_up,
                   w_down, g_final):
    G, P = a_re.shape[1:]
    gi = G // N_HALF
    ar, ai = a_re[l].astype(F32), a_im[l].astype(F32)
    dt = jnp.exp(log_dt[l].astype(F32))[:, None]
    lam_re, lam_im = ar * dt, ai * dt

    mag = jnp.exp(lam_re)
    ab_re, ab_im = mag * jnp.cos(lam_im), mag * jnp.sin(lam_im)

    den = ar * ar + ai * ai
    coef_re = ((ab_re - 1.0) * ar + ab_im * ai) / den
    coef_im = (ab_im * ar - (ab_re - 1.0) * ai) / den
    br, bi = b_re[l].astype(F32), b_im[l].astype(F32)
    bb_re = coef_re[..., None] * br - coef_im[..., None] * bi
    bb_im = coef_re[..., None] * bi + coef_im[..., None] * br

    half = lambda t: t.astype(F32).reshape((N_HALF, gi * t.shape[1], t.shape[2]))
    row = lambda v: v.astype(F32).reshape(1, -1)
    return dict(
        g_mix=row(g_mix[l]), w_in=w_in[l], w_pool=w_pool[l], pool_scale=row(pool_scale[l]),
        bb_re=half(bb_re), bb_im=half(bb_im), cc_re=half(c_re[l]), cc_im=half(c_im[l]),
        a1_re=row(ab_re), a1_im=row(ab_im),
        d_skip=row(d_skip[l]), w_glu=w_glu[l], b_glu=row(b_glu[l]), w_out=w_out[l],
        g_xattn=row(g_xattn[l]), g_mem=row(g_mem[l]), w_q=w_q[l], w_k=w_k[l], w_v=w_v[l], w_o=w_o[l],
        g_mlp=row(g_mlp[l]), w_up=w_up[l], w_down=w_down[l], g_final=row(g_final))


def kernel(x_prompt, x_sample, cache_mem_k, cache_mem_v, state_pool, state_ssm_re, state_ssm_im, mem_prompt, g_mix, w_in, w_pool, pool_scale, ssm_a_re, ssm_a_im, ssm_b_re, ssm_b_im, ssm_c_re, ssm_c_im, ssm_d, ssm_log_dt, w_glu, b_glu, w_out, g_xattn, g_mem, w_q, w_k, w_v, w_o, g_mlp, w_up, w_down, g_final):
    depth = g_mix.shape[0]
    assert depth == 1 and x_prompt.shape[0] == 1, "single layer, single prompt stream"
    Bp, S, D = x_prompt.shape
    Bs, Ts, _ = x_sample.shape
    n_mem = mem_prompt.shape[1]
    G, P = ssm_a_re.shape[1:]
    assert S % PROMPT_BLOCK == 0 and S % XATTN_BLOCK == 0
    assert PROMPT_BLOCK % PROMPT_CHUNK == 0 and Ts <= PROMPT_CHUNK

    l = 0
    p = _prepare_layer(l, g_mix, w_in, w_pool, pool_scale, ssm_a_re, ssm_a_im, ssm_b_re, ssm_b_im,
                       ssm_c_re, ssm_c_im, ssm_d, ssm_log_dt, w_glu, b_glu, w_out, g_xattn, g_mem,
                       w_q, w_k, w_v, w_o, g_mlp, w_up, w_down, g_final)

    xp1, xs1, hist_p, hre_p, him_p, hist_s, hre_s, him_s = _mixer(
        x_prompt[0], x_sample.reshape(Bs * Ts, D), jnp.transpose(state_pool[l], (1, 0, 2)),
        state_ssm_re[l].reshape(Bs * G, P), state_ssm_im[l].reshape(Bs * G, P), p, Bs, Ts)
    xp2, xs2, mk, mv = _xattn(xp1, xs1, mem_prompt[0], cache_mem_k[l:l + 1], cache_mem_v[l:l + 1], p, Bs, Ts)
    y_prompt, y_sample = _mlp(xp2, xs2, p)

    return (y_prompt[None], y_sample.reshape(Bs, Ts, D), mk, mv,
            hist_p[1:].reshape(1, Bp, POOL_HIST, -1),
            hre_p.reshape(1, Bp, G, P), him_p.reshape(1, Bp, G, P),
            jnp.transpose(hist_s, (1, 0, 2))[None], hre_s.reshape(1, Bs, G, P), him_s.reshape(1, Bs, G, P))
```

```python
import functools

import jax
import jax.numpy as jnp
from jax import lax
from jax.experimental import pallas as pl
from jax.experimental.pallas import tpu as pltpu

F32 = jnp.float32
BF16 = jnp.bfloat16

EPS = 1e-6
PAST_LEN = 1024
POOL_WINDOWS = (2, 4, 8, 16)
POOL_HIST = max(POOL_WINDOWS) - 1
HIST_ROWS = 16
MEM_HEADS = 4
N_HALF = 2
S5_TILE = 256

PROMPT_BLOCK = 512
XATTN_BLOCK = 1024
PROMPT_CHUNK = 128
FF_CHUNK = 2048
VMEM_LIMIT = 56 * 1024 * 1024


def _inv_rms(x):
    return lax.rsqrt(jnp.mean(x * x, axis=-1, keepdims=True) + EPS)


def _rmsnorm(x, g):
    return x * _inv_rms(x) * g


def _dot(a, b):
    return jnp.dot(a.astype(BF16), b.astype(BF16), preferred_element_type=F32)


def _norm_dot(x, g, w):
    return _dot(x * g, w) * _inv_rms(x)


def _pool_windows(zbuf, nrows, pos, r):
    gw = r["w_pool"].shape[-1]
    outs = []
    for g, w in enumerate(POOL_WINDOWS):
        lanes = slice(g * gw, (g + 1) * gw)
        win = zbuf[pl.ds(0, HIST_ROWS + nrows), lanes]
        k = 1
        while k < w:
            win = win + pltpu.roll(win, k, 0)
            k *= 2
        cur = zbuf[pl.ds(HIST_ROWS, nrows), lanes]
        cnt = jnp.minimum(pos + 1, w).astype(F32)
        pooled = win[HIST_ROWS:, :] / cnt - cur
        outs.append(_dot(pooled, r["w_pool"][g]))
    return jnp.concatenate(outs, axis=-1) * r["pool_scale"][...]


def _cmul(ar, ai, br, bi):
    return ar * br - ai * bi, ar * bi + ai * br


def _s5_scan(u, n_chunks, T, tag, h_carry, h_rows, r, hout_re_ref, hout_im_ref):
    neg_re, neg_im, pos_re, pos_im = (r[tag + n] for n in ("neg_re", "neg_im", "pos_re", "pos_im"))
    ac_row = "ps".index(tag)
    hw = r["bblk"].shape[1]
    nh = r["bblk"].shape[2] // 2
    u_bf = u.astype(BF16)
    r_i = lax.broadcasted_iota(jnp.int32, (T, T), 0)
    c_i = lax.broadcasted_iota(jnp.int32, (T, T), 1)
    ltri = jnp.where(r_i >= c_i, 1.0, 0.0).astype(BF16)

    ys = []
    for k in range(N_HALF):
        uk = u_bf[:, k * hw:(k + 1) * hw]
        yk = None
        for j in range(nh // S5_TILE):
            re_c = slice(j * S5_TILE, (j + 1) * S5_TILE)
            im_c = slice(nh + j * S5_TILE, nh + (j + 1) * S5_TILE)
            st = slice(k * nh + j * S5_TILE, k * nh + (j + 1) * S5_TILE)
            bu_re = _dot(uk, r["bblk"][k, :, re_c])
            bu_im = _dot(uk, r["bblk"][k, :, im_c])
            if h_rows is None:
                carry = (h_carry[0][:, st], h_carry[1][:, st])
            h_re, h_im = [], []
            for c in range(n_chunks):
                rows = slice(c * T, (c + 1) * T)
                xr, xi = _cmul(bu_re[rows], bu_im[rows], neg_re[:, st], neg_im[:, st])
                wr = _dot(ltri, xr)
                wi = _dot(ltri, xi)
                if h_rows is not None:
                    carry = (h_rows[0][pl.ds(c, 1), st], h_rows[1][pl.ds(c, 1), st])
                cr, ci = _cmul(r["ac_re"][ac_row:ac_row + 1, st], r["ac_im"][ac_row:ac_row + 1, st], carry[0], carry[1])
                sr, si = _cmul(pos_re[:, st], pos_im[:, st], wr + cr, wi + ci)
                carry = (sr[T - 1:T, :], si[T - 1:T, :])
                if h_rows is not None:
                    hout_re_ref[pl.ds(c, 1), st] = carry[0]
                    hout_im_ref[pl.ds(c, 1), st] = carry[1]
                h_re.append(sr)
                h_im.append(si)
            if h_rows is None:
                hout_re_ref[:, st] = carry[0]
                hout_im_ref[:, st] = carry[1]
            part = (_dot(jnp.concatenate(h_re, axis=0), r["cblk"][k, re_c, :])
                    + _dot(jnp.concatenate(h_im, axis=0), r["cblk"][k, im_c, :]))
            yk = part if yk is None else yk + part
        ys.append(yk)
    return jnp.concatenate(ys, axis=-1)


def _mix_tail(x, y_pool, u, y_state, r):
    y = jax.nn.gelu(y_state + r["d_skip"][...] * u)
    y = y * jax.nn.sigmoid(_dot(y, r["w_glu"][...]) + r["b_glu"][...])
    ycat = jnp.concatenate([y_pool.astype(BF16), y.astype(BF16)], axis=-1)
    return x + _dot(ycat, r["w_out"][...])


_MIXER_IN = ("xp", "xs", "hist_in", "h0_re", "h0_im",
             "g_mix", "w_in", "w_pool", "pool_scale", "bb_re", "bb_im", "cc_re", "cc_im", "a1_re", "a1_im",
             "d_skip", "w_glu", "b_glu", "w_out")
_MIXER_OUT = ("outp", "outs", "histp", "hrep", "himp", "hists", "hres", "hims")
_MIXER_BF16 = ("w_in", "w_pool", "w_glu", "w_out")
_MIXER_GEN = ("bblk", "cblk", "pneg_re", "pneg_im", "ppos_re", "ppos_im",
              "sneg_re", "sneg_im", "spos_re", "spos_im", "ac_re", "ac_im")


def _power_table(re1, im1, T, descending=False):
    SUB = min(16, T)
    assert T % SUB == 0
    n = re1.shape[-1]
    t = lax.broadcasted_iota(jnp.int32, (SUB, 1), 0)
    if descending:
        t = (SUB - 1) - t
    lo_re, lo_im = jnp.ones((SUB, n), F32), jnp.zeros((SUB, n), F32)
    cr, ci = re1, im1
    k = 1
    while k < SUB:
        bit = (t & k) != 0
        lo_re, lo_im = _cmul(lo_re, lo_im, jnp.where(bit, cr, 1.0), jnp.where(bit, ci, 0.0))
        cr, ci = _cmul(cr, ci, cr, ci)
        k *= 2
    hi_re, hi_im = [jnp.ones((1, n), F32)], [jnp.zeros((1, n), F32)]
    for _ in range(T // SUB - 1):
        nr, ni = _cmul(hi_re[-1], hi_im[-1], cr, ci)
        hi_re.append(nr)
        hi_im.append(ni)
    if descending:
        hi_re, hi_im = hi_re[::-1], hi_im[::-1]
    rep = lambda rows: jnp.concatenate([jnp.broadcast_to(v, (SUB, n)) for v in rows], axis=0)
    tile = lambda v: jnp.concatenate([v] * (T // SUB), axis=0)
    return _cmul(tile(lo_re), tile(lo_im), rep(hi_re), rep(hi_im))


def _centred_tables(are, aim, T):
    m = T // 2
    den = are * are + aim * aim
    ire, iim = are / den, -aim / den
    up_re, up_im = _power_table(are, aim, T - m)
    dn_re, dn_im = _power_table(are, aim, m, descending=True)
    dn_re, dn_im = _cmul(dn_re, dn_im, are, aim)
    iu_re, iu_im = _power_table(ire, iim, T - m)
    id_re, id_im = _power_table(ire, iim, m, descending=True)
    id_re, id_im = _cmul(id_re, id_im, ire, iim)
    neg = (jnp.concatenate([dn_re, iu_re], axis=0), jnp.concatenate([dn_im, iu_im], axis=0))
    pos = (jnp.concatenate([id_re, up_re], axis=0), jnp.concatenate([id_im, up_im], axis=0))
    return neg, pos, _cmul(dn_re[0:1, :], dn_im[0:1, :], are, aim)


def _expand_s5_params(r, gen, chunk_lengths):
    are, aim = r["a1_re"][...], r["a1_im"][...]
    for row, (tag, T) in enumerate(zip("ps", chunk_lengths)):
        neg, pos, carry = _centred_tables(are, aim, T)
        gen[tag + "neg_re"][...], gen[tag + "neg_im"][...] = neg
        gen[tag + "pos_re"][...], gen[tag + "pos_im"][...] = pos
        gen["ac_re"][row:row + 1, :], gen["ac_im"][row:row + 1, :] = carry

    n_half, rows_b, H = r["bb_re"].shape
    _, rows_c, P = r["cc_re"].shape
    nh = rows_b

    def spread(x, width):
        w = x.shape[1]
        sel = lax.rem(lax.broadcasted_iota(jnp.int32, (w, width), 1), w) == lax.broadcasted_iota(jnp.int32, (w, width), 0)
        return _dot(x, jnp.where(sel, 1.0, 0.0))

    def same_group(shape, row_size, col_size):
        return (lax.div(lax.broadcasted_iota(jnp.int32, shape, 0), row_size)
                == lax.div(lax.broadcasted_iota(jnp.int32, shape, 1), col_size))

    for k in range(n_half):
        for c, (b_src, c_src, sign) in enumerate(((r["bb_re"], r["cc_re"], 1.0), (r["bb_im"], r["cc_im"], -1.0))):
            yb = spread(b_src[k], rows_c)
            yb = jnp.where(same_group(yb.shape, P, H), yb, 0.0)
            gen["bblk"][k, :, c * nh:(c + 1) * nh] = yb.T.astype(BF16)
            yc = spread(c_src[k], rows_b)
            yc = jnp.where(same_group(yc.shape, H, P), sign * yc, 0.0)
            gen["cblk"][k, c * nh:(c + 1) * nh, :] = yc.T.astype(BF16)


def _mixer_kernel(*refs, n_prompt, n_streams, Tp, Ts):
    r = dict(zip(_MIXER_IN + _MIXER_OUT, refs))
    zbuf, ypool_ref, hp_re, hp_im, hs_re, hs_im, h0f_re, h0f_im = refs[len(_MIXER_IN) + len(_MIXER_OUT):][:8]
    n_fixed = len(_MIXER_IN) + len(_MIXER_OUT) + 8
    bf16_weights = dict(zip(_MIXER_BF16, refs[n_fixed:]))
    gen = dict(zip(_MIXER_GEN, refs[n_fixed + len(_MIXER_BF16):]))
    i = pl.program_id(0)
    ts = r["xp"].shape[0]
    d_pool = zbuf.shape[1]

    @pl.when(i == 0)
    def _():
        zbuf[0:HIST_ROWS, :] = jnp.zeros((HIST_ROWS, d_pool), F32)
        hp_re[...] = jnp.zeros(hp_re.shape, F32)
        hp_im[...] = jnp.zeros(hp_im.shape, F32)
        for name, ref in bf16_weights.items():
            ref[...] = r[name][...].astype(BF16)
        _expand_s5_params(r, gen, (Tp, Ts))

    r.update(bf16_weights)
    r.update(gen)

    @pl.when(i < n_prompt)
    def _():
        x = r["xp"][...]
        z = _dot(_rmsnorm(x, r["g_mix"][...]), r["w_in"][...])
        zbuf[HIST_ROWS:HIST_ROWS + ts, :] = z[:, :d_pool]
        u = z[:, d_pool:]
        pos = i * ts + lax.broadcasted_iota(jnp.int32, (ts, 1), 0)
        y_pool = _pool_windows(zbuf, ts, pos, r)
        y_state = _s5_scan(u, ts // Tp, Tp, "p", (hp_re[...], hp_im[...]), None, r, hp_re, hp_im)
        r["outp"][...] = _mix_tail(x, y_pool, u, y_state, r)
        tail = zbuf[ts:ts + HIST_ROWS, :]
        zbuf[0:HIST_ROWS, :] = tail
        r["histp"][...] = tail

    @pl.when(i == n_prompt)
    def _():
        ext = HIST_ROWS + Ts
        x = r["xs"][...]
        z = _dot(_rmsnorm(x, r["g_mix"][...]), r["w_in"][...])
        u = z[:, d_pool:]
        for b in range(n_streams):
            zbuf[b * ext:b * ext + 1, :] = jnp.zeros((1, d_pool), F32)
            for t in range(POOL_HIST):
                zbuf[b * ext + 1 + t:b * ext + 2 + t, :] = r["hist_in"][t, b:b + 1, :]
            zbuf[b * ext + HIST_ROWS:(b + 1) * ext, :] = z[b * Ts:(b + 1) * Ts, :d_pool]
            for t in range(POOL_HIST):
                row = (b + 1) * ext - POOL_HIST + t
                r["hists"][t, b:b + 1, :] = zbuf[row:row + 1, :]
        nrows = n_streams * ext - HIST_ROWS
        ridx = lax.broadcasted_iota(jnp.int32, (nrows, 1), 0)
        pos = PAST_LEN + lax.rem(ridx, ext)
        y_all = _pool_windows(zbuf, nrows, pos, r)
        for b in range(n_streams):
            ypool_ref[b * Ts:(b + 1) * Ts, :] = y_all[b * ext:b * ext + Ts, :]
        G, P = r["hrep"].shape
        for src, dst in ((r["h0_re"], h0f_re), (r["h0_im"], h0f_im)):
            for b in range(n_streams):
                for g in range(G):
                    dst[b:b + 1, g * P:(g + 1) * P] = src[b * G + g:b * G + g + 1, :]
        y_state = _s5_scan(u, n_streams, Ts, "s", None, (h0f_re, h0f_im), r, hs_re, hs_im)
        r["outs"][...] = _mix_tail(x, ypool_ref[...], u, y_state, r)
        for g in range(G):
            cols = slice(g * P, (g + 1) * P)
            r["hrep"][g:g + 1, :] = hp_re[:, cols]
            r["himp"][g:g + 1, :] = hp_im[:, cols]
            for b in range(n_streams):
                r["hres"][b * G + g:b * G + g + 1, :] = hs_re[b:b + 1, cols]
                r["hims"][b * G + g:b * G + g + 1, :] = hs_im[b:b + 1, cols]


def _const_spec(shape):
    nd = len(shape)
    return pl.BlockSpec(shape, lambda *_: (0,) * nd)


def _weight_spec(shape):
    nd = len(shape)
    return pl.BlockSpec(shape, lambda *_: (0,) * nd, pipeline_mode=pl.Buffered(1))


def _prompt_spec(ts, d, n_prompt):
    return pl.BlockSpec((ts, d), lambda i: (jnp.minimum(i, n_prompt - 1), 0))


def _mixer(xp, xs, hist_in, h0_re, h0_im, p, n_streams, Ts):
    S, D = xp.shape
    rows_s = xs.shape[0]
    ts = PROMPT_BLOCK
    n_prompt = S // ts
    d_pool = hist_in.shape[-1]
    n_state = p["a1_re"].shape[-1]
    P = h0_re.shape[-1]
    G = n_state // P
    assert n_streams * (HIST_ROWS + Ts) <= HIST_ROWS + ts and rows_s <= ts
    args = dict(p, xp=xp, xs=xs, hist_in=hist_in, h0_re=h0_re, h0_im=h0_im)
    ins = [args[n] for n in _MIXER_IN]
    in_specs = [_prompt_spec(ts, D, n_prompt)] + [_weight_spec(a.shape) for a in ins[1:]]
    n_half, nh, H = p["bb_re"].shape
    d_half = p["cc_re"].shape[1]
    gen_shapes = ([((n_half, d_half, 2 * nh), BF16), ((n_half, 2 * nh, d_half), BF16)]
                  + [((PROMPT_CHUNK, n_state), F32)] * 4 + [((Ts, n_state), F32)] * 4 + [((2, n_state), F32)] * 2)
    out_shapes = dict(
        outp=(S, D), outs=(rows_s, D), histp=(HIST_ROWS, d_pool), hrep=(G, P), himp=(G, P),
        hists=(POOL_HIST, n_streams, d_pool), hres=(n_streams * G, P), hims=(n_streams * G, P))
    out_specs = [_prompt_spec(ts, D, n_prompt)] + [_const_spec(out_shapes[n]) for n in _MIXER_OUT[1:]]
    return pl.pallas_call(
        functools.partial(_mixer_kernel, n_prompt=n_prompt, n_streams=n_streams, Tp=PROMPT_CHUNK, Ts=Ts),
        grid=(n_prompt + 1,),
        in_specs=in_specs,
        out_specs=out_specs,
        out_shape=[jax.ShapeDtypeStruct(out_shapes[n], F32) for n in _MIXER_OUT],
        scratch_shapes=[pltpu.VMEM((HIST_ROWS + ts, d_pool), F32),
                        pltpu.VMEM((rows_s, d_pool), F32)]
        + [pltpu.VMEM((1, n_state), F32)] * 2 + [pltpu.VMEM((n_streams, n_state), F32)] * 4
        + [pltpu.VMEM(args[n].shape, BF16) for n in _MIXER_BF16]
        + [pltpu.VMEM(shape, dtype) for shape, dtype in gen_shapes],
        compiler_params=pltpu.CompilerParams(dimension_semantics=("arbitrary",),
                                             vmem_limit_bytes=VMEM_LIMIT),
        name="mixer",
    )(*ins)


def _attend(q_heads, k_heads, v_heads, scale):
    outs = []
    for q, k, v in zip(q_heads, k_heads, v_heads):
        s = lax.dot_general(q, k.astype(BF16), (((1,), (1,)), ((), ())), preferred_element_type=F32) * scale
        e = jnp.exp(s - jnp.max(s, axis=-1, keepdims=True))
        outs.append(_dot(e, v) / jnp.sum(e, axis=-1, keepdims=True))
    return outs


_XATTN_IN = ("xp", "xs", "mem", "kc", "vc", "g_mem", "w_k", "w_v", "g_xattn", "w_q", "w_o")
_XATTN_OUT = ("outp", "outs", "mk", "mv")


def _xattn_kernel(*refs, n_prompt, n_streams, Ts, hpu, scale):
    r = dict(zip(_XATTN_IN + _XATTN_OUT, refs))
    kp_ref, vp_ref, qs_ref, os_ref, kbuf, vbuf, sem, kvf, osem = refs[len(_XATTN_IN) + len(_XATTN_OUT):]
    i = pl.program_id(0)
    hd = kp_ref.shape[-1] // MEM_HEADS
    ups = MEM_HEADS // hpu

    def unit_copies(unit, hg):
        b, slot = unit // ups, lax.rem(unit, 2)
        return [pltpu.make_async_copy(src.at[0, b, :, hg * hpu + e, :], dst.at[slot, e], sem.at[slot, j * hpu + e])
                for j, (src, dst) in enumerate(((r["kc"], kbuf), (r["vc"], vbuf))) for e in range(hpu)]

    def kv_out_copies():
        return [pltpu.make_async_copy(kvf.at[j, h], dst.at[0, 0, :, h, :], osem.at[j, h])
                for j, dst in enumerate((r["mk"], r["mv"])) for h in range(MEM_HEADS)]

    def for_unit(unit, fn):
        for hg in range(ups):
            @pl.when(lax.rem(unit, ups) == hg)
            def _():
                for cp in unit_copies(unit, hg):
                    fn(cp)

    @pl.when(i == 0)
    def _():
        for_unit(i, lambda cp: cp.start())
        mem = r["mem"][...]
        inv = _inv_rms(mem)
        mg = (mem * r["g_mem"][...]).astype(BF16)
        k = _dot(mg, r["w_k"][...]) * inv
        v = _dot(mg, r["w_v"][...]) * inv
        for h in range(MEM_HEADS):
            kvf[0, h] = k[:, h * hd:(h + 1) * hd]
            kvf[1, h] = v[:, h * hd:(h + 1) * hd]
        for cp in kv_out_copies():
            cp.start()
        kp_ref[...] = k.astype(BF16)
        vp_ref[...] = v.astype(BF16)
        q = _norm_dot(r["xs"][...], r["g_xattn"][...], r["w_q"][...]).astype(BF16)
        for h in range(MEM_HEADS):
            qs_ref[h] = q[:, h * hd:(h + 1) * hd]

    @pl.when(i < n_prompt)
    def _():
        @pl.when(i + 1 < n_prompt)
        def _():
            for_unit(i + 1, lambda cp: cp.start())

        for_unit(i, lambda cp: cp.wait())

        x = r["xp"][...]
        q = _norm_dot(x, r["g_xattn"][...], r["w_q"][...]).astype(BF16)
        o = _attend([q[:, h * hd:(h + 1) * hd] for h in range(MEM_HEADS)],
                    [kp_ref[:, h * hd:(h + 1) * hd] for h in range(MEM_HEADS)],
                    [vp_ref[:, h * hd:(h + 1) * hd] for h in range(MEM_HEADS)], scale)
        r["outp"][...] = x + _dot(jnp.concatenate(o, axis=-1), r["w_o"][...])

        b, hg, slot = i // ups, lax.rem(i, ups), lax.rem(i, 2)
        rows = pl.ds(pl.multiple_of(b * Ts, Ts), Ts)
        heads = [hg * hpu + e for e in range(hpu)]
        o_s = _attend([qs_ref[h, rows, :] for h in heads], [kbuf[slot, e] for e in range(hpu)],
                      [vbuf[slot, e] for e in range(hpu)], scale)
        for h, o_h in zip(heads, o_s):
            os_ref[h, rows, :] = o_h.astype(BF16)

    @pl.when(i == n_prompt)
    def _():
        for cp in kv_out_copies():
            cp.wait()
        o = jnp.concatenate([os_ref[h] for h in range(MEM_HEADS)], axis=-1)
        r["outs"][...] = r["xs"][...] + _dot(o, r["w_o"][...])


def _xattn(xp, xs, mem, kc, vc, p, n_streams, Ts):
    S, D = xp.shape
    rows_s = xs.shape[0]
    ts = XATTN_BLOCK
    n_prompt = S // ts
    n_mem = mem.shape[0]
    hd = D // MEM_HEADS
    hpu = n_streams * MEM_HEADS // n_prompt
    assert hpu * n_prompt == n_streams * MEM_HEADS and MEM_HEADS % hpu == 0
    args = dict(p, xp=xp, xs=xs, mem=mem, kc=kc, vc=vc)
    ins = [args[n] for n in _XATTN_IN]
    cache_spec = pl.BlockSpec(memory_space=pl.ANY)
    in_specs = [_prompt_spec(ts, D, n_prompt)]
    for n in _XATTN_IN[1:]:
        in_specs.append(cache_spec if n in ("kc", "vc") else _weight_spec(args[n].shape))
    out_shapes = dict(outp=(S, D), outs=(rows_s, D), mk=(1, 1, n_mem, MEM_HEADS, hd), mv=(1, 1, n_mem, MEM_HEADS, hd))
    out_specs = [_prompt_spec(ts, D, n_prompt), _const_spec(out_shapes["outs"]), cache_spec, cache_spec]
    return pl.pallas_call(
        functools.partial(_xattn_kernel, n_prompt=n_prompt, n_streams=n_streams, Ts=Ts, hpu=hpu, scale=hd ** -0.5),
        grid=(n_prompt + 1,),
        in_specs=in_specs,
        out_specs=out_specs,
        out_shape=[jax.ShapeDtypeStruct(out_shapes[n], F32) for n in _XATTN_OUT],
        scratch_shapes=[pltpu.VMEM((n_mem, D), BF16), pltpu.VMEM((n_mem, D), BF16),
                        pltpu.VMEM((MEM_HEADS, rows_s, hd), BF16), pltpu.VMEM((MEM_HEADS, rows_s, hd), BF16),
                        pltpu.VMEM((2, hpu, n_mem, hd), F32), pltpu.VMEM((2, hpu, n_mem, hd), F32),
                        pltpu.SemaphoreType.DMA((2, 2 * hpu)),
                        pltpu.VMEM((2, MEM_HEADS, n_mem, hd), F32), pltpu.SemaphoreType.DMA((2, MEM_HEADS))],
        compiler_params=pltpu.CompilerParams(dimension_semantics=("arbitrary",),
                                             vmem_limit_bytes=VMEM_LIMIT),
        name="xattn",
    )(*ins)


def _mlp_rows(x, g_ref, wup_ref, wdown_ref, gfin_ref):
    hg = (x * g_ref[...]).astype(BF16)
    acc = None
    for j in range(wup_ref.shape[1] // FF_CHUNK):
        cols = slice(j * FF_CHUNK, (j + 1) * FF_CHUNK)
        up = _dot(hg, wup_ref[:, cols])
        part = _dot(jnp.square(jnp.maximum(up, 0.0)), wdown_ref[cols, :])
        acc = part if acc is None else acc + part
    return _rmsnorm(x + acc * jnp.square(_inv_rms(x)), gfin_ref[...])


def _mlp_kernel(xp_ref, xs_ref, g_ref, wup_ref, wdown_ref, gfin_ref, outp_ref, outs_ref, *, n_prompt):
    i = pl.program_id(0)

    @pl.when(i < n_prompt)
    def _():
        outp_ref[...] = _mlp_rows(xp_ref[...], g_ref, wup_ref, wdown_ref, gfin_ref)

    @pl.when(i == n_prompt)
    def _():
        outs_ref[...] = _mlp_rows(xs_ref[...], g_ref, wup_ref, wdown_ref, gfin_ref)


def _mlp(xp, xs, p):
    S, D = xp.shape
    ts = PROMPT_BLOCK
    n_prompt = S // ts
    weights = (p["g_mlp"], p["w_up"], p["w_down"], p["g_final"])
    return pl.pallas_call(
        functools.partial(_mlp_kernel, n_prompt=n_prompt),
        grid=(n_prompt + 1,),
        in_specs=[_prompt_spec(ts, D, n_prompt), _weight_spec(xs.shape)] + [_weight_spec(w.shape) for w in weights],
        out_specs=[_prompt_spec(ts, D, n_prompt), _const_spec(xs.shape)],
        out_shape=[jax.ShapeDtypeStruct(xp.shape, F32), jax.ShapeDtypeStruct(xs.shape, F32)],
        compiler_params=pltpu.CompilerParams(dimension_semantics=("arbitrary",),
                                             vmem_limit_bytes=VMEM_LIMIT),
        name="mlp",
    )(xp, xs, *weights)


def _prepare_layer(l, g_mix, w_in, w_pool, pool_scale, a_re, a_im, b_re, b_im, c_re, c_im, d_skip,
                   log_dt, w_glu, b_glu, w_out, g_xattn, g_mem, w_q, w_k, w_v, w_o, g_mlp, w_up,
                   w_down, g_final):
    G, P = a_re.shape[1:]
    gi = G // N_HALF
    ar, ai = a_re[l].astype(F32), a_im[l].astype(F32)
    dt = jnp.exp(log_dt[l].astype(F32))[:, None]
    lam_re, lam_im = ar * dt, ai * dt

    mag = jnp.exp(lam_re)
    ab_re, ab_im = mag * jnp.cos(lam_im), mag * jnp.sin(lam_im)

    den = ar * ar + ai * ai
    coef_re = ((ab_re - 1.0) * ar + ab_im * ai) / den
    coef_im = (ab_im * ar - (ab_re - 1.0) * ai) / den
    br, bi = b_re[l].astype(F32), b_im[l].astype(F32)
    bb_re = coef_re[..., None] * br - coef_im[..., None] * bi
    bb_im = coef_re[..., None] * bi + coef_im[..., None] * br

    half = lambda t: t.astype(F32).reshape((N_HALF, gi * t.shape[1], t.shape[2]))
    row = lambda v: v.astype(F32).reshape(1, -1)
    return dict(
        g_mix=row(g_mix[l]), w_in=w_in[l], w_pool=w_pool[l], pool_scale=row(pool_scale[l]),
        bb_re=half(bb_re), bb_im=half(bb_im), cc_re=half(c_re[l]), cc_im=half(c_im[l]),
        a1_re=row(ab_re), a1_im=row(ab_im),
        d_skip=row(d_skip[l]), w_glu=w_glu[l], b_glu=row(b_glu[l]), w_out=w_out[l],
        g_xattn=row(g_xattn[l]), g_mem=row(g_mem[l]), w_q=w_q[l], w_k=w_k[l], w_v=w_v[l], w_o=w_o[l],
        g_mlp=row(g_mlp[l]), w_up=w_up[l], w_down=w_down[l], g_final=row(g_final))


def kernel(x_prompt, x_sample, cache_mem_k, cache_mem_v, state_pool, state_ssm_re, state_ssm_im, mem_prompt, g_mix, w_in, w_pool, pool_scale, ssm_a_re, ssm_a_im, ssm_b_re, ssm_b_im, ssm_c_re, ssm_c_im, ssm_d, ssm_log_dt, w_glu, b_glu, w_out, g_xattn, g_mem, w_q, w_k, w_v, w_o, g_mlp, w_up, w_down, g_final):
    depth = g_mix.shape[0]
    assert depth == 1 and x_prompt.shape[0] == 1, "single layer, single prompt stream"
    Bp, S, D = x_prompt.shape
    Bs, Ts, _ = x_sample.shape
    n_mem = mem_prompt.shape[1]
    G, P = ssm_a_re.shape[1:]
    assert S % PROMPT_BLOCK == 0 and S % XATTN_BLOCK == 0
    assert PROMPT_BLOCK % PROMPT_CHUNK == 0 and Ts <= PROMPT_CHUNK

    l = 0
    p = _prepare_layer(l, g_mix, w_in, w_pool, pool_scale, ssm_a_re, ssm_a_im, ssm_b_re, ssm_b_im,
                       ssm_c_re, ssm_c_im, ssm_d, ssm_log_dt, w_glu, b_glu, w_out, g_xattn, g_mem,
                       w_q, w_k, w_v, w_o, g_mlp, w_up, w_down, g_final)

    xp1, xs1, hist_p, hre_p, him_p, hist_s, hre_s, him_s = _mixer(
        x_prompt[0], x_sample.reshape(Bs * Ts, D), jnp.transpose(state_pool[l], (1, 0, 2)),
        state_ssm_re[l].reshape(Bs * G, P), state_ssm_im[l].reshape(Bs * G, P), p, Bs, Ts)
    xp2, xs2, mk, mv = _xattn(xp1, xs1, mem_prompt[0], cache_mem_k[l:l + 1], cache_mem_v[l:l + 1], p, Bs, Ts)
    y_prompt, y_sample = _mlp(xp2, xs2, p)

    return (y_prompt[None], y_sample.reshape(Bs, Ts, D), mk, mv,
            hist_p[1:].reshape(1, Bp, POOL_HIST, -1),
            hre_p.reshape(1, Bp, G, P), him_p.reshape(1, Bp, G, P),
            jnp.transpose(hist_s, (1, 0, 2))[None], hre_s.reshape(1, Bs, G, P), him_s.reshape(1, Bs, G, P))
```

```python
import functools

import jax
import jax.numpy as jnp
from jax import lax
from jax.experimental import pallas as pl
from jax.experimental.pallas import tpu as pltpu

F32 = jnp.float32
BF16 = jnp.bfloat16

EPS = 1e-6
PAST_LEN = 1024
POOL_WINDOWS = (2, 4, 8, 16)
POOL_HIST = max(POOL_WINDOWS) - 1
HIST_ROWS = 16
POOL_PAIR = 2
MEM_HEADS = 4
N_HALF = 2
S5_TILE = 256

PROMPT_BLOCK = 512
XATTN_BLOCK = 1024
PROMPT_CHUNK = 128
FF_CHUNK = 2048
VMEM_LIMIT = 56 * 1024 * 1024


def _inv_rms(x):
    return lax.rsqrt(jnp.mean(x * x, axis=-1, keepdims=True) + EPS)


def _rmsnorm(x, g):
    return x * _inv_rms(x) * g


def _dot(a, b):
    return jnp.dot(a.astype(BF16), b.astype(BF16), preferred_element_type=F32)


def _norm_dot(x, g, w):
    return _dot(x * g, w) * _inv_rms(x)


def _pool_windows(zbuf, nrows, pos, r):
    gw = r["w_pool"].shape[-1] // POOL_PAIR
    pooled = []
    for g, w in enumerate(POOL_WINDOWS):
        lanes = slice(g * gw, (g + 1) * gw)
        win = zbuf[pl.ds(0, HIST_ROWS + nrows), lanes]
        k = 1
        while k < w:
            win = win + pltpu.roll(win, k, 0)
            k *= 2
        cur = zbuf[pl.ds(HIST_ROWS, nrows), lanes]
        cnt = jnp.minimum(pos + 1, w).astype(F32)
        pooled.append((win[HIST_ROWS:, :] / cnt - cur).astype(BF16))
    outs = [_dot(jnp.concatenate(pooled[j * POOL_PAIR:(j + 1) * POOL_PAIR], axis=-1), r["w_pool"][j])
            for j in range(len(POOL_WINDOWS) // POOL_PAIR)]
    return jnp.concatenate(outs, axis=-1) * r["pool_scale"][...]


def _cmul(ar, ai, br, bi):
    return ar * br - ai * bi, ar * bi + ai * br


def _s5_scan(u, n_chunks, T, tag, h_carry, h_rows, r, hout_re_ref, hout_im_ref):
    neg_re, neg_im, pos_re, pos_im = (r[tag + n] for n in ("neg_re", "neg_im", "pos_re", "pos_im"))
    ac_row = "ps".index(tag)
    hw = r["bblk"].shape[1]
    nh = r["bblk"].shape[2] // 2
    u_bf = u.astype(BF16)
    r_i = lax.broadcasted_iota(jnp.int32, (T, T), 0)
    c_i = lax.broadcasted_iota(jnp.int32, (T, T), 1)
    ltri = jnp.where(r_i >= c_i, 1.0, 0.0).astype(BF16)

    ys = []
    for k in range(N_HALF):
        uk = u_bf[:, k * hw:(k + 1) * hw]
        yk = None
        for j in range(nh // S5_TILE):
            re_c = slice(j * S5_TILE, (j + 1) * S5_TILE)
            im_c = slice(nh + j * S5_TILE, nh + (j + 1) * S5_TILE)
            st = slice(k * nh + j * S5_TILE, k * nh + (j + 1) * S5_TILE)
            bu_re = _dot(uk, r["bblk"][k, :, re_c])
            bu_im = _dot(uk, r["bblk"][k, :, im_c])
            if h_rows is None:
                carry = (h_carry[0][:, st], h_carry[1][:, st])
            h_re, h_im = [], []
            for c in range(n_chunks):
                rows = slice(c * T, (c + 1) * T)
                xr, xi = _cmul(bu_re[rows], bu_im[rows], neg_re[:, st], neg_im[:, st])
                wr = _dot(ltri, xr)
                wi = _dot(ltri, xi)
                if h_rows is not None:
                    carry = (h_rows[0][pl.ds(c, 1), st], h_rows[1][pl.ds(c, 1), st])
                cr, ci = _cmul(r["ac_re"][ac_row:ac_row + 1, st], r["ac_im"][ac_row:ac_row + 1, st], carry[0], carry[1])
                sr, si = _cmul(pos_re[:, st], pos_im[:, st], wr + cr, wi + ci)
                carry = (sr[T - 1:T, :], si[T - 1:T, :])
                if h_rows is not None:
                    hout_re_ref[pl.ds(c, 1), st] = carry[0]
                    hout_im_ref[pl.ds(c, 1), st] = carry[1]
                h_re.append(sr)
                h_im.append(si)
            if h_rows is None:
                hout_re_ref[:, st] = carry[0]
                hout_im_ref[:, st] = carry[1]
            part = (_dot(jnp.concatenate(h_re, axis=0), r["cblk"][k, re_c, :])
                    + _dot(jnp.concatenate(h_im, axis=0), r["cblk"][k, im_c, :]))
            yk = part if yk is None else yk + part
        ys.append(yk)
    return jnp.concatenate(ys, axis=-1)


def _mix_tail(x, y_pool, u, y_state, r):
    y = jax.nn.gelu(y_state + r["d_skip"][...] * u)
    y = y * jax.nn.sigmoid(_dot(y, r["w_glu"][...]) + r["b_glu"][...])
    ycat = jnp.concatenate([y_pool.astype(BF16), y.astype(BF16)], axis=-1)
    return x + _dot(ycat, r["w_out"][...])


_MIXER_IN = ("xp", "xs", "hist_in", "h0_re", "h0_im",
             "g_mix", "w_in", "w_pool", "pool_scale", "bb_re", "bb_im", "cc_re", "cc_im", "a1_re", "a1_im",
             "d_skip", "w_glu", "b_glu", "w_out")
_MIXER_OUT = ("outp", "outs", "histp", "hrep", "himp", "hists", "hres", "hims")
_MIXER_BF16 = ("w_in", "w_pool", "w_glu", "w_out")
_MIXER_GEN = ("bblk", "cblk", "pneg_re", "pneg_im", "ppos_re", "ppos_im",
              "sneg_re", "sneg_im", "spos_re", "spos_im", "ac_re", "ac_im")


def _power_table(re1, im1, T, descending=False):
    SUB = min(16, T)
    assert T % SUB == 0
    n = re1.shape[-1]
    t = lax.broadcasted_iota(jnp.int32, (SUB, 1), 0)
    if descending:
        t = (SUB - 1) - t
    lo_re, lo_im = jnp.ones((SUB, n), F32), jnp.zeros((SUB, n), F32)
    cr, ci = re1, im1
    k = 1
    while k < SUB:
        bit = (t & k) != 0
        lo_re, lo_im = _cmul(lo_re, lo_im, jnp.where(bit, cr, 1.0), jnp.where(bit, ci, 0.0))
        cr, ci = _cmul(cr, ci, cr, ci)
        k *= 2
    hi_re, hi_im = [jnp.ones((1, n), F32)], [jnp.zeros((1, n), F32)]
    for _ in range(T // SUB - 1):
        nr, ni = _cmul(hi_re[-1], hi_im[-1], cr, ci)
        hi_re.append(nr)
        hi_im.append(ni)
    if descending:
        hi_re, hi_im = hi_re[::-1], hi_im[::-1]
    rep = lambda rows: jnp.concatenate([jnp.broadcast_to(v, (SUB, n)) for v in rows], axis=0)
    tile = lambda v: jnp.concatenate([v] * (T // SUB), axis=0)
    return _cmul(tile(lo_re), tile(lo_im), rep(hi_re), rep(hi_im))


def _centred_tables(are, aim, T):
    m = T // 2
    den = are * are + aim * aim
    ire, iim = are / den, -aim / den
    up_re, up_im = _power_table(are, aim, T - m)
    dn_re, dn_im = _power_table(are, aim, m, descending=True)
    dn_re, dn_im = _cmul(dn_re, dn_im, are, aim)
    iu_re, iu_im = _power_table(ire, iim, T - m)
    id_re, id_im = _power_table(ire, iim, m, descending=True)
    id_re, id_im = _cmul(id_re, id_im, ire, iim)
    neg = (jnp.concatenate([dn_re, iu_re], axis=0), jnp.concatenate([dn_im, iu_im], axis=0))
    pos = (jnp.concatenate([id_re, up_re], axis=0), jnp.concatenate([id_im, up_im], axis=0))
    return neg, pos, _cmul(dn_re[0:1, :], dn_im[0:1, :], are, aim)


def _expand_s5_params(r, gen, chunk_lengths):
    are, aim = r["a1_re"][...], r["a1_im"][...]
    for row, (tag, T) in enumerate(zip("ps", chunk_lengths)):
        neg, pos, carry = _centred_tables(are, aim, T)
        gen[tag + "neg_re"][...], gen[tag + "neg_im"][...] = neg
        gen[tag + "pos_re"][...], gen[tag + "pos_im"][...] = pos
        gen["ac_re"][row:row + 1, :], gen["ac_im"][row:row + 1, :] = carry

    n_half, rows_b, H = r["bb_re"].shape
    _, rows_c, P = r["cc_re"].shape
    nh = rows_b

    def spread(x, width):
        w = x.shape[1]
        sel = lax.rem(lax.broadcasted_iota(jnp.int32, (w, width), 1), w) == lax.broadcasted_iota(jnp.int32, (w, width), 0)
        return _dot(x, jnp.where(sel, 1.0, 0.0))

    def same_group(shape, row_size, col_size):
        return (lax.div(lax.broadcasted_iota(jnp.int32, shape, 0), row_size)
                == lax.div(lax.broadcasted_iota(jnp.int32, shape, 1), col_size))

    for k in range(n_half):
        for c, (b_src, c_src, sign) in enumerate(((r["bb_re"], r["cc_re"], 1.0), (r["bb_im"], r["cc_im"], -1.0))):
            yb = spread(b_src[k], rows_c)
            yb = jnp.where(same_group(yb.shape, P, H), yb, 0.0)
            gen["bblk"][k, :, c * nh:(c + 1) * nh] = yb.T.astype(BF16)
            yc = spread(c_src[k], rows_b)
            yc = jnp.where(same_group(yc.shape, H, P), sign * yc, 0.0)
            gen["cblk"][k, c * nh:(c + 1) * nh, :] = yc.T.astype(BF16)


def _mixer_kernel(*refs, n_prompt, n_streams, Tp, Ts):
    r = dict(zip(_MIXER_IN + _MIXER_OUT, refs))
    zbuf, ypool_ref, hp_re, hp_im, hs_re, hs_im, h0f_re, h0f_im = refs[len(_MIXER_IN) + len(_MIXER_OUT):][:8]
    n_fixed = len(_MIXER_IN) + len(_MIXER_OUT) + 8
    bf16_weights = dict(zip(_MIXER_BF16, refs[n_fixed:]))
    gen = dict(zip(_MIXER_GEN, refs[n_fixed + len(_MIXER_BF16):]))
    i = pl.program_id(0)
    ts = r["xp"].shape[0]
    d_pool = zbuf.shape[1]

    @pl.when(i == 0)
    def _():
        zbuf[0:HIST_ROWS, :] = jnp.zeros((HIST_ROWS, d_pool), F32)
        hp_re[...] = jnp.zeros(hp_re.shape, F32)
        hp_im[...] = jnp.zeros(hp_im.shape, F32)
        for name, ref in bf16_weights.items():
            ref[...] = r[name][...].astype(BF16)
        _expand_s5_params(r, gen, (Tp, Ts))

    r.update(bf16_weights)
    r.update(gen)

    @pl.when(i < n_prompt)
    def _():
        x = r["xp"][...]
        z = _dot(_rmsnorm(x, r["g_mix"][...]), r["w_in"][...])
        zbuf[HIST_ROWS:HIST_ROWS + ts, :] = z[:, :d_pool]
        u = z[:, d_pool:]
        pos = i * ts + lax.broadcasted_iota(jnp.int32, (ts, 1), 0)
        y_pool = _pool_windows(zbuf, ts, pos, r)
        y_state = _s5_scan(u, ts // Tp, Tp, "p", (hp_re[...], hp_im[...]), None, r, hp_re, hp_im)
        r["outp"][...] = _mix_tail(x, y_pool, u, y_state, r)
        tail = zbuf[ts:ts + HIST_ROWS, :]
        zbuf[0:HIST_ROWS, :] = tail
        r["histp"][...] = tail

    @pl.when(i == n_prompt)
    def _():
        ext = HIST_ROWS + Ts
        x = r["xs"][...]
        z = _dot(_rmsnorm(x, r["g_mix"][...]), r["w_in"][...])
        u = z[:, d_pool:]
        for b in range(n_streams):
            zbuf[b * ext:b * ext + 1, :] = jnp.zeros((1, d_pool), F32)
            for t in range(POOL_HIST):
                zbuf[b * ext + 1 + t:b * ext + 2 + t, :] = r["hist_in"][t, b:b + 1, :]
            zbuf[b * ext + HIST_ROWS:(b + 1) * ext, :] = z[b * Ts:(b + 1) * Ts, :d_pool]
            for t in range(POOL_HIST):
                row = (b + 1) * ext - POOL_HIST + t
                r["hists"][t, b:b + 1, :] = zbuf[row:row + 1, :]
        nrows = n_streams * ext - HIST_ROWS
        ridx = lax.broadcasted_iota(jnp.int32, (nrows, 1), 0)
        pos = PAST_LEN + lax.rem(ridx, ext)
        y_all = _pool_windows(zbuf, nrows, pos, r)
        for b in range(n_streams):
            ypool_ref[b * Ts:(b + 1) * Ts, :] = y_all[b * ext:b * ext + Ts, :]
        G, P = r["hrep"].shape
        for src, dst in ((r["h0_re"], h0f_re), (r["h0_im"], h0f_im)):
            for b in range(n_streams):
                for g in range(G):
                    dst[b:b + 1, g * P:(g + 1) * P] = src[b * G + g:b * G + g + 1, :]
        y_state = _s5_scan(u, n_streams, Ts, "s", None, (h0f_re, h0f_im), r, hs_re, hs_im)
        r["outs"][...] = _mix_tail(x, ypool_ref[...], u, y_state, r)
        for g in range(G):
            cols = slice(g * P, (g + 1) * P)
            r["hrep"][g:g + 1, :] = hp_re[:, cols]
            r["himp"][g:g + 1, :] = hp_im[:, cols]
            for b in range(n_streams):
                r["hres"][b * G + g:b * G + g + 1, :] = hs_re[b:b + 1, cols]
                r["hims"][b * G + g:b * G + g + 1, :] = hs_im[b:b + 1, cols]


def _const_spec(shape):
    nd = len(shape)
    return pl.BlockSpec(shape, lambda *_: (0,) * nd)


def _weight_spec(shape):
    nd = len(shape)
    return pl.BlockSpec(shape, lambda *_: (0,) * nd, pipeline_mode=pl.Buffered(1))


def _prompt_spec(ts, d, n_prompt):
    return pl.BlockSpec((ts, d), lambda i: (jnp.minimum(i, n_prompt - 1), 0))


def _mixer(xp, xs, hist_in, h0_re, h0_im, p, n_streams, Ts):
    S, D = xp.shape
    rows_s = xs.shape[0]
    ts = PROMPT_BLOCK
    n_prompt = S // ts
    d_pool = hist_in.shape[-1]
    n_state = p["a1_re"].shape[-1]
    P = h0_re.shape[-1]
    G = n_state // P
    assert n_streams * (HIST_ROWS + Ts) <= HIST_ROWS + ts and rows_s <= ts
    args = dict(p, xp=xp, xs=xs, hist_in=hist_in, h0_re=h0_re, h0_im=h0_im)
    ins = [args[n] for n in _MIXER_IN]
    in_specs = [_prompt_spec(ts, D, n_prompt)] + [_weight_spec(a.shape) for a in ins[1:]]
    n_half, nh, H = p["bb_re"].shape
    d_half = p["cc_re"].shape[1]
    gen_shapes = ([((n_half, d_half, 2 * nh), BF16), ((n_half, 2 * nh, d_half), BF16)]
                  + [((PROMPT_CHUNK, n_state), F32)] * 4 + [((Ts, n_state), F32)] * 4 + [((2, n_state), F32)] * 2)
    out_shapes = dict(
        outp=(S, D), outs=(rows_s, D), histp=(HIST_ROWS, d_pool), hrep=(G, P), himp=(G, P),
        hists=(POOL_HIST, n_streams, d_pool), hres=(n_streams * G, P), hims=(n_streams * G, P))
    out_specs = [_prompt_spec(ts, D, n_prompt)] + [_const_spec(out_shapes[n]) for n in _MIXER_OUT[1:]]
    return pl.pallas_call(
        functools.partial(_mixer_kernel, n_prompt=n_prompt, n_streams=n_streams, Tp=PROMPT_CHUNK, Ts=Ts),
        grid=(n_prompt + 1,),
        in_specs=in_specs,
        out_specs=out_specs,
        out_shape=[jax.ShapeDtypeStruct(out_shapes[n], F32) for n in _MIXER_OUT],
        scratch_shapes=[pltpu.VMEM((HIST_ROWS + ts, d_pool), F32),
                        pltpu.VMEM((rows_s, d_pool), F32)]
        + [pltpu.VMEM((1, n_state), F32)] * 2 + [pltpu.VMEM((n_streams, n_state), F32)] * 4
        + [pltpu.VMEM(args[n].shape, BF16) for n in _MIXER_BF16]
        + [pltpu.VMEM(shape, dtype) for shape, dtype in gen_shapes],
        compiler_params=pltpu.CompilerParams(dimension_semantics=("arbitrary",),
                                             vmem_limit_bytes=VMEM_LIMIT),
        name="mixer",
    )(*ins)


def _attend(q_heads, k_heads, v_heads, scale):
    outs = []
    for q, k, v in zip(q_heads, k_heads, v_heads):
        s = lax.dot_general(q, k.astype(BF16), (((1,), (1,)), ((), ())), preferred_element_type=F32) * scale
        e = jnp.exp(s - jnp.max(s, axis=-1, keepdims=True))
        outs.append(_dot(e, v) / jnp.sum(e, axis=-1, keepdims=True))
    return outs


_XATTN_IN = ("xp", "xs", "mem", "kc", "vc", "g_mem", "w_k", "w_v", "g_xattn", "w_q", "w_o")
_XATTN_OUT = ("outp", "outs", "mk", "mv")


def _xattn_kernel(*refs, n_prompt, n_streams, Ts, hpu, scale):
    r = dict(zip(_XATTN_IN + _XATTN_OUT, refs))
    kp_ref, vp_ref, qs_ref, os_ref, kbuf, vbuf, sem, kvf, osem = refs[len(_XATTN_IN) + len(_XATTN_OUT):]
    i = pl.program_id(0)
    hd = kp_ref.shape[-1] // MEM_HEADS
    ups = MEM_HEADS // hpu

    def unit_copies(unit, hg):
        b, slot = unit // ups, lax.rem(unit, 2)
        return [pltpu.make_async_copy(src.at[0, b, :, hg * hpu + e, :], dst.at[slot, e], sem.at[slot, j * hpu + e])
                for j, (src, dst) in enumerate(((r["kc"], kbuf), (r["vc"], vbuf))) for e in range(hpu)]

    def kv_out_copies():
        return [pltpu.make_async_copy(kvf.at[j, h], dst.at[0, 0, :, h, :], osem.at[j, h])
                for j, dst in enumerate((r["mk"], r["mv"])) for h in range(MEM_HEADS)]

    def for_unit(unit, fn):
        for hg in range(ups):
            @pl.when(lax.rem(unit, ups) == hg)
            def _():
                for cp in unit_copies(unit, hg):
                    fn(cp)

    @pl.when(i == 0)
    def _():
        for_unit(i, lambda cp: cp.start())
        mem = r["mem"][...]
        inv = _inv_rms(mem)
        mg = (mem * r["g_mem"][...]).astype(BF16)
        k = _dot(mg, r["w_k"][...]) * inv
        v = _dot(mg, r["w_v"][...]) * inv
        for h in range(MEM_HEADS):
            kvf[0, h] = k[:, h * hd:(h + 1) * hd]
            kvf[1, h] = v[:, h * hd:(h + 1) * hd]
        for cp in kv_out_copies():
            cp.start()
        kp_ref[...] = k.astype(BF16)
        vp_ref[...] = v.astype(BF16)
        q = _norm_dot(r["xs"][...], r["g_xattn"][...], r["w_q"][...]).astype(BF16)
        for h in range(MEM_HEADS):
            qs_ref[h] = q[:, h * hd:(h + 1) * hd]

    @pl.when(i < n_prompt)
    def _():
        @pl.when(i + 1 < n_prompt)
        def _():
            for_unit(i + 1, lambda cp: cp.start())

        for_unit(i, lambda cp: cp.wait())

        x = r["xp"][...]
        q = _norm_dot(x, r["g_xattn"][...], r["w_q"][...]).astype(BF16)
        o = _attend([q[:, h * hd:(h + 1) * hd] for h in range(MEM_HEADS)],
                    [kp_ref[:, h * hd:(h + 1) * hd] for h in range(MEM_HEADS)],
                    [vp_ref[:, h * hd:(h + 1) * hd] for h in range(MEM_HEADS)], scale)
        r["outp"][...] = x + _dot(jnp.concatenate(o, axis=-1), r["w_o"][...])

        b, hg, slot = i // ups, lax.rem(i, ups), lax.rem(i, 2)
        rows = pl.ds(pl.multiple_of(b * Ts, Ts), Ts)
        heads = [hg * hpu + e for e in range(hpu)]
        o_s = _attend([qs_ref[h, rows, :] for h in heads], [kbuf[slot, e] for e in range(hpu)],
                      [vbuf[slot, e] for e in range(hpu)], scale)
        for h, o_h in zip(heads, o_s):
            os_ref[h, rows, :] = o_h.astype(BF16)

    @pl.when(i == n_prompt)
    def _():
        for cp in kv_out_copies():
            cp.wait()
        o = jnp.concatenate([os_ref[h] for h in range(MEM_HEADS)], axis=-1)
        r["outs"][...] = r["xs"][...] + _dot(o, r["w_o"][...])


def _xattn(xp, xs, mem, kc, vc, p, n_streams, Ts):
    S, D = xp.shape
    rows_s = xs.shape[0]
    ts = XATTN_BLOCK
    n_prompt = S // ts
    n_mem = mem.shape[0]
    hd = D // MEM_HEADS
    hpu = n_streams * MEM_HEADS // n_prompt
    assert hpu * n_prompt == n_streams * MEM_HEADS and MEM_HEADS % hpu == 0
    args = dict(p, xp=xp, xs=xs, mem=mem, kc=kc, vc=vc)
    ins = [args[n] for n in _XATTN_IN]
    cache_spec = pl.BlockSpec(memory_space=pl.ANY)
    in_specs = [_prompt_spec(ts, D, n_prompt)]
    for n in _XATTN_IN[1:]:
        in_specs.append(cache_spec if n in ("kc", "vc") else _weight_spec(args[n].shape))
    out_shapes = dict(outp=(S, D), outs=(rows_s, D), mk=(1, 1, n_mem, MEM_HEADS, hd), mv=(1, 1, n_mem, MEM_HEADS, hd))
    out_specs = [_prompt_spec(ts, D, n_prompt), _const_spec(out_shapes["outs"]), cache_spec, cache_spec]
    return pl.pallas_call(
        functools.partial(_xattn_kernel, n_prompt=n_prompt, n_streams=n_streams, Ts=Ts, hpu=hpu, scale=hd ** -0.5),
        grid=(n_prompt + 1,),
        in_specs=in_specs,
        out_specs=out_specs,
        out_shape=[jax.ShapeDtypeStruct(out_shapes[n], F32) for n in _XATTN_OUT],
        scratch_shapes=[pltpu.VMEM((n_mem, D), BF16), pltpu.VMEM((n_mem, D), BF16),
                        pltpu.VMEM((MEM_HEADS, rows_s, hd), BF16), pltpu.VMEM((MEM_HEADS, rows_s, hd), BF16),
                        pltpu.VMEM((2, hpu, n_mem, hd), F32), pltpu.VMEM((2, hpu, n_mem, hd), F32),
                        pltpu.SemaphoreType.DMA((2, 2 * hpu)),
                        pltpu.VMEM((2, MEM_HEADS, n_mem, hd), F32), pltpu.SemaphoreType.DMA((2, MEM_HEADS))],
        compiler_params=pltpu.CompilerParams(dimension_semantics=("arbitrary",),
                                             vmem_limit_bytes=VMEM_LIMIT),
        name="xattn",
    )(*ins)


def _mlp_rows(x, g_ref, wup_ref, wdown_ref, gfin_ref):
    hg = (x * g_ref[...]).astype(BF16)
    acc = None
    for j in range(wup_ref.shape[1] // FF_CHUNK):
        cols = slice(j * FF_CHUNK, (j + 1) * FF_CHUNK)
        up = _dot(hg, wup_ref[:, cols])
        part = _dot(jnp.square(jnp.maximum(up, 0.0)), wdown_ref[cols, :])
        acc = part if acc is None else acc + part
    return _rmsnorm(x + acc * jnp.square(_inv_rms(x)), gfin_ref[...])


def _mlp_kernel(xp_ref, xs_ref, g_ref, wup_ref, wdown_ref, gfin_ref, outp_ref, outs_ref, *, n_prompt):
    i = pl.program_id(0)

    @pl.when(i < n_prompt)
    def _():
        outp_ref[...] = _mlp_rows(xp_ref[...], g_ref, wup_ref, wdown_ref, gfin_ref)

    @pl.when(i == n_prompt)
    def _():
        outs_ref[...] = _mlp_rows(xs_ref[...], g_ref, wup_ref, wdown_ref, gfin_ref)


def _mlp(xp, xs, p):
    S, D = xp.shape
    ts = PROMPT_BLOCK
    n_prompt = S // ts
    weights = (p["g_mlp"], p["w_up"], p["w_down"], p["g_final"])
    return pl.pallas_call(
        functools.partial(_mlp_kernel, n_prompt=n_prompt),
        grid=(n_prompt + 1,),
        in_specs=[_prompt_spec(ts, D, n_prompt), _weight_spec(xs.shape)] + [_weight_spec(w.shape) for w in weights],
        out_specs=[_prompt_spec(ts, D, n_prompt), _const_spec(xs.shape)],
        out_shape=[jax.ShapeDtypeStruct(xp.shape, F32), jax.ShapeDtypeStruct(xs.shape, F32)],
        compiler_params=pltpu.CompilerParams(dimension_semantics=("arbitrary",),
                                             vmem_limit_bytes=VMEM_LIMIT),
        name="mlp",
    )(xp, xs, *weights)


def _block_diag_pairs(w):
    g, n, _ = w.shape
    eye = jnp.eye(POOL_PAIR, dtype=w.dtype)
    return jnp.einsum("jirc,ik->jirkc", w.reshape(g // POOL_PAIR, POOL_PAIR, n, n), eye).reshape(
        g // POOL_PAIR, POOL_PAIR * n, POOL_PAIR * n)


def _prepare_layer(l, g_mix, w_in, w_pool, pool_scale, a_re, a_im, b_re, b_im, c_re, c_im, d_skip,
                   log_dt, w_glu, b_glu, w_out, g_xattn, g_mem, w_q, w_k, w_v, w_o, g_mlp, w_up,
                   w_down, g_final):
    G, P = a_re.shape[1:]
    gi = G // N_HALF
    ar, ai = a_re[l].astype(F32), a_im[l].astype(F32)
    dt = jnp.exp(log_dt[l].astype(F32))[:, None]
    lam_re, lam_im = ar * dt, ai * dt

    mag = jnp.exp(lam_re)
    ab_re, ab_im = mag * jnp.cos(lam_im), mag * jnp.sin(lam_im)

    den = ar * ar + ai * ai
    coef_re = ((ab_re - 1.0) * ar + ab_im * ai) / den
    coef_im = (ab_im * ar - (ab_re - 1.0) * ai) / den
    br, bi = b_re[l].astype(F32), b_im[l].astype(F32)
    bb_re = coef_re[..., None] * br - coef_im[..., None] * bi
    bb_im = coef_re[..., None] * bi + coef_im[..., None] * br

    half = lambda t: t.astype(F32).reshape((N_HALF, gi * t.shape[1], t.shape[2]))
    row = lambda v: v.astype(F32).reshape(1, -1)
    return dict(
        g_mix=row(g_mix[l]), w_in=w_in[l], pool_scale=row(pool_scale[l]),
        w_pool=_block_diag_pairs(w_pool[l].astype(F32)),
        bb_re=half(bb_re), bb_im=half(bb_im), cc_re=half(c_re[l]), cc_im=half(c_im[l]),
        a1_re=row(ab_re), a1_im=row(ab_im),
        d_skip=row(d_skip[l]), w_glu=w_glu[l], b_glu=row(b_glu[l]), w_out=w_out[l],
        g_xattn=row(g_xattn[l]), g_mem=row(g_mem[l]), w_q=w_q[l], w_k=w_k[l], w_v=w_v[l], w_o=w_o[l],
        g_mlp=row(g_mlp[l]), w_up=w_up[l], w_down=w_down[l], g_final=row(g_final))


def kernel(x_prompt, x_sample, cache_mem_k, cache_mem_v, state_pool, state_ssm_re, state_ssm_im, mem_prompt, g_mix, w_in, w_pool, pool_scale, ssm_a_re, ssm_a_im, ssm_b_re, ssm_b_im, ssm_c_re, ssm_c_im, ssm_d, ssm_log_dt, w_glu, b_glu, w_out, g_xattn, g_mem, w_q, w_k, w_v, w_o, g_mlp, w_up, w_down, g_final):
    depth = g_mix.shape[0]
    assert depth == 1 and x_prompt.shape[0] == 1, "single layer, single prompt stream"
    Bp, S, D = x_prompt.shape
    Bs, Ts, _ = x_sample.shape
    n_mem = mem_prompt.shape[1]
    G, P = ssm_a_re.shape[1:]
    assert S % PROMPT_BLOCK == 0 and S % XATTN_BLOCK == 0
    assert PROMPT_BLOCK % PROMPT_CHUNK == 0 and Ts <= PROMPT_CHUNK

    l = 0
    p = _prepare_layer(l, g_mix, w_in, w_pool, pool_scale, ssm_a_re, ssm_a_im, ssm_b_re, ssm_b_im,
                       ssm_c_re, ssm_c_im, ssm_d, ssm_log_dt, w_glu, b_glu, w_out, g_xattn, g_mem,
                       w_q, w_k, w_v, w_o, g_mlp, w_up, w_down, g_final)

    xp1, xs1, hist_p, hre_p, him_p, hist_s, hre_s, him_s = _mixer(
        x_prompt[0], x_sample.reshape(Bs * Ts, D), jnp.transpose(state_pool[l], (1, 0, 2)),
        state_ssm_re[l].reshape(Bs * G, P), state_ssm_im[l].reshape(Bs * G, P), p, Bs, Ts)
    xp2, xs2, mk, mv = _xattn(xp1, xs1, mem_prompt[0], cache_mem_k[l:l + 1], cache_mem_v[l:l + 1], p, Bs, Ts)
    y_prompt, y_sample = _mlp(xp2, xs2, p)

    return (y_prompt[None], y_sample.reshape(Bs, Ts, D), mk, mv,
            hist_p[1:].reshape(1, Bp, POOL_HIST, -1),
            hre_p.reshape(1, Bp, G, P), him_p.reshape(1, Bp, G, P),
            jnp.transpose(hist_s, (1, 0, 2))[None], hre_s.reshape(1, Bs, G, P), him_s.reshape(1, Bs, G, P))
```

```python
import functools

import jax
import jax.numpy as jnp
from jax import lax
from jax.experimental import pallas as pl
from jax.experimental.pallas import tpu as pltpu

F32 = jnp.float32
BF16 = jnp.bfloat16

EPS = 1e-6
PAST_LEN = 1024
POOL_WINDOWS = (2, 4, 8, 16)
POOL_HIST = max(POOL_WINDOWS) - 1
HIST_ROWS = 16
MEM_HEADS = 4
N_HALF = 2
S5_TILE = 256

PROMPT_BLOCK = 512
XATTN_BLOCK = 1024
PROMPT_CHUNK = 128
FF_CHUNK = 2048
VMEM_LIMIT = 56 * 1024 * 1024


def _inv_rms(x):
    return lax.rsqrt(jnp.mean(x * x, axis=-1, keepdims=True) + EPS)


def _rmsnorm(x, g):
    return x * _inv_rms(x) * g


def _dot(a, b):
    return jnp.dot(a.astype(BF16), b.astype(BF16), preferred_element_type=F32)


def _norm_dot(x, g, w):
    return _dot(x * g, w) * _inv_rms(x)


def _pool_windows(zbuf, nrows, pos, r):
    gw = r["w_pool"].shape[-1]
    outs = []
    for g, w in enumerate(POOL_WINDOWS):
        lanes = slice(g * gw, (g + 1) * gw)
        win = zbuf[pl.ds(0, HIST_ROWS + nrows), lanes]
        k = 1
        while k < w:
            win = win + pltpu.roll(win, k, 0)
            k *= 2
        cur = zbuf[pl.ds(HIST_ROWS, nrows), lanes]
        cnt = jnp.minimum(pos + 1, w).astype(F32)
        pooled = win[HIST_ROWS:, :] / cnt - cur
        outs.append(_dot(pooled, r["w_pool"][g]))
    return jnp.concatenate(outs, axis=-1) * r["pool_scale"][...]


def _cmul(ar, ai, br, bi):
    return ar * br - ai * bi, ar * bi + ai * br


def _s5_scan(u, n_chunks, T, tag, h_carry, h_rows, r, hout_re_ref, hout_im_ref):
    neg_re, neg_im, pos_re, pos_im = (r[tag + n] for n in ("neg_re", "neg_im", "pos_re", "pos_im"))
    ac_row = "ps".index(tag)
    hw = r["bblk"].shape[1]
    nh = r["bblk"].shape[2] // 2
    u_bf = u.astype(BF16)
    r_i = lax.broadcasted_iota(jnp.int32, (T, T), 0)
    c_i = lax.broadcasted_iota(jnp.int32, (T, T), 1)
    ltri = jnp.where(r_i >= c_i, 1.0, 0.0).astype(BF16)

    ys = []
    for k in range(N_HALF):
        uk = u_bf[:, k * hw:(k + 1) * hw]
        yk = None
        for j in range(nh // S5_TILE):
            re_c = slice(j * S5_TILE, (j + 1) * S5_TILE)
            im_c = slice(nh + j * S5_TILE, nh + (j + 1) * S5_TILE)
            st = slice(k * nh + j * S5_TILE, k * nh + (j + 1) * S5_TILE)
            bu_re = _dot(uk, r["bblk"][k, :, re_c])
            bu_im = _dot(uk, r["bblk"][k, :, im_c])
            if h_rows is None:
                carry = (h_carry[0][:, st], h_carry[1][:, st])
            h_re, h_im = [], []
            for c in range(n_chunks):
                rows = slice(c * T, (c + 1) * T)
                xr, xi = _cmul(bu_re[rows], bu_im[rows], neg_re[:, st], neg_im[:, st])
                wr = _dot(ltri, xr)
                wi = _dot(ltri, xi)
                if h_rows is not None:
                    carry = (h_rows[0][pl.ds(c, 1), st], h_rows[1][pl.ds(c, 1), st])
                cr, ci = _cmul(r["ac_re"][ac_row:ac_row + 1, st], r["ac_im"][ac_row:ac_row + 1, st], carry[0], carry[1])
                sr, si = _cmul(pos_re[:, st], pos_im[:, st], wr + cr, wi + ci)
                carry = (sr[T - 1:T, :], si[T - 1:T, :])
                if h_rows is not None:
                    hout_re_ref[pl.ds(c, 1), st] = carry[0]
                    hout_im_ref[pl.ds(c, 1), st] = carry[1]
                h_re.append(sr)
                h_im.append(si)
            if h_rows is None:
                hout_re_ref[:, st] = carry[0]
                hout_im_ref[:, st] = carry[1]
            part = (_dot(jnp.concatenate(h_re, axis=0), r["cblk"][k, re_c, :])
                    + _dot(jnp.concatenate(h_im, axis=0), r["cblk"][k, im_c, :]))
            yk = part if yk is None else yk + part
        ys.append(yk)
    return jnp.concatenate(ys, axis=-1)


def _mix_tail(x, y_pool, u, y_state, r):
    y = jax.nn.gelu(y_state + r["d_skip"][...] * u)
    y = y * jax.nn.sigmoid(_dot(y, r["w_glu"][...]) + r["b_glu"][...])
    ycat = jnp.concatenate([y_pool.astype(BF16), y.astype(BF16)], axis=-1)
    return x + _dot(ycat, r["w_out"][...])


_MIXER_IN = ("xp", "xs", "hist_in", "h0_re", "h0_im",
             "g_mix", "w_in", "w_pool", "pool_scale", "bb_re", "bb_im", "cc_re", "cc_im", "a1_re", "a1_im",
             "d_skip", "w_glu", "b_glu", "w_out")
_MIXER_OUT = ("outp", "outs", "histp", "hrep", "himp", "hists", "hres", "hims")
_MIXER_BF16 = ("w_in", "w_pool", "w_glu", "w_out")
_MIXER_GEN = ("bblk", "cblk", "pneg_re", "pneg_im", "ppos_re", "ppos_im",
              "sneg_re", "sneg_im", "spos_re", "spos_im", "ac_re", "ac_im")


def _power_table(re1, im1, T, descending=False):
    SUB = min(16, T)
    assert T % SUB == 0
    n = re1.shape[-1]
    t = lax.broadcasted_iota(jnp.int32, (SUB, 1), 0)
    if descending:
        t = (SUB - 1) - t
    lo_re, lo_im = jnp.ones((SUB, n), F32), jnp.zeros((SUB, n), F32)
    cr, ci = re1, im1
    k = 1
    while k < SUB:
        bit = (t & k) != 0
        lo_re, lo_im = _cmul(lo_re, lo_im, jnp.where(bit, cr, 1.0), jnp.where(bit, ci, 0.0))
        cr, ci = _cmul(cr, ci, cr, ci)
        k *= 2
    hi_re, hi_im = [jnp.ones((1, n), F32)], [jnp.zeros((1, n), F32)]
    for _ in range(T // SUB - 1):
        nr, ni = _cmul(hi_re[-1], hi_im[-1], cr, ci)
        hi_re.append(nr)
        hi_im.append(ni)
    if descending:
        hi_re, hi_im = hi_re[::-1], hi_im[::-1]
    rep = lambda rows: jnp.concatenate([jnp.broadcast_to(v, (SUB, n)) for v in rows], axis=0)
    tile = lambda v: jnp.concatenate([v] * (T // SUB), axis=0)
    return _cmul(tile(lo_re), tile(lo_im), rep(hi_re), rep(hi_im))


def _centred_tables(are, aim, T):
    m = T // 2
    den = are * are + aim * aim
    ire, iim = are / den, -aim / den
    up_re, up_im = _power_table(are, aim, T - m)
    dn_re, dn_im = _power_table(are, aim, m, descending=True)
    dn_re, dn_im = _cmul(dn_re, dn_im, are, aim)
    iu_re, iu_im = _power_table(ire, iim, T - m)
    id_re, id_im = _power_table(ire, iim, m, descending=True)
    id_re, id_im = _cmul(id_re, id_im, ire, iim)
    neg = (jnp.concatenate([dn_re, iu_re], axis=0), jnp.concatenate([dn_im, iu_im], axis=0))
    pos = (jnp.concatenate([id_re, up_re], axis=0), jnp.concatenate([id_im, up_im], axis=0))
    return neg, pos, _cmul(dn_re[0:1, :], dn_im[0:1, :], are, aim)


def _expand_s5_params(r, gen, chunk_lengths):
    are, aim = r["a1_re"][...], r["a1_im"][...]
    for row, (tag, T) in enumerate(zip("ps", chunk_lengths)):
        neg, pos, carry = _centred_tables(are, aim, T)
        gen[tag + "neg_re"][...], gen[tag + "neg_im"][...] = neg
        gen[tag + "pos_re"][...], gen[tag + "pos_im"][...] = pos
        gen["ac_re"][row:row + 1, :], gen["ac_im"][row:row + 1, :] = carry

    n_half, rows_b, H = r["bb_re"].shape
    _, rows_c, P = r["cc_re"].shape
    nh = rows_b

    def spread(x, width):
        w = x.shape[1]
        sel = lax.rem(lax.broadcasted_iota(jnp.int32, (w, width), 1), w) == lax.broadcasted_iota(jnp.int32, (w, width), 0)
        return _dot(x, jnp.where(sel, 1.0, 0.0))

    def same_group(shape, row_size, col_size):
        return (lax.div(lax.broadcasted_iota(jnp.int32, shape, 0), row_size)
                == lax.div(lax.broadcasted_iota(jnp.int32, shape, 1), col_size))

    for k in range(n_half):
        for c, (b_src, c_src, sign) in enumerate(((r["bb_re"], r["cc_re"], 1.0), (r["bb_im"], r["cc_im"], -1.0))):
            yb = spread(b_src[k], rows_c)
            yb = jnp.where(same_group(yb.shape, P, H), yb, 0.0)
            gen["bblk"][k, :, c * nh:(c + 1) * nh] = yb.T.astype(BF16)
            yc = spread(c_src[k], rows_b)
            yc = jnp.where(same_group(yc.shape, H, P), sign * yc, 0.0)
            gen["cblk"][k, c * nh:(c + 1) * nh, :] = yc.T.astype(BF16)


def _mixer_kernel(*refs, n_prompt, n_streams, Tp, Ts):
    r = dict(zip(_MIXER_IN + _MIXER_OUT, refs))
    zbuf, ypool_ref, hp_re, hp_im, hs_re, hs_im, h0f_re, h0f_im = refs[len(_MIXER_IN) + len(_MIXER_OUT):][:8]
    n_fixed = len(_MIXER_IN) + len(_MIXER_OUT) + 8
    bf16_weights = dict(zip(_MIXER_BF16, refs[n_fixed:]))
    gen = dict(zip(_MIXER_GEN, refs[n_fixed + len(_MIXER_BF16):]))
    i = pl.program_id(0)
    ts = r["xp"].shape[0]
    d_pool = zbuf.shape[1]

    @pl.when(i == 0)
    def _():
        zbuf[0:HIST_ROWS, :] = jnp.zeros((HIST_ROWS, d_pool), F32)
        hp_re[...] = jnp.zeros(hp_re.shape, F32)
        hp_im[...] = jnp.zeros(hp_im.shape, F32)
        for name, ref in bf16_weights.items():
            ref[...] = r[name][...].astype(BF16)
        _expand_s5_params(r, gen, (Tp, Ts))

    r.update(bf16_weights)
    r.update(gen)

    @pl.when(i < n_prompt)
    def _():
        x = r["xp"][...]
        z = _dot(_rmsnorm(x, r["g_mix"][...]), r["w_in"][...])
        zbuf[HIST_ROWS:HIST_ROWS + ts, :] = z[:, :d_pool]
        u = z[:, d_pool:]
        pos = i * ts + lax.broadcasted_iota(jnp.int32, (ts, 1), 0)
        y_pool = _pool_windows(zbuf, ts, pos, r)
        y_state = _s5_scan(u, ts // Tp, Tp, "p", (hp_re[...], hp_im[...]), None, r, hp_re, hp_im)
        r["outp"][...] = _mix_tail(x, y_pool, u, y_state, r)
        tail = zbuf[ts:ts + HIST_ROWS, :]
        zbuf[0:HIST_ROWS, :] = tail
        r["histp"][...] = tail

    @pl.when(i == n_prompt)
    def _():
        ext = HIST_ROWS + Ts
        x = r["xs"][...]
        z = _dot(_rmsnorm(x, r["g_mix"][...]), r["w_in"][...])
        u = z[:, d_pool:]
        for b in range(n_streams):
            zbuf[b * ext:b * ext + 1, :] = jnp.zeros((1, d_pool), F32)
            for t in range(POOL_HIST):
                zbuf[b * ext + 1 + t:b * ext + 2 + t, :] = r["hist_in"][t, b:b + 1, :]
            zbuf[b * ext + HIST_ROWS:(b + 1) * ext, :] = z[b * Ts:(b + 1) * Ts, :d_pool]
            for t in range(POOL_HIST):
                row = (b + 1) * ext - POOL_HIST + t
                r["hists"][t, b:b + 1, :] = zbuf[row:row + 1, :]
        nrows = n_streams * ext - HIST_ROWS
        ridx = lax.broadcasted_iota(jnp.int32, (nrows, 1), 0)
        pos = PAST_LEN + lax.rem(ridx, ext)
        y_all = _pool_windows(zbuf, nrows, pos, r)
        for b in range(n_streams):
            ypool_ref[b * Ts:(b + 1) * Ts, :] = y_all[b * ext:b * ext + Ts, :]
        G, P = r["hrep"].shape
        for src, dst in ((r["h0_re"], h0f_re), (r["h0_im"], h0f_im)):
            for b in range(n_streams):
                for g in range(G):
                    dst[b:b + 1, g * P:(g + 1) * P] = src[b * G + g:b * G + g + 1, :]
        y_state = _s5_scan(u, n_streams, Ts, "s", None, (h0f_re, h0f_im), r, hs_re, hs_im)
        r["outs"][...] = _mix_tail(x, ypool_ref[...], u, y_state, r)
        for g in range(G):
            cols = slice(g * P, (g + 1) * P)
            r["hrep"][g:g + 1, :] = hp_re[:, cols]
            r["himp"][g:g + 1, :] = hp_im[:, cols]
            for b in range(n_streams):
                r["hres"][b * G + g:b * G + g + 1, :] = hs_re[b:b + 1, cols]
                r["hims"][b * G + g:b * G + g + 1, :] = hs_im[b:b + 1, cols]


def _const_spec(shape):
    nd = len(shape)
    return pl.BlockSpec(shape, lambda *_: (0,) * nd)


def _weight_spec(shape):
    nd = len(shape)
    return pl.BlockSpec(shape, lambda *_: (0,) * nd, pipeline_mode=pl.Buffered(1))


def _prompt_spec(ts, d, n_prompt):
    return pl.BlockSpec((ts, d), lambda i: (jnp.minimum(i, n_prompt - 1), 0))


def _mixer(xp, xs, hist_in, h0_re, h0_im, p, n_streams, Ts):
    S, D = xp.shape
    rows_s = xs.shape[0]
    ts = PROMPT_BLOCK
    n_prompt = S // ts
    d_pool = hist_in.shape[-1]
    n_state = p["a1_re"].shape[-1]
    P = h0_re.shape[-1]
    G = n_state // P
    assert n_streams * (HIST_ROWS + Ts) <= HIST_ROWS + ts and rows_s <= ts
    args = dict(p, xp=xp, xs=xs, hist_in=hist_in, h0_re=h0_re, h0_im=h0_im)
    ins = [args[n] for n in _MIXER_IN]
    in_specs = [_prompt_spec(ts, D, n_prompt)] + [_weight_spec(a.shape) for a in ins[1:]]
    n_half, nh, H = p["bb_re"].shape
    d_half = p["cc_re"].shape[1]
    gen_shapes = ([((n_half, d_half, 2 * nh), BF16), ((n_half, 2 * nh, d_half), BF16)]
                  + [((PROMPT_CHUNK, n_state), F32)] * 4 + [((Ts, n_state), F32)] * 4 + [((2, n_state), F32)] * 2)
    out_shapes = dict(
        outp=(S, D), outs=(rows_s, D), histp=(HIST_ROWS, d_pool), hrep=(G, P), himp=(G, P),
        hists=(POOL_HIST, n_streams, d_pool), hres=(n_streams * G, P), hims=(n_streams * G, P))
    out_specs = [_prompt_spec(ts, D, n_prompt)] + [_const_spec(out_shapes[n]) for n in _MIXER_OUT[1:]]
    return pl.pallas_call(
        functools.partial(_mixer_kernel, n_prompt=n_prompt, n_streams=n_streams, Tp=PROMPT_CHUNK, Ts=Ts),
        grid=(n_prompt + 1,),
        in_specs=in_specs,
        out_specs=out_specs,
        out_shape=[jax.ShapeDtypeStruct(out_shapes[n], F32) for n in _MIXER_OUT],
        scratch_shapes=[pltpu.VMEM((HIST_ROWS + ts, d_pool), F32),
                        pltpu.VMEM((rows_s, d_pool), F32)]
        + [pltpu.VMEM((1, n_state), F32)] * 2 + [pltpu.VMEM((n_streams, n_state), F32)] * 4
        + [pltpu.VMEM(args[n].shape, BF16) for n in _MIXER_BF16]
        + [pltpu.VMEM(shape, dtype) for shape, dtype in gen_shapes],
        compiler_params=pltpu.CompilerParams(dimension_semantics=("arbitrary",),
                                             vmem_limit_bytes=VMEM_LIMIT),
        name="mixer",
    )(*ins)


def _attend(q_heads, k_heads, v_heads, scale):
    outs = []
    for q, k, v in zip(q_heads, k_heads, v_heads):
        s = lax.dot_general(q, k.astype(BF16), (((1,), (1,)), ((), ())), preferred_element_type=F32) * scale
        e = jnp.exp(s - jnp.max(s, axis=-1, keepdims=True))
        outs.append(_dot(e, v) / jnp.sum(e, axis=-1, keepdims=True))
    return outs


_XATTN_IN = ("xp", "xs", "mem", "kc", "vc", "g_mem", "w_k", "w_v", "g_xattn", "w_q", "w_o")
_XATTN_OUT = ("outp", "outs", "mk", "mv")


def _xattn_kernel(*refs, n_prompt, n_streams, Ts, hpu, scale):
    r = dict(zip(_XATTN_IN + _XATTN_OUT, refs))
    kp_ref, vp_ref, qs_ref, os_ref, kbuf, vbuf, sem, kvf, osem = refs[len(_XATTN_IN) + len(_XATTN_OUT):]
    i = pl.program_id(0)
    hd = kp_ref.shape[-1] // MEM_HEADS
    ups = MEM_HEADS // hpu

    def unit_copies(unit, hg):
        b, slot = unit // ups, lax.rem(unit, 2)
        return [pltpu.make_async_copy(src.at[0, b, :, hg * hpu + e, :], dst.at[slot, e], sem.at[slot, j * hpu + e])
                for j, (src, dst) in enumerate(((r["kc"], kbuf), (r["vc"], vbuf))) for e in range(hpu)]

    def kv_out_copies():
        return [pltpu.make_async_copy(kvf.at[j, h], dst.at[0, 0, :, h, :], osem.at[j, h])
                for j, dst in enumerate((r["mk"], r["mv"])) for h in range(MEM_HEADS)]

    def for_unit(unit, fn):
        for hg in range(ups):
            @pl.when(lax.rem(unit, ups) == hg)
            def _():
                for n, cp in enumerate(unit_copies(unit, hg)):
                    fn(n, cp)

    def start_copy(n, cp):
        cp.start(priority=n % 2)

    @pl.when(i == 0)
    def _():
        for_unit(i, start_copy)
        mem = r["mem"][...]
        inv = _inv_rms(mem)
        mg = (mem * r["g_mem"][...]).astype(BF16)
        k = _dot(mg, r["w_k"][...]) * inv
        v = _dot(mg, r["w_v"][...]) * inv
        for h in range(MEM_HEADS):
            kvf[0, h] = k[:, h * hd:(h + 1) * hd]
            kvf[1, h] = v[:, h * hd:(h + 1) * hd]
        for cp in kv_out_copies():
            cp.start()
        kp_ref[...] = k.astype(BF16)
        vp_ref[...] = v.astype(BF16)
        q = _norm_dot(r["xs"][...], r["g_xattn"][...], r["w_q"][...]).astype(BF16)
        for h in range(MEM_HEADS):
            qs_ref[h] = q[:, h * hd:(h + 1) * hd]

    @pl.when(i < n_prompt)
    def _():
        @pl.when(i + 1 < n_prompt)
        def _():
            for_unit(i + 1, start_copy)

        for_unit(i, lambda n, cp: cp.wait())

        x = r["xp"][...]
        q = _norm_dot(x, r["g_xattn"][...], r["w_q"][...]).astype(BF16)
        o = _attend([q[:, h * hd:(h + 1) * hd] for h in range(MEM_HEADS)],
                    [kp_ref[:, h * hd:(h + 1) * hd] for h in range(MEM_HEADS)],
                    [vp_ref[:, h * hd:(h + 1) * hd] for h in range(MEM_HEADS)], scale)
        r["outp"][...] = x + _dot(jnp.concatenate(o, axis=-1), r["w_o"][...])

        b, hg, slot = i // ups, lax.rem(i, ups), lax.rem(i, 2)
        rows = pl.ds(pl.multiple_of(b * Ts, Ts), Ts)
        heads = [hg * hpu + e for e in range(hpu)]
        o_s = _attend([qs_ref[h, rows, :] for h in heads], [kbuf[slot, e] for e in range(hpu)],
                      [vbuf[slot, e] for e in range(hpu)], scale)
        for h, o_h in zip(heads, o_s):
            os_ref[h, rows, :] = o_h.astype(BF16)

    @pl.when(i == n_prompt)
    def _():
        for cp in kv_out_copies():
            cp.wait()
        o = jnp.concatenate([os_ref[h] for h in range(MEM_HEADS)], axis=-1)
        r["outs"][...] = r["xs"][...] + _dot(o, r["w_o"][...])


def _xattn(xp, xs, mem, kc, vc, p, n_streams, Ts):
    S, D = xp.shape
    rows_s = xs.shape[0]
    ts = XATTN_BLOCK
    n_prompt = S // ts
    n_mem = mem.shape[0]
    hd = D // MEM_HEADS
    hpu = n_streams * MEM_HEADS // n_prompt
    assert hpu * n_prompt == n_streams * MEM_HEADS and MEM_HEADS % hpu == 0
    args = dict(p, xp=xp, xs=xs, mem=mem, kc=kc, vc=vc)
    ins = [args[n] for n in _XATTN_IN]
    cache_spec = pl.BlockSpec(memory_space=pl.ANY)
    in_specs = [_prompt_spec(ts, D, n_prompt)]
    for n in _XATTN_IN[1:]:
        in_specs.append(cache_spec if n in ("kc", "vc") else _weight_spec(args[n].shape))
    out_shapes = dict(outp=(S, D), outs=(rows_s, D), mk=(1, 1, n_mem, MEM_HEADS, hd), mv=(1, 1, n_mem, MEM_HEADS, hd))
    out_specs = [_prompt_spec(ts, D, n_prompt), _const_spec(out_shapes["outs"]), cache_spec, cache_spec]
    return pl.pallas_call(
        functools.partial(_xattn_kernel, n_prompt=n_prompt, n_streams=n_streams, Ts=Ts, hpu=hpu, scale=hd ** -0.5),
        grid=(n_prompt + 1,),
        in_specs=in_specs,
        out_specs=out_specs,
        out_shape=[jax.ShapeDtypeStruct(out_shapes[n], F32) for n in _XATTN_OUT],
        scratch_shapes=[pltpu.VMEM((n_mem, D), BF16), pltpu.VMEM((n_mem, D), BF16),
                        pltpu.VMEM((MEM_HEADS, rows_s, hd), BF16), pltpu.VMEM((MEM_HEADS, rows_s, hd), BF16),
                        pltpu.VMEM((2, hpu, n_mem, hd), F32), pltpu.VMEM((2, hpu, n_mem, hd), F32),
                        pltpu.SemaphoreType.DMA((2, 2 * hpu)),
                        pltpu.VMEM((2, MEM_HEADS, n_mem, hd), F32), pltpu.SemaphoreType.DMA((2, MEM_HEADS))],
        compiler_params=pltpu.CompilerParams(dimension_semantics=("arbitrary",),
                                             vmem_limit_bytes=VMEM_LIMIT),
        name="xattn",
    )(*ins)


def _mlp_rows(x, g_ref, wup_ref, wdown_ref, gfin_ref):
    hg = (x * g_ref[...]).astype(BF16)
    acc = None
    for j in range(wup_ref.shape[1] // FF_CHUNK):
        cols = slice(j * FF_CHUNK, (j + 1) * FF_CHUNK)
        up = _dot(hg, wup_ref[:, cols])
        part = _dot(jnp.square(jnp.maximum(up, 0.0)), wdown_ref[cols, :])
        acc = part if acc is None else acc + part
    return _rmsnorm(x + acc * jnp.square(_inv_rms(x)), gfin_ref[...])


def _mlp_kernel(xp_ref, xs_ref, g_ref, wup_ref, wdown_ref, gfin_ref, outp_ref, outs_ref, *, n_prompt):
    i = pl.program_id(0)

    @pl.when(i < n_prompt)
    def _():
        outp_ref[...] = _mlp_rows(xp_ref[...], g_ref, wup_ref, wdown_ref, gfin_ref)

    @pl.when(i == n_prompt)
    def _():
        outs_ref[...] = _mlp_rows(xs_ref[...], g_ref, wup_ref, wdown_ref, gfin_ref)


def _mlp(xp, xs, p):
    S, D = xp.shape
    ts = PROMPT_BLOCK
    n_prompt = S // ts
    weights = (p["g_mlp"], p["w_up"], p["w_down"], p["g_final"])
    return pl.pallas_call(
        functools.partial(_mlp_kernel, n_prompt=n_prompt),
        grid=(n_prompt + 1,),
        in_specs=[_prompt_spec(ts, D, n_prompt), _weight_spec(xs.shape)] + [_weight_spec(w.shape) for w in weights],
        out_specs=[_prompt_spec(ts, D, n_prompt), _const_spec(xs.shape)],
        out_shape=[jax.ShapeDtypeStruct(xp.shape, F32), jax.ShapeDtypeStruct(xs.shape, F32)],
        compiler_params=pltpu.CompilerParams(dimension_semantics=("arbitrary",),
                                             vmem_limit_bytes=VMEM_LIMIT),
        name="mlp",
    )(xp, xs, *weights)


def _prepare_layer(l, g_mix, w_in, w_pool, pool_scale, a_re, a_im, b_re, b_im, c_re, c_im, d_skip,
                   log_dt, w_glu, b_glu, w_out, g_xattn, g_mem, w_q, w_k, w_v, w_o, g_mlp, w_up,
                   w_down, g_final):
    G, P = a_re.shape[1:]
    gi = G // N_HALF
    ar, ai = a_re[l].astype(F32), a_im[l].astype(F32)
    dt = jnp.exp(log_dt[l].astype(F32))[:, None]
    lam_re, lam_im = ar * dt, ai * dt

    mag = jnp.exp(lam_re)
    ab_re, ab_im = mag * jnp.cos(lam_im), mag * jnp.sin(lam_im)

    den = ar * ar + ai * ai
    coef_re = ((ab_re - 1.0) * ar + ab_im * ai) / den
    coef_im = (ab_im * ar - (ab_re - 1.0) * ai) / den
    br, bi = b_re[l].astype(F32), b_im[l].astype(F32)
    bb_re = coef_re[..., None] * br - coef_im[..., None] * bi
    bb_im = coef_re[..., None] * bi + coef_im[..., None] * br

    half = lambda t: t.astype(F32).reshape((N_HALF, gi * t.shape[1], t.shape[2]))
    row = lambda v: v.astype(F32).reshape(1, -1)
    return dict(
        g_mix=row(g_mix[l]), w_in=w_in[l], w_pool=w_pool[l], pool_scale=row(pool_scale[l]),
        bb_re=half(bb_re), bb_im=half(bb_im), cc_re=half(c_re[l]), cc_im=half(c_im[l]),
        a1_re=row(ab_re), a1_im=row(ab_im),
        d_skip=row(d_skip[l]), w_glu=w_glu[l], b_glu=row(b_glu[l]), w_out=w_out[l],
        g_xattn=row(g_xattn[l]), g_mem=row(g_mem[l]), w_q=w_q[l], w_k=w_k[l], w_v=w_v[l], w_o=w_o[l],
        g_mlp=row(g_mlp[l]), w_up=w_up[l], w_down=w_down[l], g_final=row(g_final))


def kernel(x_prompt, x_sample, cache_mem_k, cache_mem_v, state_pool, state_ssm_re, state_ssm_im, mem_prompt, g_mix, w_in, w_pool, pool_scale, ssm_a_re, ssm_a_im, ssm_b_re, ssm_b_im, ssm_c_re, ssm_c_im, ssm_d, ssm_log_dt, w_glu, b_glu, w_out, g_xattn, g_mem, w_q, w_k, w_v, w_o, g_mlp, w_up, w_down, g_final):
    depth = g_mix.shape[0]
    assert depth == 1 and x_prompt.shape[0] == 1, "single layer, single prompt stream"
    Bp, S, D = x_prompt.shape
    Bs, Ts, _ = x_sample.shape
    n_mem = mem_prompt.shape[1]
    G, P = ssm_a_re.shape[1:]
    assert S % PROMPT_BLOCK == 0 and S % XATTN_BLOCK == 0
    assert PROMPT_BLOCK % PROMPT_CHUNK == 0 and Ts <= PROMPT_CHUNK

    l = 0
    p = _prepare_layer(l, g_mix, w_in, w_pool, pool_scale, ssm_a_re, ssm_a_im, ssm_b_re, ssm_b_im,
                       ssm_c_re, ssm_c_im, ssm_d, ssm_log_dt, w_glu, b_glu, w_out, g_xattn, g_mem,
                       w_q, w_k, w_v, w_o, g_mlp, w_up, w_down, g_final)

    xp1, xs1, hist_p, hre_p, him_p, hist_s, hre_s, him_s = _mixer(
        x_prompt[0], x_sample.reshape(Bs * Ts, D), jnp.transpose(state_pool[l], (1, 0, 2)),
        state_ssm_re[l].reshape(Bs * G, P), state_ssm_im[l].reshape(Bs * G, P), p, Bs, Ts)
    xp2, xs2, mk, mv = _xattn(xp1, xs1, mem_prompt[0], cache_mem_k[l:l + 1], cache_mem_v[l:l + 1], p, Bs, Ts)
    y_prompt, y_sample = _mlp(xp2, xs2, p)

    return (y_prompt[None], y_sample.reshape(Bs, Ts, D), mk, mv,
            hist_p[1:].reshape(1, Bp, POOL_HIST, -1),
            hre_p.reshape(1, Bp, G, P), him_p.reshape(1, Bp, G, P),
            jnp.transpose(hist_s, (1, 0, 2))[None], hre_s.reshape(1, Bs, G, P), him_s.reshape(1, Bs, G, P))
```
